```python
import jax, jax.numpy as jnp
from jax import lax
import numpy as np

D_MODEL = 2048
BATCH = 16
SEQ = 256
DEPTH = 1
DEC_BATCH = 8
DEC_SEQ = 2048
PAST_LEN = 256

GRID_W = 64
MIX_WIDTH = D_MODEL
HG_WIDTH = MIX_WIDTH // 2
HG_DK = 128
HG_DV = 128
HG_HEADS = HG_WIDTH // HG_DK
HG_CHUNK = 64
ATT_WIDTH = MIX_WIDTH - HG_WIDTH
ATT_HEAD_DIM = 128
ATT_HEADS = ATT_WIDTH // ATT_HEAD_DIM
ATT_KV_HEADS = 2
ATT_GROUP = ATT_HEADS // ATT_KV_HEADS
KV_WIDTH = ATT_KV_HEADS * ATT_HEAD_DIM
WINDOW = 128
ATT_BLOCK = 128
ROPE_BASE = 10000.0
ROPE_FREQS = ATT_HEAD_DIM // 4
N_EXPERTS = 16
CAPACITY_FACTOR = 2
EXPERT_FF = 5632
NORM_EPS = 1e-6
IN_SPLITS = (HG_WIDTH, 2 * HG_WIDTH, 3 * HG_WIDTH, 4 * HG_WIDTH, 5 * HG_WIDTH,
             5 * HG_WIDTH + ATT_WIDTH, 5 * HG_WIDTH + ATT_WIDTH + KV_WIDTH)
IN_WIDTH = 5 * HG_WIDTH + ATT_WIDTH + 2 * KV_WIDTH

kernel_name = "hybrid_hgrn2_swa_ec_moe_diffusion_step"


def rms_norm(x, g):
    xf = x.astype(jnp.float32)
    xf = xf * lax.rsqrt(jnp.mean(xf * xf, axis=-1, keepdims=True) + NORM_EPS)
    return xf.astype(x.dtype) * g


def ada_modulation(cond, w_ada, b_ada):
    m = jnp.einsum('bd,de->be', jax.nn.silu(cond), w_ada) + b_ada
    return jnp.split(m[:, None, :], 6, axis=-1)


def modulate(h, shift, scale):
    return h * (1 + scale) + shift


def gla_chunk_scan(q, k, v, log_f, s0):
    B, N, H, DK = q.shape
    DV = v.shape[-1]
    nc = N // HG_CHUNK

    def chunks(a):
        return a.reshape(B, nc, HG_CHUNK, H, a.shape[-1]).transpose(1, 0, 3, 2, 4)

    tri = jnp.tril(jnp.ones((HG_CHUNK, HG_CHUNK), dtype=bool))[:, :, None]

    def step(state, blk):
        qc, kc, vc, gc = blk
        b = jnp.cumsum(gc, axis=2)
        b_last = b[:, :, -1:, :]
        o_inter = jnp.einsum('bhtd,bhdv->bhtv', qc * jnp.exp(b), state)
        decay = jnp.exp(jnp.where(tri, b[:, :, :, None, :] - b[:, :, None, :, :], -jnp.inf))
        att = jnp.einsum('bhtd,bhsd,bhtsd->bhts', qc, kc, decay)
        o = o_inter + jnp.einsum('bhts,bhsv->bhtv', att, vc)
        state = (jnp.exp(b_last[:, :, 0, :])[..., None] * state
                 + jnp.einsum('bhsd,bhsv->bhdv', kc * jnp.exp(b_last - b), vc))
        return state, o

    s_final, o = lax.scan(step, s0, (chunks(q), chunks(k), chunks(v), chunks(log_f)))
    return o.transpose(1, 0, 3, 2, 4).reshape(B, N, H, DV), s_final


def hgrn2_bidirectional(hq, hf_f, hf_b, hi, hg, lb, norm_g, s0_f, s0_b):
    B, N, _ = hq.shape

    def heads(a):
        return a.astype(jnp.float32).reshape(B, N, HG_HEADS, -1)

    q = jax.nn.silu(heads(hq))
    v = heads(hi)
    lb = lb.reshape(2, HG_HEADS, HG_DK)

    def gate(logit, lb_d):
        f = lb_d + (1 - lb_d) * jax.nn.sigmoid(heads(logit))
        return 1 - f, jnp.log(f)

    k_f, lf_f = gate(hf_f, lb[0])
    k_b, lf_b = gate(hf_b, lb[1])
    o_f, s_f = gla_chunk_scan(q, k_f, v, lf_f, s0_f)
    rev = lambda a: jnp.flip(a, axis=1)
    o_b, s_b = gla_chunk_scan(rev(q), rev(k_b), rev(v), rev(lf_b), s0_b)
    o = rms_norm(o_f + rev(o_b), norm_g.reshape(HG_HEADS, HG_DV).astype(jnp.float32))
    o = o.reshape(B, N, HG_WIDTH).astype(hq.dtype) * jax.nn.silu(hg)
    return o, s_f, s_b


def axial_rope_tables(rows, dtype):
    row = jnp.repeat(jnp.arange(rows), GRID_W).astype(jnp.float32)
    col = jnp.tile(jnp.arange(GRID_W), rows).astype(jnp.float32)
    inv = ROPE_BASE ** (-jnp.arange(ROPE_FREQS, dtype=jnp.float32) / ROPE_FREQS)
    ang = jnp.stack([row[:, None] * inv, col[:, None] * inv], axis=1)
    return jnp.cos(ang).astype(dtype), jnp.sin(ang).astype(dtype)


def apply_axial_rope(x, cos, sin):
    B, N, H, HD = x.shape
    xr = x.reshape(B, N, H, 2, 2, ROPE_FREQS)
    x1, x2 = xr[..., 0, :], xr[..., 1, :]
    c, s = cos[:, None], sin[:, None]
    out = jnp.stack([x1 * c - x2 * s, x2 * c + x1 * s], axis=-2)
    return out.reshape(B, N, H, HD)


def context_attention(q, k, v, sink):
    B, C, _, _ = q.shape
    nb = C // ATT_BLOCK
    scale = ATT_HEAD_DIM ** -0.5
    qb = q.reshape(B, nb, ATT_BLOCK, ATT_KV_HEADS, ATT_GROUP, ATT_HEAD_DIM).transpose(1, 0, 2, 3, 4, 5)
    sink_l = sink.reshape(1, ATT_KV_HEADS, ATT_GROUP, 1, 1).astype(jnp.float32)

    def block(qi):
        s = jnp.einsum('bqkgd,bckd->bkgqc', qi, k).astype(jnp.float32) * scale
        logits = jnp.concatenate([jnp.broadcast_to(sink_l, s.shape[:-1] + (1,)), s], axis=-1)
        p = jax.nn.softmax(logits, axis=-1)[..., 1:].astype(v.dtype)
        return jnp.einsum('bkgqc,bckd->bqkgd', p, v)

    o = lax.map(block, qb)
    return o.transpose(1, 0, 2, 3, 4, 5).reshape(B, C, ATT_WIDTH)


def window_attention(q, k, v, k_ctx, v_ctx, sink):
    B, N, _, _ = q.shape
    nb = N // ATT_BLOCK
    scale = ATT_HEAD_DIM ** -0.5
    qb = q.reshape(B, nb, ATT_BLOCK, ATT_KV_HEADS, ATT_GROUP, ATT_HEAD_DIM)

    def band(a):
        a = a.reshape(B, nb, ATT_BLOCK, ATT_KV_HEADS, ATT_HEAD_DIM)
        a = jnp.pad(a, ((0, 0), (1, 1), (0, 0), (0, 0), (0, 0)))
        return jnp.concatenate([a[:, :-2], a[:, 1:-1], a[:, 2:]], axis=2)

    kb, vb = band(k), band(v)
    s_loc = jnp.einsum('bnqkgd,bnskd->bnkgqs', qb, kb).astype(jnp.float32) * scale
    s_ctx = jnp.einsum('bnqkgd,bckd->bnkgqc', qb, k_ctx).astype(jnp.float32) * scale
    qpos = jnp.arange(N).reshape(nb, ATT_BLOCK)
    kpos = (jnp.arange(nb)[:, None] - 1) * ATT_BLOCK + jnp.arange(3 * ATT_BLOCK)[None, :]
    valid = ((kpos[:, None, :] >= 0) & (kpos[:, None, :] < N)
             & (jnp.abs(qpos[:, :, None] - kpos[:, None, :]) <= WINDOW))
    s_loc = jnp.where(valid[None, :, None, None], s_loc, -jnp.inf)
    sink_l = jnp.broadcast_to(sink.reshape(1, 1, ATT_KV_HEADS, ATT_GROUP, 1, 1).astype(jnp.float32),
                              s_loc.shape[:-1] + (1,))
    p = jax.nn.softmax(jnp.concatenate([sink_l, s_ctx, s_loc], axis=-1), axis=-1).astype(v.dtype)
    C = k_ctx.shape[1]
    o = (jnp.einsum('bnkgqc,bckd->bnqkgd', p[..., 1:1 + C], v_ctx)
         + jnp.einsum('bnkgqs,bnskd->bnqkgd', p[..., 1 + C:], vb))
    return o.reshape(B, N, ATT_WIDTH)


def split_projection(h, w_in):
    return jnp.split(jnp.einsum('bnd,de->bne', h, w_in), IN_SPLITS, axis=-1)


def context_mixer(h, w_in, lb, hg_norm_g, sink, w_out):
    B, N, _ = h.shape
    hq, hf_f, hf_b, hi, hg, aq, ak, av = split_projection(h, w_in)
    zero = jnp.zeros((B, HG_HEADS, HG_DK, HG_DV), jnp.float32)
    o_hg, s_f, s_b = hgrn2_bidirectional(hq, hf_f, hf_b, hi, hg, lb, hg_norm_g, zero, zero)
    q = aq.reshape(B, N, ATT_HEADS, ATT_HEAD_DIM)
    k = ak.reshape(B, N, ATT_KV_HEADS, ATT_HEAD_DIM)
    v = av.reshape(B, N, ATT_KV_HEADS, ATT_HEAD_DIM)
    o_att = context_attention(q, k, v, sink)
    mix = jnp.einsum('bnc,cd->bnd', jnp.concatenate([o_hg, o_att], axis=-1), w_out)
    return mix, k, v, jnp.stack([s_f, s_b], axis=1)


def latent_mixer(h, k_ctx, v_ctx, s_ctx, cos, sin, w_in, lb, hg_norm_g, sink, w_out):
    B, N, _ = h.shape
    hq, hf_f, hf_b, hi, hg, aq, ak, av = split_projection(h, w_in)
    s_ctx = s_ctx.astype(jnp.float32)
    o_hg, _, _ = hgrn2_bidirectional(hq, hf_f, hf_b, hi, hg, lb, hg_norm_g, s_ctx[:, 0], s_ctx[:, 1])
    q = apply_axial_rope(aq.reshape(B, N, ATT_HEADS, ATT_HEAD_DIM), cos, sin)
    k = apply_axial_rope(ak.reshape(B, N, ATT_KV_HEADS, ATT_HEAD_DIM), cos, sin)
    v = av.reshape(B, N, ATT_KV_HEADS, ATT_HEAD_DIM)
    o_att = window_attention(q, k, v, k_ctx, v_ctx, sink)
    return jnp.einsum('bnc,cd->bnd', jnp.concatenate([o_hg, o_att], axis=-1), w_out)


def expert_choice_ffn(x, w_router, w_gate, w_up, w_down):
    B, N, D = x.shape
    cap = CAPACITY_FACTOR * N // N_EXPERTS
    aff = jax.nn.softmax(jnp.einsum('bnd,de->bne', x, w_router).astype(jnp.float32), axis=-1)
    g, idx = lax.top_k(aff.transpose(0, 2, 1), cap)
    bidx = jnp.arange(B)[:, None, None]
    xs = x[bidx, idx]
    hdn = jax.nn.silu(jnp.einsum('becd,edf->becf', xs, w_gate)) * jnp.einsum('becd,edf->becf', xs, w_up)
    y = jnp.einsum('becf,efd->becd', hdn, w_down) * g[..., None].astype(x.dtype)
    return jnp.zeros_like(x).at[bidx, idx].add(y)


def setup_inputs(seed: int = 0) -> dict:
    key = jax.random.key(seed)
    ks = jax.random.split(key, 24)
    n = jax.random.normal
    f32 = jnp.float32
    sd = D_MODEL ** -0.5
    return {
        "x_prompt": n(ks[0], (BATCH, SEQ, D_MODEL), f32),
        "x_sample": n(ks[1], (DEC_BATCH, DEC_SEQ, D_MODEL), f32),
        "cache_k": n(ks[2], (DEC_BATCH, DEPTH, PAST_LEN, ATT_KV_HEADS, ATT_HEAD_DIM), f32),
        "cache_v": n(ks[3], (DEC_BATCH, DEPTH, PAST_LEN, ATT_KV_HEADS, ATT_HEAD_DIM), f32),
        "state_hgrn": 0.5 * n(ks[4], (DEC_BATCH, DEPTH, 2, HG_HEADS, HG_DK, HG_DV), f32),
        "c": n(ks[5], (DEC_BATCH, D_MODEL), f32),
        "c_ctx": n(ks[6], (D_MODEL,), f32),
        "w_ada": n(ks[7], (DEPTH, D_MODEL, 6 * D_MODEL), f32) * sd,
        "b_ada": 0.01 * n(ks[8], (DEPTH, 6 * D_MODEL), f32),
        "norm1_g": 1.0 + 0.02 * n(ks[9], (DEPTH, D_MODEL), f32),
        "w_in": n(ks[10], (DEPTH, D_MODEL, IN_WIDTH), f32) * sd,
        "hg_lb": 0.1 * n(ks[11], (DEPTH + 1, 2, HG_WIDTH), f32),
        "hg_norm_g": 1.0 + 0.02 * n(ks[12], (DEPTH, HG_HEADS * HG_DV), f32),
        "attn_sink": n(ks[13], (DEPTH, ATT_HEADS), f32),
        "w_out": n(ks[14], (DEPTH, MIX_WIDTH, D_MODEL), f32) * MIX_WIDTH ** -0.5,
        "norm2_g": 1.0 + 0.02 * n(ks[15], (DEPTH, D_MODEL), f32),
        "w_router": n(ks[16], (DEPTH, D_MODEL, N_EXPERTS), f32) * sd,
        "w_gate": n(ks[17], (DEPTH, N_EXPERTS, D_MODEL, EXPERT_FF), f32) * sd,
        "w_up": n(ks[18], (DEPTH, N_EXPERTS, D_MODEL, EXPERT_FF), f32) * sd,
        "w_down": n(ks[19], (DEPTH, N_EXPERTS, EXPERT_FF, D_MODEL), f32) * EXPERT_FF ** -0.5,
        "final_norm_g": 1.0 + 0.02 * n(ks[20], (D_MODEL,), f32),
    }


def reference(x_prompt, x_sample, cache_k, cache_v, state_hgrn, c, c_ctx, w_ada, b_ada, norm1_g,
              w_in, hg_lb, hg_norm_g, attn_sink, w_out, norm2_g, w_router, w_gate, w_up, w_down,
              final_norm_g):
    rows = x_sample.shape[1] // GRID_W
    cos, sin = axial_rope_tables(rows, x_sample.dtype)
    lb_all = jnp.cumsum(jax.nn.softmax(hg_lb.astype(jnp.float32), axis=0), axis=0)
    xp, xs = x_prompt, x_sample
    new_k, new_v, new_s = [], [], []
    for l in range(DEPTH):
        lb = lb_all[l]
        sh1, sc1, g1, sh2, sc2, g2 = ada_modulation(c_ctx[None, :], w_ada[l], b_ada[l])
        h = modulate(rms_norm(xp, norm1_g[l]), sh1, sc1)
        mix, k_l, v_l, s_l = context_mixer(h, w_in[l], lb, hg_norm_g[l], attn_sink[l], w_out[l])
        xp = xp + g1 * mix
        h = modulate(rms_norm(xp, norm2_g[l]), sh2, sc2)
        xp = xp + g2 * expert_choice_ffn(h, w_router[l], w_gate[l], w_up[l], w_down[l])
        new_k.append(k_l)
        new_v.append(v_l)
        new_s.append(s_l)
        sh1, sc1, g1, sh2, sc2, g2 = ada_modulation(c, w_ada[l], b_ada[l])
        h = modulate(rms_norm(xs, norm1_g[l]), sh1, sc1)
        mix = latent_mixer(h, cache_k[:, l], cache_v[:, l], state_hgrn[:, l], cos, sin,
                           w_in[l], lb, hg_norm_g[l], attn_sink[l], w_out[l])
        xs = xs + g1 * mix
        h = modulate(rms_norm(xs, norm2_g[l]), sh2, sc2)
        xs = xs + g2 * expert_choice_ffn(h, w_router[l], w_gate[l], w_up[l], w_down[l])
    y_prompt = rms_norm(xp, final_norm_g)
    y_sample = rms_norm(xs, final_norm_g)
    new_cache_k = jnp.stack(new_k, axis=1)
    new_cache_v = jnp.stack(new_v, axis=1)
    new_state_hgrn = jnp.stack(new_s, axis=1)
    return (y_prompt, y_sample, new_cache_k, new_cache_v, new_state_hgrn)
```

```python
import functools

import jax
import jax.numpy as jnp
from jax import lax
from jax.experimental import pallas as pl
from jax.experimental.pallas import tpu as pltpu

F32 = jnp.float32
BF16 = jnp.bfloat16

D_MODEL = 2048
HG_WIDTH = 1024
HG_HEADS = 8
HEAD_DIM = 128
ATT_HEADS = 8
ATT_KV_HEADS = 2
ATT_GROUP = ATT_HEADS // ATT_KV_HEADS
KV_WIDTH = ATT_KV_HEADS * HEAD_DIM
ATT_WIDTH = ATT_HEADS * HEAD_DIM
ATT_BLOCK = 128
GRID_W = 64
ROPE_BASE = 10000.0
ROPE_FREQS = HEAD_DIM // 4
N_EXPERTS = 16
CAPACITY_FACTOR = 2
EXPERT_FF = 5632
NORM_EPS = 1e-6
IN_WIDTH = 5 * HG_WIDTH + ATT_WIDTH + 2 * KV_WIDTH
N_MOD = 6
COND_ROWS = 16

HG_CHUNK = 128
HG_DIAG = 8
SEL_BLOCK = 256

VMEM_LIMIT = 56 * 1024 * 1024


def _cparams(*sem):
    return pltpu.CompilerParams(dimension_semantics=sem, vmem_limit_bytes=VMEM_LIMIT)


def _sigmoid(x):
    return 1.0 / (1.0 + jnp.exp(-x))


def _silu(x):
    return x * _sigmoid(x)


def _dot(a, b):
    return jnp.dot(a, b, preferred_element_type=F32)


def _dot_nt(a, b):
    return lax.dot_general(a, b, (((1,), (1,)), ((), ())), preferred_element_type=F32)


def _dot_tn(a, b):
    return lax.dot_general(a, b, (((0,), (0,)), ((), ())), preferred_element_type=F32)


def _ada_kernel(c_ref, w_ref, b_ref, o_ref):
    s = _silu(c_ref[...]).astype(BF16)
    o_ref[...] = _dot(s, w_ref[...].astype(BF16)) + b_ref[...]


def _ada_modulation(cond, w_ada, b_ada):
    tn = 1024
    n = w_ada.shape[1]
    return pl.pallas_call(
        _ada_kernel,
        out_shape=jax.ShapeDtypeStruct((COND_ROWS, n), F32),
        grid=(n // tn,),
        in_specs=[pl.BlockSpec((COND_ROWS, D_MODEL), lambda j: (0, 0)),
                  pl.BlockSpec((D_MODEL, tn), lambda j: (0, j)),
                  pl.BlockSpec((1, tn), lambda j: (0, j))],
        out_specs=pl.BlockSpec((COND_ROWS, tn), lambda j: (0, j)),
        compiler_params=_cparams("arbitrary"),
        name="ada_modulation",
    )(cond, w_ada, b_ada.reshape(1, n))


def _norm_modulate(x, gain, shift, scale):
    var = jnp.mean(x * x, axis=-1, keepdims=True)
    return (x * lax.rsqrt(var + NORM_EPS) * gain) * (1.0 + scale) + shift


def _inproj_kernel(x_ref, mod_ref, g_ref, w_ref, o_ref, h_ref, *, rows):
    @pl.when(pl.program_id(1) == 0)
    def _():
        shift = mod_ref[0, 0:1, :]
        scale = mod_ref[0, 1:2, :]
        gain = g_ref[...]

        def body(r, carry):
            sl = pl.ds(pl.multiple_of(r * rows, rows), rows)
            h_ref[sl, :] = _norm_modulate(x_ref[sl, :], gain, shift, scale).astype(BF16)
            return carry

        lax.fori_loop(0, x_ref.shape[0] // rows, body, 0)

    o_ref[...] = _dot(h_ref[...], w_ref[...])


def _in_projection(x, mod, gain, w_bf16, *, mod_base, rows_per_mod):
    m = x.shape[0]
    tm, tn = 1024, 512
    return pl.pallas_call(
        functools.partial(_inproj_kernel, rows=128),
        out_shape=jax.ShapeDtypeStruct((m, IN_WIDTH), F32),
        grid=(m // tm, IN_WIDTH // tn),
        in_specs=[pl.BlockSpec((tm, D_MODEL), lambda i, j: (i, 0)),
                  pl.BlockSpec((1, N_MOD, D_MODEL), lambda i, j: (mod_base + (i * tm) // rows_per_mod, 0, 0)),
                  pl.BlockSpec((1, D_MODEL), lambda i, j: (0, 0)),
                  pl.BlockSpec((D_MODEL, tn), lambda i, j: (0, j))],
        out_specs=pl.BlockSpec((tm, tn), lambda i, j: (i, j)),
        scratch_shapes=[pltpu.VMEM((tm, D_MODEL), BF16)],
        compiler_params=_cparams("parallel", "arbitrary"),
        name="in_projection",
    )(x, mod, gain, w_bf16)


def _hgrn_codes(reverse):
    L = HG_CHUNK
    t = lax.broadcasted_iota(jnp.int32, (L, L), 0)
    s = lax.broadcasted_iota(jnp.int32, (L, L), 1)
    code = jnp.zeros((L, L), jnp.int32)
    h = L // 2
    while h >= HG_DIAG:
        same = (t & ~(2 * h - 1)) == (s & ~(2 * h - 1))
        t_hi = (t & h) != 0
        s_hi = (s & h) != 0
        pair = (s_hi & ~t_hi) if reverse else (t_hi & ~s_hi)
        code = jnp.where(same & pair, h * 16, code)
        h //= 2
    same8 = (t & ~(HG_DIAG - 1)) == (s & ~(HG_DIAG - 1))
    causal = (s >= t) if reverse else (s <= t)
    code = jnp.where(same8 & causal, (s & (HG_DIAG - 1)) + 1, code)
    return code


def _cumsum_rows(tri_bf16, g):
    g1 = g.astype(BF16)
    r1 = g - g1.astype(F32)
    g2 = r1.astype(BF16)
    g3 = (r1 - g2.astype(F32)).astype(BF16)
    return _dot(tri_bf16, g1) + _dot(tri_bf16, g2) + _dot(tri_bf16, g3)


def _hgrn_chunk(q, k, g, v, st_ref, code, tri, reverse):
    L = HG_CHUNK
    b = _cumsum_rows(tri, g)
    b_tot = b[0:1, :] if reverse else b[L - 1:L, :]
    st = st_ref[...]
    o = _dot_nt((q * jnp.exp(b)).astype(BF16), st.astype(BF16))
    k_out = (k * jnp.exp(b_tot - b)).astype(BF16)
    v_bf = v.astype(BF16)
    st_ref[...] = st * jnp.exp(b_tot) + _dot_tn(v_bf, k_out)

    att = jnp.zeros((L, L), F32)
    h = L // 2
    while h >= HG_DIAG:
        parts = []
        for p in range(L // (2 * h)):
            m = p * 2 * h + (h if reverse else h - 1)
            parts.append(jnp.broadcast_to(b[m:m + 1, :], (2 * h, HEAD_DIM)))
        ref = parts[0] if len(parts) == 1 else jnp.concatenate(parts, axis=0)
        e = jnp.exp(-jnp.abs(b - ref))
        a = _dot_nt((q * e).astype(BF16), (k * e).astype(BF16))
        att = jnp.where(code == h * 16, a, att)
        h //= 2

    nb = L // HG_DIAG
    k3 = k.reshape(nb, HG_DIAG, HEAD_DIM)
    b3 = b.reshape(nb, HG_DIAG, HEAD_DIM)
    for j in range(HG_DIAG):
        kj = jnp.broadcast_to(k3[:, j:j + 1, :], (nb, HG_DIAG, HEAD_DIM)).reshape(L, HEAD_DIM)
        bj = jnp.broadcast_to(b3[:, j:j + 1, :], (nb, HG_DIAG, HEAD_DIM)).reshape(L, HEAD_DIM)
        x = q * kj * jnp.exp(jnp.minimum(b - bj, 0.0))
        r = jnp.sum(x, axis=-1, keepdims=True)
        att = jnp.where(code == j + 1, r, att)

    return o + _dot(att.astype(BF16), v_bf)


def _hgrn_kernel(q_ref, ff_ref, fb_ref, v_ref, gate_ref, lb_ref, ng_ref, s0_ref,
                 o_ref, sout_ref, of_ref, ob_ref, stf_ref, stb_ref, *, n_tok):
    L = HG_CHUNK
    nc = n_tok // L
    stf_ref[...] = s0_ref[0, 0, 0, 0].T
    stb_ref[...] = s0_ref[0, 0, 1, 0].T
    lb_f = lb_ref[0:1, :]
    lb_b = lb_ref[1:2, :]
    code_f = _hgrn_codes(False)
    code_b = _hgrn_codes(True)
    ti = lax.broadcasted_iota(jnp.int32, (L, L), 0)
    si = lax.broadcasted_iota(jnp.int32, (L, L), 1)
    tri_f = jnp.where(si <= ti, 1.0, 0.0).astype(BF16)
    tri_b = jnp.where(si >= ti, 1.0, 0.0).astype(BF16)

    def one(row, f_ref, lb, st_ref, code, tri, reverse, out_ref):
        sl = pl.ds(row, L)
        q = _silu(q_ref[sl, :])
        f = lb + (1.0 - lb) * _sigmoid(f_ref[sl, :])
        out_ref[sl, :] = _hgrn_chunk(q, 1.0 - f, jnp.log(f), v_ref[sl, :], st_ref, code, tri, reverse)

    def body(c, carry):
        one(pl.multiple_of(c * L, L), ff_ref, lb_f, stf_ref, code_f, tri_f, False, of_ref)
        one(pl.multiple_of((nc - 1 - c) * L, L), fb_ref, lb_b, stb_ref, code_b, tri_b, True, ob_ref)
        return carry

    lax.fori_loop(0, nc, body, 0)

    sout_ref[0, 0, 0, 0] = stf_ref[...].T
    sout_ref[0, 0, 1, 0] = stb_ref[...].T
    gain = ng_ref[...]

    def fin(c, carry):
        sl = pl.ds(pl.multiple_of(c * L, L), L)
        o = of_ref[sl, :] + ob_ref[sl, :]
        var = jnp.mean(o * o, axis=-1, keepdims=True)
        o_ref[sl, :] = ((o * lax.rsqrt(var + NORM_EPS) * gain) * _silu(gate_ref[sl, :])).astype(BF16)
        return carry

    lax.fori_loop(0, nc, fin, 0)


def _hgrn(proj, lb, norm_g, state, *, n_batch, n_tok):
    col = lambda k: (lambda b, h: (b, k * HG_HEADS + h))
    tok_spec = lambda k: pl.BlockSpec((n_tok, HEAD_DIM), col(k))
    st_spec = pl.BlockSpec((1, 1, 2, 1, HEAD_DIM, HEAD_DIM), lambda b, h: (b, 0, 0, h, 0, 0))
    return pl.pallas_call(
        functools.partial(_hgrn_kernel, n_tok=n_tok),
        out_shape=(jax.ShapeDtypeStruct((n_batch * n_tok, HG_WIDTH), BF16),
                   jax.ShapeDtypeStruct((n_batch, 1, 2, HG_HEADS, HEAD_DIM, HEAD_DIM), F32)),
        grid=(n_batch, HG_HEADS),
        in_specs=[tok_spec(0), tok_spec(1), tok_spec(2), tok_spec(3), tok_spec(4),
                  pl.BlockSpec((2, HEAD_DIM), lambda b, h: (0, h)),
                  pl.BlockSpec((1, HEAD_DIM), lambda b, h: (0, h)),
                  st_spec],
        out_specs=(pl.BlockSpec((n_tok, HEAD_DIM), lambda b, h: (b, h)), st_spec),
        scratch_shapes=[pltpu.VMEM((n_tok, HEAD_DIM), F32), pltpu.VMEM((n_tok, HEAD_DIM), F32),
                        pltpu.VMEM((HEAD_DIM, HEAD_DIM), F32), pltpu.VMEM((HEAD_DIM, HEAD_DIM), F32)],
        compiler_params=_cparams("parallel", "parallel"),
        name="hgrn2_scan",
    )(proj, proj, proj, proj, proj, lb, norm_g, state)


def _stack_heads(x, kvh):
    return jnp.concatenate(
        [x[:, (kvh * ATT_GROUP + g) * HEAD_DIM:(kvh * ATT_GROUP + g + 1) * HEAD_DIM] for g in range(ATT_GROUP)],
        axis=0)


def _sink_column(sink_ref, kvh, rows):
    return jnp.concatenate(
        [jnp.full((rows, 1), sink_ref[kvh * ATT_GROUP + g], F32) for g in range(ATT_GROUP)], axis=0)


def _softmax_av(scores, values, sink_col):
    m = sink_col
    for s in scores:
        m = jnp.maximum(m, jnp.max(s, axis=-1, keepdims=True))
    denom = jnp.exp(sink_col - m)
    o = None
    for s, v in zip(scores, values):
        p = jnp.exp(s - m)
        denom = denom + jnp.sum(p, axis=-1, keepdims=True)
        pv = _dot(p.astype(BF16), v)
        o = pv if o is None else o + pv
    return o / denom


def _ctx_attn_kernel(sink_ref, q_ref, k_ref, v_ref, o_ref):
    rows = q_ref.shape[0]
    scale = HEAD_DIM ** -0.5
    q_all = q_ref[...]
    for kvh in range(ATT_KV_HEADS):
        q = _stack_heads(q_all, kvh).astype(BF16)
        k = k_ref[:, kvh * HEAD_DIM:(kvh + 1) * HEAD_DIM].astype(BF16)
        v = v_ref[:, kvh * HEAD_DIM:(kvh + 1) * HEAD_DIM].astype(BF16)
        o = _softmax_av([_dot_nt(q, k) * scale], [v], _sink_column(sink_ref, kvh, rows))
        for g in range(ATT_GROUP):
            hd = kvh * ATT_GROUP + g
            o_ref[:, hd * HEAD_DIM:(hd + 1) * HEAD_DIM] = o[g * rows:(g + 1) * rows, :].astype(BF16)


def _context_attention(proj, sink, *, n_batch, n_tok):
    q_col = 5 * HG_WIDTH // ATT_WIDTH
    k_col = (5 * HG_WIDTH + ATT_WIDTH) // KV_WIDTH
    return pl.pallas_call(
        _ctx_attn_kernel,
        out_shape=jax.ShapeDtypeStruct((n_batch * n_tok, ATT_WIDTH), BF16),
        grid_spec=pltpu.PrefetchScalarGridSpec(
            num_scalar_prefetch=1,
            grid=(n_batch,),
            in_specs=[pl.BlockSpec((n_tok, ATT_WIDTH), lambda b, s: (b, q_col)),
                      pl.BlockSpec((n_tok, KV_WIDTH), lambda b, s: (b, k_col)),
                      pl.BlockSpec((n_tok, KV_WIDTH), lambda b, s: (b, k_col + 1))],
            out_specs=pl.BlockSpec((n_tok, ATT_WIDTH), lambda b, s: (b, 0))),
        compiler_params=_cparams("parallel"),
        name="context_attention",
    )(sink, proj, proj, proj)


def _rope(x, cos, sin_signed, even_group):
    partner = jnp.where(even_group, pltpu.roll(x, HEAD_DIM - ROPE_FREQS, 1), pltpu.roll(x, ROPE_FREQS, 1))
    return x * cos + partner * sin_signed


def _win_attn_kernel(sink_ref, q_ref, k_ref, v_ref, ck_ref, cv_ref, cos_ref, sin_ref, o_ref,
                     kpad_ref, vpad_ref, *, n_tok):
    blk = ATT_BLOCK
    nb = n_tok // blk
    i = pl.program_id(1)
    scale = HEAD_DIM ** -0.5
    lane = lax.broadcasted_iota(jnp.int32, (blk, HEAD_DIM), 1)
    even_group = (lane & ROPE_FREQS) == 0

    @pl.when(i == 0)
    def _():
        zeros = jnp.zeros((blk, KV_WIDTH), BF16)
        kpad_ref[0:blk, :] = zeros
        vpad_ref[0:blk, :] = zeros
        kpad_ref[blk + n_tok:2 * blk + n_tok, :] = zeros
        vpad_ref[blk + n_tok:2 * blk + n_tok, :] = zeros

        def body(r, carry):
            src = pl.ds(pl.multiple_of(r * blk, blk), blk)
            dst = pl.ds(pl.multiple_of((r + 1) * blk, blk), blk)
            cos = cos_ref[src, :]
            sin = sin_ref[src, :]
            for kvh in range(ATT_KV_HEADS):
                cols = slice(kvh * HEAD_DIM, (kvh + 1) * HEAD_DIM)
                kpad_ref[dst, cols] = _rope(k_ref[src, cols], cos, sin, even_group).astype(BF16)
            vpad_ref[dst, :] = v_ref[src, :].astype(BF16)
            return carry

        lax.fori_loop(0, nb, body, 0)

    rows = pl.ds(pl.multiple_of(i * blk, blk), blk)
    cos = cos_ref[rows, :]
    sin = sin_ref[rows, :]
    band = pl.ds(pl.multiple_of(i * blk, blk), 3 * blk)
    r = lax.broadcasted_iota(jnp.int32, (blk, 3 * blk), 0)
    j = lax.broadcasted_iota(jnp.int32, (blk, 3 * blk), 1)
    kpos = j + (i - 1) * blk
    valid = (j >= r) & (j <= r + 2 * blk) & (kpos >= 0) & (kpos < n_tok)
    valid = jnp.concatenate([valid] * ATT_GROUP, axis=0)
    q_all = q_ref[...]
    for kvh in range(ATT_KV_HEADS):
        cols = slice(kvh * HEAD_DIM, (kvh + 1) * HEAD_DIM)
        q = jnp.concatenate(
            [_rope(q_all[:, (kvh * ATT_GROUP + g) * HEAD_DIM:(kvh * ATT_GROUP + g + 1) * HEAD_DIM],
                   cos, sin, even_group) for g in range(ATT_GROUP)], axis=0).astype(BF16)
        s_ctx = _dot_nt(q, ck_ref[0, :, cols].astype(BF16)) * scale
        s_loc = jnp.where(valid, _dot_nt(q, kpad_ref[band, cols]) * scale, -jnp.inf)
        o = _softmax_av([s_ctx, s_loc], [cv_ref[0, :, cols].astype(BF16), vpad_ref[band, cols]],
                        _sink_column(sink_ref, kvh, blk))
        for g in range(ATT_GROUP):
            hd = kvh * ATT_GROUP + g
            o_ref[:, hd * HEAD_DIM:(hd + 1) * HEAD_DIM] = o[g * blk:(g + 1) * blk, :].astype(BF16)


def _window_attention(proj, cache_k, cache_v, sink, cos, sin_signed, *, n_batch, n_tok):
    nb = n_tok // ATT_BLOCK
    n_ctx = cache_k.shape[1]
    q_col = 5 * HG_WIDTH // ATT_WIDTH
    k_col = (5 * HG_WIDTH + ATT_WIDTH) // KV_WIDTH
    table = pl.BlockSpec((n_tok, HEAD_DIM), lambda b, i, s: (0, 0))
    cache = pl.BlockSpec((1, n_ctx, KV_WIDTH), lambda b, i, s: (b, 0, 0))
    return pl.pallas_call(
        functools.partial(_win_attn_kernel, n_tok=n_tok),
        out_shape=jax.ShapeDtypeStruct((n_batch * n_tok, ATT_WIDTH), BF16),
        grid_spec=pltpu.PrefetchScalarGridSpec(
            num_scalar_prefetch=1,
            grid=(n_batch, nb),
            in_specs=[pl.BlockSpec((ATT_BLOCK, ATT_WIDTH), lambda b, i, s: (b * nb + i, q_col)),
                      pl.BlockSpec((n_tok, KV_WIDTH), lambda b, i, s: (b, k_col)),
                      pl.BlockSpec((n_tok, KV_WIDTH), lambda b, i, s: (b, k_col + 1)),
                      cache, cache, table, table],
            out_specs=pl.BlockSpec((ATT_BLOCK, ATT_WIDTH), lambda b, i, s: (b * nb + i, 0)),
            scratch_shapes=[pltpu.VMEM((n_tok + 2 * ATT_BLOCK, KV_WIDTH), BF16),
                            pltpu.VMEM((n_tok + 2 * ATT_BLOCK, KV_WIDTH), BF16)]),
        compiler_params=_cparams("parallel", "arbitrary"),
        name="window_attention",
    )(sink, proj, proj, proj, cache_k, cache_v, cos, sin_signed)


def _rope_tables(n_tok):
    rows = n_tok // GRID_W
    row = jnp.repeat(jnp.arange(rows), GRID_W).astype(F32)
    col = jnp.tile(jnp.arange(GRID_W), rows).astype(F32)
    inv = ROPE_BASE ** (-jnp.arange(ROPE_FREQS, dtype=F32) / ROPE_FREQS)
    ar, ac = row[:, None] * inv, col[:, None] * inv
    cr, sr, cc, sc = jnp.cos(ar), jnp.sin(ar), jnp.cos(ac), jnp.sin(ac)
    return (jnp.concatenate([cr, cr, cc, cc], axis=1), jnp.concatenate([-sr, sr, -sc, sc], axis=1))


def _outproj_kernel(hg_ref, att_ref, w_ref, x_ref, mod_ref, g_ref, xo_ref, h_ref):
    mix = _dot(hg_ref[...], w_ref[0:HG_WIDTH, :]) + _dot(att_ref[...], w_ref[HG_WIDTH:, :])
    x = x_ref[...] + mod_ref[0, 2:3, :] * mix
    xo_ref[...] = x
    h_ref[...] = _norm_modulate(x, g_ref[...], mod_ref[0, 3:4, :], mod_ref[0, 4:5, :]).astype(BF16)


def _out_projection(o_hg, o_att, w_bf16, x, mod, gain, *, mod_base, rows_per_mod):
    m = x.shape[0]
    tm = 256
    row = lambda i: (i, 0)
    return pl.pallas_call(
        _outproj_kernel,
        out_shape=(jax.ShapeDtypeStruct((m, D_MODEL), F32), jax.ShapeDtypeStruct((m, D_MODEL), BF16)),
        grid=(m // tm,),
        in_specs=[pl.BlockSpec((tm, HG_WIDTH), row),
                  pl.BlockSpec((tm, ATT_WIDTH), row),
                  pl.BlockSpec((HG_WIDTH + ATT_WIDTH, D_MODEL), lambda i: (0, 0)),
                  pl.BlockSpec((tm, D_MODEL), row),
                  pl.BlockSpec((1, N_MOD, D_MODEL), lambda i: (mod_base + (i * tm) // rows_per_mod, 0, 0)),
                  pl.BlockSpec((1, D_MODEL), lambda i: (0, 0))],
        out_specs=(pl.BlockSpec((tm, D_MODEL), row), pl.BlockSpec((tm, D_MODEL), row)),
        compiler_params=_cparams("parallel"),
        name="out_projection",
    )(o_hg, o_att, w_bf16, x, mod, gain)


def _prefix_count(x):
    n = x.shape[1]
    i = lax.broadcasted_iota(jnp.int32, (SEL_BLOCK, SEL_BLOCK), 0)
    j = lax.broadcasted_iota(jnp.int32, (SEL_BLOCK, SEL_BLOCK), 1)
    upper = jnp.where(i < j, 1.0, 0.0).astype(BF16)
    off = jnp.zeros((x.shape[0], 1), F32)
    outs = []
    for blk in range(n // SEL_BLOCK):
        xb = x[:, blk * SEL_BLOCK:(blk + 1) * SEL_BLOCK]
        outs.append(_dot(xb.astype(BF16), upper) + off)
        off = off + jnp.sum(xb, axis=-1, keepdims=True)
    return outs[0] if len(outs) == 1 else jnp.concatenate(outs, axis=1)


def _route_kernel(*refs, cap, aliased):
    if aliased:
        h_ref, w_ref, _, _, x_ref, g_ref, slot_t_ref, slot_ref, aff_ref = refs
    else:
        h_ref, w_ref, x_ref, g_ref, slot_t_ref, slot_ref, aff_ref = refs
    n_tok = h_ref.shape[0]
    e = pl.program_id(1)

    @pl.when(e == 0)
    def _():
        logits = _dot_nt(w_ref[...], h_ref[...])
        ex = jnp.exp(logits - jnp.max(logits, axis=0, keepdims=True))
        aff = ex / jnp.sum(ex, axis=0, keepdims=True)
        aff_ref[...] = aff
        bits = pltpu.bitcast(aff, jnp.int32)

        def bisect(it, thr):
            cand = thr | jnp.left_shift(jnp.int32(1), 30 - it)
            cnt = jnp.sum(jnp.where(bits >= cand, 1.0, 0.0), axis=-1, keepdims=True)
            return jnp.where(cnt >= cap, cand, thr)

        thr = lax.fori_loop(0, 31, bisect, jnp.zeros((N_EXPERTS, 1), jnp.int32))
        above = jnp.where(bits > thr, 1.0, 0.0)
        tied = jnp.where(bits == thr, 1.0, 0.0)
        room = cap - jnp.sum(above, axis=-1, keepdims=True)
        sel = above + tied * jnp.where(_prefix_count(tied) < room, 1.0, 0.0)
        slot = jnp.where(sel > 0.0, _prefix_count(sel), -1.0)
        slot_ref[...] = slot
        i = lax.broadcasted_iota(jnp.int32, (SEL_BLOCK, SEL_BLOCK), 0)
        j = lax.broadcasted_iota(jnp.int32, (SEL_BLOCK, SEL_BLOCK), 1)
        eye = jnp.where(i == j, 1.0, 0.0).astype(BF16)
        for blk in range(n_tok // SEL_BLOCK):
            cols = slice(blk * SEL_BLOCK, (blk + 1) * SEL_BLOCK)
            slot_t_ref[0, cols, :] = _dot_nt(eye, slot[:, cols].astype(BF16))

    row = pl.ds(e, 1)
    c = lax.broadcasted_iota(jnp.int32, (cap, n_tok), 0).astype(F32)
    hit = c == slot_ref[row, :]
    x_ref[0] = _dot(jnp.where(hit, 1.0, 0.0).astype(BF16), h_ref[...]).astype(BF16)
    g_ref[0] = jnp.sum(jnp.where(hit, aff_ref[row, :], 0.0), axis=-1, keepdims=True)


def _route_gather(h, w_router_t, buffers, *, n_batch, n_tok, rows_total, row_block_off):
    cap = CAPACITY_FACTOR * n_tok // N_EXPERTS
    aliased = buffers is not None
    out_block = lambda w: pl.BlockSpec((1, cap, w), lambda b, e: (e, row_block_off + b, 0))
    in_specs = [pl.BlockSpec((n_tok, D_MODEL), lambda b, e: (b, 0)),
                pl.BlockSpec((N_EXPERTS, D_MODEL), lambda b, e: (0, 0))]
    args = [h, w_router_t]
    if aliased:
        in_specs += [pl.BlockSpec(memory_space=pl.ANY), pl.BlockSpec(memory_space=pl.ANY)]
        args += list(buffers)
    return pl.pallas_call(
        functools.partial(_route_kernel, cap=cap, aliased=aliased),
        out_shape=(jax.ShapeDtypeStruct((N_EXPERTS, rows_total, D_MODEL), BF16),
                   jax.ShapeDtypeStruct((N_EXPERTS, rows_total, 1), F32),
                   jax.ShapeDtypeStruct((n_batch, n_tok, N_EXPERTS), F32)),
        grid=(n_batch, N_EXPERTS),
        in_specs=in_specs,
        out_specs=(out_block(D_MODEL), out_block(1),
                   pl.BlockSpec((1, n_tok, N_EXPERTS), lambda b, e: (b, 0, 0))),
        scratch_shapes=[pltpu.VMEM((N_EXPERTS, n_tok), F32), pltpu.VMEM((N_EXPERTS, n_tok), F32)],
        input_output_aliases={2: 0, 3: 1} if aliased else {},
        compiler_params=_cparams("parallel", "arbitrary"),
        name="route_gather",
    )(*args)


def _moe_kernel(x_ref, g_ref, wg_ref, wu_ref, wd_ref, y_ref, acc_ref, wgb_ref, wub_ref, wdb_ref, *, rows):
    f = pl.program_id(2)
    wgb_ref[...] = wg_ref[0].astype(BF16)
    wub_ref[...] = wu_ref[0].astype(BF16)
    wdb_ref[...] = wd_ref[0].astype(BF16)
    for r in range(x_ref.shape[1] // rows):
        sl = slice(r * rows, (r + 1) * rows)
        x = x_ref[0, sl, :]
        hid = (_silu(_dot(x, wgb_ref[...])) * _dot(x, wub_ref[...])).astype(BF16)
        y = _dot(hid, wdb_ref[...])

        @pl.when(f == 0)
        def _():
            acc_ref[sl, :] = y

        @pl.when(f > 0)
        def _():
            acc_ref[sl, :] += y

    @pl.when(f == pl.num_programs(2) - 1)
    def _():
        y_ref[0] = (acc_ref[...] * g_ref[0]).astype(BF16)


def _experts(x, gate, w_gate, w_up, w_down):
    n_rows = x.shape[1]
    tr, tf = n_rows // 2, 256
    return pl.pallas_call(
        functools.partial(_moe_kernel, rows=256),
        out_shape=jax.ShapeDtypeStruct((N_EXPERTS, n_rows, D_MODEL), BF16),
        grid=(N_EXPERTS, n_rows // tr, EXPERT_FF // tf),
        in_specs=[pl.BlockSpec((1, tr, D_MODEL), lambda e, r, f: (e, r, 0)),
                  pl.BlockSpec((1, tr, 1), lambda e, r, f: (e, r, 0)),
                  pl.BlockSpec((1, D_MODEL, tf), lambda e, r, f: (e, 0, f)),
                  pl.BlockSpec((1, D_MODEL, tf), lambda e, r, f: (e, 0, f)),
                  pl.BlockSpec((1, tf, D_MODEL), lambda e, r, f: (e, f, 0))],
        out_specs=pl.BlockSpec((1, tr, D_MODEL), lambda e, r, f: (e, r, 0)),
        scratch_shapes=[pltpu.VMEM((tr, D_MODEL), F32),
                        pltpu.VMEM((D_MODEL, tf), BF16), pltpu.VMEM((D_MODEL, tf), BF16),
                        pltpu.VMEM((tf, D_MODEL), BF16)],
        compiler_params=_cparams("parallel", "parallel", "arbitrary"),
        name="expert_swiglu",
    )(x, gate, w_gate, w_up, w_down)


def _combine_kernel(y_ref, slot_ref, x_ref, mod_ref, g_ref, o_ref):
    tt = x_ref.shape[0]
    cap = y_ref.shape[1]
    c = lax.broadcasted_iota(jnp.int32, (tt, cap), 1).astype(F32)
    slots = slot_ref[0]
    acc = jnp.zeros((tt, D_MODEL), F32)
    for e in range(N_EXPERTS):
        hit = c == slots[:, e:e + 1]
        acc = acc + _dot(jnp.where(hit, 1.0, 0.0).astype(BF16), y_ref[e])
    x = x_ref[...] + mod_ref[0, 5:6, :] * acc
    var = jnp.mean(x * x, axis=-1, keepdims=True)
    o_ref[...] = x * lax.rsqrt(var + NORM_EPS) * g_ref[...]


def _combine(y, slot_t, x_mid, mod, final_g, *, n_batch, n_tok, row_block_off, mod_base, mod_per_batch):
    cap = CAPACITY_FACTOR * n_tok // N_EXPERTS
    tt = min(n_tok, 256)
    nt = n_tok // tt
    return pl.pallas_call(
        _combine_kernel,
        out_shape=jax.ShapeDtypeStruct((n_batch * n_tok, D_MODEL), F32),
        grid=(n_batch, nt),
        in_specs=[pl.BlockSpec((N_EXPERTS, cap, D_MODEL), lambda b, t: (0, row_block_off + b, 0)),
                  pl.BlockSpec((1, tt, N_EXPERTS), lambda b, t: (b, t, 0)),
                  pl.BlockSpec((tt, D_MODEL), lambda b, t: (b * nt + t, 0)),
                  pl.BlockSpec((1, N_MOD, D_MODEL), lambda b, t: (mod_base + b * mod_per_batch, 0, 0)),
                  pl.BlockSpec((1, D_MODEL), lambda b, t: (0, 0))],
        out_specs=pl.BlockSpec((tt, D_MODEL), lambda b, t: (b * nt + t, 0)),
        compiler_params=_cparams("parallel", "arbitrary"),
        name="combine_final_norm",
    )(y, slot_t, x_mid, mod, final_g)


def kernel(x_prompt, x_sample, cache_k, cache_v, state_hgrn, c, c_ctx, w_ada, b_ada, norm1_g, w_in, hg_lb,
           hg_norm_g, attn_sink, w_out, norm2_g, w_router, w_gate, w_up, w_down, final_norm_g):
    n_p, t_p, _ = x_prompt.shape
    n_s, t_s, _ = x_sample.shape
    assert w_ada.shape[0] == 1 and 1 + n_s <= COND_ROWS
    layer = 0

    cond = jnp.zeros((COND_ROWS, D_MODEL), F32).at[0].set(c_ctx).at[1:1 + n_s].set(c)
    mod = _ada_modulation(cond, w_ada[layer], b_ada[layer]).reshape(COND_ROWS, N_MOD, D_MODEL)
    lb = jnp.cumsum(jax.nn.softmax(hg_lb.astype(F32), axis=0), axis=0)[layer]
    w_in_b = w_in[layer].astype(BF16)
    w_out_b = w_out[layer].astype(BF16)
    w_router_t = w_router[layer].T.astype(BF16)
    norm1 = norm1_g[layer].reshape(1, D_MODEL)
    norm2 = norm2_g[layer].reshape(1, D_MODEL)
    hg_gain = hg_norm_g[layer].reshape(1, HG_WIDTH)
    final_g = final_norm_g.reshape(1, D_MODEL)
    sink = attn_sink[layer]
    cos, sin_signed = _rope_tables(t_s)

    xp = x_prompt.reshape(n_p * t_p, D_MODEL)
    xs = x_sample.reshape(n_s * t_s, D_MODEL)
    groups = dict(p=dict(mod_base=0, rows_per_mod=n_p * t_p), s=dict(mod_base=1, rows_per_mod=t_s))

    proj_p = _in_projection(xp, mod, norm1, w_in_b, **groups["p"])
    proj_s = _in_projection(xs, mod, norm1, w_in_b, **groups["s"])

    zero_state = jnp.zeros((n_p, 1, 2, HG_HEADS, HEAD_DIM, HEAD_DIM), F32)
    ohg_p, new_state = _hgrn(proj_p, lb, hg_gain, zero_state, n_batch=n_p, n_tok=t_p)
    ohg_s, _ = _hgrn(proj_s, lb, hg_gain, state_hgrn[:, layer:layer + 1].astype(F32), n_batch=n_s, n_tok=t_s)

    oatt_p = _context_attention(proj_p, sink, n_batch=n_p, n_tok=t_p)
    n_ctx = cache_k.shape[2]
    oatt_s = _window_attention(proj_s, cache_k[:, layer].reshape(n_s, n_ctx, KV_WIDTH),
                               cache_v[:, layer].reshape(n_s, n_ctx, KV_WIDTH), sink, cos, sin_signed,
                               n_batch=n_s, n_tok=t_s)

    xmid_p, h2_p = _out_projection(ohg_p, oatt_p, w_out_b, xp, mod, norm2, **groups["p"])
    xmid_s, h2_s = _out_projection(ohg_s, oatt_s, w_out_b, xs, mod, norm2, **groups["s"])

    cap_p = CAPACITY_FACTOR * t_p // N_EXPERTS
    cap_s = CAPACITY_FACTOR * t_s // N_EXPERTS
    rows_total = n_p * cap_p + n_s * cap_s
    assert (n_p * cap_p) % cap_s == 0
    off_s = n_p * cap_p // cap_s
    xg, gate, slot_p = _route_gather(h2_p, w_router_t, None, n_batch=n_p, n_tok=t_p,
                                     rows_total=rows_total, row_block_off=0)
    xg, gate, slot_s = _route_gather(h2_s, w_router_t, (xg, gate), n_batch=n_s, n_tok=t_s,
                                     rows_total=rows_total, row_block_off=off_s)

    y = _experts(xg, gate, w_gate[layer], w_up[layer], w_down[layer])

    y_prompt = _combine(y, slot_p, xmid_p, mod, final_g, n_batch=n_p, n_tok=t_p, row_block_off=0,
                        mod_base=0, mod_per_batch=0)
    y_sample = _combine(y, slot_s, xmid_s, mod, final_g, n_batch=n_s, n_tok=t_s, row_block_off=off_s,
                        mod_base=1, mod_per_batch=1)

    k_col = 5 * HG_WIDTH + ATT_WIDTH
    new_k = proj_p[:, k_col:k_col + KV_WIDTH].reshape(n_p, 1, t_p, ATT_KV_HEADS, HEAD_DIM)
    new_v = proj_p[:, k_col + KV_WIDTH:k_col + 2 * KV_WIDTH].reshape(n_p, 1, t_p, ATT_KV_HEADS, HEAD_DIM)
    return (y_prompt.reshape(n_p, t_p, D_MODEL), y_sample.reshape(n_s, t_s, D_MODEL), new_k, new_v, new_state)
```

```python
import functools

import jax
import jax.numpy as jnp
from jax import lax
from jax.experimental import pallas as pl
from jax.experimental.pallas import tpu as pltpu

F32 = jnp.float32
BF16 = jnp.bfloat16

D_MODEL = 2048
HG_WIDTH = 1024
HG_HEADS = 8
HEAD_DIM = 128
ATT_HEADS = 8
ATT_KV_HEADS = 2
ATT_GROUP = ATT_HEADS // ATT_KV_HEADS
KV_WIDTH = ATT_KV_HEADS * HEAD_DIM
ATT_WIDTH = ATT_HEADS * HEAD_DIM
ATT_BLOCK = 128
GRID_W = 64
ROPE_BASE = 10000.0
ROPE_FREQS = HEAD_DIM // 4
N_EXPERTS = 16
CAPACITY_FACTOR = 2
EXPERT_FF = 5632
NORM_EPS = 1e-6
IN_WIDTH = 5 * HG_WIDTH + ATT_WIDTH + 2 * KV_WIDTH
N_MOD = 6
COND_ROWS = 16

HG_CHUNK = 128
HG_DIAG = 8
SEL_BLOCK = 256

VMEM_LIMIT = 56 * 1024 * 1024


def _cparams(*sem):
    return pltpu.CompilerParams(dimension_semantics=sem, vmem_limit_bytes=VMEM_LIMIT)


def _sigmoid(x):
    return 1.0 / (1.0 + jnp.exp(-x))


def _silu(x):
    return x * _sigmoid(x)


def _dot(a, b):
    return jnp.dot(a, b, preferred_element_type=F32)


def _dot_nt(a, b):
    return lax.dot_general(a, b, (((1,), (1,)), ((), ())), preferred_element_type=F32)


def _dot_tn(a, b):
    return lax.dot_general(a, b, (((0,), (0,)), ((), ())), preferred_element_type=F32)


def _ada_kernel(c_ref, w_ref, b_ref, o_ref):
    s = _silu(c_ref[...]).astype(BF16)
    o_ref[...] = _dot(s, w_ref[...].astype(BF16)) + b_ref[...]


def _ada_modulation(cond, w_ada, b_ada):
    tn = 1024
    n = w_ada.shape[1]
    return pl.pallas_call(
        _ada_kernel,
        out_shape=jax.ShapeDtypeStruct((COND_ROWS, n), F32),
        grid=(n // tn,),
        in_specs=[pl.BlockSpec((COND_ROWS, D_MODEL), lambda j: (0, 0)),
                  pl.BlockSpec((D_MODEL, tn), lambda j: (0, j)),
                  pl.BlockSpec((1, tn), lambda j: (0, j))],
        out_specs=pl.BlockSpec((COND_ROWS, tn), lambda j: (0, j)),
        compiler_params=_cparams("arbitrary"),
        name="ada_modulation",
    )(cond, w_ada, b_ada.reshape(1, n))


def _norm_modulate(x, gain, shift, scale):
    var = jnp.mean(x * x, axis=-1, keepdims=True)
    return (x * lax.rsqrt(var + NORM_EPS) * gain) * (1.0 + scale) + shift


def _inproj_kernel(x_ref, mod_ref, g_ref, w_ref, o_ref, h_ref, *, rows):
    @pl.when(pl.program_id(1) == 0)
    def _():
        shift = mod_ref[0, 0:1, :]
        scale = mod_ref[0, 1:2, :]
        gain = g_ref[...]

        def body(r, carry):
            sl = pl.ds(pl.multiple_of(r * rows, rows), rows)
            h_ref[sl, :] = _norm_modulate(x_ref[sl, :], gain, shift, scale).astype(BF16)
            return carry

        lax.fori_loop(0, x_ref.shape[0] // rows, body, 0)

    o_ref[...] = _dot(h_ref[...], w_ref[...])


def _in_projection(x, mod, gain, w_bf16, *, mod_base, rows_per_mod):
    m = x.shape[0]
    tm, tn = 1024, 512
    return pl.pallas_call(
        functools.partial(_inproj_kernel, rows=128),
        out_shape=jax.ShapeDtypeStruct((m, IN_WIDTH), F32),
        grid=(m // tm, IN_WIDTH // tn),
        in_specs=[pl.BlockSpec((tm, D_MODEL), lambda i, j: (i, 0)),
                  pl.BlockSpec((1, N_MOD, D_MODEL), lambda i, j: (mod_base + (i * tm) // rows_per_mod, 0, 0)),
                  pl.BlockSpec((1, D_MODEL), lambda i, j: (0, 0)),
                  pl.BlockSpec((D_MODEL, tn), lambda i, j: (0, j))],
        out_specs=pl.BlockSpec((tm, tn), lambda i, j: (i, j)),
        scratch_shapes=[pltpu.VMEM((tm, D_MODEL), BF16)],
        compiler_params=_cparams("parallel", "arbitrary"),
        name="in_projection",
    )(x, mod, gain, w_bf16)


def _hgrn_codes(reverse):
    L = HG_CHUNK
    t = lax.broadcasted_iota(jnp.int32, (L, L), 0)
    s = lax.broadcasted_iota(jnp.int32, (L, L), 1)
    code = jnp.zeros((L, L), jnp.int32)
    h = L // 2
    while h >= HG_DIAG:
        same = (t & ~(2 * h - 1)) == (s & ~(2 * h - 1))
        t_hi = (t & h) != 0
        s_hi = (s & h) != 0
        pair = (s_hi & ~t_hi) if reverse else (t_hi & ~s_hi)
        code = jnp.where(same & pair, h * 16, code)
        h //= 2
    same8 = (t & ~(HG_DIAG - 1)) == (s & ~(HG_DIAG - 1))
    causal = (s >= t) if reverse else (s <= t)
    return jnp.where(same8 & causal, 1, code)


def _cumsum_rows(tri_bf16, g):
    g1 = g.astype(BF16)
    r1 = g - g1.astype(F32)
    g2 = r1.astype(BF16)
    g3 = (r1 - g2.astype(F32)).astype(BF16)
    return _dot(tri_bf16, g1) + _dot(tri_bf16, g2) + _dot(tri_bf16, g3)


def _hgrn_intra(q, k, b, v_bf, code, spread, reverse):
    L = HG_CHUNK
    q_bf = q.astype(BF16)
    k_bf = k.astype(BF16)
    att = jnp.zeros((L, L), F32)
    h = L // 2
    while h >= HG_DIAG:
        parts = []
        for p in range(L // (2 * h)):
            m = p * 2 * h + (h if reverse else h - 1)
            parts.append(jnp.broadcast_to(b[m:m + 1, :], (2 * h, HEAD_DIM)))
        ref = parts[0] if len(parts) == 1 else jnp.concatenate(parts, axis=0)
        e = jnp.exp2((-jnp.abs(b - ref)).astype(BF16))
        a = _dot_nt(q_bf * e, k_bf * e)
        att = jnp.where(code == h * 16, a, att)
        h //= 2

    nb = L // HG_DIAG
    k3 = k.reshape(nb, HG_DIAG, HEAD_DIM)
    b3 = b.reshape(nb, HG_DIAG, HEAD_DIM)
    xs = []
    for j in range(HG_DIAG):
        kj = jnp.broadcast_to(k3[:, j:j + 1, :], (nb, HG_DIAG, HEAD_DIM)).reshape(L, HEAD_DIM)
        bj = jnp.broadcast_to(b3[:, j:j + 1, :], (nb, HG_DIAG, HEAD_DIM)).reshape(L, HEAD_DIM)
        xs.append(q_bf * kj.astype(BF16) * jnp.exp2(jnp.minimum(b - bj, 0.0).astype(BF16)))
    att = jnp.where(code == 1, _dot(jnp.concatenate(xs, axis=1), spread), att)
    return _dot(att.astype(BF16), v_bf)


def _hgrn_chunk(q, k, g, v, st_ref, code, tri, spread, reverse):
    L = HG_CHUNK
    b = _cumsum_rows(tri, g)
    b_tot = b[0:1, :] if reverse else b[L - 1:L, :]
    v_bf = v.astype(BF16)
    o_intra = _hgrn_intra(q, k, b, v_bf, code, spread, reverse)
    st = st_ref[...]
    o = _dot_nt((q * jnp.exp2(b)).astype(BF16), st.astype(BF16))
    k_out = (k * jnp.exp2(b_tot - b)).astype(BF16)
    st_ref[...] = st * jnp.exp2(b_tot) + _dot_tn(v_bf, k_out)
    return o + o_intra


def _hgrn_kernel(q_ref, ff_ref, fb_ref, v_ref, gate_ref, lb_ref, ng_ref, s0_ref,
                 o_ref, sout_ref, of_ref, ob_ref, stf_ref, stb_ref, *, n_tok):
    L = HG_CHUNK
    nc = n_tok // L
    stf_ref[...] = s0_ref[0, 0, 0, 0].T
    stb_ref[...] = s0_ref[0, 0, 1, 0].T
    lb_f = lb_ref[0:1, :]
    lb_b = lb_ref[1:2, :]
    code_f = _hgrn_codes(False)
    code_b = _hgrn_codes(True)
    ti = lax.broadcasted_iota(jnp.int32, (L, L), 0)
    si = lax.broadcasted_iota(jnp.int32, (L, L), 1)
    tri_f = jnp.where(si <= ti, 1.0, 0.0).astype(BF16)
    tri_b = jnp.where(si >= ti, 1.0, 0.0).astype(BF16)
    kd = lax.broadcasted_iota(jnp.int32, (HG_DIAG * HEAD_DIM, L), 0)
    sd = lax.broadcasted_iota(jnp.int32, (HG_DIAG * HEAD_DIM, L), 1)
    spread = jnp.where((kd >> 7) == (sd & (HG_DIAG - 1)), 1.0, 0.0).astype(BF16)

    def one(row, f_ref, lb, st_ref, code, tri, reverse, out_ref):
        sl = pl.ds(row, L)
        q = _silu(q_ref[sl, :])
        f = lb + (1.0 - lb) * _sigmoid(f_ref[sl, :])
        out_ref[sl, :] = _hgrn_chunk(q, 1.0 - f, jnp.log2(f), v_ref[sl, :], st_ref, code, tri, spread, reverse)

    def body(c, carry):
        for u in range(unroll):
            cu = c * unroll + u
            one(pl.multiple_of(cu * L, L), ff_ref, lb_f, stf_ref, code_f, tri_f, False, of_ref)
            one(pl.multiple_of((nc - 1 - cu) * L, L), fb_ref, lb_b, stb_ref, code_b, tri_b, True, ob_ref)
        return carry

    unroll = 2
    assert nc % unroll == 0
    lax.fori_loop(0, nc // unroll, body, 0)

    sout_ref[0, 0, 0, 0] = stf_ref[...].T
    sout_ref[0, 0, 1, 0] = stb_ref[...].T
    gain = ng_ref[...]

    def fin(c, carry):
        sl = pl.ds(pl.multiple_of(c * L, L), L)
        o = of_ref[sl, :] + ob_ref[sl, :]
        var = jnp.mean(o * o, axis=-1, keepdims=True)
        o_ref[sl, :] = ((o * lax.rsqrt(var + NORM_EPS) * gain) * _silu(gate_ref[sl, :])).astype(BF16)
        return carry

    lax.fori_loop(0, nc, fin, 0)


def _hgrn(proj, lb, norm_g, state, *, n_batch, n_tok):
    col = lambda k: (lambda b, h: (b, k * HG_HEADS + h))
    tok_spec = lambda k: pl.BlockSpec((n_tok, HEAD_DIM), col(k))
    st_spec = pl.BlockSpec((1, 1, 2, 1, HEAD_DIM, HEAD_DIM), lambda b, h: (b, 0, 0, h, 0, 0))
    return pl.pallas_call(
        functools.partial(_hgrn_kernel, n_tok=n_tok),
        out_shape=(jax.ShapeDtypeStruct((n_batch * n_tok, HG_WIDTH), BF16),
                   jax.ShapeDtypeStruct((n_batch, 1, 2, HG_HEADS, HEAD_DIM, HEAD_DIM), F32)),
        grid=(n_batch, HG_HEADS),
        in_specs=[tok_spec(0), tok_spec(1), tok_spec(2), tok_spec(3), tok_spec(4),
                  pl.BlockSpec((2, HEAD_DIM), lambda b, h: (0, h)),
                  pl.BlockSpec((1, HEAD_DIM), lambda b, h: (0, h)),
                  st_spec],
        out_specs=(pl.BlockSpec((n_tok, HEAD_DIM), lambda b, h: (b, h)), st_spec),
        scratch_shapes=[pltpu.VMEM((n_tok, HEAD_DIM), F32), pltpu.VMEM((n_tok, HEAD_DIM), F32),
                        pltpu.VMEM((HEAD_DIM, HEAD_DIM), F32), pltpu.VMEM((HEAD_DIM, HEAD_DIM), F32)],
        compiler_params=_cparams("parallel", "parallel"),
        name="hgrn2_scan",
    )(proj, proj, proj, proj, proj, lb, norm_g, state)


def _stack_heads(x, kvh):
    return jnp.concatenate(
        [x[:, (kvh * ATT_GROUP + g) * HEAD_DIM:(kvh * ATT_GROUP + g + 1) * HEAD_DIM] for g in range(ATT_GROUP)],
        axis=0)


def _sink_column(sink_ref, kvh, rows):
    return jnp.concatenate(
        [jnp.full((rows, 1), sink_ref[kvh * ATT_GROUP + g], F32) for g in range(ATT_GROUP)], axis=0)


def _softmax_av(scores, values, sink_col):
    m = sink_col
    for s in scores:
        m = jnp.maximum(m, jnp.max(s, axis=-1, keepdims=True))
    denom = jnp.exp(sink_col - m)
    o = None
    for s, v in zip(scores, values):
        p = jnp.exp(s - m)
        denom = denom + jnp.sum(p, axis=-1, keepdims=True)
        pv = _dot(p.astype(BF16), v)
        o = pv if o is None else o + pv
    return o / denom


def _ctx_attn_kernel(sink_ref, q_ref, k_ref, v_ref, o_ref):
    rows = q_ref.shape[0]
    scale = HEAD_DIM ** -0.5
    q_all = q_ref[...]
    for kvh in range(ATT_KV_HEADS):
        q = _stack_heads(q_all, kvh).astype(BF16)
        k = k_ref[:, kvh * HEAD_DIM:(kvh + 1) * HEAD_DIM].astype(BF16)
        v = v_ref[:, kvh * HEAD_DIM:(kvh + 1) * HEAD_DIM].astype(BF16)
        o = _softmax_av([_dot_nt(q, k) * scale], [v], _sink_column(sink_ref, kvh, rows))
        for g in range(ATT_GROUP):
            hd = kvh * ATT_GROUP + g
            o_ref[:, hd * HEAD_DIM:(hd + 1) * HEAD_DIM] = o[g * rows:(g + 1) * rows, :].astype(BF16)


def _context_attention(proj, sink, *, n_batch, n_tok):
    q_col = 5 * HG_WIDTH // ATT_WIDTH
    k_col = (5 * HG_WIDTH + ATT_WIDTH) // KV_WIDTH
    return pl.pallas_call(
        _ctx_attn_kernel,
        out_shape=jax.ShapeDtypeStruct((n_batch * n_tok, ATT_WIDTH), BF16),
        grid_spec=pltpu.PrefetchScalarGridSpec(
            num_scalar_prefetch=1,
            grid=(n_batch,),
            in_specs=[pl.BlockSpec((n_tok, ATT_WIDTH), lambda b, s: (b, q_col)),
                      pl.BlockSpec((n_tok, KV_WIDTH), lambda b, s: (b, k_col)),
                      pl.BlockSpec((n_tok, KV_WIDTH), lambda b, s: (b, k_col + 1))],
            out_specs=pl.BlockSpec((n_tok, ATT_WIDTH), lambda b, s: (b, 0))),
        compiler_params=_cparams("parallel"),
        name="context_attention",
    )(sink, proj, proj, proj)


def _rope(x, cos, sin_signed, even_group):
    partner = jnp.where(even_group, pltpu.roll(x, HEAD_DIM - ROPE_FREQS, 1), pltpu.roll(x, ROPE_FREQS, 1))
    return x * cos + partner * sin_signed


def _win_attn_kernel(sink_ref, q_ref, k_ref, v_ref, ck_ref, cv_ref, cos_ref, sin_ref, o_ref,
                     kpad_ref, vpad_ref, *, n_tok):
    blk = ATT_BLOCK
    nb = n_tok // blk
    i = pl.program_id(1)
    scale = HEAD_DIM ** -0.5
    lane = lax.broadcasted_iota(jnp.int32, (blk, HEAD_DIM), 1)
    even_group = (lane & ROPE_FREQS) == 0

    @pl.when(i == 0)
    def _():
        zeros = jnp.zeros((blk, KV_WIDTH), BF16)
        kpad_ref[0:blk, :] = zeros
        vpad_ref[0:blk, :] = zeros
        kpad_ref[blk + n_tok:2 * blk + n_tok, :] = zeros
        vpad_ref[blk + n_tok:2 * blk + n_tok, :] = zeros

        def body(r, carry):
            src = pl.ds(pl.multiple_of(r * blk, blk), blk)
            dst = pl.ds(pl.multiple_of((r + 1) * blk, blk), blk)
            cos = cos_ref[src, :]
            sin = sin_ref[src, :]
            for kvh in range(ATT_KV_HEADS):
                cols = slice(kvh * HEAD_DIM, (kvh + 1) * HEAD_DIM)
                kpad_ref[dst, cols] = _rope(k_ref[src, cols], cos, sin, even_group).astype(BF16)
            vpad_ref[dst, :] = v_ref[src, :].astype(BF16)
            return carry

        lax.fori_loop(0, nb, body, 0)

    rows = pl.ds(pl.multiple_of(i * blk, blk), blk)
    cos = cos_ref[rows, :]
    sin = sin_ref[rows, :]
    band = pl.ds(pl.multiple_of(i * blk, blk), 3 * blk)
    r = lax.broadcasted_iota(jnp.int32, (blk, 3 * blk), 0)
    j = lax.broadcasted_iota(jnp.int32, (blk, 3 * blk), 1)
    kpos = j + (i - 1) * blk
    valid = (j >= r) & (j <= r + 2 * blk) & (kpos >= 0) & (kpos < n_tok)
    valid = jnp.concatenate([valid] * ATT_GROUP, axis=0)
    q_all = q_ref[...]
    for kvh in range(ATT_KV_HEADS):
        cols = slice(kvh * HEAD_DIM, (kvh + 1) * HEAD_DIM)
        q = jnp.concatenate(
            [_rope(q_all[:, (kvh * ATT_GROUP + g) * HEAD_DIM:(kvh * ATT_GROUP + g + 1) * HEAD_DIM],
                   cos, sin, even_group) for g in range(ATT_GROUP)], axis=0).astype(BF16)
        s_ctx = _dot_nt(q, ck_ref[0, :, cols].astype(BF16)) * scale
        s_loc = jnp.where(valid, _dot_nt(q, kpad_ref[band, cols]) * scale, -jnp.inf)
        o = _softmax_av([s_ctx, s_loc], [cv_ref[0, :, cols].astype(BF16), vpad_ref[band, cols]],
                        _sink_column(sink_ref, kvh, blk))
        for g in range(ATT_GROUP):
            hd = kvh * ATT_GROUP + g
            o_ref[:, hd * HEAD_DIM:(hd + 1) * HEAD_DIM] = o[g * blk:(g + 1) * blk, :].astype(BF16)


def _window_attention(proj, cache_k, cache_v, sink, cos, sin_signed, *, n_batch, n_tok):
    nb = n_tok // ATT_BLOCK
    n_ctx = cache_k.shape[1]
    q_col = 5 * HG_WIDTH // ATT_WIDTH
    k_col = (5 * HG_WIDTH + ATT_WIDTH) // KV_WIDTH
    table = pl.BlockSpec((n_tok, HEAD_DIM), lambda b, i, s: (0, 0))
    cache = pl.BlockSpec((1, n_ctx, KV_WIDTH), lambda b, i, s: (b, 0, 0))
    return pl.pallas_call(
        functools.partial(_win_attn_kernel, n_tok=n_tok),
        out_shape=jax.ShapeDtypeStruct((n_batch * n_tok, ATT_WIDTH), BF16),
        grid_spec=pltpu.PrefetchScalarGridSpec(
            num_scalar_prefetch=1,
            grid=(n_batch, nb),
            in_specs=[pl.BlockSpec((ATT_BLOCK, ATT_WIDTH), lambda b, i, s: (b * nb + i, q_col)),
                      pl.BlockSpec((n_tok, KV_WIDTH), lambda b, i, s: (b, k_col)),
                      pl.BlockSpec((n_tok, KV_WIDTH), lambda b, i, s: (b, k_col + 1)),
                      cache, cache, table, table],
            out_specs=pl.BlockSpec((ATT_BLOCK, ATT_WIDTH), lambda b, i, s: (b * nb + i, 0)),
            scratch_shapes=[pltpu.VMEM((n_tok + 2 * ATT_BLOCK, KV_WIDTH), BF16),
                            pltpu.VMEM((n_tok + 2 * ATT_BLOCK, KV_WIDTH), BF16)]),
        compiler_params=_cparams("parallel", "arbitrary"),
        name="window_attention",
    )(sink, proj, proj, proj, cache_k, cache_v, cos, sin_signed)


def _rope_tables(n_tok):
    rows = n_tok // GRID_W
    row = jnp.repeat(jnp.arange(rows), GRID_W).astype(F32)
    col = jnp.tile(jnp.arange(GRID_W), rows).astype(F32)
    inv = ROPE_BASE ** (-jnp.arange(ROPE_FREQS, dtype=F32) / ROPE_FREQS)
    ar, ac = row[:, None] * inv, col[:, None] * inv
    cr, sr, cc, sc = jnp.cos(ar), jnp.sin(ar), jnp.cos(ac), jnp.sin(ac)
    return (jnp.concatenate([cr, cr, cc, cc], axis=1), jnp.concatenate([-sr, sr, -sc, sc], axis=1))


def _outproj_kernel(hg_ref, att_ref, w_ref, x_ref, mod_ref, g_ref, xo_ref, h_ref):
    mix = _dot(hg_ref[...], w_ref[0:HG_WIDTH, :]) + _dot(att_ref[...], w_ref[HG_WIDTH:, :])
    x = x_ref[...] + mod_ref[0, 2:3, :] * mix
    xo_ref[...] = x
    h_ref[...] = _norm_modulate(x, g_ref[...], mod_ref[0, 3:4, :], mod_ref[0, 4:5, :]).astype(BF16)


def _out_projection(o_hg, o_att, w_bf16, x, mod, gain, *, mod_base, rows_per_mod):
    m = x.shape[0]
    tm = 256
    row = lambda i: (i, 0)
    return pl.pallas_call(
        _outproj_kernel,
        out_shape=(jax.ShapeDtypeStruct((m, D_MODEL), F32), jax.ShapeDtypeStruct((m, D_MODEL), BF16)),
        grid=(m // tm,),
        in_specs=[pl.BlockSpec((tm, HG_WIDTH), row),
                  pl.BlockSpec((tm, ATT_WIDTH), row),
                  pl.BlockSpec((HG_WIDTH + ATT_WIDTH, D_MODEL), lambda i: (0, 0)),
                  pl.BlockSpec((tm, D_MODEL), row),
                  pl.BlockSpec((1, N_MOD, D_MODEL), lambda i: (mod_base + (i * tm) // rows_per_mod, 0, 0)),
                  pl.BlockSpec((1, D_MODEL), lambda i: (0, 0))],
        out_specs=(pl.BlockSpec((tm, D_MODEL), row), pl.BlockSpec((tm, D_MODEL), row)),
        compiler_params=_cparams("parallel"),
        name="out_projection",
    )(o_hg, o_att, w_bf16, x, mod, gain)


def _prefix_count(x):
    n = x.shape[1]
    i = lax.broadcasted_iota(jnp.int32, (SEL_BLOCK, SEL_BLOCK), 0)
    j = lax.broadcasted_iota(jnp.int32, (SEL_BLOCK, SEL_BLOCK), 1)
    upper = jnp.where(i < j, 1.0, 0.0).astype(BF16)
    off = jnp.zeros((x.shape[0], 1), F32)
    outs = []
    for blk in range(n // SEL_BLOCK):
        xb = x[:, blk * SEL_BLOCK:(blk + 1) * SEL_BLOCK]
        outs.append(_dot(xb.astype(BF16), upper) + off)
        off = off + jnp.sum(xb, axis=-1, keepdims=True)
    return outs[0] if len(outs) == 1 else jnp.concatenate(outs, axis=1)


def _route_kernel(*refs, cap, aliased):
    if aliased:
        h_ref, w_ref, _, _, x_ref, g_ref, slot_t_ref, slot_ref, aff_ref = refs
    else:
        h_ref, w_ref, x_ref, g_ref, slot_t_ref, slot_ref, aff_ref = refs
    n_tok = h_ref.shape[0]
    e = pl.program_id(1)

    @pl.when(e == 0)
    def _():
        logits = _dot_nt(w_ref[...], h_ref[...])
        ex = jnp.exp(logits - jnp.max(logits, axis=0, keepdims=True))
        aff = ex / jnp.sum(ex, axis=0, keepdims=True)
        aff_ref[...] = aff
        bits = pltpu.bitcast(aff, jnp.int32)

        def bisect(it, thr):
            cand = thr | jnp.left_shift(jnp.int32(1), 30 - it)
            cnt = jnp.sum(jnp.where(bits >= cand, 1.0, 0.0), axis=-1, keepdims=True)
            return jnp.where(cnt >= cap, cand, thr)

        thr = lax.fori_loop(0, 31, bisect, jnp.zeros((N_EXPERTS, 1), jnp.int32))
        above = jnp.where(bits > thr, 1.0, 0.0)
        tied = jnp.where(bits == thr, 1.0, 0.0)
        room = cap - jnp.sum(above, axis=-1, keepdims=True)
        sel = above + tied * jnp.where(_prefix_count(tied) < room, 1.0, 0.0)
        slot = jnp.where(sel > 0.0, _prefix_count(sel), -1.0)
        slot_ref[...] = slot
        i = lax.broadcasted_iota(jnp.int32, (SEL_BLOCK, SEL_BLOCK), 0)
        j = lax.broadcasted_iota(jnp.int32, (SEL_BLOCK, SEL_BLOCK), 1)
        eye = jnp.where(i == j, 1.0, 0.0).astype(BF16)
        for blk in range(n_tok // SEL_BLOCK):
            cols = slice(blk * SEL_BLOCK, (blk + 1) * SEL_BLOCK)
            slot_t_ref[0, cols, :] = _dot_nt(eye, slot[:, cols].astype(BF16))

    row = pl.ds(e, 1)
    c = lax.broadcasted_iota(jnp.int32, (cap, n_tok), 0).astype(F32)
    hit = c == slot_ref[row, :]
    x_ref[0] = _dot(jnp.where(hit, 1.0, 0.0).astype(BF16), h_ref[...]).astype(BF16)
    g_ref[0] = jnp.sum(jnp.where(hit, aff_ref[row, :], 0.0), axis=-1, keepdims=True)


def _route_gather(h, w_router_t, buffers, *, n_batch, n_tok, rows_total, row_block_off):
    cap = CAPACITY_FACTOR * n_tok // N_EXPERTS
    aliased = buffers is not None
    out_block = lambda w: pl.BlockSpec((1, cap, w), lambda b, e: (e, row_block_off + b, 0))
    in_specs = [pl.BlockSpec((n_tok, D_MODEL), lambda b, e: (b, 0)),
                pl.BlockSpec((N_EXPERTS, D_MODEL), lambda b, e: (0, 0))]
    args = [h, w_router_t]
    if aliased:
        in_specs += [pl.BlockSpec(memory_space=pl.ANY), pl.BlockSpec(memory_space=pl.ANY)]
        args += list(buffers)
    return pl.pallas_call(
        functools.partial(_route_kernel, cap=cap, aliased=aliased),
        out_shape=(jax.ShapeDtypeStruct((N_EXPERTS, rows_total, D_MODEL), BF16),
                   jax.ShapeDtypeStruct((N_EXPERTS, rows_total, 1), F32),
                   jax.ShapeDtypeStruct((n_batch, n_tok, N_EXPERTS), F32)),
        grid=(n_batch, N_EXPERTS),
        in_specs=in_specs,
        out_specs=(out_block(D_MODEL), out_block(1),
                   pl.BlockSpec((1, n_tok, N_EXPERTS), lambda b, e: (b, 0, 0))),
        scratch_shapes=[pltpu.VMEM((N_EXPERTS, n_tok), F32), pltpu.VMEM((N_EXPERTS, n_tok), F32)],
        input_output_aliases={2: 0, 3: 1} if aliased else {},
        compiler_params=_cparams("parallel", "arbitrary"),
        name="route_gather",
    )(*args)


def _moe_kernel(x_ref, g_ref, wg_ref, wu_ref, wd_ref, y_ref, hid_ref, *, n_ff):
    s = pl.program_id(2)
    tf = wg_ref.shape[2]

    @pl.when(s < n_ff)
    def _():
        x = x_ref[0]
        hid = (_silu(_dot(x, wg_ref[0].astype(BF16))) * _dot(x, wu_ref[0].astype(BF16))).astype(BF16)
        for f in range(n_ff):
            @pl.when(s == f)
            def _():
                hid_ref[:, f * tf:(f + 1) * tf] = hid

    @pl.when(s >= n_ff)
    def _():
        y = _dot(hid_ref[...], wd_ref[0].astype(BF16))
        y_ref[0] = (y * g_ref[0]).astype(BF16)


def _experts(x, gate, w_gate, w_up, w_down):
    n_rows = x.shape[1]
    tr, tf, tn = n_rows // 2, 256, 256
    n_ff, n_out = EXPERT_FF // tf, D_MODEL // tn
    up_tile = lambda e, r, s: (e, 0, jnp.minimum(s, n_ff - 1))
    out_tile = lambda s: jnp.maximum(s - n_ff, 0)
    return pl.pallas_call(
        functools.partial(_moe_kernel, n_ff=n_ff),
        out_shape=jax.ShapeDtypeStruct((N_EXPERTS, n_rows, D_MODEL), BF16),
        grid=(N_EXPERTS, n_rows // tr, n_ff + n_out),
        in_specs=[pl.BlockSpec((1, tr, D_MODEL), lambda e, r, s: (e, r, 0), pipeline_mode=pl.Buffered(1)),
                  pl.BlockSpec((1, tr, 1), lambda e, r, s: (e, r, 0)),
                  pl.BlockSpec((1, D_MODEL, tf), up_tile),
                  pl.BlockSpec((1, D_MODEL, tf), up_tile),
                  pl.BlockSpec((1, EXPERT_FF, tn), lambda e, r, s: (e, 0, out_tile(s)))],
        out_specs=pl.BlockSpec((1, tr, tn), lambda e, r, s: (e, r, out_tile(s))),
        scratch_shapes=[pltpu.VMEM((tr, EXPERT_FF), BF16)],
        compiler_params=_cparams("parallel", "parallel", "arbitrary"),
        name="expert_swiglu",
    )(x, gate, w_gate, w_up, w_down)


def _combine_kernel(y_ref, slot_ref, x_ref, mod_ref, g_ref, o_ref):
    tt = x_ref.shape[0]
    cap = y_ref.shape[1]
    c = lax.broadcasted_iota(jnp.int32, (tt, cap), 1).astype(F32)
    slots = slot_ref[0]
    acc = jnp.zeros((tt, D_MODEL), F32)
    for e in range(N_EXPERTS):
        hit = c == slots[:, e:e + 1]
        acc = acc + _dot(jnp.where(hit, 1.0, 0.0).astype(BF16), y_ref[e])
    x = x_ref[...] + mod_ref[0, 5:6, :] * acc
    var = jnp.mean(x * x, axis=-1, keepdims=True)
    o_ref[...] = x * lax.rsqrt(var + NORM_EPS) * g_ref[...]


def _combine(y, slot_t, x_mid, mod, final_g, *, n_batch, n_tok, row_block_off, mod_base, mod_per_batch):
    cap = CAPACITY_FACTOR * n_tok // N_EXPERTS
    tt = min(n_tok, 256)
    nt = n_tok // tt
    return pl.pallas_call(
        _combine_kernel,
        out_shape=jax.ShapeDtypeStruct((n_batch * n_tok, D_MODEL), F32),
        grid=(n_batch, nt),
        in_specs=[pl.BlockSpec((N_EXPERTS, cap, D_MODEL), lambda b, t: (0, row_block_off + b, 0)),
                  pl.BlockSpec((1, tt, N_EXPERTS), lambda b, t: (b, t, 0)),
                  pl.BlockSpec((tt, D_MODEL), lambda b, t: (b * nt + t, 0)),
                  pl.BlockSpec((1, N_MOD, D_MODEL), lambda b, t: (mod_base + b * mod_per_batch, 0, 0)),
                  pl.BlockSpec((1, D_MODEL), lambda b, t: (0, 0))],
        out_specs=pl.BlockSpec((tt, D_MODEL), lambda b, t: (b * nt + t, 0)),
        compiler_params=_cparams("parallel", "arbitrary"),
        name="combine_final_norm",
    )(y, slot_t, x_mid, mod, final_g)


def kernel(x_prompt, x_sample, cache_k, cache_v, state_hgrn, c, c_ctx, w_ada, b_ada, norm1_g, w_in, hg_lb,
           hg_norm_g, attn_sink, w_out, norm2_g, w_router, w_gate, w_up, w_down, final_norm_g):
    n_p, t_p, _ = x_prompt.shape
    n_s, t_s, _ = x_sample.shape
    assert w_ada.shape[0] == 1 and 1 + n_s <= COND_ROWS
    layer = 0

    cond = jnp.zeros((COND_ROWS, D_MODEL), F32).at[0].set(c_ctx).at[1:1 + n_s].set(c)
    mod = _ada_modulation(cond, w_ada[layer], b_ada[layer]).reshape(COND_ROWS, N_MOD, D_MODEL)
    lb = jnp.cumsum(jax.nn.softmax(hg_lb.astype(F32), axis=0), axis=0)[layer]
    w_in_b = w_in[layer].astype(BF16)
    w_out_b = w_out[layer].astype(BF16)
    w_router_t = w_router[layer].T.astype(BF16)
    norm1 = norm1_g[layer].reshape(1, D_MODEL)
    norm2 = norm2_g[layer].reshape(1, D_MODEL)
    hg_gain = hg_norm_g[layer].reshape(1, HG_WIDTH)
    final_g = final_norm_g.reshape(1, D_MODEL)
    sink = attn_sink[layer]
    cos, sin_signed = _rope_tables(t_s)

    xp = x_prompt.reshape(n_p * t_p, D_MODEL)
    xs = x_sample.reshape(n_s * t_s, D_MODEL)
    groups = dict(p=dict(mod_base=0, rows_per_mod=n_p * t_p), s=dict(mod_base=1, rows_per_mod=t_s))

    proj_p = _in_projection(xp, mod, norm1, w_in_b, **groups["p"])
    proj_s = _in_projection(xs, mod, norm1, w_in_b, **groups["s"])

    zero_state = jnp.zeros((n_p, 1, 2, HG_HEADS, HEAD_DIM, HEAD_DIM), F32)
    ohg_p, new_state = _hgrn(proj_p, lb, hg_gain, zero_state, n_batch=n_p, n_tok=t_p)
    ohg_s, _ = _hgrn(proj_s, lb, hg_gain, state_hgrn[:, layer:layer + 1].astype(F32), n_batch=n_s, n_tok=t_s)

    oatt_p = _context_attention(proj_p, sink, n_batch=n_p, n_tok=t_p)
    n_ctx = cache_k.shape[2]
    oatt_s = _window_attention(proj_s, cache_k[:, layer].reshape(n_s, n_ctx, KV_WIDTH),
                               cache_v[:, layer].reshape(n_s, n_ctx, KV_WIDTH), sink, cos, sin_signed,
                               n_batch=n_s, n_tok=t_s)

    xmid_p, h2_p = _out_projection(ohg_p, oatt_p, w_out_b, xp, mod, norm2, **groups["p"])
    xmid_s, h2_s = _out_projection(ohg_s, oatt_s, w_out_b, xs, mod, norm2, **groups["s"])

    cap_p = CAPACITY_FACTOR * t_p // N_EXPERTS
    cap_s = CAPACITY_FACTOR * t_s // N_EXPERTS
    rows_total = n_p * cap_p + n_s * cap_s
    assert (n_p * cap_p) % cap_s == 0
    off_s = n_p * cap_p // cap_s
    xg, gate, slot_p = _route_gather(h2_p, w_router_t, None, n_batch=n_p, n_tok=t_p,
                                     rows_total=rows_total, row_block_off=0)
    xg, gate, slot_s = _route_gather(h2_s, w_router_t, (xg, gate), n_batch=n_s, n_tok=t_s,
                                     rows_total=rows_total, row_block_off=off_s)

    y = _experts(xg, gate, w_gate[layer], w_up[layer], w_down[layer])

    y_prompt = _combine(y, slot_p, xmid_p, mod, final_g, n_batch=n_p, n_tok=t_p, row_block_off=0,
                        mod_base=0, mod_per_batch=0)
    y_sample = _combine(y, slot_s, xmid_s, mod, final_g, n_batch=n_s, n_tok=t_s, row_block_off=off_s,
                        mod_base=1, mod_per_batch=1)

    k_col = 5 * HG_WIDTH + ATT_WIDTH
    new_k = proj_p[:, k_col:k_col + KV_WIDTH].reshape(n_p, 1, t_p, ATT_KV_HEADS, HEAD_DIM)
    new_v = proj_p[:, k_col + KV_WIDTH:k_col + 2 * KV_WIDTH].reshape(n_p, 1, t_p, ATT_KV_HEADS, HEAD_DIM)
    return (y_prompt.reshape(n_p, t_p, D_MODEL), y_sample.reshape(n_s, t_s, D_MODEL), new_k, new_v, new_state)
```

```python
import functools

import jax
import jax.numpy as jnp
from jax import lax
from jax.experimental import pallas as pl
from jax.experimental.pallas import tpu as pltpu

F32 = jnp.float32
BF16 = jnp.bfloat16

D_MODEL = 2048
HG_WIDTH = 1024
HG_HEADS = 8
HEAD_DIM = 128
ATT_HEADS = 8
ATT_KV_HEADS = 2
ATT_GROUP = ATT_HEADS // ATT_KV_HEADS
KV_WIDTH = ATT_KV_HEADS * HEAD_DIM
ATT_WIDTH = ATT_HEADS * HEAD_DIM
ATT_BLOCK = 128
GRID_W = 64
ROPE_BASE = 10000.0
ROPE_FREQS = HEAD_DIM // 4
N_EXPERTS = 16
CAPACITY_FACTOR = 2
EXPERT_FF = 5632
NORM_EPS = 1e-6
IN_WIDTH = 5 * HG_WIDTH + ATT_WIDTH + 2 * KV_WIDTH
N_MOD = 6
COND_ROWS = 16

HG_CHUNK = 128
HG_DIAG = 8
SEL_BLOCK = 256

VMEM_LIMIT = 56 * 1024 * 1024


def _cparams(*sem):
    return pltpu.CompilerParams(dimension_semantics=sem, vmem_limit_bytes=VMEM_LIMIT)


def _sigmoid(x):
    return 1.0 / (1.0 + jnp.exp(-x))


def _silu(x):
    return x * _sigmoid(x)


def _dot(a, b):
    return jnp.dot(a, b, preferred_element_type=F32)


def _dot_nt(a, b):
    return lax.dot_general(a, b, (((1,), (1,)), ((), ())), preferred_element_type=F32)


def _dot_tn(a, b):
    return lax.dot_general(a, b, (((0,), (0,)), ((), ())), preferred_element_type=F32)


def _ada_kernel(c_ref, w_ref, b_ref, o_ref):
    s = _silu(c_ref[...]).astype(BF16)
    o_ref[...] = _dot(s, w_ref[...].astype(BF16)) + b_ref[...]


def _ada_modulation(cond, w_ada, b_ada):
    tn = 1024
    n = w_ada.shape[1]
    return pl.pallas_call(
        _ada_kernel,
        out_shape=jax.ShapeDtypeStruct((COND_ROWS, n), F32),
        grid=(n // tn,),
        in_specs=[pl.BlockSpec((COND_ROWS, D_MODEL), lambda j: (0, 0)),
                  pl.BlockSpec((D_MODEL, tn), lambda j: (0, j)),
                  pl.BlockSpec((1, tn), lambda j: (0, j))],
        out_specs=pl.BlockSpec((COND_ROWS, tn), lambda j: (0, j)),
        compiler_params=_cparams("arbitrary"),
        name="ada_modulation",
    )(cond, w_ada, b_ada.reshape(1, n))


def _norm_modulate(x, gain, shift, scale):
    var = jnp.mean(x * x, axis=-1, keepdims=True)
    return (x * lax.rsqrt(var + NORM_EPS) * gain) * (1.0 + scale) + shift


def _inproj_kernel(x_ref, mod_ref, g_ref, w_ref, o_ref, h_ref, *, rows):
    @pl.when(pl.program_id(1) == 0)
    def _():
        shift = mod_ref[0, 0:1, :]
        scale = mod_ref[0, 1:2, :]
        gain = g_ref[...]

        def body(r, carry):
            sl = pl.ds(pl.multiple_of(r * rows, rows), rows)
            h_ref[sl, :] = _norm_modulate(x_ref[sl, :], gain, shift, scale).astype(BF16)
            return carry

        lax.fori_loop(0, x_ref.shape[0] // rows, body, 0)

    w = w_ref[...]
    chunk = 256
    for r in range(x_ref.shape[0] // chunk):
        sl = slice(r * chunk, (r + 1) * chunk)
        o_ref[sl, :] = _dot(h_ref[sl, :], w)


def _in_projection(x, mod, gain, w_bf16, *, mod_base, rows_per_mod):
    m = x.shape[0]
    tm, tn = 1024, 512
    return pl.pallas_call(
        functools.partial(_inproj_kernel, rows=128),
        out_shape=jax.ShapeDtypeStruct((m, IN_WIDTH), F32),
        grid=(m // tm, IN_WIDTH // tn),
        in_specs=[pl.BlockSpec((tm, D_MODEL), lambda i, j: (i, 0)),
                  pl.BlockSpec((1, N_MOD, D_MODEL), lambda i, j: (mod_base + (i * tm) // rows_per_mod, 0, 0)),
                  pl.BlockSpec((1, D_MODEL), lambda i, j: (0, 0)),
                  pl.BlockSpec((D_MODEL, tn), lambda i, j: (0, j))],
        out_specs=pl.BlockSpec((tm, tn), lambda i, j: (i, j)),
        scratch_shapes=[pltpu.VMEM((tm, D_MODEL), BF16)],
        compiler_params=_cparams("parallel", "arbitrary"),
        name="in_projection",
    )(x, mod, gain, w_bf16)


def _hgrn_codes(reverse):
    L = HG_CHUNK
    t = lax.broadcasted_iota(jnp.int32, (L, L), 0)
    s = lax.broadcasted_iota(jnp.int32, (L, L), 1)
    code = jnp.where(t == s, 1, 0)
    h = L // 2
    while h >= 1:
        same = (t & ~(2 * h - 1)) == (s & ~(2 * h - 1))
        t_hi = (t & h) != 0
        s_hi = (s & h) != 0
        pair = (s_hi & ~t_hi) if reverse else (t_hi & ~s_hi)
        code = jnp.where(same & pair, h * 16, code)
        h //= 2
    return code


def _cumsum_rows(tri_bf16, g):
    g1 = g.astype(BF16)
    r1 = g - g1.astype(F32)
    g2 = r1.astype(BF16)
    g3 = (r1 - g2.astype(F32)).astype(BF16)
    return _dot(tri_bf16, g1) + _dot(tri_bf16, g2) + _dot(tri_bf16, g3)


def _hgrn_intra(q, k, f, b, v_bf, code, reverse):
    L = HG_CHUNK
    G = HG_DIAG
    q_bf = q.astype(BF16)
    k_bf = k.astype(BF16)

    def level(h, ref, att):
        e = jnp.exp2((-jnp.abs(b - ref)).astype(BF16))
        return jnp.where(code == h * 16, _dot_nt(q_bf * e, k_bf * e), att)

    att = jnp.where(code == 1, _dot_nt(q_bf, k_bf), 0.0)
    h = L // 2
    while h >= G:
        parts = []
        for p in range(L // (2 * h)):
            m = p * 2 * h + (h if reverse else h - 1)
            parts.append(jnp.broadcast_to(b[m:m + 1, :], (2 * h, HEAD_DIM)))
        att = level(h, parts[0] if len(parts) == 1 else jnp.concatenate(parts, axis=0), att)
        h //= 2

    b3 = b.reshape(L // G, G, HEAD_DIM)

    def group_row(r):
        return jnp.broadcast_to(b3[:, r:r + 1, :], (L // G, G, HEAD_DIM)).reshape(L, HEAD_DIM)

    row = lax.broadcasted_iota(jnp.int32, (L, HEAD_DIM), 0)
    att = level(4, group_row(4 if reverse else 3), att)
    lo, hi = (2, 6) if reverse else (1, 5)
    att = level(2, jnp.where((row & 4) == 0, group_row(lo), group_row(hi)), att)
    att = jnp.where(code == 16, _dot_nt(q_bf * f.astype(BF16), k_bf), att)
    return _dot(att.astype(BF16), v_bf)


def _hgrn_chunk(q, f, v, st_ref, code, tri, reverse):
    L = HG_CHUNK
    k = 1.0 - f
    b = _cumsum_rows(tri, jnp.log2(f))
    b_tot = b[0:1, :] if reverse else b[L - 1:L, :]
    v_bf = v.astype(BF16)
    o_intra = _hgrn_intra(q, k, f, b, v_bf, code, reverse)
    st = st_ref[...]
    o = _dot_nt((q * jnp.exp2(b)).astype(BF16), st.astype(BF16))
    k_out = (k * jnp.exp2(b_tot - b)).astype(BF16)
    st_ref[...] = st * jnp.exp2(b_tot) + _dot_tn(v_bf, k_out)
    return o + o_intra


def _hgrn_kernel(q_ref, ff_ref, fb_ref, v_ref, gate_ref, lb_ref, ng_ref, s0_ref,
                 o_ref, sout_ref, of_ref, ob_ref, stf_ref, stb_ref, *, n_tok):
    L = HG_CHUNK
    nc = n_tok // L
    stf_ref[...] = s0_ref[0, 0, 0, 0].T
    stb_ref[...] = s0_ref[0, 0, 1, 0].T
    lb_f = lb_ref[0:1, :]
    lb_b = lb_ref[1:2, :]
    code_f = _hgrn_codes(False)
    code_b = _hgrn_codes(True)
    ti = lax.broadcasted_iota(jnp.int32, (L, L), 0)
    si = lax.broadcasted_iota(jnp.int32, (L, L), 1)
    tri_f = jnp.where(si <= ti, 1.0, 0.0).astype(BF16)
    tri_b = jnp.where(si >= ti, 1.0, 0.0).astype(BF16)

    def one(row, f_ref, lb, st_ref, code, tri, reverse, out_ref):
        sl = pl.ds(row, L)
        q = _silu(q_ref[sl, :])
        f = lb + (1.0 - lb) * _sigmoid(f_ref[sl, :])
        out_ref[sl, :] = _hgrn_chunk(q, f, v_ref[sl, :], st_ref, code, tri, reverse)

    def body(c, carry):
        for u in range(unroll):
            cu = c * unroll + u
            one(pl.multiple_of(cu * L, L), ff_ref, lb_f, stf_ref, code_f, tri_f, False, of_ref)
            one(pl.multiple_of((nc - 1 - cu) * L, L), fb_ref, lb_b, stb_ref, code_b, tri_b, True, ob_ref)
        return carry

    unroll = 2
    assert nc % unroll == 0
    lax.fori_loop(0, nc // unroll, body, 0)

    sout_ref[0, 0, 0, 0] = stf_ref[...].T
    sout_ref[0, 0, 1, 0] = stb_ref[...].T
    gain = ng_ref[...]

    def fin(c, carry):
        sl = pl.ds(pl.multiple_of(c * L, L), L)
        o = of_ref[sl, :] + ob_ref[sl, :]
        var = jnp.mean(o * o, axis=-1, keepdims=True)
        o_ref[sl, :] = ((o * lax.rsqrt(var + NORM_EPS) * gain) * _silu(gate_ref[sl, :])).astype(BF16)
        return carry

    lax.fori_loop(0, nc, fin, 0)


def _hgrn(proj, lb, norm_g, state, *, n_batch, n_tok):
    col = lambda k: (lambda b, h: (b, k * HG_HEADS + h))
    tok_spec = lambda k: pl.BlockSpec((n_tok, HEAD_DIM), col(k))
    st_spec = pl.BlockSpec((1, 1, 2, 1, HEAD_DIM, HEAD_DIM), lambda b, h: (b, 0, 0, h, 0, 0))
    return pl.pallas_call(
        functools.partial(_hgrn_kernel, n_tok=n_tok),
        out_shape=(jax.ShapeDtypeStruct((n_batch * n_tok, HG_WIDTH), BF16),
                   jax.ShapeDtypeStruct((n_batch, 1, 2, HG_HEADS, HEAD_DIM, HEAD_DIM), F32)),
        grid=(n_batch, HG_HEADS),
        in_specs=[tok_spec(0), tok_spec(1), tok_spec(2), tok_spec(3), tok_spec(4),
                  pl.BlockSpec((2, HEAD_DIM), lambda b, h: (0, h)),
                  pl.BlockSpec((1, HEAD_DIM), lambda b, h: (0, h)),
                  st_spec],
        out_specs=(pl.BlockSpec((n_tok, HEAD_DIM), lambda b, h: (b, h)), st_spec),
        scratch_shapes=[pltpu.VMEM((n_tok, HEAD_DIM), F32), pltpu.VMEM((n_tok, HEAD_DIM), F32),
                        pltpu.VMEM((HEAD_DIM, HEAD_DIM), F32), pltpu.VMEM((HEAD_DIM, HEAD_DIM), F32)],
        compiler_params=_cparams("parallel", "parallel"),
        name="hgrn2_scan",
    )(proj, proj, proj, proj, proj, lb, norm_g, state)


def _stack_heads(x, kvh):
    return jnp.concatenate(
        [x[:, (kvh * ATT_GROUP + g) * HEAD_DIM:(kvh * ATT_GROUP + g + 1) * HEAD_DIM] for g in range(ATT_GROUP)],
        axis=0)


def _sink_column(sink_ref, kvh, rows):
    return jnp.concatenate(
        [jnp.full((rows, 1), sink_ref[kvh * ATT_GROUP + g], F32) for g in range(ATT_GROUP)], axis=0)


def _softmax_av(scores, values, sink_col):
    m = sink_col
    for s in scores:
        m = jnp.maximum(m, jnp.max(s, axis=-1, keepdims=True))
    denom = jnp.exp(sink_col - m)
    o = None
    for s, v in zip(scores, values):
        p = jnp.exp(s - m)
        denom = denom + jnp.sum(p, axis=-1, keepdims=True)
        pv = _dot(p.astype(BF16), v)
        o = pv if o is None else o + pv
    return o / denom


def _ctx_attn_kernel(sink_ref, q_ref, k_ref, v_ref, o_ref):
    rows = q_ref.shape[0]
    scale = HEAD_DIM ** -0.5
    q_all = q_ref[...]
    for kvh in range(ATT_KV_HEADS):
        q = _stack_heads(q_all, kvh).astype(BF16)
        k = k_ref[:, kvh * HEAD_DIM:(kvh + 1) * HEAD_DIM].astype(BF16)
        v = v_ref[:, kvh * HEAD_DIM:(kvh + 1) * HEAD_DIM].astype(BF16)
        o = _softmax_av([_dot_nt(q, k) * scale], [v], _sink_column(sink_ref, kvh, rows))
        for g in range(ATT_GROUP):
            hd = kvh * ATT_GROUP + g
            o_ref[:, hd * HEAD_DIM:(hd + 1) * HEAD_DIM] = o[g * rows:(g + 1) * rows, :].astype(BF16)


def _context_attention(proj, sink, *, n_batch, n_tok):
    q_col = 5 * HG_WIDTH // ATT_WIDTH
    k_col = (5 * HG_WIDTH + ATT_WIDTH) // KV_WIDTH
    return pl.pallas_call(
        _ctx_attn_kernel,
        out_shape=jax.ShapeDtypeStruct((n_batch * n_tok, ATT_WIDTH), BF16),
        grid_spec=pltpu.PrefetchScalarGridSpec(
            num_scalar_prefetch=1,
            grid=(n_batch,),
            in_specs=[pl.BlockSpec((n_tok, ATT_WIDTH), lambda b, s: (b, q_col)),
                      pl.BlockSpec((n_tok, KV_WIDTH), lambda b, s: (b, k_col)),
                      pl.BlockSpec((n_tok, KV_WIDTH), lambda b, s: (b, k_col + 1))],
            out_specs=pl.BlockSpec((n_tok, ATT_WIDTH), lambda b, s: (b, 0))),
        compiler_params=_cparams("parallel"),
        name="context_attention",
    )(sink, proj, proj, proj)


def _rope(x, cos, sin_signed, even_group):
    partner = jnp.where(even_group, pltpu.roll(x, HEAD_DIM - ROPE_FREQS, 1), pltpu.roll(x, ROPE_FREQS, 1))
    return x * cos + partner * sin_signed


def _win_attn_kernel(sink_ref, q_ref, k_ref, v_ref, ck_ref, cv_ref, cos_ref, sin_ref, o_ref,
                     kpad_ref, vpad_ref, *, n_tok):
    blk = ATT_BLOCK
    nb = n_tok // blk
    i = pl.program_id(1)
    scale = HEAD_DIM ** -0.5
    lane = lax.broadcasted_iota(jnp.int32, (blk, HEAD_DIM), 1)
    even_group = (lane & ROPE_FREQS) == 0

    @pl.when(i == 0)
    def _():
        zeros = jnp.zeros((blk, KV_WIDTH), BF16)
        kpad_ref[0:blk, :] = zeros
        vpad_ref[0:blk, :] = zeros
        kpad_ref[blk + n_tok:2 * blk + n_tok, :] = zeros
        vpad_ref[blk + n_tok:2 * blk + n_tok, :] = zeros

        def body(r, carry):
            src = pl.ds(pl.multiple_of(r * blk, blk), blk)
            dst = pl.ds(pl.multiple_of((r + 1) * blk, blk), blk)
            cos = cos_ref[src, :]
            sin = sin_ref[src, :]
            for kvh in range(ATT_KV_HEADS):
                cols = slice(kvh * HEAD_DIM, (kvh + 1) * HEAD_DIM)
                kpad_ref[dst, cols] = _rope(k_ref[src, cols], cos, sin, even_group).astype(BF16)
            vpad_ref[dst, :] = v_ref[src, :].astype(BF16)
            return carry

        lax.fori_loop(0, nb, body, 0)

    rows = pl.ds(pl.multiple_of(i * blk, blk), blk)
    cos = cos_ref[rows, :]
    sin = sin_ref[rows, :]
    band = pl.ds(pl.multiple_of(i * blk, blk), 3 * blk)
    r = lax.broadcasted_iota(jnp.int32, (blk, 3 * blk), 0)
    j = lax.broadcasted_iota(jnp.int32, (blk, 3 * blk), 1)
    kpos = j + (i - 1) * blk
    valid = (j >= r) & (j <= r + 2 * blk) & (kpos >= 0) & (kpos < n_tok)
    valid = jnp.concatenate([valid] * ATT_GROUP, axis=0)
    q_all = q_ref[...]
    for kvh in range(ATT_KV_HEADS):
        cols = slice(kvh * HEAD_DIM, (kvh + 1) * HEAD_DIM)
        q = jnp.concatenate(
            [_rope(q_all[:, (kvh * ATT_GROUP + g) * HEAD_DIM:(kvh * ATT_GROUP + g + 1) * HEAD_DIM],
                   cos, sin, even_group) for g in range(ATT_GROUP)], axis=0).astype(BF16)
        s_ctx = _dot_nt(q, ck_ref[0, :, cols].astype(BF16)) * scale
        s_loc = jnp.where(valid, _dot_nt(q, kpad_ref[band, cols]) * scale, -jnp.inf)
        o = _softmax_av([s_ctx, s_loc], [cv_ref[0, :, cols].astype(BF16), vpad_ref[band, cols]],
                        _sink_column(sink_ref, kvh, blk))
        for g in range(ATT_GROUP):
            hd = kvh * ATT_GROUP + g
            o_ref[:, hd * HEAD_DIM:(hd + 1) * HEAD_DIM] = o[g * blk:(g + 1) * blk, :].astype(BF16)


def _window_attention(proj, cache_k, cache_v, sink, cos, sin_signed, *, n_batch, n_tok):
    nb = n_tok // ATT_BLOCK
    n_ctx = cache_k.shape[1]
    q_col = 5 * HG_WIDTH // ATT_WIDTH
    k_col = (5 * HG_WIDTH + ATT_WIDTH) // KV_WIDTH
    table = pl.BlockSpec((n_tok, HEAD_DIM), lambda b, i, s: (0, 0))
    cache = pl.BlockSpec((1, n_ctx, KV_WIDTH), lambda b, i, s: (b, 0, 0))
    return pl.pallas_call(
        functools.partial(_win_attn_kernel, n_tok=n_tok),
        out_shape=jax.ShapeDtypeStruct((n_batch * n_tok, ATT_WIDTH), BF16),
        grid_spec=pltpu.PrefetchScalarGridSpec(
            num_scalar_prefetch=1,
            grid=(n_batch, nb),
            in_specs=[pl.BlockSpec((ATT_BLOCK, ATT_WIDTH), lambda b, i, s: (b * nb + i, q_col)),
                      pl.BlockSpec((n_tok, KV_WIDTH), lambda b, i, s: (b, k_col)),
                      pl.BlockSpec((n_tok, KV_WIDTH), lambda b, i, s: (b, k_col + 1)),
                      cache, cache, table, table],
            out_specs=pl.BlockSpec((ATT_BLOCK, ATT_WIDTH), lambda b, i, s: (b * nb + i, 0)),
            scratch_shapes=[pltpu.VMEM((n_tok + 2 * ATT_BLOCK, KV_WIDTH), BF16),
                            pltpu.VMEM((n_tok + 2 * ATT_BLOCK, KV_WIDTH), BF16)]),
        compiler_params=_cparams("parallel", "arbitrary"),
        name="window_attention",
    )(sink, proj, proj, proj, cache_k, cache_v, cos, sin_signed)


def _rope_tables(n_tok):
    rows = n_tok // GRID_W
    row = jnp.repeat(jnp.arange(rows), GRID_W).astype(F32)
    col = jnp.tile(jnp.arange(GRID_W), rows).astype(F32)
    inv = ROPE_BASE ** (-jnp.arange(ROPE_FREQS, dtype=F32) / ROPE_FREQS)
    ar, ac = row[:, None] * inv, col[:, None] * inv
    cr, sr, cc, sc = jnp.cos(ar), jnp.sin(ar), jnp.cos(ac), jnp.sin(ac)
    return (jnp.concatenate([cr, cr, cc, cc], axis=1), jnp.concatenate([-sr, sr, -sc, sc], axis=1))


def _outproj_kernel(hg_ref, att_ref, w_ref, x_ref, mod_ref, g_ref, xo_ref, h_ref):
    mix = _dot(hg_ref[...], w_ref[0:HG_WIDTH, :]) + _dot(att_ref[...], w_ref[HG_WIDTH:, :])
    x = x_ref[...] + mod_ref[0, 2:3, :] * mix
    xo_ref[...] = x
    h_ref[...] = _norm_modulate(x, g_ref[...], mod_ref[0, 3:4, :], mod_ref[0, 4:5, :]).astype(BF16)


def _out_projection(o_hg, o_att, w_bf16, x, mod, gain, *, mod_base, rows_per_mod):
    m = x.shape[0]
    tm = 256
    row = lambda i: (i, 0)
    return pl.pallas_call(
        _outproj_kernel,
        out_shape=(jax.ShapeDtypeStruct((m, D_MODEL), F32), jax.ShapeDtypeStruct((m, D_MODEL), BF16)),
        grid=(m // tm,),
        in_specs=[pl.BlockSpec((tm, HG_WIDTH), row),
                  pl.BlockSpec((tm, ATT_WIDTH), row),
                  pl.BlockSpec((HG_WIDTH + ATT_WIDTH, D_MODEL), lambda i: (0, 0)),
                  pl.BlockSpec((tm, D_MODEL), row),
                  pl.BlockSpec((1, N_MOD, D_MODEL), lambda i: (mod_base + (i * tm) // rows_per_mod, 0, 0)),
                  pl.BlockSpec((1, D_MODEL), lambda i: (0, 0))],
        out_specs=(pl.BlockSpec((tm, D_MODEL), row), pl.BlockSpec((tm, D_MODEL), row)),
        compiler_params=_cparams("parallel"),
        name="out_projection",
    )(o_hg, o_att, w_bf16, x, mod, gain)


def _prefix_count(x):
    n = x.shape[1]
    i = lax.broadcasted_iota(jnp.int32, (SEL_BLOCK, SEL_BLOCK), 0)
    j = lax.broadcasted_iota(jnp.int32, (SEL_BLOCK, SEL_BLOCK), 1)
    upper = jnp.where(i < j, 1.0, 0.0).astype(BF16)
    off = jnp.zeros((x.shape[0], 1), F32)
    outs = []
    for blk in range(n // SEL_BLOCK):
        xb = x[:, blk * SEL_BLOCK:(blk + 1) * SEL_BLOCK]
        outs.append(_dot(xb.astype(BF16), upper) + off)
        off = off + jnp.sum(xb, axis=-1, keepdims=True)
    return outs[0] if len(outs) == 1 else jnp.concatenate(outs, axis=1)


def _route_kernel(*refs, cap, aliased):
    if aliased:
        h_ref, w_ref, _, _, x_ref, g_ref, slot_t_ref, slot_ref, aff_ref = refs
    else:
        h_ref, w_ref, x_ref, g_ref, slot_t_ref, slot_ref, aff_ref = refs
    n_tok = h_ref.shape[0]
    e = pl.program_id(1)

    @pl.when(e == 0)
    def _():
        logits = _dot_nt(w_ref[...], h_ref[...])
        ex = jnp.exp(logits - jnp.max(logits, axis=0, keepdims=True))
        aff = ex / jnp.sum(ex, axis=0, keepdims=True)
        aff_ref[...] = aff
        bits = pltpu.bitcast(aff, jnp.int32)

        def bisect(it, thr):
            cand = thr | jnp.left_shift(jnp.int32(1), 30 - it)
            cnt = jnp.sum(jnp.where(bits >= cand, 1.0, 0.0), axis=-1, keepdims=True)
            return jnp.where(cnt >= cap, cand, thr)

        thr = lax.fori_loop(0, 31, bisect, jnp.zeros((N_EXPERTS, 1), jnp.int32))
        above = jnp.where(bits > thr, 1.0, 0.0)
        tied = jnp.where(bits == thr, 1.0, 0.0)
        room = cap - jnp.sum(above, axis=-1, keepdims=True)
        sel = above + tied * jnp.where(_prefix_count(tied) < room, 1.0, 0.0)
        slot = jnp.where(sel > 0.0, _prefix_count(sel), -1.0)
        slot_ref[...] = slot
        i = lax.broadcasted_iota(jnp.int32, (SEL_BLOCK, SEL_BLOCK), 0)
        j = lax.broadcasted_iota(jnp.int32, (SEL_BLOCK, SEL_BLOCK), 1)
        eye = jnp.where(i == j, 1.0, 0.0).astype(BF16)
        for blk in range(n_tok // SEL_BLOCK):
            cols = slice(blk * SEL_BLOCK, (blk + 1) * SEL_BLOCK)
            slot_t_ref[0, cols, :] = _dot_nt(eye, slot[:, cols].astype(BF16))

    row = pl.ds(e, 1)
    c = lax.broadcasted_iota(jnp.int32, (cap, n_tok), 0).astype(F32)
    hit = c == slot_ref[row, :]
    x_ref[0] = _dot(jnp.where(hit, 1.0, 0.0).astype(BF16), h_ref[...]).astype(BF16)
    g_ref[0] = jnp.sum(jnp.where(hit, aff_ref[row, :], 0.0), axis=-1, keepdims=True)


def _route_gather(h, w_router_t, buffers, *, n_batch, n_tok, rows_total, row_block_off):
    cap = CAPACITY_FACTOR * n_tok // N_EXPERTS
    aliased = buffers is not None
    out_block = lambda w: pl.BlockSpec((1, cap, w), lambda b, e: (e, row_block_off + b, 0))
    in_specs = [pl.BlockSpec((n_tok, D_MODEL), lambda b, e: (b, 0)),
                pl.BlockSpec((N_EXPERTS, D_MODEL), lambda b, e: (0, 0))]
    args = [h, w_router_t]
    if aliased:
        in_specs += [pl.BlockSpec(memory_space=pl.ANY), pl.BlockSpec(memory_space=pl.ANY)]
        args += list(buffers)
    return pl.pallas_call(
        functools.partial(_route_kernel, cap=cap, aliased=aliased),
        out_shape=(jax.ShapeDtypeStruct((N_EXPERTS, rows_total, D_MODEL), BF16),
                   jax.ShapeDtypeStruct((N_EXPERTS, rows_total, 1), F32),
                   jax.ShapeDtypeStruct((n_batch, n_tok, N_EXPERTS), F32)),
        grid=(n_batch, N_EXPERTS),
        in_specs=in_specs,
        out_specs=(out_block(D_MODEL), out_block(1),
                   pl.BlockSpec((1, n_tok, N_EXPERTS), lambda b, e: (b, 0, 0))),
        scratch_shapes=[pltpu.VMEM((N_EXPERTS, n_tok), F32), pltpu.VMEM((N_EXPERTS, n_tok), F32)],
        input_output_aliases={2: 0, 3: 1} if aliased else {},
        compiler_params=_cparams("parallel", "arbitrary"),
        name="route_gather",
    )(*args)


def _moe_kernel(x_ref, g_ref, wg_ref, wu_ref, wd_ref, y_ref, hid_ref, *, n_ff, rows):
    s = pl.program_id(2)
    tf = wg_ref.shape[2]

    @pl.when(s < n_ff)
    def _():
        wg = wg_ref[0].astype(BF16)
        wu = wu_ref[0].astype(BF16)
        cols = pl.ds(pl.multiple_of(s * tf, tf), tf)
        for r in range(x_ref.shape[1] // rows):
            sl = slice(r * rows, (r + 1) * rows)
            x = x_ref[0, sl, :]
            hid_ref[sl, cols] = (_silu(_dot(x, wg)) * _dot(x, wu)).astype(BF16)

    @pl.when(s >= n_ff)
    def _():
        y = _dot(hid_ref[...], wd_ref[0].astype(BF16))
        y_ref[0] = (y * g_ref[0]).astype(BF16)


def _experts(x, gate, w_gate, w_up, w_down):
    n_rows = x.shape[1]
    tr, tf, tn = n_rows // 2, 256, 256
    n_ff, n_out = EXPERT_FF // tf, D_MODEL // tn
    up_tile = lambda e, r, s: (e, 0, jnp.minimum(s, n_ff - 1))
    out_tile = lambda s: jnp.maximum(s - n_ff, 0)
    return pl.pallas_call(
        functools.partial(_moe_kernel, n_ff=n_ff, rows=256),
        out_shape=jax.ShapeDtypeStruct((N_EXPERTS, n_rows, D_MODEL), BF16),
        grid=(N_EXPERTS, n_rows // tr, n_ff + n_out),
        in_specs=[pl.BlockSpec((1, tr, D_MODEL), lambda e, r, s: (e, r, 0), pipeline_mode=pl.Buffered(1)),
                  pl.BlockSpec((1, tr, 1), lambda e, r, s: (e, r, 0)),
                  pl.BlockSpec((1, D_MODEL, tf), up_tile),
                  pl.BlockSpec((1, D_MODEL, tf), up_tile),
                  pl.BlockSpec((1, EXPERT_FF, tn), lambda e, r, s: (e, 0, out_tile(s)))],
        out_specs=pl.BlockSpec((1, tr, tn), lambda e, r, s: (e, r, out_tile(s))),
        scratch_shapes=[pltpu.VMEM((tr, EXPERT_FF), BF16)],
        compiler_params=_cparams("parallel", "parallel", "arbitrary"),
        name="expert_swiglu",
    )(x, gate, w_gate, w_up, w_down)


def _combine_kernel(y_ref, slot_ref, x_ref, mod_ref, g_ref, o_ref):
    tt = x_ref.shape[0]
    cap = y_ref.shape[1]
    c = lax.broadcasted_iota(jnp.int32, (tt, cap), 1).astype(F32)
    slots = slot_ref[0]
    acc = jnp.zeros((tt, D_MODEL), F32)
    for e in range(N_EXPERTS):
        hit = c == slots[:, e:e + 1]
        acc = acc + _dot(jnp.where(hit, 1.0, 0.0).astype(BF16), y_ref[e])
    x = x_ref[...] + mod_ref[0, 5:6, :] * acc
    var = jnp.mean(x * x, axis=-1, keepdims=True)
    o_ref[...] = x * lax.rsqrt(var + NORM_EPS) * g_ref[...]


def _combine(y, slot_t, x_mid, mod, final_g, *, n_batch, n_tok, row_block_off, mod_base, mod_per_batch):
    cap = CAPACITY_FACTOR * n_tok // N_EXPERTS
    tt = min(n_tok, 256)
    nt = n_tok // tt
    return pl.pallas_call(
        _combine_kernel,
        out_shape=jax.ShapeDtypeStruct((n_batch * n_tok, D_MODEL), F32),
        grid=(n_batch, nt),
        in_specs=[pl.BlockSpec((N_EXPERTS, cap, D_MODEL), lambda b, t: (0, row_block_off + b, 0)),
                  pl.BlockSpec((1, tt, N_EXPERTS), lambda b, t: (b, t, 0)),
                  pl.BlockSpec((tt, D_MODEL), lambda b, t: (b * nt + t, 0)),
                  pl.BlockSpec((1, N_MOD, D_MODEL), lambda b, t: (mod_base + b * mod_per_batch, 0, 0)),
                  pl.BlockSpec((1, D_MODEL), lambda b, t: (0, 0))],
        out_specs=pl.BlockSpec((tt, D_MODEL), lambda b, t: (b * nt + t, 0)),
        compiler_params=_cparams("parallel", "arbitrary"),
        name="combine_final_norm",
    )(y, slot_t, x_mid, mod, final_g)


def kernel(x_prompt, x_sample, cache_k, cache_v, state_hgrn, c, c_ctx, w_ada, b_ada, norm1_g, w_in, hg_lb,
           hg_norm_g, attn_sink, w_out, norm2_g, w_router, w_gate, w_up, w_down, final_norm_g):
    n_p, t_p, _ = x_prompt.shape
    n_s, t_s, _ = x_sample.shape
    assert w_ada.shape[0] == 1 and 1 + n_s <= COND_ROWS
    layer = 0

    cond = jnp.zeros((COND_ROWS, D_MODEL), F32).at[0].set(c_ctx).at[1:1 + n_s].set(c)
    mod = _ada_modulation(cond, w_ada[layer], b_ada[layer]).reshape(COND_ROWS, N_MOD, D_MODEL)
    lb = jnp.cumsum(jax.nn.softmax(hg_lb.astype(F32), axis=0), axis=0)[layer]
    w_in_b = w_in[layer].astype(BF16)
    w_out_b = w_out[layer].astype(BF16)
    w_router_t = w_router[layer].T.astype(BF16)
    norm1 = norm1_g[layer].reshape(1, D_MODEL)
    norm2 = norm2_g[layer].reshape(1, D_MODEL)
    hg_gain = hg_norm_g[layer].reshape(1, HG_WIDTH)
    final_g = final_norm_g.reshape(1, D_MODEL)
    sink = attn_sink[layer]
    cos, sin_signed = _rope_tables(t_s)

    xp = x_prompt.reshape(n_p * t_p, D_MODEL)
    xs = x_sample.reshape(n_s * t_s, D_MODEL)
    groups = dict(p=dict(mod_base=0, rows_per_mod=n_p * t_p), s=dict(mod_base=1, rows_per_mod=t_s))

    proj_p = _in_projection(xp, mod, norm1, w_in_b, **groups["p"])
    proj_s = _in_projection(xs, mod, norm1, w_in_b, **groups["s"])

    zero_state = jnp.zeros((n_p, 1, 2, HG_HEADS, HEAD_DIM, HEAD_DIM), F32)
    ohg_p, new_state = _hgrn(proj_p, lb, hg_gain, zero_state, n_batch=n_p, n_tok=t_p)
    ohg_s, _ = _hgrn(proj_s, lb, hg_gain, state_hgrn[:, layer:layer + 1].astype(F32), n_batch=n_s, n_tok=t_s)

    oatt_p = _context_attention(proj_p, sink, n_batch=n_p, n_tok=t_p)
    n_ctx = cache_k.shape[2]
    oatt_s = _window_attention(proj_s, cache_k[:, layer].reshape(n_s, n_ctx, KV_WIDTH),
                               cache_v[:, layer].reshape(n_s, n_ctx, KV_WIDTH), sink, cos, sin_signed,
                               n_batch=n_s, n_tok=t_s)

    xmid_p, h2_p = _out_projection(ohg_p, oatt_p, w_out_b, xp, mod, norm2, **groups["p"])
    xmid_s, h2_s = _out_projection(ohg_s, oatt_s, w_out_b, xs, mod, norm2, **groups["s"])

    cap_p = CAPACITY_FACTOR * t_p // N_EXPERTS
    cap_s = CAPACITY_FACTOR * t_s // N_EXPERTS
    rows_total = n_p * cap_p + n_s * cap_s
    assert (n_p * cap_p) % cap_s == 0
    off_s = n_p * cap_p // cap_s
    xg, gate, slot_p = _route_gather(h2_p, w_router_t, None, n_batch=n_p, n_tok=t_p,
                                     rows_total=rows_total, row_block_off=0)
    xg, gate, slot_s = _route_gather(h2_s, w_router_t, (xg, gate), n_batch=n_s, n_tok=t_s,
                                     rows_total=rows_total, row_block_off=off_s)

    y = _experts(xg, gate, w_gate[layer], w_up[layer], w_down[layer])

    y_prompt = _combine(y, slot_p, xmid_p, mod, final_g, n_batch=n_p, n_tok=t_p, row_block_off=0,
                        mod_base=0, mod_per_batch=0)
    y_sample = _combine(y, slot_s, xmid_s, mod, final_g, n_batch=n_s, n_tok=t_s, row_block_off=off_s,
                        mod_base=1, mod_per_batch=1)

    k_col = 5 * HG_WIDTH + ATT_WIDTH
    new_k = proj_p[:, k_col:k_col + KV_WIDTH].reshape(n_p, 1, t_p, ATT_KV_HEADS, HEAD_DIM)
    new_v = proj_p[:, k_col + KV_WIDTH:k_col + 2 * KV_WIDTH].reshape(n_p, 1, t_p, ATT_KV_HEADS, HEAD_DIM)
    return (y_prompt.reshape(n_p, t_p, D_MODEL), y_sample.reshape(n_s, t_s, D_MODEL), new_k, new_v, new_state)
```

```python
import functools

import jax
import jax.numpy as jnp
from jax import lax
from jax.experimental import pallas as pl
from jax.experimental.pallas import tpu as pltpu

F32 = jnp.float32
BF16 = jnp.bfloat16

D_MODEL = 2048
HG_WIDTH = 1024
HG_HEADS = 8
HEAD_DIM = 128
ATT_HEADS = 8
ATT_KV_HEADS = 2
ATT_GROUP = ATT_HEADS // ATT_KV_HEADS
KV_WIDTH = ATT_KV_HEADS * HEAD_DIM
ATT_WIDTH = ATT_HEADS * HEAD_DIM
ATT_BLOCK = 128
GRID_W = 64
ROPE_BASE = 10000.0
ROPE_FREQS = HEAD_DIM // 4
N_EXPERTS = 16
CAPACITY_FACTOR = 2
EXPERT_FF = 5632
NORM_EPS = 1e-6
IN_WIDTH = 5 * HG_WIDTH + ATT_WIDTH + 2 * KV_WIDTH
N_MOD = 6
COND_ROWS = 16

HG_CHUNK = 128
HG_DIAG = 8
SEL_BLOCK = 256

VMEM_LIMIT = 56 * 1024 * 1024


def _cparams(*sem):
    return pltpu.CompilerParams(dimension_semantics=sem, vmem_limit_bytes=VMEM_LIMIT)


def _sigmoid(x):
    return 1.0 / (1.0 + jnp.exp(-x))


def _silu(x):
    return x * _sigmoid(x)


def _dot(a, b):
    return jnp.dot(a, b, preferred_element_type=F32)


def _dot_nt(a, b):
    return lax.dot_general(a, b, (((1,), (1,)), ((), ())), preferred_element_type=F32)


def _dot_tn(a, b):
    return lax.dot_general(a, b, (((0,), (0,)), ((), ())), preferred_element_type=F32)


def _ada_kernel(c_ref, w_ref, b_ref, o_ref):
    s = _silu(c_ref[...]).astype(BF16)
    o_ref[...] = _dot(s, w_ref[...].astype(BF16)) + b_ref[...]


def _ada_modulation(cond, w_ada, b_ada):
    tn = 1024
    n = w_ada.shape[1]
    return pl.pallas_call(
        _ada_kernel,
        out_shape=jax.ShapeDtypeStruct((COND_ROWS, n), F32),
        grid=(n // tn,),
        in_specs=[pl.BlockSpec((COND_ROWS, D_MODEL), lambda j: (0, 0)),
                  pl.BlockSpec((D_MODEL, tn), lambda j: (0, j)),
                  pl.BlockSpec((1, tn), lambda j: (0, j))],
        out_specs=pl.BlockSpec((COND_ROWS, tn), lambda j: (0, j)),
        compiler_params=_cparams("arbitrary"),
        name="ada_modulation",
    )(cond, w_ada, b_ada.reshape(1, n))


def _norm_modulate(x, gain, shift, scale):
    var = jnp.mean(x * x, axis=-1, keepdims=True)
    return (x * lax.rsqrt(var + NORM_EPS) * gain) * (1.0 + scale) + shift


def _inproj_kernel(x_ref, mod_ref, g_ref, w_ref, o_ref, h_ref, *, rows):
    @pl.when(pl.program_id(1) == 0)
    def _():
        shift = mod_ref[0, 0:1, :]
        scale = mod_ref[0, 1:2, :]
        gain = g_ref[...]

        def body(r, carry):
            sl = pl.ds(pl.multiple_of(r * rows, rows), rows)
            h_ref[sl, :] = _norm_modulate(x_ref[sl, :], gain, shift, scale).astype(BF16)
            return carry

        lax.fori_loop(0, x_ref.shape[0] // rows, body, 0)

    w = w_ref[...]
    chunk = 256
    for r in range(x_ref.shape[0] // chunk):
        sl = slice(r * chunk, (r + 1) * chunk)
        o_ref[sl, :] = _dot(h_ref[sl, :], w)


def _in_projection(x, mod, gain, w_bf16, *, mod_base, rows_per_mod):
    m = x.shape[0]
    tm, tn = 1024, 512
    return pl.pallas_call(
        functools.partial(_inproj_kernel, rows=128),
        out_shape=jax.ShapeDtypeStruct((m, IN_WIDTH), F32),
        grid=(m // tm, IN_WIDTH // tn),
        in_specs=[pl.BlockSpec((tm, D_MODEL), lambda i, j: (i, 0)),
                  pl.BlockSpec((1, N_MOD, D_MODEL), lambda i, j: (mod_base + (i * tm) // rows_per_mod, 0, 0)),
                  pl.BlockSpec((1, D_MODEL), lambda i, j: (0, 0)),
                  pl.BlockSpec((D_MODEL, tn), lambda i, j: (0, j))],
        out_specs=pl.BlockSpec((tm, tn), lambda i, j: (i, j)),
        scratch_shapes=[pltpu.VMEM((tm, D_MODEL), BF16)],
        compiler_params=_cparams("parallel", "arbitrary"),
        name="in_projection",
    )(x, mod, gain, w_bf16)


def _hgrn_codes(reverse):
    L = HG_CHUNK
    t = lax.broadcasted_iota(jnp.int32, (L, L), 0)
    s = lax.broadcasted_iota(jnp.int32, (L, L), 1)
    code = jnp.where(t == s, 1, 0)
    h = L // 2
    while h >= 1:
        same = (t & ~(2 * h - 1)) == (s & ~(2 * h - 1))
        t_hi = (t & h) != 0
        s_hi = (s & h) != 0
        pair = (s_hi & ~t_hi) if reverse else (t_hi & ~s_hi)
        code = jnp.where(same & pair, h * 16, code)
        h //= 2
    return code


def _cumsum_rows(tri_bf16, g):
    g1 = g.astype(BF16)
    r1 = g - g1.astype(F32)
    g2 = r1.astype(BF16)
    g3 = (r1 - g2.astype(F32)).astype(BF16)
    return _dot(tri_bf16, g1) + _dot(tri_bf16, g2) + _dot(tri_bf16, g3)


def _hgrn_intra(q, k, f, b, v_bf, code, reverse):
    L = HG_CHUNK
    G = HG_DIAG
    q_bf = q.astype(BF16)
    k_bf = k.astype(BF16)

    def level(h, ref, att):
        e = jnp.exp2((-jnp.abs(b - ref)).astype(BF16))
        return jnp.where(code == h * 16, _dot_nt(q_bf * e, k_bf * e), att)

    att = jnp.where(code == 1, _dot_nt(q_bf, k_bf), 0.0)
    h = L // 2
    while h >= G:
        parts = []
        for p in range(L // (2 * h)):
            m = p * 2 * h + (h if reverse else h - 1)
            parts.append(jnp.broadcast_to(b[m:m + 1, :], (2 * h, HEAD_DIM)))
        att = level(h, parts[0] if len(parts) == 1 else jnp.concatenate(parts, axis=0), att)
        h //= 2

    b3 = b.reshape(L // G, G, HEAD_DIM)

    def group_row(r):
        return jnp.broadcast_to(b3[:, r:r + 1, :], (L // G, G, HEAD_DIM)).reshape(L, HEAD_DIM)

    row = lax.broadcasted_iota(jnp.int32, (L, HEAD_DIM), 0)
    att = level(4, group_row(4 if reverse else 3), att)
    lo, hi = (2, 6) if reverse else (1, 5)
    att = level(2, jnp.where((row & 4) == 0, group_row(lo), group_row(hi)), att)
    att = jnp.where(code == 16, _dot_nt(q_bf * f.astype(BF16), k_bf), att)
    return _dot(att.astype(BF16), v_bf)


def _hgrn_prepare(q, f, v, code, tri, reverse):
    L = HG_CHUNK
    k = 1.0 - f
    b = _cumsum_rows(tri, jnp.log2(f))
    b_tot = b[0:1, :] if reverse else b[L - 1:L, :]
    v_bf = v.astype(BF16)
    o_intra = _hgrn_intra(q, k, f, b, v_bf, code, reverse)
    q_in = (q * jnp.exp2(b)).astype(BF16)
    k_out = (k * jnp.exp2(b_tot - b)).astype(BF16)
    return o_intra, q_in, k_out, v_bf, jnp.exp2(b_tot)


def _hgrn_advance(st, prepared):
    o_intra, q_in, k_out, v_bf, decay = prepared
    return o_intra + _dot_nt(q_in, st.astype(BF16)), st * decay + _dot_tn(v_bf, k_out)


def _hgrn_kernel(q_ref, ff_ref, fb_ref, v_ref, gate_ref, lb_ref, ng_ref, s0_ref,
                 o_ref, sout_ref, of_ref, ob_ref, *, n_tok):
    L = HG_CHUNK
    nc = n_tok // L
    lb_f = lb_ref[0:1, :]
    lb_b = lb_ref[1:2, :]
    code_f = _hgrn_codes(False)
    code_b = _hgrn_codes(True)
    ti = lax.broadcasted_iota(jnp.int32, (L, L), 0)
    si = lax.broadcasted_iota(jnp.int32, (L, L), 1)
    tri_f = jnp.where(si <= ti, 1.0, 0.0).astype(BF16)
    tri_b = jnp.where(si >= ti, 1.0, 0.0).astype(BF16)

    def prepare(row, f_ref, lb, code, tri, reverse):
        sl = pl.ds(row, L)
        q = _silu(q_ref[sl, :])
        f = lb + (1.0 - lb) * _sigmoid(f_ref[sl, :])
        return sl, _hgrn_prepare(q, f, v_ref[sl, :], code, tri, reverse)

    def body(c, states):
        st_f, st_b = states
        fwd, bwd = [], []
        for u in range(unroll):
            cu = c * unroll + u
            fwd.append(prepare(pl.multiple_of(cu * L, L), ff_ref, lb_f, code_f, tri_f, False))
            bwd.append(prepare(pl.multiple_of((nc - 1 - cu) * L, L), fb_ref, lb_b, code_b, tri_b, True))
        for (sl_f, prep_f), (sl_b, prep_b) in zip(fwd, bwd):
            of_ref[sl_f, :], st_f = _hgrn_advance(st_f, prep_f)
            ob_ref[sl_b, :], st_b = _hgrn_advance(st_b, prep_b)
        return st_f, st_b

    unroll = 2
    assert nc % unroll == 0
    st_f, st_b = lax.fori_loop(0, nc // unroll, body, (s0_ref[0, 0, 0, 0].T, s0_ref[0, 0, 1, 0].T))
    sout_ref[0, 0, 0, 0] = st_f.T
    sout_ref[0, 0, 1, 0] = st_b.T
    gain = ng_ref[...]

    def fin(c, carry):
        sl = pl.ds(pl.multiple_of(c * L, L), L)
        o = of_ref[sl, :] + ob_ref[sl, :]
        var = jnp.mean(o * o, axis=-1, keepdims=True)
        o_ref[sl, :] = ((o * lax.rsqrt(var + NORM_EPS) * gain) * _silu(gate_ref[sl, :])).astype(BF16)
        return carry

    lax.fori_loop(0, nc, fin, 0)


def _hgrn(proj, lb, norm_g, state, *, n_batch, n_tok):
    col = lambda k: (lambda b, h: (b, k * HG_HEADS + h))
    tok_spec = lambda k: pl.BlockSpec((n_tok, HEAD_DIM), col(k))
    st_spec = pl.BlockSpec((1, 1, 2, 1, HEAD_DIM, HEAD_DIM), lambda b, h: (b, 0, 0, h, 0, 0))
    return pl.pallas_call(
        functools.partial(_hgrn_kernel, n_tok=n_tok),
        out_shape=(jax.ShapeDtypeStruct((n_batch * n_tok, HG_WIDTH), BF16),
                   jax.ShapeDtypeStruct((n_batch, 1, 2, HG_HEADS, HEAD_DIM, HEAD_DIM), F32)),
        grid=(n_batch, HG_HEADS),
        in_specs=[tok_spec(0), tok_spec(1), tok_spec(2), tok_spec(3), tok_spec(4),
                  pl.BlockSpec((2, HEAD_DIM), lambda b, h: (0, h)),
                  pl.BlockSpec((1, HEAD_DIM), lambda b, h: (0, h)),
                  st_spec],
        out_specs=(pl.BlockSpec((n_tok, HEAD_DIM), lambda b, h: (b, h)), st_spec),
        scratch_shapes=[pltpu.VMEM((n_tok, HEAD_DIM), F32), pltpu.VMEM((n_tok, HEAD_DIM), F32)],
        compiler_params=_cparams("parallel", "parallel"),
        name="hgrn2_scan",
    )(proj, proj, proj, proj, proj, lb, norm_g, state)


def _stack_heads(x, kvh):
    return jnp.concatenate(
        [x[:, (kvh * ATT_GROUP + g) * HEAD_DIM:(kvh * ATT_GROUP + g + 1) * HEAD_DIM] for g in range(ATT_GROUP)],
        axis=0)


def _sink_column(sink_ref, kvh, rows):
    return jnp.concatenate(
        [jnp.full((rows, 1), sink_ref[kvh * ATT_GROUP + g], F32) for g in range(ATT_GROUP)], axis=0)


def _softmax_av(scores, values, sink_col):
    m = sink_col
    for s in scores:
        m = jnp.maximum(m, jnp.max(s, axis=-1, keepdims=True))
    denom = jnp.exp(sink_col - m)
    o = None
    for s, v in zip(scores, values):
        p = jnp.exp(s - m)
        denom = denom + jnp.sum(p, axis=-1, keepdims=True)
        pv = _dot(p.astype(BF16), v)
        o = pv if o is None else o + pv
    return o / denom


def _ctx_attn_kernel(sink_ref, q_ref, k_ref, v_ref, o_ref):
    rows = q_ref.shape[0]
    scale = HEAD_DIM ** -0.5
    q_all = q_ref[...]
    for kvh in range(ATT_KV_HEADS):
        q = _stack_heads(q_all, kvh).astype(BF16)
        k = k_ref[:, kvh * HEAD_DIM:(kvh + 1) * HEAD_DIM].astype(BF16)
        v = v_ref[:, kvh * HEAD_DIM:(kvh + 1) * HEAD_DIM].astype(BF16)
        o = _softmax_av([_dot_nt(q, k) * scale], [v], _sink_column(sink_ref, kvh, rows))
        for g in range(ATT_GROUP):
            hd = kvh * ATT_GROUP + g
            o_ref[:, hd * HEAD_DIM:(hd + 1) * HEAD_DIM] = o[g * rows:(g + 1) * rows, :].astype(BF16)


def _context_attention(proj, sink, *, n_batch, n_tok):
    q_col = 5 * HG_WIDTH // ATT_WIDTH
    k_col = (5 * HG_WIDTH + ATT_WIDTH) // KV_WIDTH
    return pl.pallas_call(
        _ctx_attn_kernel,
        out_shape=jax.ShapeDtypeStruct((n_batch * n_tok, ATT_WIDTH), BF16),
        grid_spec=pltpu.PrefetchScalarGridSpec(
            num_scalar_prefetch=1,
            grid=(n_batch,),
            in_specs=[pl.BlockSpec((n_tok, ATT_WIDTH), lambda b, s: (b, q_col)),
                      pl.BlockSpec((n_tok, KV_WIDTH), lambda b, s: (b, k_col)),
                      pl.BlockSpec((n_tok, KV_WIDTH), lambda b, s: (b, k_col + 1))],
            out_specs=pl.BlockSpec((n_tok, ATT_WIDTH), lambda b, s: (b, 0))),
        compiler_params=_cparams("parallel"),
        name="context_attention",
    )(sink, proj, proj, proj)


def _rope(x, cos, sin_signed, even_group):
    partner = jnp.where(even_group, pltpu.roll(x, HEAD_DIM - ROPE_FREQS, 1), pltpu.roll(x, ROPE_FREQS, 1))
    return x * cos + partner * sin_signed


def _win_attn_kernel(sink_ref, q_ref, k_ref, v_ref, ck_ref, cv_ref, cos_ref, sin_ref, o_ref,
                     kpad_ref, vpad_ref, *, n_tok):
    blk = ATT_BLOCK
    nb = n_tok // blk
    i = pl.program_id(1)
    scale = HEAD_DIM ** -0.5
    lane = lax.broadcasted_iota(jnp.int32, (blk, HEAD_DIM), 1)
    even_group = (lane & ROPE_FREQS) == 0

    @pl.when(i == 0)
    def _():
        zeros = jnp.zeros((blk, KV_WIDTH), BF16)
        kpad_ref[0:blk, :] = zeros
        vpad_ref[0:blk, :] = zeros
        kpad_ref[blk + n_tok:2 * blk + n_tok, :] = zeros
        vpad_ref[blk + n_tok:2 * blk + n_tok, :] = zeros

        def body(r, carry):
            src = pl.ds(pl.multiple_of(r * blk, blk), blk)
            dst = pl.ds(pl.multiple_of((r + 1) * blk, blk), blk)
            cos = cos_ref[src, :]
            sin = sin_ref[src, :]
            for kvh in range(ATT_KV_HEADS):
                cols = slice(kvh * HEAD_DIM, (kvh + 1) * HEAD_DIM)
                kpad_ref[dst, cols] = _rope(k_ref[src, cols], cos, sin, even_group).astype(BF16)
            vpad_ref[dst, :] = v_ref[src, :].astype(BF16)
            return carry

        lax.fori_loop(0, nb, body, 0)

    rows = pl.ds(pl.multiple_of(i * blk, blk), blk)
    cos = cos_ref[rows, :]
    sin = sin_ref[rows, :]
    band = pl.ds(pl.multiple_of(i * blk, blk), 3 * blk)
    r = lax.broadcasted_iota(jnp.int32, (blk, 3 * blk), 0)
    j = lax.broadcasted_iota(jnp.int32, (blk, 3 * blk), 1)
    kpos = j + (i - 1) * blk
    valid = (j >= r) & (j <= r + 2 * blk) & (kpos >= 0) & (kpos < n_tok)
    valid = jnp.concatenate([valid] * ATT_GROUP, axis=0)
    q_all = q_ref[...]
    for kvh in range(ATT_KV_HEADS):
        cols = slice(kvh * HEAD_DIM, (kvh + 1) * HEAD_DIM)
        q = jnp.concatenate(
            [_rope(q_all[:, (kvh * ATT_GROUP + g) * HEAD_DIM:(kvh * ATT_GROUP + g + 1) * HEAD_DIM],
                   cos, sin, even_group) for g in range(ATT_GROUP)], axis=0).astype(BF16)
        s_ctx = _dot_nt(q, ck_ref[0, :, cols].astype(BF16)) * scale
        s_loc = jnp.where(valid, _dot_nt(q, kpad_ref[band, cols]) * scale, -jnp.inf)
        o = _softmax_av([s_ctx, s_loc], [cv_ref[0, :, cols].astype(BF16), vpad_ref[band, cols]],
                        _sink_column(sink_ref, kvh, blk))
        for g in range(ATT_GROUP):
            hd = kvh * ATT_GROUP + g
            o_ref[:, hd * HEAD_DIM:(hd + 1) * HEAD_DIM] = o[g * blk:(g + 1) * blk, :].astype(BF16)


def _window_attention(proj, cache_k, cache_v, sink, cos, sin_signed, *, n_batch, n_tok):
    nb = n_tok // ATT_BLOCK
    n_ctx = cache_k.shape[1]
    q_col = 5 * HG_WIDTH // ATT_WIDTH
    k_col = (5 * HG_WIDTH + ATT_WIDTH) // KV_WIDTH
    table = pl.BlockSpec((n_tok, HEAD_DIM), lambda b, i, s: (0, 0))
    cache = pl.BlockSpec((1, n_ctx, KV_WIDTH), lambda b, i, s: (b, 0, 0))
    return pl.pallas_call(
        functools.partial(_win_attn_kernel, n_tok=n_tok),
        out_shape=jax.ShapeDtypeStruct((n_batch * n_tok, ATT_WIDTH), BF16),
        grid_spec=pltpu.PrefetchScalarGridSpec(
            num_scalar_prefetch=1,
            grid=(n_batch, nb),
            in_specs=[pl.BlockSpec((ATT_BLOCK, ATT_WIDTH), lambda b, i, s: (b * nb + i, q_col)),
                      pl.BlockSpec((n_tok, KV_WIDTH), lambda b, i, s: (b, k_col)),
                      pl.BlockSpec((n_tok, KV_WIDTH), lambda b, i, s: (b, k_col + 1)),
                      cache, cache, table, table],
            out_specs=pl.BlockSpec((ATT_BLOCK, ATT_WIDTH), lambda b, i, s: (b * nb + i, 0)),
            scratch_shapes=[pltpu.VMEM((n_tok + 2 * ATT_BLOCK, KV_WIDTH), BF16),
                            pltpu.VMEM((n_tok + 2 * ATT_BLOCK, KV_WIDTH), BF16)]),
        compiler_params=_cparams("parallel", "arbitrary"),
        name="window_attention",
    )(sink, proj, proj, proj, cache_k, cache_v, cos, sin_signed)


def _rope_tables(n_tok):
    rows = n_tok // GRID_W
    row = jnp.repeat(jnp.arange(rows), GRID_W).astype(F32)
    col = jnp.tile(jnp.arange(GRID_W), rows).astype(F32)
    inv = ROPE_BASE ** (-jnp.arange(ROPE_FREQS, dtype=F32) / ROPE_FREQS)
    ar, ac = row[:, None] * inv, col[:, None] * inv
    cr, sr, cc, sc = jnp.cos(ar), jnp.sin(ar), jnp.cos(ac), jnp.sin(ac)
    return (jnp.concatenate([cr, cr, cc, cc], axis=1), jnp.concatenate([-sr, sr, -sc, sc], axis=1))


def _outproj_kernel(hg_ref, att_ref, w_ref, x_ref, mod_ref, g_ref, xo_ref, h_ref):
    mix = _dot(hg_ref[...], w_ref[0:HG_WIDTH, :]) + _dot(att_ref[...], w_ref[HG_WIDTH:, :])
    x = x_ref[...] + mod_ref[0, 2:3, :] * mix
    xo_ref[...] = x
    h_ref[...] = _norm_modulate(x, g_ref[...], mod_ref[0, 3:4, :], mod_ref[0, 4:5, :]).astype(BF16)


def _out_projection(o_hg, o_att, w_bf16, x, mod, gain, *, mod_base, rows_per_mod):
    m = x.shape[0]
    tm = 256
    row = lambda i: (i, 0)
    return pl.pallas_call(
        _outproj_kernel,
        out_shape=(jax.ShapeDtypeStruct((m, D_MODEL), F32), jax.ShapeDtypeStruct((m, D_MODEL), BF16)),
        grid=(m // tm,),
        in_specs=[pl.BlockSpec((tm, HG_WIDTH), row),
                  pl.BlockSpec((tm, ATT_WIDTH), row),
                  pl.BlockSpec((HG_WIDTH + ATT_WIDTH, D_MODEL), lambda i: (0, 0)),
                  pl.BlockSpec((tm, D_MODEL), row),
                  pl.BlockSpec((1, N_MOD, D_MODEL), lambda i: (mod_base + (i * tm) // rows_per_mod, 0, 0)),
                  pl.BlockSpec((1, D_MODEL), lambda i: (0, 0))],
        out_specs=(pl.BlockSpec((tm, D_MODEL), row), pl.BlockSpec((tm, D_MODEL), row)),
        compiler_params=_cparams("parallel"),
        name="out_projection",
    )(o_hg, o_att, w_bf16, x, mod, gain)


def _prefix_count(x):
    n = x.shape[1]
    i = lax.broadcasted_iota(jnp.int32, (SEL_BLOCK, SEL_BLOCK), 0)
    j = lax.broadcasted_iota(jnp.int32, (SEL_BLOCK, SEL_BLOCK), 1)
    upper = jnp.where(i < j, 1.0, 0.0).astype(BF16)
    off = jnp.zeros((x.shape[0], 1), F32)
    outs = []
    for blk in range(n // SEL_BLOCK):
        xb = x[:, blk * SEL_BLOCK:(blk + 1) * SEL_BLOCK]
        outs.append(_dot(xb.astype(BF16), upper) + off)
        off = off + jnp.sum(xb, axis=-1, keepdims=True)
    return outs[0] if len(outs) == 1 else jnp.concatenate(outs, axis=1)


def _route_kernel(*refs, cap, aliased):
    if aliased:
        h_ref, w_ref, _, _, x_ref, g_ref, slot_t_ref, slot_ref, aff_ref = refs
    else:
        h_ref, w_ref, x_ref, g_ref, slot_t_ref, slot_ref, aff_ref = refs
    n_tok = h_ref.shape[0]
    e = pl.program_id(1)

    @pl.when(e == 0)
    def _():
        logits = _dot_nt(w_ref[...], h_ref[...])
        ex = jnp.exp(logits - jnp.max(logits, axis=0, keepdims=True))
        aff = ex / jnp.sum(ex, axis=0, keepdims=True)
        aff_ref[...] = aff
        bits = pltpu.bitcast(aff, jnp.int32)

        def bisect(it, thr):
            cand = thr | jnp.left_shift(jnp.int32(1), 30 - it)
            cnt = jnp.sum(jnp.where(bits >= cand, 1.0, 0.0), axis=-1, keepdims=True)
            return jnp.where(cnt >= cap, cand, thr)

        thr = lax.fori_loop(0, 31, bisect, jnp.zeros((N_EXPERTS, 1), jnp.int32))
        above = jnp.where(bits > thr, 1.0, 0.0)
        tied = jnp.where(bits == thr, 1.0, 0.0)
        room = cap - jnp.sum(above, axis=-1, keepdims=True)
        sel = above + tied * jnp.where(_prefix_count(tied) < room, 1.0, 0.0)
        slot = jnp.where(sel > 0.0, _prefix_count(sel), -1.0)
        slot_ref[...] = slot
        i = lax.broadcasted_iota(jnp.int32, (SEL_BLOCK, SEL_BLOCK), 0)
        j = lax.broadcasted_iota(jnp.int32, (SEL_BLOCK, SEL_BLOCK), 1)
        eye = jnp.where(i == j, 1.0, 0.0).astype(BF16)
        for blk in range(n_tok // SEL_BLOCK):
            cols = slice(blk * SEL_BLOCK, (blk + 1) * SEL_BLOCK)
            slot_t_ref[0, cols, :] = _dot_nt(eye, slot[:, cols].astype(BF16))

    row = pl.ds(e, 1)
    c = lax.broadcasted_iota(jnp.int32, (cap, n_tok), 0).astype(F32)
    hit = c == slot_ref[row, :]
    x_ref[0] = _dot(jnp.where(hit, 1.0, 0.0).astype(BF16), h_ref[...]).astype(BF16)
    g_ref[0] = jnp.sum(jnp.where(hit, aff_ref[row, :], 0.0), axis=-1, keepdims=True)


def _route_gather(h, w_router_t, buffers, *, n_batch, n_tok, rows_total, row_block_off):
    cap = CAPACITY_FACTOR * n_tok // N_EXPERTS
    aliased = buffers is not None
    out_block = lambda w: pl.BlockSpec((1, cap, w), lambda b, e: (e, row_block_off + b, 0))
    in_specs = [pl.BlockSpec((n_tok, D_MODEL), lambda b, e: (b, 0)),
                pl.BlockSpec((N_EXPERTS, D_MODEL), lambda b, e: (0, 0))]
    args = [h, w_router_t]
    if aliased:
        in_specs += [pl.BlockSpec(memory_space=pl.ANY), pl.BlockSpec(memory_space=pl.ANY)]
        args += list(buffers)
    return pl.pallas_call(
        functools.partial(_route_kernel, cap=cap, aliased=aliased),
        out_shape=(jax.ShapeDtypeStruct((N_EXPERTS, rows_total, D_MODEL), BF16),
                   jax.ShapeDtypeStruct((N_EXPERTS, rows_total, 1), F32),
                   jax.ShapeDtypeStruct((n_batch, n_tok, N_EXPERTS), F32)),
        grid=(n_batch, N_EXPERTS),
        in_specs=in_specs,
        out_specs=(out_block(D_MODEL), out_block(1),
                   pl.BlockSpec((1, n_tok, N_EXPERTS), lambda b, e: (b, 0, 0))),
        scratch_shapes=[pltpu.VMEM((N_EXPERTS, n_tok), F32), pltpu.VMEM((N_EXPERTS, n_tok), F32)],
        input_output_aliases={2: 0, 3: 1} if aliased else {},
        compiler_params=_cparams("parallel", "arbitrary"),
        name="route_gather",
    )(*args)


def _moe_kernel(x_ref, g_ref, wg_ref, wu_ref, wd_ref, y_ref, hid_ref, *, n_ff, rows):
    s = pl.program_id(2)
    tf = wg_ref.shape[2]

    @pl.when(s < n_ff)
    def _():
        wg = wg_ref[0].astype(BF16)
        wu = wu_ref[0].astype(BF16)
        cols = pl.ds(pl.multiple_of(s * tf, tf), tf)
        for r in range(x_ref.shape[1] // rows):
            sl = slice(r * rows, (r + 1) * rows)
            x = x_ref[0, sl, :]
            hid_ref[sl, cols] = (_silu(_dot(x, wg)) * _dot(x, wu)).astype(BF16)

    @pl.when(s >= n_ff)
    def _():
        y = _dot(hid_ref[...], wd_ref[0].astype(BF16))
        y_ref[0] = (y * g_ref[0]).astype(BF16)


def _experts(x, gate, w_gate, w_up, w_down):
    n_rows = x.shape[1]
    tr, tf, tn = n_rows // 2, 512, 256
    n_ff, n_out = EXPERT_FF // tf, D_MODEL // tn
    up_tile = lambda e, r, s: (e, 0, jnp.minimum(s, n_ff - 1))
    out_tile = lambda s: jnp.maximum(s - n_ff, 0)
    return pl.pallas_call(
        functools.partial(_moe_kernel, n_ff=n_ff, rows=256),
        out_shape=jax.ShapeDtypeStruct((N_EXPERTS, n_rows, D_MODEL), BF16),
        grid=(N_EXPERTS, n_rows // tr, n_ff + n_out),
        in_specs=[pl.BlockSpec((1, tr, D_MODEL), lambda e, r, s: (e, r, 0), pipeline_mode=pl.Buffered(1)),
                  pl.BlockSpec((1, tr, 1), lambda e, r, s: (e, r, 0)),
                  pl.BlockSpec((1, D_MODEL, tf), up_tile),
                  pl.BlockSpec((1, D_MODEL, tf), up_tile),
                  pl.BlockSpec((1, EXPERT_FF, tn), lambda e, r, s: (e, 0, out_tile(s)))],
        out_specs=pl.BlockSpec((1, tr, tn), lambda e, r, s: (e, r, out_tile(s))),
        scratch_shapes=[pltpu.VMEM((tr, EXPERT_FF), BF16)],
        compiler_params=_cparams("parallel", "parallel", "arbitrary"),
        name="expert_swiglu",
    )(x, gate, w_gate, w_up, w_down)


def _combine_kernel(y_ref, slot_ref, x_ref, mod_ref, g_ref, o_ref):
    tt = x_ref.shape[0]
    cap = y_ref.shape[1]
    c = lax.broadcasted_iota(jnp.int32, (tt, cap), 1).astype(F32)
    slots = slot_ref[0]
    acc = jnp.zeros((tt, D_MODEL), F32)
    for e in range(N_EXPERTS):
        hit = c == slots[:, e:e + 1]
        acc = acc + _dot(jnp.where(hit, 1.0, 0.0).astype(BF16), y_ref[e])
    x = x_ref[...] + mod_ref[0, 5:6, :] * acc
    var = jnp.mean(x * x, axis=-1, keepdims=True)
    o_ref[...] = x * lax.rsqrt(var + NORM_EPS) * g_ref[...]


def _combine(y, slot_t, x_mid, mod, final_g, *, n_batch, n_tok, row_block_off, mod_base, mod_per_batch):
    cap = CAPACITY_FACTOR * n_tok // N_EXPERTS
    tt = min(n_tok, 256)
    nt = n_tok // tt
    return pl.pallas_call(
        _combine_kernel,
        out_shape=jax.ShapeDtypeStruct((n_batch * n_tok, D_MODEL), F32),
        grid=(n_batch, nt),
        in_specs=[pl.BlockSpec((N_EXPERTS, cap, D_MODEL), lambda b, t: (0, row_block_off + b, 0)),
                  pl.BlockSpec((1, tt, N_EXPERTS), lambda b, t: (b, t, 0)),
                  pl.BlockSpec((tt, D_MODEL), lambda b, t: (b * nt + t, 0)),
                  pl.BlockSpec((1, N_MOD, D_MODEL), lambda b, t: (mod_base + b * mod_per_batch, 0, 0)),
                  pl.BlockSpec((1, D_MODEL), lambda b, t: (0, 0))],
        out_specs=pl.BlockSpec((tt, D_MODEL), lambda b, t: (b * nt + t, 0)),
        compiler_params=_cparams("parallel", "arbitrary"),
        name="combine_final_norm",
    )(y, slot_t, x_mid, mod, final_g)


def kernel(x_prompt, x_sample, cache_k, cache_v, state_hgrn, c, c_ctx, w_ada, b_ada, norm1_g, w_in, hg_lb,
           hg_norm_g, attn_sink, w_out, norm2_g, w_router, w_gate, w_up, w_down, final_norm_g):
    n_p, t_p, _ = x_prompt.shape
    n_s, t_s, _ = x_sample.shape
    assert w_ada.shape[0] == 1 and 1 + n_s <= COND_ROWS
    layer = 0

    cond = jnp.zeros((COND_ROWS, D_MODEL), F32).at[0].set(c_ctx).at[1:1 + n_s].set(c)
    mod = _ada_modulation(cond, w_ada[layer], b_ada[layer]).reshape(COND_ROWS, N_MOD, D_MODEL)
    lb = jnp.cumsum(jax.nn.softmax(hg_lb.astype(F32), axis=0), axis=0)[layer]
    w_in_b = w_in[layer].astype(BF16)
    w_out_b = w_out[layer].astype(BF16)
    w_router_t = w_router[layer].T.astype(BF16)
    norm1 = norm1_g[layer].reshape(1, D_MODEL)
    norm2 = norm2_g[layer].reshape(1, D_MODEL)
    hg_gain = hg_norm_g[layer].reshape(1, HG_WIDTH)
    final_g = final_norm_g.reshape(1, D_MODEL)
    sink = attn_sink[layer]
    cos, sin_signed = _rope_tables(t_s)

    xp = x_prompt.reshape(n_p * t_p, D_MODEL)
    xs = x_sample.reshape(n_s * t_s, D_MODEL)
    groups = dict(p=dict(mod_base=0, rows_per_mod=n_p * t_p), s=dict(mod_base=1, rows_per_mod=t_s))

    proj_p = _in_projection(xp, mod, norm1, w_in_b, **groups["p"])
    proj_s = _in_projection(xs, mod, norm1, w_in_b, **groups["s"])

    zero_state = jnp.zeros((n_p, 1, 2, HG_HEADS, HEAD_DIM, HEAD_DIM), F32)
    ohg_p, new_state = _hgrn(proj_p, lb, hg_gain, zero_state, n_batch=n_p, n_tok=t_p)
    ohg_s, _ = _hgrn(proj_s, lb, hg_gain, state_hgrn[:, layer:layer + 1].astype(F32), n_batch=n_s, n_tok=t_s)

    oatt_p = _context_attention(proj_p, sink, n_batch=n_p, n_tok=t_p)
    n_ctx = cache_k.shape[2]
    oatt_s = _window_attention(proj_s, cache_k[:, layer].reshape(n_s, n_ctx, KV_WIDTH),
                               cache_v[:, layer].reshape(n_s, n_ctx, KV_WIDTH), sink, cos, sin_signed,
                               n_batch=n_s, n_tok=t_s)

    xmid_p, h2_p = _out_projection(ohg_p, oatt_p, w_out_b, xp, mod, norm2, **groups["p"])
    xmid_s, h2_s = _out_projection(ohg_s, oatt_s, w_out_b, xs, mod, norm2, **groups["s"])

    cap_p = CAPACITY_FACTOR * t_p // N_EXPERTS
    cap_s = CAPACITY_FACTOR * t_s // N_EXPERTS
    rows_total = n_p * cap_p + n_s * cap_s
    assert (n_p * cap_p) % cap_s == 0
    off_s = n_p * cap_p // cap_s
    xg, gate, slot_p = _route_gather(h2_p, w_router_t, None, n_batch=n_p, n_tok=t_p,
                                     rows_total=rows_total, row_block_off=0)
    xg, gate, slot_s = _route_gather(h2_s, w_router_t, (xg, gate), n_batch=n_s, n_tok=t_s,
                                     rows_total=rows_total, row_block_off=off_s)

    y = _experts(xg, gate, w_gate[layer], w_up[layer], w_down[layer])

    y_prompt = _combine(y, slot_p, xmid_p, mod, final_g, n_batch=n_p, n_tok=t_p, row_block_off=0,
                        mod_base=0, mod_per_batch=0)
    y_sample = _combine(y, slot_s, xmid_s, mod, final_g, n_batch=n_s, n_tok=t_s, row_block_off=off_s,
                        mod_base=1, mod_per_batch=1)

    k_col = 5 * HG_WIDTH + ATT_WIDTH
    new_k = proj_p[:, k_col:k_col + KV_WIDTH].reshape(n_p, 1, t_p, ATT_KV_HEADS, HEAD_DIM)
    new_v = proj_p[:, k_col + KV_WIDTH:k_col + 2 * KV_WIDTH].reshape(n_p, 1, t_p, ATT_KV_HEADS, HEAD_DIM)
    return (y_prompt.reshape(n_p, t_p, D_MODEL), y_sample.reshape(n_s, t_s, D_MODEL), new_k, new_v, new_state)
```

```python
import functools

import jax
import jax.numpy as jnp
from jax import lax
from jax.experimental import pallas as pl
from jax.experimental.pallas import tpu as pltpu

F32 = jnp.float32
BF16 = jnp.bfloat16

D_MODEL = 2048
HG_WIDTH = 1024
HG_HEADS = 8
HEAD_DIM = 128
ATT_HEADS = 8
ATT_KV_HEADS = 2
ATT_GROUP = ATT_HEADS // ATT_KV_HEADS
KV_WIDTH = ATT_KV_HEADS * HEAD_DIM
ATT_WIDTH = ATT_HEADS * HEAD_DIM
ATT_BLOCK = 128
GRID_W = 64
ROPE_BASE = 10000.0
ROPE_FREQS = HEAD_DIM // 4
N_EXPERTS = 16
CAPACITY_FACTOR = 2
EXPERT_FF = 5632
NORM_EPS = 1e-6
IN_WIDTH = 5 * HG_WIDTH + ATT_WIDTH + 2 * KV_WIDTH
N_MOD = 6
COND_ROWS = 16

HG_CHUNK = 128
HG_DIAG = 8
SEL_BLOCK = 256

VMEM_LIMIT = 56 * 1024 * 1024


def _cparams(*sem):
    return pltpu.CompilerParams(dimension_semantics=sem, vmem_limit_bytes=VMEM_LIMIT)


def _sigmoid(x):
    return 1.0 / (1.0 + jnp.exp(-x))


def _silu(x):
    return x * _sigmoid(x)


def _dot(a, b):
    return jnp.dot(a, b, preferred_element_type=F32)


def _dot_nt(a, b):
    return lax.dot_general(a, b, (((1,), (1,)), ((), ())), preferred_element_type=F32)


def _dot_tn(a, b):
    return lax.dot_general(a, b, (((0,), (0,)), ((), ())), preferred_element_type=F32)


def _ada_kernel(c_ref, w_ref, b_ref, o_ref):
    s = _silu(c_ref[...]).astype(BF16)
    o_ref[...] = _dot(s, w_ref[...].astype(BF16)) + b_ref[...]


def _ada_modulation(cond, w_ada, b_ada):
    tn = 1024
    n = w_ada.shape[1]
    return pl.pallas_call(
        _ada_kernel,
        out_shape=jax.ShapeDtypeStruct((COND_ROWS, n), F32),
        grid=(n // tn,),
        in_specs=[pl.BlockSpec((COND_ROWS, D_MODEL), lambda j: (0, 0)),
                  pl.BlockSpec((D_MODEL, tn), lambda j: (0, j)),
                  pl.BlockSpec((1, tn), lambda j: (0, j))],
        out_specs=pl.BlockSpec((COND_ROWS, tn), lambda j: (0, j)),
        compiler_params=_cparams("arbitrary"),
        name="ada_modulation",
    )(cond, w_ada, b_ada.reshape(1, n))


def _norm_modulate(x, gain, shift, scale):
    var = jnp.mean(x * x, axis=-1, keepdims=True)
    return (x * lax.rsqrt(var + NORM_EPS) * gain) * (1.0 + scale) + shift


def _inproj_kernel(x_ref, mod_ref, g_ref, w_ref, o_ref, h_ref, *, rows):
    @pl.when(pl.program_id(1) == 0)
    def _():
        shift = mod_ref[0, 0:1, :]
        scale = mod_ref[0, 1:2, :]
        gain = g_ref[...]

        def body(r, carry):
            sl = pl.ds(pl.multiple_of(r * rows, rows), rows)
            h_ref[sl, :] = _norm_modulate(x_ref[sl, :], gain, shift, scale).astype(BF16)
            return carry

        lax.fori_loop(0, x_ref.shape[0] // rows, body, 0)

    w = w_ref[...]
    chunk = 256
    for r in range(x_ref.shape[0] // chunk):
        sl = slice(r * chunk, (r + 1) * chunk)
        o_ref[sl, :] = _dot(h_ref[sl, :], w)


def _in_projection(x, mod, gain, w_bf16, *, mod_base, rows_per_mod):
    m = x.shape[0]
    tm, tn = 1024, 512
    return pl.pallas_call(
        functools.partial(_inproj_kernel, rows=128),
        out_shape=jax.ShapeDtypeStruct((m, IN_WIDTH), F32),
        grid=(m // tm, IN_WIDTH // tn),
        in_specs=[pl.BlockSpec((tm, D_MODEL), lambda i, j: (i, 0)),
                  pl.BlockSpec((1, N_MOD, D_MODEL), lambda i, j: (mod_base + (i * tm) // rows_per_mod, 0, 0)),
                  pl.BlockSpec((1, D_MODEL), lambda i, j: (0, 0)),
                  pl.BlockSpec((D_MODEL, tn), lambda i, j: (0, j))],
        out_specs=pl.BlockSpec((tm, tn), lambda i, j: (i, j)),
        scratch_shapes=[pltpu.VMEM((tm, D_MODEL), BF16)],
        compiler_params=_cparams("parallel", "arbitrary"),
        name="in_projection",
    )(x, mod, gain, w_bf16)


def _hgrn_codes(reverse):
    L = HG_CHUNK
    t = lax.broadcasted_iota(jnp.int32, (L, L), 0)
    s = lax.broadcasted_iota(jnp.int32, (L, L), 1)
    code = jnp.where(t == s, 1, 0)
    h = L // 2
    while h >= 1:
        same = (t & ~(2 * h - 1)) == (s & ~(2 * h - 1))
        t_hi = (t & h) != 0
        s_hi = (s & h) != 0
        pair = (s_hi & ~t_hi) if reverse else (t_hi & ~s_hi)
        code = jnp.where(same & pair, h * 16, code)
        h //= 2
    return code


def _cumsum_rows(tri_bf16, g):
    g1 = g.astype(BF16)
    r1 = g - g1.astype(F32)
    g2 = r1.astype(BF16)
    g3 = (r1 - g2.astype(F32)).astype(BF16)
    return _dot(tri_bf16, g1) + _dot(tri_bf16, g2) + _dot(tri_bf16, g3)


def _hgrn_intra(q, k, f, b, v_bf, code, reverse):
    L = HG_CHUNK
    G = HG_DIAG
    q_bf = q.astype(BF16)
    k_bf = k.astype(BF16)

    def level(h, ref, att):
        neg_abs = pltpu.bitcast(pltpu.bitcast(b - ref, jnp.int32) | jnp.int32(-2 ** 31), F32)
        e = jnp.exp2(neg_abs.astype(BF16))
        return jnp.where(code == h * 16, _dot_nt(q_bf * e, k_bf * e), att)

    att = jnp.where(code == 1, _dot_nt(q_bf, k_bf), 0.0)
    h = L // 2
    while h >= G:
        parts = []
        for p in range(L // (2 * h)):
            m = p * 2 * h + (h if reverse else h - 1)
            parts.append(jnp.broadcast_to(b[m:m + 1, :], (2 * h, HEAD_DIM)))
        att = level(h, parts[0] if len(parts) == 1 else jnp.concatenate(parts, axis=0), att)
        h //= 2

    b3 = b.reshape(L // G, G, HEAD_DIM)

    def group_row(r):
        return jnp.broadcast_to(b3[:, r:r + 1, :], (L // G, G, HEAD_DIM)).reshape(L, HEAD_DIM)

    row = lax.broadcasted_iota(jnp.int32, (L, HEAD_DIM), 0)
    att = level(4, group_row(4 if reverse else 3), att)
    lo, hi = (2, 6) if reverse else (1, 5)
    att = level(2, jnp.where((row & 4) == 0, group_row(lo), group_row(hi)), att)
    att = jnp.where(code == 16, _dot_nt(q_bf * f.astype(BF16), k_bf), att)
    return _dot(att.astype(BF16), v_bf)


def _hgrn_prepare(q, f, v, code, tri, reverse):
    L = HG_CHUNK
    k = 1.0 - f
    b = _cumsum_rows(tri, jnp.log2(f))
    b_tot = b[0:1, :] if reverse else b[L - 1:L, :]
    v_bf = v.astype(BF16)
    o_intra = _hgrn_intra(q, k, f, b, v_bf, code, reverse)
    q_in = (q * jnp.exp2(b)).astype(BF16)
    k_out = (k * jnp.exp2(b_tot - b)).astype(BF16)
    return o_intra, q_in, k_out, v_bf, jnp.exp2(b_tot)


def _hgrn_advance(st, prepared):
    o_intra, q_in, k_out, v_bf, decay = prepared
    return o_intra + _dot_nt(q_in, st.astype(BF16)), st * decay + _dot_tn(v_bf, k_out)


def _hgrn_kernel(q_ref, ff_ref, fb_ref, v_ref, gate_ref, lb_ref, ng_ref, s0_ref,
                 o_ref, sout_ref, of_ref, ob_ref, code_ref, *, n_tok):
    L = HG_CHUNK
    nc = n_tok // L
    lb_f = lb_ref[0:1, :]
    lb_b = lb_ref[1:2, :]

    @pl.when((pl.program_id(0) == 0) & (pl.program_id(1) == 0))
    def _():
        code_ref[0] = _hgrn_codes(False)
        code_ref[1] = _hgrn_codes(True)

    code_f = code_ref[0]
    code_b = code_ref[1]
    ti = lax.broadcasted_iota(jnp.int32, (L, L), 0)
    si = lax.broadcasted_iota(jnp.int32, (L, L), 1)
    tri_f = jnp.where(si <= ti, 1.0, 0.0).astype(BF16)
    tri_b = jnp.where(si >= ti, 1.0, 0.0).astype(BF16)

    def prepare(row, f_ref, lb, code, tri, reverse):
        sl = pl.ds(row, L)
        q = _silu(q_ref[sl, :])
        f = lb + (1.0 - lb) * _sigmoid(f_ref[sl, :])
        return sl, _hgrn_prepare(q, f, v_ref[sl, :], code, tri, reverse)

    def body(c, states):
        st_f, st_b = states
        fwd, bwd = [], []
        for u in range(unroll):
            cu = c * unroll + u
            fwd.append(prepare(pl.multiple_of(cu * L, L), ff_ref, lb_f, code_f, tri_f, False))
            bwd.append(prepare(pl.multiple_of((nc - 1 - cu) * L, L), fb_ref, lb_b, code_b, tri_b, True))
        for (sl_f, prep_f), (sl_b, prep_b) in zip(fwd, bwd):
            of_ref[sl_f, :], st_f = _hgrn_advance(st_f, prep_f)
            ob_ref[sl_b, :], st_b = _hgrn_advance(st_b, prep_b)
        return st_f, st_b

    unroll = 2
    assert nc % unroll == 0
    st_f, st_b = lax.fori_loop(0, nc // unroll, body, (s0_ref[0, 0, 0, 0].T, s0_ref[0, 0, 1, 0].T))
    sout_ref[0, 0, 0, 0] = st_f.T
    sout_ref[0, 0, 1, 0] = st_b.T
    gain = ng_ref[...]

    def fin(c, carry):
        sl = pl.ds(pl.multiple_of(c * L, L), L)
        o = of_ref[sl, :] + ob_ref[sl, :]
        var = jnp.mean(o * o, axis=-1, keepdims=True)
        o_ref[sl, :] = ((o * lax.rsqrt(var + NORM_EPS) * gain) * _silu(gate_ref[sl, :])).astype(BF16)
        return carry

    lax.fori_loop(0, nc, fin, 0)


def _hgrn(proj, lb, norm_g, state, *, n_batch, n_tok):
    col = lambda k: (lambda b, h: (b, k * HG_HEADS + h))
    tok_spec = lambda k: pl.BlockSpec((n_tok, HEAD_DIM), col(k))
    st_spec = pl.BlockSpec((1, 1, 2, 1, HEAD_DIM, HEAD_DIM), lambda b, h: (b, 0, 0, h, 0, 0))
    return pl.pallas_call(
        functools.partial(_hgrn_kernel, n_tok=n_tok),
        out_shape=(jax.ShapeDtypeStruct((n_batch * n_tok, HG_WIDTH), BF16),
                   jax.ShapeDtypeStruct((n_batch, 1, 2, HG_HEADS, HEAD_DIM, HEAD_DIM), F32)),
        grid=(n_batch, HG_HEADS),
        in_specs=[tok_spec(0), tok_spec(1), tok_spec(2), tok_spec(3), tok_spec(4),
                  pl.BlockSpec((2, HEAD_DIM), lambda b, h: (0, h)),
                  pl.BlockSpec((1, HEAD_DIM), lambda b, h: (0, h)),
                  st_spec],
        out_specs=(pl.BlockSpec((n_tok, HEAD_DIM), lambda b, h: (b, h)), st_spec),
        scratch_shapes=[pltpu.VMEM((n_tok, HEAD_DIM), F32), pltpu.VMEM((n_tok, HEAD_DIM), F32),
                        pltpu.VMEM((2, HG_CHUNK, HG_CHUNK), jnp.int32)],
        compiler_params=_cparams("arbitrary", "arbitrary"),
        name="hgrn2_scan",
    )(proj, proj, proj, proj, proj, lb, norm_g, state)


def _stack_heads(x, kvh):
    return jnp.concatenate(
        [x[:, (kvh * ATT_GROUP + g) * HEAD_DIM:(kvh * ATT_GROUP + g + 1) * HEAD_DIM] for g in range(ATT_GROUP)],
        axis=0)


def _sink_column(sink_ref, kvh, rows):
    return jnp.concatenate(
        [jnp.full((rows, 1), sink_ref[kvh * ATT_GROUP + g], F32) for g in range(ATT_GROUP)],
        axis=0) * (1.0 / SOFTMAX_SCALE)


SOFTMAX_SCALE = HEAD_DIM ** -0.5
EXP2_SCALE = SOFTMAX_SCALE * 1.4426950408889634


def _ones_column(rows):
    lane = lax.broadcasted_iota(jnp.int32, (rows, HEAD_DIM), 1)
    return jnp.where(lane == 0, 1.0, 0.0).astype(BF16)


def _softmax_av(scores, values, sink_col):
    m = sink_col
    for s in scores:
        m = jnp.maximum(m, jnp.max(s, axis=-1, keepdims=True))
    acc = None
    for s, v in zip(scores, values):
        pv = _dot(jnp.exp2(((s - m) * EXP2_SCALE).astype(BF16)), v)
        acc = pv if acc is None else acc + pv
    denom = acc[:, HEAD_DIM:HEAD_DIM + 1] + jnp.exp2((sink_col - m) * EXP2_SCALE)
    return acc[:, :HEAD_DIM] / denom


def _ctx_attn_kernel(sink_ref, q_ref, k_ref, v_ref, o_ref):
    rows = q_ref.shape[0]
    q_all = q_ref[...]
    ones = _ones_column(k_ref.shape[0])
    for kvh in range(ATT_KV_HEADS):
        q = _stack_heads(q_all, kvh).astype(BF16)
        k = k_ref[:, kvh * HEAD_DIM:(kvh + 1) * HEAD_DIM].astype(BF16)
        v = jnp.concatenate([v_ref[:, kvh * HEAD_DIM:(kvh + 1) * HEAD_DIM].astype(BF16), ones], axis=1)
        o = _softmax_av([_dot_nt(q, k)], [v], _sink_column(sink_ref, kvh, rows))
        for g in range(ATT_GROUP):
            hd = kvh * ATT_GROUP + g
            o_ref[:, hd * HEAD_DIM:(hd + 1) * HEAD_DIM] = o[g * rows:(g + 1) * rows, :].astype(BF16)


def _context_attention(proj, sink, *, n_batch, n_tok):
    q_col = 5 * HG_WIDTH // ATT_WIDTH
    k_col = (5 * HG_WIDTH + ATT_WIDTH) // KV_WIDTH
    return pl.pallas_call(
        _ctx_attn_kernel,
        out_shape=jax.ShapeDtypeStruct((n_batch * n_tok, ATT_WIDTH), BF16),
        grid_spec=pltpu.PrefetchScalarGridSpec(
            num_scalar_prefetch=1,
            grid=(n_batch,),
            in_specs=[pl.BlockSpec((n_tok, ATT_WIDTH), lambda b, s: (b, q_col)),
                      pl.BlockSpec((n_tok, KV_WIDTH), lambda b, s: (b, k_col)),
                      pl.BlockSpec((n_tok, KV_WIDTH), lambda b, s: (b, k_col + 1))],
            out_specs=pl.BlockSpec((n_tok, ATT_WIDTH), lambda b, s: (b, 0))),
        compiler_params=_cparams("parallel"),
        name="context_attention",
    )(sink, proj, proj, proj)


def _rope(x, cos, sin_signed, even_group):
    partner = jnp.where(even_group, pltpu.roll(x, HEAD_DIM - ROPE_FREQS, 1), pltpu.roll(x, ROPE_FREQS, 1))
    return x * cos + partner * sin_signed


def _win_attn_kernel(sink_ref, q_ref, k_ref, v_ref, ck_ref, cv_ref, cos_ref, sin_ref, o_ref,
                     kpad_ref, vpad_ref, *, n_tok):
    blk = ATT_BLOCK
    nb = n_tok // blk
    i = pl.program_id(1)
    lane = lax.broadcasted_iota(jnp.int32, (blk, HEAD_DIM), 1)
    even_group = (lane & ROPE_FREQS) == 0

    @pl.when(i == 0)
    def _():
        kpad_ref[0:blk, :] = jnp.zeros((blk, KV_WIDTH), BF16)
        kpad_ref[blk + n_tok:2 * blk + n_tok, :] = jnp.zeros((blk, KV_WIDTH), BF16)
        vpad_ref[0:blk, :] = jnp.zeros((blk, 2 * KV_WIDTH), BF16)
        vpad_ref[blk + n_tok:2 * blk + n_tok, :] = jnp.zeros((blk, 2 * KV_WIDTH), BF16)
        ones = _ones_column(blk)

        def body(r, carry):
            src = pl.ds(pl.multiple_of(r * blk, blk), blk)
            dst = pl.ds(pl.multiple_of((r + 1) * blk, blk), blk)
            cos = cos_ref[src, :]
            sin = sin_ref[src, :]
            for kvh in range(ATT_KV_HEADS):
                cols = slice(kvh * HEAD_DIM, (kvh + 1) * HEAD_DIM)
                kpad_ref[dst, cols] = _rope(k_ref[src, cols], cos, sin, even_group).astype(BF16)
                vpad_ref[dst, 2 * kvh * HEAD_DIM:(2 * kvh + 1) * HEAD_DIM] = v_ref[src, cols].astype(BF16)
                vpad_ref[dst, (2 * kvh + 1) * HEAD_DIM:(2 * kvh + 2) * HEAD_DIM] = ones
            return carry

        lax.fori_loop(0, nb, body, 0)

    rows = pl.ds(pl.multiple_of(i * blk, blk), blk)
    cos = cos_ref[rows, :]
    sin = sin_ref[rows, :]
    band = pl.ds(pl.multiple_of(i * blk, blk), 3 * blk)
    r = lax.broadcasted_iota(jnp.int32, (blk, 3 * blk), 0)
    j = lax.broadcasted_iota(jnp.int32, (blk, 3 * blk), 1)
    kpos = j + (i - 1) * blk
    valid = (j >= r) & (j <= r + 2 * blk) & (kpos >= 0) & (kpos < n_tok)
    valid = jnp.concatenate([valid] * ATT_GROUP, axis=0)
    q_all = q_ref[...]
    ctx_ones = _ones_column(ck_ref.shape[1])
    for kvh in range(ATT_KV_HEADS):
        cols = slice(kvh * HEAD_DIM, (kvh + 1) * HEAD_DIM)
        q = jnp.concatenate(
            [_rope(q_all[:, (kvh * ATT_GROUP + g) * HEAD_DIM:(kvh * ATT_GROUP + g + 1) * HEAD_DIM],
                   cos, sin, even_group) for g in range(ATT_GROUP)], axis=0).astype(BF16)
        s_ctx = _dot_nt(q, ck_ref[0, :, cols].astype(BF16))
        s_loc = jnp.where(valid, _dot_nt(q, kpad_ref[band, cols]), -jnp.inf)
        v_ctx = jnp.concatenate([cv_ref[0, :, cols].astype(BF16), ctx_ones], axis=1)
        v_loc = vpad_ref[band, 2 * kvh * HEAD_DIM:(2 * kvh + 2) * HEAD_DIM]
        o = _softmax_av([s_ctx, s_loc], [v_ctx, v_loc], _sink_column(sink_ref, kvh, blk))
        for g in range(ATT_GROUP):
            hd = kvh * ATT_GROUP + g
            o_ref[:, hd * HEAD_DIM:(hd + 1) * HEAD_DIM] = o[g * blk:(g + 1) * blk, :].astype(BF16)


def _window_attention(proj, cache_k, cache_v, sink, cos, sin_signed, *, n_batch, n_tok):
    nb = n_tok // ATT_BLOCK
    n_ctx = cache_k.shape[1]
    q_col = 5 * HG_WIDTH // ATT_WIDTH
    k_col = (5 * HG_WIDTH + ATT_WIDTH) // KV_WIDTH
    table = pl.BlockSpec((n_tok, HEAD_DIM), lambda b, i, s: (0, 0))
    cache = pl.BlockSpec((1, n_ctx, KV_WIDTH), lambda b, i, s: (b, 0, 0))
    return pl.pallas_call(
        functools.partial(_win_attn_kernel, n_tok=n_tok),
        out_shape=jax.ShapeDtypeStruct((n_batch * n_tok, ATT_WIDTH), BF16),
        grid_spec=pltpu.PrefetchScalarGridSpec(
            num_scalar_prefetch=1,
            grid=(n_batch, nb),
            in_specs=[pl.BlockSpec((ATT_BLOCK, ATT_WIDTH), lambda b, i, s: (b * nb + i, q_col)),
                      pl.BlockSpec((n_tok, KV_WIDTH), lambda b, i, s: (b, k_col)),
                      pl.BlockSpec((n_tok, KV_WIDTH), lambda b, i, s: (b, k_col + 1)),
                      cache, cache, table, table],
            out_specs=pl.BlockSpec((ATT_BLOCK, ATT_WIDTH), lambda b, i, s: (b * nb + i, 0)),
            scratch_shapes=[pltpu.VMEM((n_tok + 2 * ATT_BLOCK, KV_WIDTH), BF16),
                            pltpu.VMEM((n_tok + 2 * ATT_BLOCK, 2 * KV_WIDTH), BF16)]),
        compiler_params=_cparams("parallel", "arbitrary"),
        name="window_attention",
    )(sink, proj, proj, proj, cache_k, cache_v, cos, sin_signed)


def _rope_tables(n_tok):
    rows = n_tok // GRID_W
    row = jnp.repeat(jnp.arange(rows), GRID_W).astype(F32)
    col = jnp.tile(jnp.arange(GRID_W), rows).astype(F32)
    inv = ROPE_BASE ** (-jnp.arange(ROPE_FREQS, dtype=F32) / ROPE_FREQS)
    ar, ac = row[:, None] * inv, col[:, None] * inv
    cr, sr, cc, sc = jnp.cos(ar), jnp.sin(ar), jnp.cos(ac), jnp.sin(ac)
    return (jnp.concatenate([cr, cr, cc, cc], axis=1), jnp.concatenate([-sr, sr, -sc, sc], axis=1))


def _outproj_kernel(hg_ref, att_ref, w_ref, x_ref, mod_ref, g_ref, xo_ref, h_ref):
    mix = _dot(hg_ref[...], w_ref[0:HG_WIDTH, :]) + _dot(att_ref[...], w_ref[HG_WIDTH:, :])
    x = x_ref[...] + mod_ref[0, 2:3, :] * mix
    xo_ref[...] = x
    h_ref[...] = _norm_modulate(x, g_ref[...], mod_ref[0, 3:4, :], mod_ref[0, 4:5, :]).astype(BF16)


def _out_projection(o_hg, o_att, w_bf16, x, mod, gain, *, mod_base, rows_per_mod):
    m = x.shape[0]
    tm = 256
    row = lambda i: (i, 0)
    return pl.pallas_call(
        _outproj_kernel,
        out_shape=(jax.ShapeDtypeStruct((m, D_MODEL), F32), jax.ShapeDtypeStruct((m, D_MODEL), BF16)),
        grid=(m // tm,),
        in_specs=[pl.BlockSpec((tm, HG_WIDTH), row),
                  pl.BlockSpec((tm, ATT_WIDTH), row),
                  pl.BlockSpec((HG_WIDTH + ATT_WIDTH, D_MODEL), lambda i: (0, 0)),
                  pl.BlockSpec((tm, D_MODEL), row),
                  pl.BlockSpec((1, N_MOD, D_MODEL), lambda i: (mod_base + (i * tm) // rows_per_mod, 0, 0)),
                  pl.BlockSpec((1, D_MODEL), lambda i: (0, 0))],
        out_specs=(pl.BlockSpec((tm, D_MODEL), row), pl.BlockSpec((tm, D_MODEL), row)),
        compiler_params=_cparams("parallel"),
        name="out_projection",
    )(o_hg, o_att, w_bf16, x, mod, gain)


def _prefix_count(x):
    n = x.shape[1]
    i = lax.broadcasted_iota(jnp.int32, (SEL_BLOCK, SEL_BLOCK), 0)
    j = lax.broadcasted_iota(jnp.int32, (SEL_BLOCK, SEL_BLOCK), 1)
    upper = jnp.where(i < j, 1.0, 0.0).astype(BF16)
    off = jnp.zeros((x.shape[0], 1), F32)
    outs = []
    for blk in range(n // SEL_BLOCK):
        xb = x[:, blk * SEL_BLOCK:(blk + 1) * SEL_BLOCK]
        outs.append(_dot(xb.astype(BF16), upper) + off)
        off = off + jnp.sum(xb, axis=-1, keepdims=True)
    return outs[0] if len(outs) == 1 else jnp.concatenate(outs, axis=1)


def _route_select(h_ref, w_ref, slot_t_ref, cap):
    n_tok = h_ref.shape[0]
    logits = _dot_nt(w_ref[...], h_ref[...])
    ex = jnp.exp(logits - jnp.max(logits, axis=0, keepdims=True))
    aff = ex / jnp.sum(ex, axis=0, keepdims=True)
    bits = pltpu.bitcast(aff, jnp.int32)

    def bisect(it, thr):
        cand = thr | jnp.left_shift(jnp.int32(1), 30 - it)
        cnt = jnp.sum(jnp.where(bits >= cand, 1.0, 0.0), axis=-1, keepdims=True)
        return jnp.where(cnt >= cap, cand, thr)

    thr = lax.fori_loop(0, 31, bisect, jnp.zeros((N_EXPERTS, 1), jnp.int32))
    above = jnp.where(bits > thr, 1.0, 0.0)
    tied = jnp.where(bits == thr, 1.0, 0.0)
    room = cap - jnp.sum(above, axis=-1, keepdims=True)
    sel = above + tied * jnp.where(_prefix_count(tied) < room, 1.0, 0.0)
    slot = jnp.where(sel > 0.0, _prefix_count(sel), -1.0)
    i = lax.broadcasted_iota(jnp.int32, (SEL_BLOCK, SEL_BLOCK), 0)
    j = lax.broadcasted_iota(jnp.int32, (SEL_BLOCK, SEL_BLOCK), 1)
    eye = jnp.where(i == j, 1.0, 0.0).astype(BF16)
    for blk in range(n_tok // SEL_BLOCK):
        cols = slice(blk * SEL_BLOCK, (blk + 1) * SEL_BLOCK)
        slot_t_ref[0, cols, :] = _dot_nt(eye, slot[:, cols].astype(BF16))
    return slot, aff


def _one_hot_gather(slot_rows, aff_rows, h_ref, rows, cap):
    n_tok = h_ref.shape[0]
    c = (lax.broadcasted_iota(jnp.int32, (rows, n_tok), 0) & (cap - 1)).astype(F32)
    hit = c == slot_rows
    x = _dot(jnp.where(hit, 1.0, 0.0).astype(BF16), h_ref[...]).astype(BF16)
    return x, jnp.sum(jnp.where(hit, aff_rows, 0.0), axis=-1, keepdims=True)


def _route_request_kernel(h_ref, w_ref, x_ref, g_ref, slot_t_ref, *, cap):
    n_tok = h_ref.shape[0]
    slot, aff = _route_select(h_ref, w_ref, slot_t_ref, cap)
    per_row = lambda a: jnp.concatenate(
        [jnp.broadcast_to(a[e:e + 1, :], (cap, n_tok)) for e in range(N_EXPERTS)], axis=0)
    x, g = _one_hot_gather(per_row(slot), per_row(aff), h_ref, N_EXPERTS * cap, cap)
    for e in range(N_EXPERTS):
        x_ref[e] = x[e * cap:(e + 1) * cap, :]
        g_ref[e] = g[e * cap:(e + 1) * cap, :]


def _route_expert_kernel(h_ref, w_ref, xh_ref, gh_ref, x_ref, g_ref, slot_t_ref, slot_ref, aff_ref, *,
                         cap, n_batch):
    b = pl.program_id(0)
    e = pl.program_id(1)

    @pl.when((b < n_batch) & (e == 0))
    def _():
        slot, aff = _route_select(h_ref, w_ref, slot_t_ref, cap)
        slot_ref[...] = slot
        aff_ref[...] = aff

    @pl.when(b < n_batch)
    def _():
        row = pl.ds(e, 1)
        x_ref[0], g_ref[0] = _one_hot_gather(slot_ref[row, :], aff_ref[row, :], h_ref, cap, cap)

    @pl.when(b >= n_batch)
    def _():
        x_ref[0] = xh_ref[0]
        g_ref[0] = gh_ref[0]


def _route_gather_requests(h, w_router_t, *, n_batch, n_tok):
    cap = CAPACITY_FACTOR * n_tok // N_EXPERTS
    out_block = lambda w: pl.BlockSpec((N_EXPERTS, cap, w), lambda b: (0, b, 0))
    return pl.pallas_call(
        functools.partial(_route_request_kernel, cap=cap),
        out_shape=(jax.ShapeDtypeStruct((N_EXPERTS, n_batch * cap, D_MODEL), BF16),
                   jax.ShapeDtypeStruct((N_EXPERTS, n_batch * cap, 1), F32),
                   jax.ShapeDtypeStruct((n_batch, n_tok, N_EXPERTS), F32)),
        grid=(n_batch,),
        in_specs=[pl.BlockSpec((n_tok, D_MODEL), lambda b: (b, 0)),
                  pl.BlockSpec((N_EXPERTS, D_MODEL), lambda b: (0, 0))],
        out_specs=(out_block(D_MODEL), out_block(1), pl.BlockSpec((1, n_tok, N_EXPERTS), lambda b: (b, 0, 0))),
        compiler_params=_cparams("parallel"),
        name="route_gather_requests",
    )(h, w_router_t)


def _route_gather_experts(h, w_router_t, x_head, g_head, *, n_batch, n_tok):
    cap = CAPACITY_FACTOR * n_tok // N_EXPERTS
    head_rows = x_head.shape[1]
    assert head_rows % cap == 0
    n_head = head_rows // cap
    rows_total = head_rows + n_batch * cap
    last = n_batch - 1
    out_row = lambda b: jnp.where(b < n_batch, n_head + b, b - n_batch)
    head_row = lambda b: jnp.maximum(b - n_batch, 0)
    out_block = lambda w: pl.BlockSpec((1, cap, w), lambda b, e: (e, out_row(b), 0))
    head_block = lambda w: pl.BlockSpec((1, cap, w), lambda b, e: (e, head_row(b), 0))
    return pl.pallas_call(
        functools.partial(_route_expert_kernel, cap=cap, n_batch=n_batch),
        out_shape=(jax.ShapeDtypeStruct((N_EXPERTS, rows_total, D_MODEL), BF16),
                   jax.ShapeDtypeStruct((N_EXPERTS, rows_total, 1), F32),
                   jax.ShapeDtypeStruct((n_batch, n_tok, N_EXPERTS), F32)),
        grid=(n_batch + n_head, N_EXPERTS),
        in_specs=[pl.BlockSpec((n_tok, D_MODEL), lambda b, e: (jnp.minimum(b, last), 0)),
                  pl.BlockSpec((N_EXPERTS, D_MODEL), lambda b, e: (0, 0)),
                  head_block(D_MODEL), head_block(1)],
        out_specs=(out_block(D_MODEL), out_block(1),
                   pl.BlockSpec((1, n_tok, N_EXPERTS), lambda b, e: (jnp.minimum(b, last), 0, 0))),
        scratch_shapes=[pltpu.VMEM((N_EXPERTS, n_tok), F32), pltpu.VMEM((N_EXPERTS, n_tok), F32)],
        compiler_params=_cparams("arbitrary", "arbitrary"),
        name="route_gather_experts",
    )(h, w_router_t, x_head, g_head)


def _moe_kernel(x_ref, g_ref, wg_ref, wu_ref, wd_ref, y_ref, hid_ref, *, n_ff, rows):
    s = pl.program_id(2)
    tf = wg_ref.shape[2]

    @pl.when(s < n_ff)
    def _():
        wg = wg_ref[0].astype(BF16)
        wu = wu_ref[0].astype(BF16)
        cols = pl.ds(pl.multiple_of(s * tf, tf), tf)
        for r in range(x_ref.shape[1] // rows):
            sl = slice(r * rows, (r + 1) * rows)
            x = x_ref[0, sl, :]
            hid_ref[sl, cols] = (_silu(_dot(x, wg)) * _dot(x, wu)).astype(BF16)

    @pl.when(s >= n_ff)
    def _():
        y = _dot(hid_ref[...], wd_ref[0].astype(BF16))
        y_ref[0] = (y * g_ref[0]).astype(BF16)


def _experts(x, gate, w_gate, w_up, w_down):
    n_rows = x.shape[1]
    tr, tf, tn = n_rows // 2, 512, 256
    n_ff, n_out = EXPERT_FF // tf, D_MODEL // tn
    up_tile = lambda e, r, s: (e, 0, jnp.minimum(s, n_ff - 1))
    out_tile = lambda s: jnp.maximum(s - n_ff, 0)
    return pl.pallas_call(
        functools.partial(_moe_kernel, n_ff=n_ff, rows=256),
        out_shape=jax.ShapeDtypeStruct((N_EXPERTS, n_rows, D_MODEL), BF16),
        grid=(N_EXPERTS, n_rows // tr, n_ff + n_out),
        in_specs=[pl.BlockSpec((1, tr, D_MODEL), lambda e, r, s: (e, r, 0), pipeline_mode=pl.Buffered(1)),
                  pl.BlockSpec((1, tr, 1), lambda e, r, s: (e, r, 0)),
                  pl.BlockSpec((1, D_MODEL, tf), up_tile),
                  pl.BlockSpec((1, D_MODEL, tf), up_tile),
                  pl.BlockSpec((1, EXPERT_FF, tn), lambda e, r, s: (e, 0, out_tile(s)))],
        out_specs=pl.BlockSpec((1, tr, tn), lambda e, r, s: (e, r, out_tile(s))),
        scratch_shapes=[pltpu.VMEM((tr, EXPERT_FF), BF16)],
        compiler_params=_cparams("parallel", "parallel", "arbitrary"),
        name="expert_swiglu",
    )(x, gate, w_gate, w_up, w_down)


def _combine_kernel(y_ref, slot_ref, x_ref, mod_ref, g_ref, o_ref):
    tt = x_ref.shape[0]
    cap = y_ref.shape[1]
    c = lax.broadcasted_iota(jnp.int32, (tt, cap), 1).astype(F32)
    slots = slot_ref[0]
    acc = jnp.zeros((tt, D_MODEL), F32)
    for e in range(N_EXPERTS):
        hit = c == slots[:, e:e + 1]
        acc = acc + _dot(jnp.where(hit, 1.0, 0.0).astype(BF16), y_ref[e])
    x = x_ref[...] + mod_ref[0, 5:6, :] * acc
    var = jnp.mean(x * x, axis=-1, keepdims=True)
    o_ref[...] = x * lax.rsqrt(var + NORM_EPS) * g_ref[...]


def _combine(y, slot_t, x_mid, mod, final_g, *, n_batch, n_tok, row_block_off, mod_base, mod_per_batch):
    cap = CAPACITY_FACTOR * n_tok // N_EXPERTS
    tt = min(n_tok, 256)
    nt = n_tok // tt
    return pl.pallas_call(
        _combine_kernel,
        out_shape=jax.ShapeDtypeStruct((n_batch * n_tok, D_MODEL), F32),
        grid=(n_batch, nt),
        in_specs=[pl.BlockSpec((N_EXPERTS, cap, D_MODEL), lambda b, t: (0, row_block_off + b, 0)),
                  pl.BlockSpec((1, tt, N_EXPERTS), lambda b, t: (b, t, 0)),
                  pl.BlockSpec((tt, D_MODEL), lambda b, t: (b * nt + t, 0)),
                  pl.BlockSpec((1, N_MOD, D_MODEL), lambda b, t: (mod_base + b * mod_per_batch, 0, 0)),
                  pl.BlockSpec((1, D_MODEL), lambda b, t: (0, 0))],
        out_specs=pl.BlockSpec((tt, D_MODEL), lambda b, t: (b * nt + t, 0)),
        compiler_params=_cparams("parallel", "arbitrary"),
        name="combine_final_norm",
    )(y, slot_t, x_mid, mod, final_g)


def kernel(x_prompt, x_sample, cache_k, cache_v, state_hgrn, c, c_ctx, w_ada, b_ada, norm1_g, w_in, hg_lb,
           hg_norm_g, attn_sink, w_out, norm2_g, w_router, w_gate, w_up, w_down, final_norm_g):
    n_p, t_p, _ = x_prompt.shape
    n_s, t_s, _ = x_sample.shape
    assert w_ada.shape[0] == 1 and 1 + n_s <= COND_ROWS
    layer = 0

    cond = jnp.zeros((COND_ROWS, D_MODEL), F32).at[0].set(c_ctx).at[1:1 + n_s].set(c)
    mod = _ada_modulation(cond, w_ada[layer], b_ada[layer]).reshape(COND_ROWS, N_MOD, D_MODEL)
    lb = jnp.cumsum(jax.nn.softmax(hg_lb.astype(F32), axis=0), axis=0)[layer]
    w_in_b = w_in[layer].astype(BF16)
    w_out_b = w_out[layer].astype(BF16)
    w_router_t = w_router[layer].T.astype(BF16)
    norm1 = norm1_g[layer].reshape(1, D_MODEL)
    norm2 = norm2_g[layer].reshape(1, D_MODEL)
    hg_gain = hg_norm_g[layer].reshape(1, HG_WIDTH)
    final_g = final_norm_g.reshape(1, D_MODEL)
    sink = attn_sink[layer]
    cos, sin_signed = _rope_tables(t_s)

    xp = x_prompt.reshape(n_p * t_p, D_MODEL)
    xs = x_sample.reshape(n_s * t_s, D_MODEL)
    groups = dict(p=dict(mod_base=0, rows_per_mod=n_p * t_p), s=dict(mod_base=1, rows_per_mod=t_s))

    proj_p = _in_projection(xp, mod, norm1, w_in_b, **groups["p"])
    proj_s = _in_projection(xs, mod, norm1, w_in_b, **groups["s"])

    zero_state = jnp.zeros((n_p, 1, 2, HG_HEADS, HEAD_DIM, HEAD_DIM), F32)
    ohg_p, new_state = _hgrn(proj_p, lb, hg_gain, zero_state, n_batch=n_p, n_tok=t_p)
    ohg_s, _ = _hgrn(proj_s, lb, hg_gain, state_hgrn[:, layer:layer + 1].astype(F32), n_batch=n_s, n_tok=t_s)

    oatt_p = _context_attention(proj_p, sink, n_batch=n_p, n_tok=t_p)
    n_ctx = cache_k.shape[2]
    oatt_s = _window_attention(proj_s, cache_k[:, layer].reshape(n_s, n_ctx, KV_WIDTH),
                               cache_v[:, layer].reshape(n_s, n_ctx, KV_WIDTH), sink, cos, sin_signed,
                               n_batch=n_s, n_tok=t_s)

    xmid_p, h2_p = _out_projection(ohg_p, oatt_p, w_out_b, xp, mod, norm2, **groups["p"])
    xmid_s, h2_s = _out_projection(ohg_s, oatt_s, w_out_b, xs, mod, norm2, **groups["s"])

    cap_p = CAPACITY_FACTOR * t_p // N_EXPERTS
    cap_s = CAPACITY_FACTOR * t_s // N_EXPERTS
    off_s = n_p * cap_p // cap_s
    xg_p, gate_p, slot_p = _route_gather_requests(h2_p, w_router_t, n_batch=n_p, n_tok=t_p)
    xg, gate, slot_s = _route_gather_experts(h2_s, w_router_t, xg_p, gate_p, n_batch=n_s, n_tok=t_s)

    y = _experts(xg, gate, w_gate[layer], w_up[layer], w_down[layer])

    y_prompt = _combine(y, slot_p, xmid_p, mod, final_g, n_batch=n_p, n_tok=t_p, row_block_off=0,
                        mod_base=0, mod_per_batch=0)
    y_sample = _combine(y, slot_s, xmid_s, mod, final_g, n_batch=n_s, n_tok=t_s, row_block_off=off_s,
                        mod_base=1, mod_per_batch=1)

    k_col = 5 * HG_WIDTH + ATT_WIDTH
    new_k = proj_p[:, k_col:k_col + KV_WIDTH].reshape(n_p, 1, t_p, ATT_KV_HEADS, HEAD_DIM)
    new_v = proj_p[:, k_col + KV_WIDTH:k_col + 2 * KV_WIDTH].reshape(n_p, 1, t_p, ATT_KV_HEADS, HEAD_DIM)
    return (y_prompt.reshape(n_p, t_p, D_MODEL), y_sample.reshape(n_s, t_s, D_MODEL), new_k, new_v, new_state)
```

```python
import functools

import jax
import jax.numpy as jnp
from jax import lax
from jax.experimental import pallas as pl
from jax.experimental.pallas import tpu as pltpu

F32 = jnp.float32
BF16 = jnp.bfloat16

D_MODEL = 2048
HG_WIDTH = 1024
HG_HEADS = 8
HEAD_DIM = 128
ATT_HEADS = 8
ATT_KV_HEADS = 2
ATT_GROUP = ATT_HEADS // ATT_KV_HEADS
KV_WIDTH = ATT_KV_HEADS * HEAD_DIM
ATT_WIDTH = ATT_HEADS * HEAD_DIM
ATT_BLOCK = 128
GRID_W = 64
ROPE_BASE = 10000.0
ROPE_FREQS = HEAD_DIM // 4
N_EXPERTS = 16
CAPACITY_FACTOR = 2
EXPERT_FF = 5632
NORM_EPS = 1e-6
IN_WIDTH = 5 * HG_WIDTH + ATT_WIDTH + 2 * KV_WIDTH
N_MOD = 6
COND_ROWS = 16

HG_CHUNK = 128
HG_DIAG = 8
SEL_BLOCK = 256
TOK_BLOCK = 256
PIECE = 16

VMEM_LIMIT = 56 * 1024 * 1024


def _cparams(*sem):
    return pltpu.CompilerParams(dimension_semantics=sem, vmem_limit_bytes=VMEM_LIMIT)


def _sigmoid(x):
    return 1.0 / (1.0 + jnp.exp(-x))


def _silu(x):
    return x * _sigmoid(x)


def _dot(a, b):
    return jnp.dot(a, b, preferred_element_type=F32)


def _dot_nt(a, b):
    return lax.dot_general(a, b, (((1,), (1,)), ((), ())), preferred_element_type=F32)


def _dot_tn(a, b):
    return lax.dot_general(a, b, (((0,), (0,)), ((), ())), preferred_element_type=F32)


def _ada_kernel(c_ref, w_ref, b_ref, o_ref):
    s = _silu(c_ref[...]).astype(BF16)
    o_ref[...] = _dot(s, w_ref[...].astype(BF16)) + b_ref[...]


def _ada_modulation(cond, w_ada, b_ada):
    tn = 1024
    n = w_ada.shape[1]
    return pl.pallas_call(
        _ada_kernel,
        out_shape=jax.ShapeDtypeStruct((COND_ROWS, n), F32),
        grid=(n // tn,),
        in_specs=[pl.BlockSpec((COND_ROWS, D_MODEL), lambda j: (0, 0)),
                  pl.BlockSpec((D_MODEL, tn), lambda j: (0, j)),
                  pl.BlockSpec((1, tn), lambda j: (0, j))],
        out_specs=pl.BlockSpec((COND_ROWS, tn), lambda j: (0, j)),
        compiler_params=_cparams("arbitrary"),
        name="ada_modulation",
    )(cond, w_ada, b_ada.reshape(1, n))


def _norm_modulate(x, gain, shift, scale):
    var = jnp.mean(x * x, axis=-1, keepdims=True)
    return (x * lax.rsqrt(var + NORM_EPS) * gain) * (1.0 + scale) + shift


def _inproj_kernel(x_ref, mod_ref, g_ref, w_ref, o_ref, h_ref, *, rows):
    @pl.when(pl.program_id(1) == 0)
    def _():
        shift = mod_ref[0, 0:1, :]
        scale = mod_ref[0, 1:2, :]
        gain = g_ref[...]

        def body(r, carry):
            sl = pl.ds(pl.multiple_of(r * rows, rows), rows)
            h_ref[sl, :] = _norm_modulate(x_ref[sl, :], gain, shift, scale).astype(BF16)
            return carry

        lax.fori_loop(0, x_ref.shape[0] // rows, body, 0)

    w = w_ref[...]
    chunk = 256
    for r in range(x_ref.shape[0] // chunk):
        sl = slice(r * chunk, (r + 1) * chunk)
        o_ref[sl, :] = _dot(h_ref[sl, :], w)


def _in_projection(x, mod, gain, w_bf16, *, mod_base, rows_per_mod):
    m = x.shape[0]
    tm, tn = 1024, 512
    return pl.pallas_call(
        functools.partial(_inproj_kernel, rows=128),
        out_shape=jax.ShapeDtypeStruct((m, IN_WIDTH), F32),
        grid=(m // tm, IN_WIDTH // tn),
        in_specs=[pl.BlockSpec((tm, D_MODEL), lambda i, j: (i, 0)),
                  pl.BlockSpec((1, N_MOD, D_MODEL), lambda i, j: (mod_base + (i * tm) // rows_per_mod, 0, 0)),
                  pl.BlockSpec((1, D_MODEL), lambda i, j: (0, 0)),
                  pl.BlockSpec((D_MODEL, tn), lambda i, j: (0, j))],
        out_specs=pl.BlockSpec((tm, tn), lambda i, j: (i, j)),
        scratch_shapes=[pltpu.VMEM((tm, D_MODEL), BF16)],
        compiler_params=_cparams("parallel", "arbitrary"),
        name="in_projection",
    )(x, mod, gain, w_bf16)


def _hgrn_codes(reverse):
    L = HG_CHUNK
    t = lax.broadcasted_iota(jnp.int32, (L, L), 0)
    s = lax.broadcasted_iota(jnp.int32, (L, L), 1)
    code = jnp.where(t == s, 1, 0)
    h = L // 2
    while h >= 1:
        same = (t & ~(2 * h - 1)) == (s & ~(2 * h - 1))
        t_hi = (t & h) != 0
        s_hi = (s & h) != 0
        pair = (s_hi & ~t_hi) if reverse else (t_hi & ~s_hi)
        code = jnp.where(same & pair, h * 16, code)
        h //= 2
    return code


def _cumsum_rows(tri_bf16, g):
    g1 = g.astype(BF16)
    r1 = g - g1.astype(F32)
    g2 = r1.astype(BF16)
    g3 = (r1 - g2.astype(F32)).astype(BF16)
    return _dot(tri_bf16, g1) + _dot(tri_bf16, g2) + _dot(tri_bf16, g3)


def _hgrn_intra(q, k, f, b, v_bf, code, reverse):
    L = HG_CHUNK
    G = HG_DIAG
    q_bf = q.astype(BF16)
    k_bf = k.astype(BF16)

    def level(h, ref, att):
        neg_abs = pltpu.bitcast(pltpu.bitcast(b - ref, jnp.int32) | jnp.int32(-2 ** 31), F32)
        e = jnp.exp2(neg_abs.astype(BF16))
        return jnp.where(code == h * 16, _dot_nt(q_bf * e, k_bf * e), att)

    att = jnp.where(code == 1, _dot_nt(q_bf, k_bf), 0.0)
    h = L // 2
    while h >= G:
        parts = []
        for p in range(L // (2 * h)):
            m = p * 2 * h + (h if reverse else h - 1)
            parts.append(jnp.broadcast_to(b[m:m + 1, :], (2 * h, HEAD_DIM)))
        att = level(h, parts[0] if len(parts) == 1 else jnp.concatenate(parts, axis=0), att)
        h //= 2

    b3 = b.reshape(L // G, G, HEAD_DIM)

    def group_row(r):
        return jnp.broadcast_to(b3[:, r:r + 1, :], (L // G, G, HEAD_DIM)).reshape(L, HEAD_DIM)

    row = lax.broadcasted_iota(jnp.int32, (L, HEAD_DIM), 0)
    att = level(4, group_row(4 if reverse else 3), att)
    lo, hi = (2, 6) if reverse else (1, 5)
    att = level(2, jnp.where((row & 4) == 0, group_row(lo), group_row(hi)), att)
    att = jnp.where(code == 16, _dot_nt(q_bf * f.astype(BF16), k_bf), att)
    return _dot(att.astype(BF16), v_bf)


def _hgrn_prepare(q, f, v, code, tri, reverse):
    L = HG_CHUNK
    k = 1.0 - f
    b = _cumsum_rows(tri, jnp.log2(f))
    b_tot = b[0:1, :] if reverse else b[L - 1:L, :]
    v_bf = v.astype(BF16)
    o_intra = _hgrn_intra(q, k, f, b, v_bf, code, reverse)
    q_in = (q * jnp.exp2(b)).astype(BF16)
    k_out = (k * jnp.exp2(b_tot - b)).astype(BF16)
    return o_intra, q_in, k_out, v_bf, jnp.exp2(b_tot)


def _hgrn_advance(st, prepared):
    o_intra, q_in, k_out, v_bf, decay = prepared
    return o_intra + _dot_nt(q_in, st.astype(BF16)), st * decay + _dot_tn(v_bf, k_out)


def _hgrn_kernel(q_ref, ff_ref, fb_ref, v_ref, gate_ref, lb_ref, ng_ref, s0_ref,
                 o_ref, sout_ref, of_ref, ob_ref, code_ref, *, n_tok):
    L = HG_CHUNK
    nc = n_tok // L
    lb_f = lb_ref[0:1, :]
    lb_b = lb_ref[1:2, :]

    @pl.when((pl.program_id(0) == 0) & (pl.program_id(1) == 0))
    def _():
        code_ref[0] = _hgrn_codes(False)
        code_ref[1] = _hgrn_codes(True)

    code_f = code_ref[0]
    code_b = code_ref[1]
    ti = lax.broadcasted_iota(jnp.int32, (L, L), 0)
    si = lax.broadcasted_iota(jnp.int32, (L, L), 1)
    tri_f = jnp.where(si <= ti, 1.0, 0.0).astype(BF16)
    tri_b = jnp.where(si >= ti, 1.0, 0.0).astype(BF16)

    def prepare(row, f_ref, lb, code, tri, reverse):
        sl = pl.ds(row, L)
        q = _silu(q_ref[sl, :])
        f = lb + (1.0 - lb) * _sigmoid(f_ref[sl, :])
        return sl, _hgrn_prepare(q, f, v_ref[sl, :], code, tri, reverse)

    def body(c, states):
        st_f, st_b = states
        fwd, bwd = [], []
        for u in range(unroll):
            cu = c * unroll + u
            fwd.append(prepare(pl.multiple_of(cu * L, L), ff_ref, lb_f, code_f, tri_f, False))
            bwd.append(prepare(pl.multiple_of((nc - 1 - cu) * L, L), fb_ref, lb_b, code_b, tri_b, True))
        for (sl_f, prep_f), (sl_b, prep_b) in zip(fwd, bwd):
            of_ref[sl_f, :], st_f = _hgrn_advance(st_f, prep_f)
            ob_ref[sl_b, :], st_b = _hgrn_advance(st_b, prep_b)
        return st_f, st_b

    unroll = 2
    assert nc % unroll == 0
    st_f, st_b = lax.fori_loop(0, nc // unroll, body, (s0_ref[0, 0, 0, 0].T, s0_ref[0, 0, 1, 0].T))
    sout_ref[0, 0, 0, 0] = st_f.T
    sout_ref[0, 0, 1, 0] = st_b.T
    gain = ng_ref[...]

    def fin(c, carry):
        sl = pl.ds(pl.multiple_of(c * L, L), L)
        o = of_ref[sl, :] + ob_ref[sl, :]
        var = jnp.mean(o * o, axis=-1, keepdims=True)
        o_ref[sl, :] = ((o * lax.rsqrt(var + NORM_EPS) * gain) * _silu(gate_ref[sl, :])).astype(BF16)
        return carry

    lax.fori_loop(0, nc, fin, 0)


def _hgrn(proj, lb, norm_g, state, *, n_batch, n_tok):
    col = lambda k: (lambda b, h: (b, k * HG_HEADS + h))
    tok_spec = lambda k: pl.BlockSpec((n_tok, HEAD_DIM), col(k))
    st_spec = pl.BlockSpec((1, 1, 2, 1, HEAD_DIM, HEAD_DIM), lambda b, h: (b, 0, 0, h, 0, 0))
    return pl.pallas_call(
        functools.partial(_hgrn_kernel, n_tok=n_tok),
        out_shape=(jax.ShapeDtypeStruct((n_batch * n_tok, HG_WIDTH), BF16),
                   jax.ShapeDtypeStruct((n_batch, 1, 2, HG_HEADS, HEAD_DIM, HEAD_DIM), F32)),
        grid=(n_batch, HG_HEADS),
        in_specs=[tok_spec(0), tok_spec(1), tok_spec(2), tok_spec(3), tok_spec(4),
                  pl.BlockSpec((2, HEAD_DIM), lambda b, h: (0, h)),
                  pl.BlockSpec((1, HEAD_DIM), lambda b, h: (0, h)),
                  st_spec],
        out_specs=(pl.BlockSpec((n_tok, HEAD_DIM), lambda b, h: (b, h)), st_spec),
        scratch_shapes=[pltpu.VMEM((n_tok, HEAD_DIM), F32), pltpu.VMEM((n_tok, HEAD_DIM), F32),
                        pltpu.VMEM((2, HG_CHUNK, HG_CHUNK), jnp.int32)],
        compiler_params=_cparams("arbitrary", "arbitrary"),
        name="hgrn2_scan",
    )(proj, proj, proj, proj, proj, lb, norm_g, state)


def _stack_heads(x, kvh):
    return jnp.concatenate(
        [x[:, (kvh * ATT_GROUP + g) * HEAD_DIM:(kvh * ATT_GROUP + g + 1) * HEAD_DIM] for g in range(ATT_GROUP)],
        axis=0)


def _sink_column(sink_ref, kvh, rows):
    return jnp.concatenate(
        [jnp.full((rows, 1), sink_ref[kvh * ATT_GROUP + g], F32) for g in range(ATT_GROUP)],
        axis=0) * (1.0 / SOFTMAX_SCALE)


SOFTMAX_SCALE = HEAD_DIM ** -0.5
EXP2_SCALE = SOFTMAX_SCALE * 1.4426950408889634


def _ones_column(rows):
    lane = lax.broadcasted_iota(jnp.int32, (rows, HEAD_DIM), 1)
    return jnp.where(lane == 0, 1.0, 0.0).astype(BF16)


def _softmax_av(scores, values, sink_col):
    m = sink_col
    for s in scores:
        m = jnp.maximum(m, jnp.max(s, axis=-1, keepdims=True))
    acc = None
    for s, v in zip(scores, values):
        pv = _dot(jnp.exp2(((s - m) * EXP2_SCALE).astype(BF16)), v)
        acc = pv if acc is None else acc + pv
    denom = acc[:, HEAD_DIM:HEAD_DIM + 1] + jnp.exp2((sink_col - m) * EXP2_SCALE)
    return acc[:, :HEAD_DIM] / denom


def _ctx_attn_kernel(sink_ref, q_ref, k_ref, v_ref, o_ref):
    rows = q_ref.shape[0]
    q_all = q_ref[...]
    ones = _ones_column(k_ref.shape[0])
    for kvh in range(ATT_KV_HEADS):
        q = _stack_heads(q_all, kvh).astype(BF16)
        k = k_ref[:, kvh * HEAD_DIM:(kvh + 1) * HEAD_DIM].astype(BF16)
        v = jnp.concatenate([v_ref[:, kvh * HEAD_DIM:(kvh + 1) * HEAD_DIM].astype(BF16), ones], axis=1)
        o = _softmax_av([_dot_nt(q, k)], [v], _sink_column(sink_ref, kvh, rows))
        for g in range(ATT_GROUP):
            hd = kvh * ATT_GROUP + g
            o_ref[:, hd * HEAD_DIM:(hd + 1) * HEAD_DIM] = o[g * rows:(g + 1) * rows, :].astype(BF16)


def _context_attention(proj, sink, *, n_batch, n_tok):
    q_col = 5 * HG_WIDTH // ATT_WIDTH
    k_col = (5 * HG_WIDTH + ATT_WIDTH) // KV_WIDTH
    return pl.pallas_call(
        _ctx_attn_kernel,
        out_shape=jax.ShapeDtypeStruct((n_batch * n_tok, ATT_WIDTH), BF16),
        grid_spec=pltpu.PrefetchScalarGridSpec(
            num_scalar_prefetch=1,
            grid=(n_batch,),
            in_specs=[pl.BlockSpec((n_tok, ATT_WIDTH), lambda b, s: (b, q_col)),
                      pl.BlockSpec((n_tok, KV_WIDTH), lambda b, s: (b, k_col)),
                      pl.BlockSpec((n_tok, KV_WIDTH), lambda b, s: (b, k_col + 1))],
            out_specs=pl.BlockSpec((n_tok, ATT_WIDTH), lambda b, s: (b, 0))),
        compiler_params=_cparams("parallel"),
        name="context_attention",
    )(sink, proj, proj, proj)


def _rope(x, cos, sin_signed, even_group):
    partner = jnp.where(even_group, pltpu.roll(x, HEAD_DIM - ROPE_FREQS, 1), pltpu.roll(x, ROPE_FREQS, 1))
    return x * cos + partner * sin_signed


def _win_attn_kernel(sink_ref, q_ref, k_ref, v_ref, ck_ref, cv_ref, cos_ref, sin_ref, o_ref,
                     kpad_ref, vpad_ref, *, n_tok):
    blk = ATT_BLOCK
    nb = n_tok // blk
    i = pl.program_id(1)
    lane = lax.broadcasted_iota(jnp.int32, (blk, HEAD_DIM), 1)
    even_group = (lane & ROPE_FREQS) == 0

    @pl.when(i == 0)
    def _():
        kpad_ref[0:blk, :] = jnp.zeros((blk, KV_WIDTH), BF16)
        kpad_ref[blk + n_tok:2 * blk + n_tok, :] = jnp.zeros((blk, KV_WIDTH), BF16)
        vpad_ref[0:blk, :] = jnp.zeros((blk, 2 * KV_WIDTH), BF16)
        vpad_ref[blk + n_tok:2 * blk + n_tok, :] = jnp.zeros((blk, 2 * KV_WIDTH), BF16)
        ones = _ones_column(blk)

        def body(r, carry):
            src = pl.ds(pl.multiple_of(r * blk, blk), blk)
            dst = pl.ds(pl.multiple_of((r + 1) * blk, blk), blk)
            cos = cos_ref[src, :]
            sin = sin_ref[src, :]
            for kvh in range(ATT_KV_HEADS):
                cols = slice(kvh * HEAD_DIM, (kvh + 1) * HEAD_DIM)
                kpad_ref[dst, cols] = _rope(k_ref[src, cols], cos, sin, even_group).astype(BF16)
                vpad_ref[dst, 2 * kvh * HEAD_DIM:(2 * kvh + 1) * HEAD_DIM] = v_ref[src, cols].astype(BF16)
                vpad_ref[dst, (2 * kvh + 1) * HEAD_DIM:(2 * kvh + 2) * HEAD_DIM] = ones
            return carry

        lax.fori_loop(0, nb, body, 0)

    rows = pl.ds(pl.multiple_of(i * blk, blk), blk)
    cos = cos_ref[rows, :]
    sin = sin_ref[rows, :]
    band = pl.ds(pl.multiple_of(i * blk, blk), 3 * blk)
    r = lax.broadcasted_iota(jnp.int32, (blk, 3 * blk), 0)
    j = lax.broadcasted_iota(jnp.int32, (blk, 3 * blk), 1)
    kpos = j + (i - 1) * blk
    valid = (j >= r) & (j <= r + 2 * blk) & (kpos >= 0) & (kpos < n_tok)
    valid = jnp.concatenate([valid] * ATT_GROUP, axis=0)
    q_all = q_ref[...]
    ctx_ones = _ones_column(ck_ref.shape[1])
    for kvh in range(ATT_KV_HEADS):
        cols = slice(kvh * HEAD_DIM, (kvh + 1) * HEAD_DIM)
        q = jnp.concatenate(
            [_rope(q_all[:, (kvh * ATT_GROUP + g) * HEAD_DIM:(kvh * ATT_GROUP + g + 1) * HEAD_DIM],
                   cos, sin, even_group) for g in range(ATT_GROUP)], axis=0).astype(BF16)
        s_ctx = _dot_nt(q, ck_ref[0, :, cols].astype(BF16))
        s_loc = jnp.where(valid, _dot_nt(q, kpad_ref[band, cols]), -jnp.inf)
        v_ctx = jnp.concatenate([cv_ref[0, :, cols].astype(BF16), ctx_ones], axis=1)
        v_loc = vpad_ref[band, 2 * kvh * HEAD_DIM:(2 * kvh + 2) * HEAD_DIM]
        o = _softmax_av([s_ctx, s_loc], [v_ctx, v_loc], _sink_column(sink_ref, kvh, blk))
        for g in range(ATT_GROUP):
            hd = kvh * ATT_GROUP + g
            o_ref[:, hd * HEAD_DIM:(hd + 1) * HEAD_DIM] = o[g * blk:(g + 1) * blk, :].astype(BF16)


def _window_attention(proj, cache_k, cache_v, sink, cos, sin_signed, *, n_batch, n_tok):
    nb = n_tok // ATT_BLOCK
    n_ctx = cache_k.shape[1]
    q_col = 5 * HG_WIDTH // ATT_WIDTH
    k_col = (5 * HG_WIDTH + ATT_WIDTH) // KV_WIDTH
    table = pl.BlockSpec((n_tok, HEAD_DIM), lambda b, i, s: (0, 0))
    cache = pl.BlockSpec((1, n_ctx, KV_WIDTH), lambda b, i, s: (b, 0, 0))
    return pl.pallas_call(
        functools.partial(_win_attn_kernel, n_tok=n_tok),
        out_shape=jax.ShapeDtypeStruct((n_batch * n_tok, ATT_WIDTH), BF16),
        grid_spec=pltpu.PrefetchScalarGridSpec(
            num_scalar_prefetch=1,
            grid=(n_batch, nb),
            in_specs=[pl.BlockSpec((ATT_BLOCK, ATT_WIDTH), lambda b, i, s: (b * nb + i, q_col)),
                      pl.BlockSpec((n_tok, KV_WIDTH), lambda b, i, s: (b, k_col)),
                      pl.BlockSpec((n_tok, KV_WIDTH), lambda b, i, s: (b, k_col + 1)),
                      cache, cache, table, table],
            out_specs=pl.BlockSpec((ATT_BLOCK, ATT_WIDTH), lambda b, i, s: (b * nb + i, 0)),
            scratch_shapes=[pltpu.VMEM((n_tok + 2 * ATT_BLOCK, KV_WIDTH), BF16),
                            pltpu.VMEM((n_tok + 2 * ATT_BLOCK, 2 * KV_WIDTH), BF16)]),
        compiler_params=_cparams("parallel", "arbitrary"),
        name="window_attention",
    )(sink, proj, proj, proj, cache_k, cache_v, cos, sin_signed)


def _rope_tables(n_tok):
    rows = n_tok // GRID_W
    row = jnp.repeat(jnp.arange(rows), GRID_W).astype(F32)
    col = jnp.tile(jnp.arange(GRID_W), rows).astype(F32)
    inv = ROPE_BASE ** (-jnp.arange(ROPE_FREQS, dtype=F32) / ROPE_FREQS)
    ar, ac = row[:, None] * inv, col[:, None] * inv
    cr, sr, cc, sc = jnp.cos(ar), jnp.sin(ar), jnp.cos(ac), jnp.sin(ac)
    return (jnp.concatenate([cr, cr, cc, cc], axis=1), jnp.concatenate([-sr, sr, -sc, sc], axis=1))


def _outproj_kernel(hg_ref, att_ref, w_ref, x_ref, mod_ref, g_ref, xo_ref, h_ref):
    mix = _dot(hg_ref[...], w_ref[0:HG_WIDTH, :]) + _dot(att_ref[...], w_ref[HG_WIDTH:, :])
    x = x_ref[...] + mod_ref[0, 2:3, :] * mix
    xo_ref[...] = x
    h_ref[...] = _norm_modulate(x, g_ref[...], mod_ref[0, 3:4, :], mod_ref[0, 4:5, :]).astype(BF16)


def _out_projection(o_hg, o_att, w_bf16, x, mod, gain, *, mod_base, rows_per_mod):
    m = x.shape[0]
    tm = 256
    row = lambda i: (i, 0)
    return pl.pallas_call(
        _outproj_kernel,
        out_shape=(jax.ShapeDtypeStruct((m, D_MODEL), F32), jax.ShapeDtypeStruct((m, D_MODEL), BF16)),
        grid=(m // tm,),
        in_specs=[pl.BlockSpec((tm, HG_WIDTH), row),
                  pl.BlockSpec((tm, ATT_WIDTH), row),
                  pl.BlockSpec((HG_WIDTH + ATT_WIDTH, D_MODEL), lambda i: (0, 0)),
                  pl.BlockSpec((tm, D_MODEL), row),
                  pl.BlockSpec((1, N_MOD, D_MODEL), lambda i: (mod_base + (i * tm) // rows_per_mod, 0, 0)),
                  pl.BlockSpec((1, D_MODEL), lambda i: (0, 0))],
        out_specs=(pl.BlockSpec((tm, D_MODEL), row), pl.BlockSpec((tm, D_MODEL), row)),
        compiler_params=_cparams("parallel"),
        name="out_projection",
    )(o_hg, o_att, w_bf16, x, mod, gain)


def _prefix_count(x):
    n = x.shape[1]
    i = lax.broadcasted_iota(jnp.int32, (SEL_BLOCK, SEL_BLOCK), 0)
    j = lax.broadcasted_iota(jnp.int32, (SEL_BLOCK, SEL_BLOCK), 1)
    upper = jnp.where(i < j, 1.0, 0.0).astype(BF16)
    off = jnp.zeros((x.shape[0], 1), F32)
    outs = []
    for blk in range(n // SEL_BLOCK):
        xb = x[:, blk * SEL_BLOCK:(blk + 1) * SEL_BLOCK]
        outs.append(_dot(xb.astype(BF16), upper) + off)
        off = off + jnp.sum(xb, axis=-1, keepdims=True)
    return outs[0] if len(outs) == 1 else jnp.concatenate(outs, axis=1)


def _route_select(h_ref, w_ref, slot_t_ref, cap):
    n_tok = h_ref.shape[0]
    logits = _dot_nt(w_ref[...], h_ref[...])
    ex = jnp.exp(logits - jnp.max(logits, axis=0, keepdims=True))
    aff = ex / jnp.sum(ex, axis=0, keepdims=True)
    bits = pltpu.bitcast(aff, jnp.int32)

    def bisect(it, thr):
        cand = thr | jnp.left_shift(jnp.int32(1), 30 - it)
        cnt = jnp.sum(jnp.where(bits >= cand, 1.0, 0.0), axis=-1, keepdims=True)
        return jnp.where(cnt >= cap, cand, thr)

    thr = lax.fori_loop(0, 31, bisect, jnp.zeros((N_EXPERTS, 1), jnp.int32))
    above = jnp.where(bits > thr, 1.0, 0.0)
    tied = jnp.where(bits == thr, 1.0, 0.0)
    room = cap - jnp.sum(above, axis=-1, keepdims=True)
    sel = above + tied * jnp.where(_prefix_count(tied) < room, 1.0, 0.0)
    slot = jnp.where(sel > 0.0, _prefix_count(sel), -1.0)
    i = lax.broadcasted_iota(jnp.int32, (SEL_BLOCK, SEL_BLOCK), 0)
    j = lax.broadcasted_iota(jnp.int32, (SEL_BLOCK, SEL_BLOCK), 1)
    eye = jnp.where(i == j, 1.0, 0.0).astype(BF16)
    for blk in range(n_tok // SEL_BLOCK):
        cols = slice(blk * SEL_BLOCK, (blk + 1) * SEL_BLOCK)
        slot_t_ref[0, cols, :] = _dot_nt(eye, slot[:, cols].astype(BF16))
    return slot, aff


def _one_hot_gather(slot_rows, aff_rows, h_ref, rows, cap):
    n_tok = h_ref.shape[0]
    c = (lax.broadcasted_iota(jnp.int32, (rows, n_tok), 0) & (cap - 1)).astype(F32)
    hit = c == slot_rows
    x = _dot(jnp.where(hit, 1.0, 0.0).astype(BF16), h_ref[...]).astype(BF16)
    return x, jnp.sum(jnp.where(hit, aff_rows, 0.0), axis=-1, keepdims=True)


def _route_request_kernel(h_ref, w_ref, x_ref, g_ref, slot_t_ref, *, cap):
    n_tok = h_ref.shape[0]
    slot, aff = _route_select(h_ref, w_ref, slot_t_ref, cap)
    per_row = lambda a: jnp.concatenate(
        [jnp.broadcast_to(a[e:e + 1, :], (cap, n_tok)) for e in range(N_EXPERTS)], axis=0)
    x, g = _one_hot_gather(per_row(slot), per_row(aff), h_ref, N_EXPERTS * cap, cap)
    for e in range(N_EXPERTS):
        x_ref[e] = x[e * cap:(e + 1) * cap, :]
        g_ref[e] = g[e * cap:(e + 1) * cap, :]


def _route_only_kernel(h_ref, w_ref, slot_ref, aff_ref, slot_t_ref, bounds_ref, *, cap):
    n_tok = h_ref.shape[0]
    slot, aff = _route_select(h_ref, w_ref, slot_t_ref, cap)
    slot_ref[0] = slot
    aff_ref[0] = aff
    n = lax.broadcasted_iota(jnp.int32, (n_tok, 128), 0)
    t = lax.broadcasted_iota(jnp.int32, (n_tok, 128), 1)
    before = jnp.where(n < t * TOK_BLOCK, 1.0, 0.0).astype(BF16)
    chosen = jnp.where(slot >= 0.0, 1.0, 0.0).astype(BF16)
    bounds_ref[0] = _dot(chosen, before).astype(jnp.int32)


def _gather_window_kernel(bounds_ref, slot_ref, aff_ref, h_ref, xh_ref, gh_ref, x_ref, g_ref, xc_ref, gc_ref, *,
                          n_batch, group):
    b = pl.program_id(0)
    eg = pl.program_id(1)
    n_tok = h_ref.shape[0]
    nt = n_tok // TOK_BLOCK

    @pl.when(b >= n_batch)
    def _():
        x_ref[...] = xh_ref[...]
        g_ref[...] = gh_ref[...]

    @pl.when(b < n_batch)
    def _():
        x_ref[...] = jnp.zeros(x_ref.shape, BF16)
        g_ref[...] = jnp.zeros(g_ref.shape, F32)
        row_i = lax.broadcasted_iota(jnp.int32, (TOK_BLOCK, TOK_BLOCK), 0)

        def token_block(t, carry):
            rows_t = pl.ds(pl.multiple_of(t * TOK_BLOCK, TOK_BLOCK), TOK_BLOCK)
            off = 0
            starts, lengths, offs, packed_row, affs = [], [], [], [], []
            for k in range(group):
                e = eg * group + k
                start, length = _window(bounds_ref, (b * N_EXPERTS + e) * (nt + 1), t)
                s = slot_ref[0, pl.ds(e, 1), rows_t]
                packed_row.append(jnp.where(s >= 0.0, s + (off - start).astype(F32), -1.0))
                affs.append(aff_ref[0, pl.ds(e, 1), rows_t])
                starts.append(start)
                lengths.append(length)
                offs.append(off)
                off = off + length

            def row_block(j, carry):
                r = (row_i + j * TOK_BLOCK).astype(F32)
                hit = None
                gsum = jnp.zeros((TOK_BLOCK, 1), F32)
                for k in range(group):
                    hk = r == packed_row[k]
                    gsum = gsum + jnp.sum(jnp.where(hk, affs[k], 0.0), axis=-1, keepdims=True)
                    hit = hk if hit is None else (hit | hk)
                xc_ref[...] = _dot(jnp.where(hit, 1.0, 0.0).astype(BF16), h_ref[rows_t, :]).astype(BF16)
                gc_ref[...] = gsum
                for k in range(group):
                    lo = jnp.clip(offs[k] - j * TOK_BLOCK, 0, TOK_BLOCK) // PIECE
                    hi = jnp.clip(offs[k] + lengths[k] - j * TOK_BLOCK, 0, TOK_BLOCK) // PIECE

                    def place(i, carry, k=k):
                        src = pl.ds(pl.multiple_of(i * PIECE, PIECE), PIECE)
                        dst = pl.ds(pl.multiple_of(starts[k] + j * TOK_BLOCK + i * PIECE - offs[k], PIECE), PIECE)
                        x_ref[k, dst, :] = x_ref[k, dst, :] + xc_ref[src, :]
                        g_ref[k, dst, :] = g_ref[k, dst, :] + gc_ref[src, :]
                        return carry

                    lax.fori_loop(lo, hi, place, 0)
                return carry

            lax.fori_loop(0, (off + TOK_BLOCK - 1) // TOK_BLOCK, row_block, 0)
            return carry

        lax.fori_loop(0, nt, token_block, 0)


def _route_gather_requests(h, w_router_t, *, n_batch, n_tok):
    cap = CAPACITY_FACTOR * n_tok // N_EXPERTS
    out_block = lambda w: pl.BlockSpec((N_EXPERTS, cap, w), lambda b: (0, b, 0))
    return pl.pallas_call(
        functools.partial(_route_request_kernel, cap=cap),
        out_shape=(jax.ShapeDtypeStruct((N_EXPERTS, n_batch * cap, D_MODEL), BF16),
                   jax.ShapeDtypeStruct((N_EXPERTS, n_batch * cap, 1), F32),
                   jax.ShapeDtypeStruct((n_batch, n_tok, N_EXPERTS), F32)),
        grid=(n_batch,),
        in_specs=[pl.BlockSpec((n_tok, D_MODEL), lambda b: (b, 0)),
                  pl.BlockSpec((N_EXPERTS, D_MODEL), lambda b: (0, 0))],
        out_specs=(out_block(D_MODEL), out_block(1), pl.BlockSpec((1, n_tok, N_EXPERTS), lambda b: (b, 0, 0))),
        compiler_params=_cparams("parallel"),
        name="route_gather_requests",
    )(h, w_router_t)


def _route_only(h, w_router_t, *, n_batch, n_tok):
    cap = CAPACITY_FACTOR * n_tok // N_EXPERTS
    nt = n_tok // TOK_BLOCK
    expert_major = pl.BlockSpec((1, N_EXPERTS, n_tok), lambda b: (b, 0, 0))
    slot, aff, slot_t, bounds = pl.pallas_call(
        functools.partial(_route_only_kernel, cap=cap),
        out_shape=(jax.ShapeDtypeStruct((n_batch, N_EXPERTS, n_tok), F32),
                   jax.ShapeDtypeStruct((n_batch, N_EXPERTS, n_tok), F32),
                   jax.ShapeDtypeStruct((n_batch, n_tok, N_EXPERTS), F32),
                   jax.ShapeDtypeStruct((n_batch, N_EXPERTS, 128), jnp.int32)),
        grid=(n_batch,),
        in_specs=[pl.BlockSpec((n_tok, D_MODEL), lambda b: (b, 0)),
                  pl.BlockSpec((N_EXPERTS, D_MODEL), lambda b: (0, 0))],
        out_specs=(expert_major, expert_major, pl.BlockSpec((1, n_tok, N_EXPERTS), lambda b: (b, 0, 0)),
                   pl.BlockSpec((1, N_EXPERTS, 128), lambda b: (b, 0, 0))),
        compiler_params=_cparams("parallel"),
        name="route_select",
    )(h, w_router_t)
    return slot, aff, slot_t, bounds[:, :, :nt + 1].reshape(-1)


def _gather_windows(bounds, slot, aff, h, x_head, g_head, *, n_batch, n_tok):
    cap = CAPACITY_FACTOR * n_tok // N_EXPERTS
    group = 4
    head_rows = x_head.shape[1]
    assert head_rows % cap == 0 and N_EXPERTS % group == 0
    n_head = head_rows // cap
    rows_total = head_rows + n_batch * cap
    last = n_batch - 1
    request = lambda b: jnp.minimum(b, last)
    out_row = lambda b: jnp.where(b < n_batch, n_head + b, b - n_batch)
    head_row = lambda b: jnp.maximum(b - n_batch, 0)
    out_block = lambda w: pl.BlockSpec((group, cap, w), lambda b, g, s: (g, out_row(b), 0))
    head_block = lambda w: pl.BlockSpec((group, cap, w), lambda b, g, s: (g, head_row(b), 0))
    expert_major = pl.BlockSpec((1, N_EXPERTS, n_tok), lambda b, g, s: (request(b), 0, 0))
    return pl.pallas_call(
        functools.partial(_gather_window_kernel, n_batch=n_batch, group=group),
        out_shape=(jax.ShapeDtypeStruct((N_EXPERTS, rows_total, D_MODEL), BF16),
                   jax.ShapeDtypeStruct((N_EXPERTS, rows_total, 1), F32)),
        grid_spec=pltpu.PrefetchScalarGridSpec(
            num_scalar_prefetch=1,
            grid=(n_batch + n_head, N_EXPERTS // group),
            in_specs=[expert_major, expert_major,
                      pl.BlockSpec((n_tok, D_MODEL), lambda b, g, s: (request(b), 0)),
                      head_block(D_MODEL), head_block(1)],
            out_specs=(out_block(D_MODEL), out_block(1)),
            scratch_shapes=[pltpu.VMEM((TOK_BLOCK, D_MODEL), BF16), pltpu.VMEM((TOK_BLOCK, 1), F32)]),
        compiler_params=_cparams("arbitrary", "arbitrary"),
        name="gather_windows",
    )(bounds, slot, aff, h, x_head, g_head)


def _moe_kernel(x_ref, g_ref, wg_ref, wu_ref, wd_ref, y_ref, hid_ref, *, n_ff, rows):
    s = pl.program_id(2)
    tf = wg_ref.shape[2]

    @pl.when(s < n_ff)
    def _():
        wg = wg_ref[0].astype(BF16)
        wu = wu_ref[0].astype(BF16)
        cols = pl.ds(pl.multiple_of(s * tf, tf), tf)
        for r in range(x_ref.shape[1] // rows):
            sl = slice(r * rows, (r + 1) * rows)
            x = x_ref[0, sl, :]
            hid_ref[sl, cols] = (_silu(_dot(x, wg)) * _dot(x, wu)).astype(BF16)

    @pl.when(s >= n_ff)
    def _():
        y = _dot(hid_ref[...], wd_ref[0].astype(BF16))
        y_ref[0] = (y * g_ref[0]).astype(BF16)


def _experts(x, gate, w_gate, w_up, w_down):
    n_rows = x.shape[1]
    tr, tf, tn = n_rows // 2, 512, 256
    n_ff, n_out = EXPERT_FF // tf, D_MODEL // tn
    up_tile = lambda e, r, s: (e, 0, jnp.minimum(s, n_ff - 1))
    out_tile = lambda s: jnp.maximum(s - n_ff, 0)
    return pl.pallas_call(
        functools.partial(_moe_kernel, n_ff=n_ff, rows=256),
        out_shape=jax.ShapeDtypeStruct((N_EXPERTS, n_rows, D_MODEL), BF16),
        grid=(N_EXPERTS, n_rows // tr, n_ff + n_out),
        in_specs=[pl.BlockSpec((1, tr, D_MODEL), lambda e, r, s: (e, r, 0), pipeline_mode=pl.Buffered(1)),
                  pl.BlockSpec((1, tr, 1), lambda e, r, s: (e, r, 0)),
                  pl.BlockSpec((1, D_MODEL, tf), up_tile),
                  pl.BlockSpec((1, D_MODEL, tf), up_tile),
                  pl.BlockSpec((1, EXPERT_FF, tn), lambda e, r, s: (e, 0, out_tile(s)))],
        out_specs=pl.BlockSpec((1, tr, tn), lambda e, r, s: (e, r, out_tile(s))),
        scratch_shapes=[pltpu.VMEM((tr, EXPERT_FF), BF16)],
        compiler_params=_cparams("parallel", "parallel", "arbitrary"),
        name="expert_swiglu",
    )(x, gate, w_gate, w_up, w_down)


def _window(bounds_ref, base, t):
    p0 = bounds_ref[base + t]
    p1 = bounds_ref[base + t + 1]
    start = (p0 // PIECE) * PIECE
    return start, jnp.where(p1 > p0, ((p1 - start + PIECE - 1) // PIECE) * PIECE, 0)


def _combine_kernel(bounds_ref, y_ref, slot_ref, x_ref, mod_ref, g_ref, o_ref, ybuf_ref, acc_ref, *, nt):
    b = pl.program_id(0)
    t = pl.program_id(1)
    tt = x_ref.shape[0]
    slots = slot_ref[0]
    lane_k = lax.broadcasted_iota(jnp.int32, (tt, TOK_BLOCK), 1)
    off = 0
    packed_col = []
    for e in range(N_EXPERTS):
        start, length = _window(bounds_ref, (b * N_EXPERTS + e) * (nt + 1), t)

        def copy(i, carry, e=e, start=start, off=off):
            dst = pl.ds(pl.multiple_of(off + i * PIECE, PIECE), PIECE)
            ybuf_ref[dst, :] = y_ref[e, pl.ds(pl.multiple_of(start + i * PIECE, PIECE), PIECE), :]
            return carry

        lax.fori_loop(0, length // PIECE, copy, 0)
        s = slots[:, e:e + 1]
        packed_col.append(jnp.where(s >= 0.0, s + (off - start).astype(F32), -1.0))
        off = off + length
    n_blocks = (off + TOK_BLOCK - 1) // TOK_BLOCK

    def clear(i, carry):
        ybuf_ref[pl.ds(pl.multiple_of(off + i * PIECE, PIECE), PIECE), :] = jnp.zeros((PIECE, D_MODEL), BF16)
        return carry

    lax.fori_loop(0, (n_blocks * TOK_BLOCK - off) // PIECE, clear, 0)
    acc_ref[...] = jnp.zeros(acc_ref.shape, F32)

    def block(j, carry):
        k = (lane_k + j * TOK_BLOCK).astype(F32)
        hit = k == packed_col[0]
        for e in range(1, N_EXPERTS):
            hit = hit | (k == packed_col[e])
        rows = pl.ds(pl.multiple_of(j * TOK_BLOCK, TOK_BLOCK), TOK_BLOCK)
        acc_ref[...] += _dot(jnp.where(hit, 1.0, 0.0).astype(BF16), ybuf_ref[rows, :])
        return carry

    lax.fori_loop(0, n_blocks, block, 0)
    x = x_ref[...] + mod_ref[0, 5:6, :] * acc_ref[...]
    var = jnp.mean(x * x, axis=-1, keepdims=True)
    o_ref[...] = x * lax.rsqrt(var + NORM_EPS) * g_ref[...]


def _combine(bounds, y, slot_t, x_mid, mod, final_g, *, n_batch, n_tok, row_block_off, mod_base, mod_per_batch):
    cap = CAPACITY_FACTOR * n_tok // N_EXPERTS
    tt = TOK_BLOCK
    nt = n_tok // tt
    packed_rows = N_EXPERTS * (min(cap, tt) + PIECE)
    packed_rows = -(-packed_rows // TOK_BLOCK) * TOK_BLOCK
    return pl.pallas_call(
        functools.partial(_combine_kernel, nt=nt),
        out_shape=jax.ShapeDtypeStruct((n_batch * n_tok, D_MODEL), F32),
        grid_spec=pltpu.PrefetchScalarGridSpec(
            num_scalar_prefetch=1,
            grid=(n_batch, nt),
            in_specs=[pl.BlockSpec((N_EXPERTS, cap, D_MODEL), lambda b, t, s: (0, row_block_off + b, 0),
                                   pipeline_mode=pl.Buffered(1)),
                      pl.BlockSpec((1, tt, N_EXPERTS), lambda b, t, s: (b, t, 0)),
                      pl.BlockSpec((tt, D_MODEL), lambda b, t, s: (b * nt + t, 0)),
                      pl.BlockSpec((1, N_MOD, D_MODEL), lambda b, t, s: (mod_base + b * mod_per_batch, 0, 0)),
                      pl.BlockSpec((1, D_MODEL), lambda b, t, s: (0, 0))],
            out_specs=pl.BlockSpec((tt, D_MODEL), lambda b, t, s: (b * nt + t, 0)),
            scratch_shapes=[pltpu.VMEM((packed_rows, D_MODEL), BF16), pltpu.VMEM((tt, D_MODEL), F32)]),
        compiler_params=_cparams("arbitrary", "arbitrary"),
        name="combine_final_norm",
    )(bounds, y, slot_t, x_mid, mod, final_g)


def kernel(x_prompt, x_sample, cache_k, cache_v, state_hgrn, c, c_ctx, w_ada, b_ada, norm1_g, w_in, hg_lb,
           hg_norm_g, attn_sink, w_out, norm2_g, w_router, w_gate, w_up, w_down, final_norm_g):
    n_p, t_p, _ = x_prompt.shape
    n_s, t_s, _ = x_sample.shape
    assert w_ada.shape[0] == 1 and 1 + n_s <= COND_ROWS
    layer = 0

    cond = jnp.zeros((COND_ROWS, D_MODEL), F32).at[0].set(c_ctx).at[1:1 + n_s].set(c)
    mod = _ada_modulation(cond, w_ada[layer], b_ada[layer]).reshape(COND_ROWS, N_MOD, D_MODEL)
    lb = jnp.cumsum(jax.nn.softmax(hg_lb.astype(F32), axis=0), axis=0)[layer]
    w_in_b = w_in[layer].astype(BF16)
    w_out_b = w_out[layer].astype(BF16)
    w_router_t = w_router[layer].T.astype(BF16)
    norm1 = norm1_g[layer].reshape(1, D_MODEL)
    norm2 = norm2_g[layer].reshape(1, D_MODEL)
    hg_gain = hg_norm_g[layer].reshape(1, HG_WIDTH)
    final_g = final_norm_g.reshape(1, D_MODEL)
    sink = attn_sink[layer]
    cos, sin_signed = _rope_tables(t_s)

    xp = x_prompt.reshape(n_p * t_p, D_MODEL)
    xs = x_sample.reshape(n_s * t_s, D_MODEL)
    groups = dict(p=dict(mod_base=0, rows_per_mod=n_p * t_p), s=dict(mod_base=1, rows_per_mod=t_s))

    proj_p = _in_projection(xp, mod, norm1, w_in_b, **groups["p"])
    proj_s = _in_projection(xs, mod, norm1, w_in_b, **groups["s"])

    zero_state = jnp.zeros((n_p, 1, 2, HG_HEADS, HEAD_DIM, HEAD_DIM), F32)
    ohg_p, new_state = _hgrn(proj_p, lb, hg_gain, zero_state, n_batch=n_p, n_tok=t_p)
    ohg_s, _ = _hgrn(proj_s, lb, hg_gain, state_hgrn[:, layer:layer + 1].astype(F32), n_batch=n_s, n_tok=t_s)

    oatt_p = _context_attention(proj_p, sink, n_batch=n_p, n_tok=t_p)
    n_ctx = cache_k.shape[2]
    oatt_s = _window_attention(proj_s, cache_k[:, layer].reshape(n_s, n_ctx, KV_WIDTH),
                               cache_v[:, layer].reshape(n_s, n_ctx, KV_WIDTH), sink, cos, sin_signed,
                               n_batch=n_s, n_tok=t_s)

    xmid_p, h2_p = _out_projection(ohg_p, oatt_p, w_out_b, xp, mod, norm2, **groups["p"])
    xmid_s, h2_s = _out_projection(ohg_s, oatt_s, w_out_b, xs, mod, norm2, **groups["s"])

    cap_p = CAPACITY_FACTOR * t_p // N_EXPERTS
    cap_s = CAPACITY_FACTOR * t_s // N_EXPERTS
    off_s = n_p * cap_p // cap_s
    xg_p, gate_p, slot_p = _route_gather_requests(h2_p, w_router_t, n_batch=n_p, n_tok=t_p)
    slot_em, aff_em, slot_s, bounds_s = _route_only(h2_s, w_router_t, n_batch=n_s, n_tok=t_s)
    xg, gate = _gather_windows(bounds_s, slot_em, aff_em, h2_s, xg_p, gate_p, n_batch=n_s, n_tok=t_s)
    bounds_p = jnp.tile(jnp.array([0, cap_p], jnp.int32), n_p * N_EXPERTS)

    y = _experts(xg, gate, w_gate[layer], w_up[layer], w_down[layer])

    y_prompt = _combine(bounds_p, y, slot_p, xmid_p, mod, final_g, n_batch=n_p, n_tok=t_p, row_block_off=0,
                        mod_base=0, mod_per_batch=0)
    y_sample = _combine(bounds_s, y, slot_s, xmid_s, mod, final_g, n_batch=n_s, n_tok=t_s, row_block_off=off_s,
                        mod_base=1, mod_per_batch=1)

    k_col = 5 * HG_WIDTH + ATT_WIDTH
    new_k = proj_p[:, k_col:k_col + KV_WIDTH].reshape(n_p, 1, t_p, ATT_KV_HEADS, HEAD_DIM)
    new_v = proj_p[:, k_col + KV_WIDTH:k_col + 2 * KV_WIDTH].reshape(n_p, 1, t_p, ATT_KV_HEADS, HEAD_DIM)
    return (y_prompt.reshape(n_p, t_p, D_MODEL), y_sample.reshape(n_s, t_s, D_MODEL), new_k, new_v, new_state)
```

```python
import functools

import jax
import jax.numpy as jnp
from jax import lax
from jax.experimental import pallas as pl
from jax.experimental.pallas import tpu as pltpu

F32 = jnp.float32
BF16 = jnp.bfloat16

D_MODEL = 2048
HG_WIDTH = 1024
HG_HEADS = 8
HEAD_DIM = 128
ATT_HEADS = 8
ATT_KV_HEADS = 2
ATT_GROUP = ATT_HEADS // ATT_KV_HEADS
KV_WIDTH = ATT_KV_HEADS * HEAD_DIM
ATT_WIDTH = ATT_HEADS * HEAD_DIM
ATT_BLOCK = 128
GRID_W = 64
ROPE_BASE = 10000.0
ROPE_FREQS = HEAD_DIM // 4
N_EXPERTS = 16
CAPACITY_FACTOR = 2
EXPERT_FF = 5632
NORM_EPS = 1e-6
IN_WIDTH = 5 * HG_WIDTH + ATT_WIDTH + 2 * KV_WIDTH
N_MOD = 6
COND_ROWS = 16

HG_CHUNK = 128
HG_DIAG = 8
SEL_BLOCK = 256
TOK_BLOCK = 256
PIECE = 16

VMEM_LIMIT = 56 * 1024 * 1024


def _cparams(*sem):
    return pltpu.CompilerParams(dimension_semantics=sem, vmem_limit_bytes=VMEM_LIMIT)


def _sigmoid(x):
    return 1.0 / (1.0 + jnp.exp(-x))


def _silu(x):
    return x * _sigmoid(x)


def _dot(a, b):
    return jnp.dot(a, b, preferred_element_type=F32)


def _dot_nt(a, b):
    return lax.dot_general(a, b, (((1,), (1,)), ((), ())), preferred_element_type=F32)


def _dot_tn(a, b):
    return lax.dot_general(a, b, (((0,), (0,)), ((), ())), preferred_element_type=F32)


def _ada_kernel(c_ref, w_ref, b_ref, o_ref):
    s = _silu(c_ref[...]).astype(BF16)
    o_ref[...] = _dot(s, w_ref[...].astype(BF16)) + b_ref[...]


def _ada_modulation(cond, w_ada, b_ada):
    tn = 1024
    n = w_ada.shape[1]
    return pl.pallas_call(
        _ada_kernel,
        out_shape=jax.ShapeDtypeStruct((COND_ROWS, n), F32),
        grid=(n // tn,),
        in_specs=[pl.BlockSpec((COND_ROWS, D_MODEL), lambda j: (0, 0)),
                  pl.BlockSpec((D_MODEL, tn), lambda j: (0, j)),
                  pl.BlockSpec((1, tn), lambda j: (0, j))],
        out_specs=pl.BlockSpec((COND_ROWS, tn), lambda j: (0, j)),
        compiler_params=_cparams("arbitrary"),
        name="ada_modulation",
    )(cond, w_ada, b_ada.reshape(1, n))


def _norm_modulate(x, gain, shift, scale):
    var = jnp.mean(x * x, axis=-1, keepdims=True)
    return (x * lax.rsqrt(var + NORM_EPS) * gain) * (1.0 + scale) + shift


def _inproj_kernel(x_ref, mod_ref, g_ref, w_ref, o_ref, h_ref, *, rows):
    @pl.when(pl.program_id(1) == 0)
    def _():
        shift = mod_ref[0, 0:1, :]
        scale = mod_ref[0, 1:2, :]
        gain = g_ref[...]

        def body(r, carry):
            sl = pl.ds(pl.multiple_of(r * rows, rows), rows)
            h_ref[sl, :] = _norm_modulate(x_ref[sl, :], gain, shift, scale).astype(BF16)
            return carry

        lax.fori_loop(0, x_ref.shape[0] // rows, body, 0)

    w = w_ref[...]
    chunk = 256
    for r in range(x_ref.shape[0] // chunk):
        sl = slice(r * chunk, (r + 1) * chunk)
        o_ref[sl, :] = _dot(h_ref[sl, :], w)


def _in_projection(x, mod, gain, w_bf16, *, mod_base, rows_per_mod):
    m = x.shape[0]
    tm, tn = 1024, 512
    return pl.pallas_call(
        functools.partial(_inproj_kernel, rows=128),
        out_shape=jax.ShapeDtypeStruct((m, IN_WIDTH), F32),
        grid=(m // tm, IN_WIDTH // tn),
        in_specs=[pl.BlockSpec((tm, D_MODEL), lambda i, j: (i, 0)),
                  pl.BlockSpec((1, N_MOD, D_MODEL), lambda i, j: (mod_base + (i * tm) // rows_per_mod, 0, 0)),
                  pl.BlockSpec((1, D_MODEL), lambda i, j: (0, 0)),
                  pl.BlockSpec((D_MODEL, tn), lambda i, j: (0, j))],
        out_specs=pl.BlockSpec((tm, tn), lambda i, j: (i, j)),
        scratch_shapes=[pltpu.VMEM((tm, D_MODEL), BF16)],
        compiler_params=_cparams("parallel", "arbitrary"),
        name="in_projection",
    )(x, mod, gain, w_bf16)


def _hgrn_codes(reverse):
    L = HG_CHUNK
    t = lax.broadcasted_iota(jnp.int32, (L, L), 0)
    s = lax.broadcasted_iota(jnp.int32, (L, L), 1)
    code = jnp.where(t == s, 1, 0)
    h = L // 2
    while h >= 1:
        same = (t & ~(2 * h - 1)) == (s & ~(2 * h - 1))
        t_hi = (t & h) != 0
        s_hi = (s & h) != 0
        pair = (s_hi & ~t_hi) if reverse else (t_hi & ~s_hi)
        code = jnp.where(same & pair, h * 16, code)
        h //= 2
    return code


def _cumsum_rows(tri_bf16, g):
    g1 = g.astype(BF16)
    r1 = g - g1.astype(F32)
    g2 = r1.astype(BF16)
    g3 = (r1 - g2.astype(F32)).astype(BF16)
    return _dot(tri_bf16, g1) + _dot(tri_bf16, g2) + _dot(tri_bf16, g3)


def _hgrn_intra(q, k, f, b, v_bf, code, reverse):
    L = HG_CHUNK
    G = HG_DIAG
    q_bf = q.astype(BF16)
    k_bf = k.astype(BF16)

    def level(h, ref, att):
        neg_abs = pltpu.bitcast(pltpu.bitcast(b - ref, jnp.int32) | jnp.int32(-2 ** 31), F32)
        e = jnp.exp2(neg_abs.astype(BF16))
        return jnp.where(code == h * 16, _dot_nt(q_bf * e, k_bf * e), att)

    att = jnp.where(code == 1, _dot_nt(q_bf, k_bf), 0.0)
    h = L // 2
    while h >= G:
        parts = []
        for p in range(L // (2 * h)):
            m = p * 2 * h + (h if reverse else h - 1)
            parts.append(jnp.broadcast_to(b[m:m + 1, :], (2 * h, HEAD_DIM)))
        att = level(h, parts[0] if len(parts) == 1 else jnp.concatenate(parts, axis=0), att)
        h //= 2

    b3 = b.reshape(L // G, G, HEAD_DIM)

    def group_row(r):
        return jnp.broadcast_to(b3[:, r:r + 1, :], (L // G, G, HEAD_DIM)).reshape(L, HEAD_DIM)

    row = lax.broadcasted_iota(jnp.int32, (L, HEAD_DIM), 0)
    att = level(4, group_row(4 if reverse else 3), att)
    lo, hi = (2, 6) if reverse else (1, 5)
    att = level(2, jnp.where((row & 4) == 0, group_row(lo), group_row(hi)), att)
    att = jnp.where(code == 16, _dot_nt(q_bf * f.astype(BF16), k_bf), att)
    return _dot(att.astype(BF16), v_bf)


def _hgrn_prepare(q, f, v, code, tri, reverse):
    L = HG_CHUNK
    k = 1.0 - f
    b = _cumsum_rows(tri, jnp.log2(f))
    b_tot = b[0:1, :] if reverse else b[L - 1:L, :]
    v_bf = v.astype(BF16)
    o_intra = _hgrn_intra(q, k, f, b, v_bf, code, reverse)
    q_in = (q * jnp.exp2(b)).astype(BF16)
    k_out = (k * jnp.exp2(b_tot - b)).astype(BF16)
    return o_intra, q_in, k_out, v_bf, jnp.exp2(b_tot)


def _hgrn_advance(st, prepared):
    o_intra, q_in, k_out, v_bf, decay = prepared
    return o_intra + _dot_nt(q_in, st.astype(BF16)), st * decay + _dot_tn(v_bf, k_out)


def _hgrn_kernel(q_ref, ff_ref, fb_ref, v_ref, gate_ref, lb_ref, ng_ref, s0_ref,
                 o_ref, sout_ref, of_ref, ob_ref, code_ref, *, n_tok):
    L = HG_CHUNK
    nc = n_tok // L
    lb_f = lb_ref[0:1, :]
    lb_b = lb_ref[1:2, :]

    @pl.when((pl.program_id(0) == 0) & (pl.program_id(1) == 0))
    def _():
        code_ref[0] = _hgrn_codes(False)
        code_ref[1] = _hgrn_codes(True)

    code_f = code_ref[0]
    code_b = code_ref[1]
    ti = lax.broadcasted_iota(jnp.int32, (L, L), 0)
    si = lax.broadcasted_iota(jnp.int32, (L, L), 1)
    tri_f = jnp.where(si <= ti, 1.0, 0.0).astype(BF16)
    tri_b = jnp.where(si >= ti, 1.0, 0.0).astype(BF16)

    def prepare(row, f_ref, lb, code, tri, reverse):
        sl = pl.ds(row, L)
        q = _silu(q_ref[sl, :])
        f = lb + (1.0 - lb) * _sigmoid(f_ref[sl, :])
        return sl, _hgrn_prepare(q, f, v_ref[sl, :], code, tri, reverse)

    def body(c, states):
        st_f, st_b = states
        fwd, bwd = [], []
        for u in range(unroll):
            cu = c * unroll + u
            fwd.append(prepare(pl.multiple_of(cu * L, L), ff_ref, lb_f, code_f, tri_f, False))
            bwd.append(prepare(pl.multiple_of((nc - 1 - cu) * L, L), fb_ref, lb_b, code_b, tri_b, True))
        for (sl_f, prep_f), (sl_b, prep_b) in zip(fwd, bwd):
            of_ref[sl_f, :], st_f = _hgrn_advance(st_f, prep_f)
            ob_ref[sl_b, :], st_b = _hgrn_advance(st_b, prep_b)
        return st_f, st_b

    unroll = 2
    assert nc % unroll == 0
    st_f, st_b = lax.fori_loop(0, nc // unroll, body, (s0_ref[0, 0, 0, 0].T, s0_ref[0, 0, 1, 0].T))
    sout_ref[0, 0, 0, 0] = st_f.T
    sout_ref[0, 0, 1, 0] = st_b.T
    gain = ng_ref[...]

    def fin(c, carry):
        sl = pl.ds(pl.multiple_of(c * L, L), L)
        o = of_ref[sl, :] + ob_ref[sl, :]
        var = jnp.mean(o * o, axis=-1, keepdims=True)
        o_ref[sl, :] = ((o * lax.rsqrt(var + NORM_EPS) * gain) * _silu(gate_ref[sl, :])).astype(BF16)
        return carry

    lax.fori_loop(0, nc, fin, 0)


def _hgrn(proj, lb, norm_g, state, *, n_batch, n_tok):
    col = lambda k: (lambda b, h: (b, k * HG_HEADS + h))
    tok_spec = lambda k: pl.BlockSpec((n_tok, HEAD_DIM), col(k))
    st_spec = pl.BlockSpec((1, 1, 2, 1, HEAD_DIM, HEAD_DIM), lambda b, h: (b, 0, 0, h, 0, 0))
    return pl.pallas_call(
        functools.partial(_hgrn_kernel, n_tok=n_tok),
        out_shape=(jax.ShapeDtypeStruct((n_batch * n_tok, HG_WIDTH), BF16),
                   jax.ShapeDtypeStruct((n_batch, 1, 2, HG_HEADS, HEAD_DIM, HEAD_DIM), F32)),
        grid=(n_batch, HG_HEADS),
        in_specs=[tok_spec(0), tok_spec(1), tok_spec(2), tok_spec(3), tok_spec(4),
                  pl.BlockSpec((2, HEAD_DIM), lambda b, h: (0, h)),
                  pl.BlockSpec((1, HEAD_DIM), lambda b, h: (0, h)),
                  st_spec],
        out_specs=(pl.BlockSpec((n_tok, HEAD_DIM), lambda b, h: (b, h)), st_spec),
        scratch_shapes=[pltpu.VMEM((n_tok, HEAD_DIM), F32), pltpu.VMEM((n_tok, HEAD_DIM), F32),
                        pltpu.VMEM((2, HG_CHUNK, HG_CHUNK), jnp.int32)],
        compiler_params=_cparams("arbitrary", "arbitrary"),
        name="hgrn2_scan",
    )(proj, proj, proj, proj, proj, lb, norm_g, state)


def _stack_heads(x, kvh):
    return jnp.concatenate(
        [x[:, (kvh * ATT_GROUP + g) * HEAD_DIM:(kvh * ATT_GROUP + g + 1) * HEAD_DIM] for g in range(ATT_GROUP)],
        axis=0)


def _sink_column(sink_ref, kvh, rows):
    return jnp.concatenate(
        [jnp.full((rows, 1), sink_ref[kvh * ATT_GROUP + g], F32) for g in range(ATT_GROUP)],
        axis=0) * (1.0 / SOFTMAX_SCALE)


SOFTMAX_SCALE = HEAD_DIM ** -0.5
EXP2_SCALE = SOFTMAX_SCALE * 1.4426950408889634


def _ones_column(rows):
    lane = lax.broadcasted_iota(jnp.int32, (rows, HEAD_DIM), 1)
    return jnp.where(lane == 0, 1.0, 0.0).astype(BF16)


def _softmax_av(scores, values, sink_col):
    m = sink_col
    for s in scores:
        m = jnp.maximum(m, jnp.max(s, axis=-1, keepdims=True))
    acc = None
    for s, v in zip(scores, values):
        pv = _dot(jnp.exp2(((s - m) * EXP2_SCALE).astype(BF16)), v)
        acc = pv if acc is None else acc + pv
    denom = acc[:, HEAD_DIM:HEAD_DIM + 1] + jnp.exp2((sink_col - m) * EXP2_SCALE)
    return acc[:, :HEAD_DIM] / denom


def _ctx_attn_kernel(sink_ref, q_ref, k_ref, v_ref, o_ref):
    rows = q_ref.shape[0]
    q_all = q_ref[...]
    ones = _ones_column(k_ref.shape[0])
    for kvh in range(ATT_KV_HEADS):
        q = _stack_heads(q_all, kvh).astype(BF16)
        k = k_ref[:, kvh * HEAD_DIM:(kvh + 1) * HEAD_DIM].astype(BF16)
        v = jnp.concatenate([v_ref[:, kvh * HEAD_DIM:(kvh + 1) * HEAD_DIM].astype(BF16), ones], axis=1)
        o = _softmax_av([_dot_nt(q, k)], [v], _sink_column(sink_ref, kvh, rows))
        for g in range(ATT_GROUP):
            hd = kvh * ATT_GROUP + g
            o_ref[:, hd * HEAD_DIM:(hd + 1) * HEAD_DIM] = o[g * rows:(g + 1) * rows, :].astype(BF16)


def _context_attention(proj, sink, *, n_batch, n_tok):
    q_col = 5 * HG_WIDTH // ATT_WIDTH
    k_col = (5 * HG_WIDTH + ATT_WIDTH) // KV_WIDTH
    return pl.pallas_call(
        _ctx_attn_kernel,
        out_shape=jax.ShapeDtypeStruct((n_batch * n_tok, ATT_WIDTH), BF16),
        grid_spec=pltpu.PrefetchScalarGridSpec(
            num_scalar_prefetch=1,
            grid=(n_batch,),
            in_specs=[pl.BlockSpec((n_tok, ATT_WIDTH), lambda b, s: (b, q_col)),
                      pl.BlockSpec((n_tok, KV_WIDTH), lambda b, s: (b, k_col)),
                      pl.BlockSpec((n_tok, KV_WIDTH), lambda b, s: (b, k_col + 1))],
            out_specs=pl.BlockSpec((n_tok, ATT_WIDTH), lambda b, s: (b, 0))),
        compiler_params=_cparams("parallel"),
        name="context_attention",
    )(sink, proj, proj, proj)


def _rope(x, cos, sin_signed, even_group):
    partner = jnp.where(even_group, pltpu.roll(x, HEAD_DIM - ROPE_FREQS, 1), pltpu.roll(x, ROPE_FREQS, 1))
    return x * cos + partner * sin_signed


def _win_attn_kernel(sink_ref, q_ref, k_ref, v_ref, ck_ref, cv_ref, cos_ref, sin_ref, o_ref,
                     kpad_ref, vpad_ref, *, n_tok):
    blk = ATT_BLOCK
    nb = n_tok // blk
    i = pl.program_id(1)
    lane = lax.broadcasted_iota(jnp.int32, (blk, HEAD_DIM), 1)
    even_group = (lane & ROPE_FREQS) == 0

    @pl.when(i == 0)
    def _():
        kpad_ref[0:blk, :] = jnp.zeros((blk, KV_WIDTH), BF16)
        kpad_ref[blk + n_tok:2 * blk + n_tok, :] = jnp.zeros((blk, KV_WIDTH), BF16)
        vpad_ref[0:blk, :] = jnp.zeros((blk, 2 * KV_WIDTH), BF16)
        vpad_ref[blk + n_tok:2 * blk + n_tok, :] = jnp.zeros((blk, 2 * KV_WIDTH), BF16)
        ones = _ones_column(blk)

        def body(r, carry):
            src = pl.ds(pl.multiple_of(r * blk, blk), blk)
            dst = pl.ds(pl.multiple_of((r + 1) * blk, blk), blk)
            cos = cos_ref[src, :]
            sin = sin_ref[src, :]
            for kvh in range(ATT_KV_HEADS):
                cols = slice(kvh * HEAD_DIM, (kvh + 1) * HEAD_DIM)
                kpad_ref[dst, cols] = _rope(k_ref[src, cols], cos, sin, even_group).astype(BF16)
                vpad_ref[dst, 2 * kvh * HEAD_DIM:(2 * kvh + 1) * HEAD_DIM] = v_ref[src, cols].astype(BF16)
                vpad_ref[dst, (2 * kvh + 1) * HEAD_DIM:(2 * kvh + 2) * HEAD_DIM] = ones
            return carry

        lax.fori_loop(0, nb, body, 0)

    rows = pl.ds(pl.multiple_of(i * blk, blk), blk)
    cos = cos_ref[rows, :]
    sin = sin_ref[rows, :]
    band = pl.ds(pl.multiple_of(i * blk, blk), 3 * blk)
    r = lax.broadcasted_iota(jnp.int32, (blk, 3 * blk), 0)
    j = lax.broadcasted_iota(jnp.int32, (blk, 3 * blk), 1)
    kpos = j + (i - 1) * blk
    valid = (j >= r) & (j <= r + 2 * blk) & (kpos >= 0) & (kpos < n_tok)
    valid = jnp.concatenate([valid] * ATT_GROUP, axis=0)
    q_all = q_ref[...]
    ctx_ones = _ones_column(ck_ref.shape[1])
    for kvh in range(ATT_KV_HEADS):
        cols = slice(kvh * HEAD_DIM, (kvh + 1) * HEAD_DIM)
        q = jnp.concatenate(
            [_rope(q_all[:, (kvh * ATT_GROUP + g) * HEAD_DIM:(kvh * ATT_GROUP + g + 1) * HEAD_DIM],
                   cos, sin, even_group) for g in range(ATT_GROUP)], axis=0).astype(BF16)
        s_ctx = _dot_nt(q, ck_ref[0, :, cols].astype(BF16))
        s_loc = jnp.where(valid, _dot_nt(q, kpad_ref[band, cols]), -jnp.inf)
        v_ctx = jnp.concatenate([cv_ref[0, :, cols].astype(BF16), ctx_ones], axis=1)
        v_loc = vpad_ref[band, 2 * kvh * HEAD_DIM:(2 * kvh + 2) * HEAD_DIM]
        o = _softmax_av([s_ctx, s_loc], [v_ctx, v_loc], _sink_column(sink_ref, kvh, blk))
        for g in range(ATT_GROUP):
            hd = kvh * ATT_GROUP + g
            o_ref[:, hd * HEAD_DIM:(hd + 1) * HEAD_DIM] = o[g * blk:(g + 1) * blk, :].astype(BF16)


def _window_attention(proj, cache_k, cache_v, sink, cos, sin_signed, *, n_batch, n_tok):
    nb = n_tok // ATT_BLOCK
    n_ctx = cache_k.shape[1]
    q_col = 5 * HG_WIDTH // ATT_WIDTH
    k_col = (5 * HG_WIDTH + ATT_WIDTH) // KV_WIDTH
    table = pl.BlockSpec((n_tok, HEAD_DIM), lambda b, i, s: (0, 0))
    cache = pl.BlockSpec((1, n_ctx, KV_WIDTH), lambda b, i, s: (b, 0, 0))
    return pl.pallas_call(
        functools.partial(_win_attn_kernel, n_tok=n_tok),
        out_shape=jax.ShapeDtypeStruct((n_batch * n_tok, ATT_WIDTH), BF16),
        grid_spec=pltpu.PrefetchScalarGridSpec(
            num_scalar_prefetch=1,
            grid=(n_batch, nb),
            in_specs=[pl.BlockSpec((ATT_BLOCK, ATT_WIDTH), lambda b, i, s: (b * nb + i, q_col)),
                      pl.BlockSpec((n_tok, KV_WIDTH), lambda b, i, s: (b, k_col)),
                      pl.BlockSpec((n_tok, KV_WIDTH), lambda b, i, s: (b, k_col + 1)),
                      cache, cache, table, table],
            out_specs=pl.BlockSpec((ATT_BLOCK, ATT_WIDTH), lambda b, i, s: (b * nb + i, 0)),
            scratch_shapes=[pltpu.VMEM((n_tok + 2 * ATT_BLOCK, KV_WIDTH), BF16),
                            pltpu.VMEM((n_tok + 2 * ATT_BLOCK, 2 * KV_WIDTH), BF16)]),
        compiler_params=_cparams("parallel", "arbitrary"),
        name="window_attention",
    )(sink, proj, proj, proj, cache_k, cache_v, cos, sin_signed)


def _rope_tables(n_tok):
    rows = n_tok // GRID_W
    row = jnp.repeat(jnp.arange(rows), GRID_W).astype(F32)
    col = jnp.tile(jnp.arange(GRID_W), rows).astype(F32)
    inv = ROPE_BASE ** (-jnp.arange(ROPE_FREQS, dtype=F32) / ROPE_FREQS)
    ar, ac = row[:, None] * inv, col[:, None] * inv
    cr, sr, cc, sc = jnp.cos(ar), jnp.sin(ar), jnp.cos(ac), jnp.sin(ac)
    return (jnp.concatenate([cr, cr, cc, cc], axis=1), jnp.concatenate([-sr, sr, -sc, sc], axis=1))


def _outproj_kernel(hg_ref, att_ref, w_ref, x_ref, mod_ref, g_ref, xo_ref, h_ref):
    mix = _dot(hg_ref[...], w_ref[0:HG_WIDTH, :]) + _dot(att_ref[...], w_ref[HG_WIDTH:, :])
    x = x_ref[...] + mod_ref[0, 2:3, :] * mix
    xo_ref[...] = x
    h_ref[...] = _norm_modulate(x, g_ref[...], mod_ref[0, 3:4, :], mod_ref[0, 4:5, :]).astype(BF16)


def _out_projection(o_hg, o_att, w_bf16, x, mod, gain, *, mod_base, rows_per_mod):
    m = x.shape[0]
    tm = 256
    row = lambda i: (i, 0)
    return pl.pallas_call(
        _outproj_kernel,
        out_shape=(jax.ShapeDtypeStruct((m, D_MODEL), F32), jax.ShapeDtypeStruct((m, D_MODEL), BF16)),
        grid=(m // tm,),
        in_specs=[pl.BlockSpec((tm, HG_WIDTH), row),
                  pl.BlockSpec((tm, ATT_WIDTH), row),
                  pl.BlockSpec((HG_WIDTH + ATT_WIDTH, D_MODEL), lambda i: (0, 0)),
                  pl.BlockSpec((tm, D_MODEL), row),
                  pl.BlockSpec((1, N_MOD, D_MODEL), lambda i: (mod_base + (i * tm) // rows_per_mod, 0, 0)),
                  pl.BlockSpec((1, D_MODEL), lambda i: (0, 0))],
        out_specs=(pl.BlockSpec((tm, D_MODEL), row), pl.BlockSpec((tm, D_MODEL), row)),
        compiler_params=_cparams("parallel"),
        name="out_projection",
    )(o_hg, o_att, w_bf16, x, mod, gain)


def _prefix_count(x):
    n = x.shape[1]
    i = lax.broadcasted_iota(jnp.int32, (SEL_BLOCK, SEL_BLOCK), 0)
    j = lax.broadcasted_iota(jnp.int32, (SEL_BLOCK, SEL_BLOCK), 1)
    upper = jnp.where(i < j, 1.0, 0.0).astype(BF16)
    off = jnp.zeros((x.shape[0], 1), F32)
    outs = []
    for blk in range(n // SEL_BLOCK):
        xb = x[:, blk * SEL_BLOCK:(blk + 1) * SEL_BLOCK]
        outs.append(_dot(xb.astype(BF16), upper) + off)
        off = off + jnp.sum(xb, axis=-1, keepdims=True)
    return outs[0] if len(outs) == 1 else jnp.concatenate(outs, axis=1)


def _route_select(h_ref, w_ref, cap):
    logits = _dot_nt(w_ref[...], h_ref[...])
    ex = jnp.exp(logits - jnp.max(logits, axis=0, keepdims=True))
    aff = ex / jnp.sum(ex, axis=0, keepdims=True)
    bits = pltpu.bitcast(aff, jnp.int32)

    def bisect(it, thr):
        cand = thr | jnp.left_shift(jnp.int32(1), 30 - it)
        cnt = jnp.sum(jnp.where(bits >= cand, 1.0, 0.0), axis=-1, keepdims=True)
        return jnp.where(cnt >= cap, cand, thr)

    thr = lax.fori_loop(0, 31, bisect, jnp.zeros((N_EXPERTS, 1), jnp.int32))
    above = jnp.where(bits > thr, 1.0, 0.0)
    tied = jnp.where(bits == thr, 1.0, 0.0)
    room = cap - jnp.sum(above, axis=-1, keepdims=True)
    sel = above + tied * jnp.where(_prefix_count(tied) < room, 1.0, 0.0)
    return jnp.where(sel > 0.0, _prefix_count(sel), -1.0), aff


def _one_hot_gather(slot_rows, aff_rows, h_ref, rows, cap):
    n_tok = h_ref.shape[0]
    c = (lax.broadcasted_iota(jnp.int32, (rows, n_tok), 0) & (cap - 1)).astype(F32)
    hit = c == slot_rows
    x = _dot(jnp.where(hit, 1.0, 0.0).astype(BF16), h_ref[...]).astype(BF16)
    return x, jnp.sum(jnp.where(hit, aff_rows, 0.0), axis=-1, keepdims=True)


def _route_request_kernel(h_ref, w_ref, x_ref, g_ref, slot_ref, *, cap):
    n_tok = h_ref.shape[0]
    slot, aff = _route_select(h_ref, w_ref, cap)
    slot_ref[0] = slot
    per_row = lambda a: jnp.concatenate(
        [jnp.broadcast_to(a[e:e + 1, :], (cap, n_tok)) for e in range(N_EXPERTS)], axis=0)
    x, g = _one_hot_gather(per_row(slot), per_row(aff), h_ref, N_EXPERTS * cap, cap)
    for e in range(N_EXPERTS):
        x_ref[e] = x[e * cap:(e + 1) * cap, :]
        g_ref[e] = g[e * cap:(e + 1) * cap, :]


def _route_only_kernel(h_ref, w_ref, slot_ref, aff_ref, bounds_ref, *, cap):
    n_tok = h_ref.shape[0]
    slot, aff = _route_select(h_ref, w_ref, cap)
    slot_ref[0] = slot
    aff_ref[0] = aff
    n = lax.broadcasted_iota(jnp.int32, (n_tok, 128), 0)
    t = lax.broadcasted_iota(jnp.int32, (n_tok, 128), 1)
    before = jnp.where(n < t * TOK_BLOCK, 1.0, 0.0).astype(BF16)
    chosen = jnp.where(slot >= 0.0, 1.0, 0.0).astype(BF16)
    bounds_ref[0] = _dot(chosen, before).astype(jnp.int32)


def _gather_window_kernel(bounds_ref, slot_ref, aff_ref, h_ref, xh_ref, gh_ref, x_ref, g_ref, xo_ref, go_ref, *,
                          n_batch, group):
    b = pl.program_id(0)
    eg = pl.program_id(1)
    n_tok = h_ref.shape[0]
    nt = n_tok // TOK_BLOCK
    cap = x_ref.shape[1]
    region = TOK_BLOCK // group

    @pl.when(b >= n_batch)
    def _():
        x_ref[...] = xh_ref[...]
        g_ref[...] = gh_ref[...]

    @pl.when(b < n_batch)
    def _():
        x_ref[...] = jnp.zeros(x_ref.shape, BF16)
        g_ref[...] = jnp.zeros(g_ref.shape, F32)
        local_i = lax.broadcasted_iota(jnp.int32, (region, TOK_BLOCK), 0).astype(F32)
        over_i = lax.broadcasted_iota(jnp.int32, (TOK_BLOCK, TOK_BLOCK), 0)

        def token_block(t, carry):
            toks = pl.ds(pl.multiple_of(t * TOK_BLOCK, TOK_BLOCK), TOK_BLOCK)
            hits, gates, firsts, n_overs, slots, affs = [], [], [], [], [], []
            for k in range(group):
                e = eg * group + k
                start, length = _window(bounds_ref, (b * N_EXPERTS + e) * (nt + 1), t)
                first = jnp.minimum(start, cap - region)
                s = slot_ref[0, pl.ds(e, 1), toks]
                a = aff_ref[0, pl.ds(e, 1), toks]
                hit = local_i == jnp.where(s >= 0.0, s - first.astype(F32), -1.0)
                hits.append(jnp.where(hit, 1.0, 0.0).astype(BF16))
                gates.append(jnp.sum(jnp.where(hit, a, 0.0), axis=-1, keepdims=True))
                firsts.append(first)
                n_overs.append(start + length - first - region)
                slots.append(s)
                affs.append(a)
            xc = _dot(jnp.concatenate(hits, axis=0), h_ref[toks, :]).astype(BF16)
            for k in range(group):
                dst = pl.ds(pl.multiple_of(firsts[k], PIECE), region)
                x_ref[k, dst, :] = x_ref[k, dst, :] + xc[k * region:(k + 1) * region, :]
                g_ref[k, dst, :] = g_ref[k, dst, :] + gates[k]
            for k in range(group):
                @pl.when(n_overs[k] > 0)
                def _(k=k):
                    base = firsts[k] + region
                    hit = (over_i + base).astype(F32) == slots[k]
                    xo_ref[...] = _dot(jnp.where(hit, 1.0, 0.0).astype(BF16), h_ref[toks, :]).astype(BF16)
                    go_ref[...] = jnp.sum(jnp.where(hit, affs[k], 0.0), axis=-1, keepdims=True)

                    def place(i, carry):
                        src = pl.ds(pl.multiple_of(i * PIECE, PIECE), PIECE)
                        dst = pl.ds(pl.multiple_of(base + i * PIECE, PIECE), PIECE)
                        x_ref[k, dst, :] = x_ref[k, dst, :] + xo_ref[src, :]
                        g_ref[k, dst, :] = g_ref[k, dst, :] + go_ref[src, :]
                        return carry

                    lax.fori_loop(0, n_overs[k] // PIECE, place, 0)
            return carry

        lax.fori_loop(0, nt, token_block, 0)


def _route_gather_requests(h, w_router_t, *, n_batch, n_tok):
    cap = CAPACITY_FACTOR * n_tok // N_EXPERTS
    out_block = lambda w: pl.BlockSpec((N_EXPERTS, cap, w), lambda b: (0, b, 0))
    return pl.pallas_call(
        functools.partial(_route_request_kernel, cap=cap),
        out_shape=(jax.ShapeDtypeStruct((N_EXPERTS, n_batch * cap, D_MODEL), BF16),
                   jax.ShapeDtypeStruct((N_EXPERTS, n_batch * cap, 1), F32),
                   jax.ShapeDtypeStruct((n_batch, N_EXPERTS, n_tok), F32)),
        grid=(n_batch,),
        in_specs=[pl.BlockSpec((n_tok, D_MODEL), lambda b: (b, 0)),
                  pl.BlockSpec((N_EXPERTS, D_MODEL), lambda b: (0, 0))],
        out_specs=(out_block(D_MODEL), out_block(1), pl.BlockSpec((1, N_EXPERTS, n_tok), lambda b: (b, 0, 0))),
        compiler_params=_cparams("parallel"),
        name="route_gather_requests",
    )(h, w_router_t)


def _route_only(h, w_router_t, *, n_batch, n_tok):
    cap = CAPACITY_FACTOR * n_tok // N_EXPERTS
    nt = n_tok // TOK_BLOCK
    expert_major = pl.BlockSpec((1, N_EXPERTS, n_tok), lambda b: (b, 0, 0))
    slot, aff, bounds = pl.pallas_call(
        functools.partial(_route_only_kernel, cap=cap),
        out_shape=(jax.ShapeDtypeStruct((n_batch, N_EXPERTS, n_tok), F32),
                   jax.ShapeDtypeStruct((n_batch, N_EXPERTS, n_tok), F32),
                   jax.ShapeDtypeStruct((n_batch, N_EXPERTS, 128), jnp.int32)),
        grid=(n_batch,),
        in_specs=[pl.BlockSpec((n_tok, D_MODEL), lambda b: (b, 0)),
                  pl.BlockSpec((N_EXPERTS, D_MODEL), lambda b: (0, 0))],
        out_specs=(expert_major, expert_major, pl.BlockSpec((1, N_EXPERTS, 128), lambda b: (b, 0, 0))),
        compiler_params=_cparams("parallel"),
        name="route_select",
    )(h, w_router_t)
    return slot, aff, bounds[:, :, :nt + 1].reshape(-1)


def _gather_windows(bounds, slot, aff, h, x_head, g_head, *, n_batch, n_tok):
    cap = CAPACITY_FACTOR * n_tok // N_EXPERTS
    group = 4
    head_rows = x_head.shape[1]
    assert head_rows % cap == 0 and N_EXPERTS % group == 0
    n_head = head_rows // cap
    rows_total = head_rows + n_batch * cap
    last = n_batch - 1
    request = lambda b: jnp.minimum(b, last)
    out_row = lambda b: jnp.where(b < n_batch, n_head + b, b - n_batch)
    head_row = lambda b: jnp.maximum(b - n_batch, 0)
    out_block = lambda w: pl.BlockSpec((group, cap, w), lambda b, g, s: (g, out_row(b), 0))
    head_block = lambda w: pl.BlockSpec((group, cap, w), lambda b, g, s: (g, head_row(b), 0))
    expert_major = pl.BlockSpec((1, N_EXPERTS, n_tok), lambda b, g, s: (request(b), 0, 0))
    return pl.pallas_call(
        functools.partial(_gather_window_kernel, n_batch=n_batch, group=group),
        out_shape=(jax.ShapeDtypeStruct((N_EXPERTS, rows_total, D_MODEL), BF16),
                   jax.ShapeDtypeStruct((N_EXPERTS, rows_total, 1), F32)),
        grid_spec=pltpu.PrefetchScalarGridSpec(
            num_scalar_prefetch=1,
            grid=(n_batch + n_head, N_EXPERTS // group),
            in_specs=[expert_major, expert_major,
                      pl.BlockSpec((n_tok, D_MODEL), lambda b, g, s: (request(b), 0)),
                      head_block(D_MODEL), head_block(1)],
            out_specs=(out_block(D_MODEL), out_block(1)),
            scratch_shapes=[pltpu.VMEM((TOK_BLOCK, D_MODEL), BF16), pltpu.VMEM((TOK_BLOCK, 1), F32)]),
        compiler_params=_cparams("arbitrary", "arbitrary"),
        name="gather_windows",
    )(bounds, slot, aff, h, x_head, g_head)


def _moe_kernel(x_ref, g_ref, wg_ref, wu_ref, wd_ref, y_ref, hid_ref, *, n_ff, rows):
    s = pl.program_id(2)
    tf = wg_ref.shape[2]

    @pl.when(s < n_ff)
    def _():
        wg = wg_ref[0].astype(BF16)
        wu = wu_ref[0].astype(BF16)
        cols = pl.ds(pl.multiple_of(s * tf, tf), tf)
        for r in range(x_ref.shape[1] // rows):
            sl = slice(r * rows, (r + 1) * rows)
            x = x_ref[0, sl, :]
            hid_ref[sl, cols] = (_silu(_dot(x, wg)) * _dot(x, wu)).astype(BF16)

    @pl.when(s >= n_ff)
    def _():
        y = _dot(hid_ref[...], wd_ref[0].astype(BF16))
        y_ref[0] = (y * g_ref[0]).astype(BF16)


def _experts(x, gate, w_gate, w_up, w_down):
    n_rows = x.shape[1]
    tr, tf, tn = n_rows // 2, 512, 256
    n_ff, n_out = EXPERT_FF // tf, D_MODEL // tn
    up_tile = lambda e, r, s: (e, 0, jnp.minimum(s, n_ff - 1))
    out_tile = lambda s: jnp.maximum(s - n_ff, 0)
    return pl.pallas_call(
        functools.partial(_moe_kernel, n_ff=n_ff, rows=256),
        out_shape=jax.ShapeDtypeStruct((N_EXPERTS, n_rows, D_MODEL), BF16),
        grid=(N_EXPERTS, n_rows // tr, n_ff + n_out),
        in_specs=[pl.BlockSpec((1, tr, D_MODEL), lambda e, r, s: (e, r, 0), pipeline_mode=pl.Buffered(1)),
                  pl.BlockSpec((1, tr, 1), lambda e, r, s: (e, r, 0)),
                  pl.BlockSpec((1, D_MODEL, tf), up_tile),
                  pl.BlockSpec((1, D_MODEL, tf), up_tile),
                  pl.BlockSpec((1, EXPERT_FF, tn), lambda e, r, s: (e, 0, out_tile(s)))],
        out_specs=pl.BlockSpec((1, tr, tn), lambda e, r, s: (e, r, out_tile(s))),
        scratch_shapes=[pltpu.VMEM((tr, EXPERT_FF), BF16)],
        compiler_params=_cparams("parallel", "parallel", "arbitrary"),
        name="expert_swiglu",
    )(x, gate, w_gate, w_up, w_down)


def _window(bounds_ref, base, t):
    p0 = bounds_ref[base + t]
    p1 = bounds_ref[base + t + 1]
    start = (p0 // PIECE) * PIECE
    return start, jnp.where(p1 > p0, ((p1 - start + PIECE - 1) // PIECE) * PIECE, 0)


def _combine_kernel(bounds_ref, y_ref, slot_ref, x_ref, mod_ref, g_ref, o_ref, ybuf_ref, obuf_ref, acc_ref, *,
                    nt, region):
    b = pl.program_id(0)
    t = pl.program_id(1)
    tt = x_ref.shape[0]
    cap = y_ref.shape[1]
    local_i = lax.broadcasted_iota(jnp.int32, (region, tt), 0).astype(F32)
    hits = []
    for e in range(N_EXPERTS):
        start, _ = _window(bounds_ref, (b * N_EXPERTS + e) * (nt + 1), t)
        first = jnp.minimum(start, cap - region)
        ybuf_ref[e * region:(e + 1) * region, :] = y_ref[e, pl.ds(pl.multiple_of(first, PIECE), region), :]
        s = slot_ref[0, e:e + 1, :]
        hit = local_i == jnp.where(s >= 0.0, s - first.astype(F32), -1.0)
        hits.append(jnp.where(hit, 1.0, 0.0).astype(BF16))
    acc_ref[...] = _dot_tn(jnp.concatenate(hits, axis=0), ybuf_ref[...])
    over_i = lax.broadcasted_iota(jnp.int32, (TOK_BLOCK, tt), 0)

    def overflow(e, carry):
        start, length = _window(bounds_ref, (b * N_EXPERTS + e) * (nt + 1), t)
        base = jnp.minimum(start, cap - region) + region
        n_over = start + length - base

        @pl.when(n_over > 0)
        def _():
            def copy(i, carry):
                obuf_ref[pl.ds(pl.multiple_of(i * PIECE, PIECE), PIECE), :] = (
                    y_ref[e, pl.ds(pl.multiple_of(base + i * PIECE, PIECE), PIECE), :])
                return carry

            def clear(i, carry):
                obuf_ref[pl.ds(pl.multiple_of(n_over + i * PIECE, PIECE), PIECE), :] = jnp.zeros((PIECE, D_MODEL), BF16)
                return carry

            lax.fori_loop(0, n_over // PIECE, copy, 0)
            lax.fori_loop(0, (TOK_BLOCK - n_over) // PIECE, clear, 0)
            hit = (over_i + base).astype(F32) == slot_ref[0, pl.ds(e, 1), :]
            acc_ref[...] += _dot_tn(jnp.where(hit, 1.0, 0.0).astype(BF16), obuf_ref[...])

        return carry

    lax.fori_loop(0, N_EXPERTS, overflow, 0)
    x = x_ref[...] + mod_ref[0, 5:6, :] * acc_ref[...]
    var = jnp.mean(x * x, axis=-1, keepdims=True)
    o_ref[...] = x * lax.rsqrt(var + NORM_EPS) * g_ref[...]


def _combine(bounds, y, slot, x_mid, mod, final_g, *, n_batch, n_tok, row_block_off, mod_base, mod_per_batch):
    cap = CAPACITY_FACTOR * n_tok // N_EXPERTS
    tt = TOK_BLOCK
    nt = n_tok // tt
    region = min(64, cap)
    assert cap - region <= TOK_BLOCK
    return pl.pallas_call(
        functools.partial(_combine_kernel, nt=nt, region=region),
        out_shape=jax.ShapeDtypeStruct((n_batch * n_tok, D_MODEL), F32),
        grid_spec=pltpu.PrefetchScalarGridSpec(
            num_scalar_prefetch=1,
            grid=(n_batch, nt),
            in_specs=[pl.BlockSpec((N_EXPERTS, cap, D_MODEL), lambda b, t, s: (0, row_block_off + b, 0)),
                      pl.BlockSpec((1, N_EXPERTS, tt), lambda b, t, s: (b, 0, t)),
                      pl.BlockSpec((tt, D_MODEL), lambda b, t, s: (b * nt + t, 0)),
                      pl.BlockSpec((1, N_MOD, D_MODEL), lambda b, t, s: (mod_base + b * mod_per_batch, 0, 0)),
                      pl.BlockSpec((1, D_MODEL), lambda b, t, s: (0, 0))],
            out_specs=pl.BlockSpec((tt, D_MODEL), lambda b, t, s: (b * nt + t, 0)),
            scratch_shapes=[pltpu.VMEM((N_EXPERTS * region, D_MODEL), BF16),
                            pltpu.VMEM((TOK_BLOCK, D_MODEL), BF16), pltpu.VMEM((tt, D_MODEL), F32)]),
        compiler_params=_cparams("arbitrary", "arbitrary"),
        name="combine_final_norm",
    )(bounds, y, slot, x_mid, mod, final_g)


def kernel(x_prompt, x_sample, cache_k, cache_v, state_hgrn, c, c_ctx, w_ada, b_ada, norm1_g, w_in, hg_lb,
           hg_norm_g, attn_sink, w_out, norm2_g, w_router, w_gate, w_up, w_down, final_norm_g):
    n_p, t_p, _ = x_prompt.shape
    n_s, t_s, _ = x_sample.shape
    assert w_ada.shape[0] == 1 and 1 + n_s <= COND_ROWS
    layer = 0

    cond = jnp.zeros((COND_ROWS, D_MODEL), F32).at[0].set(c_ctx).at[1:1 + n_s].set(c)
    mod = _ada_modulation(cond, w_ada[layer], b_ada[layer]).reshape(COND_ROWS, N_MOD, D_MODEL)
    lb = jnp.cumsum(jax.nn.softmax(hg_lb.astype(F32), axis=0), axis=0)[layer]
    w_in_b = w_in[layer].astype(BF16)
    w_out_b = w_out[layer].astype(BF16)
    w_router_t = w_router[layer].T.astype(BF16)
    norm1 = norm1_g[layer].reshape(1, D_MODEL)
    norm2 = norm2_g[layer].reshape(1, D_MODEL)
    hg_gain = hg_norm_g[layer].reshape(1, HG_WIDTH)
    final_g = final_norm_g.reshape(1, D_MODEL)
    sink = attn_sink[layer]
    cos, sin_signed = _rope_tables(t_s)

    xp = x_prompt.reshape(n_p * t_p, D_MODEL)
    xs = x_sample.reshape(n_s * t_s, D_MODEL)
    groups = dict(p=dict(mod_base=0, rows_per_mod=n_p * t_p), s=dict(mod_base=1, rows_per_mod=t_s))

    proj_p = _in_projection(xp, mod, norm1, w_in_b, **groups["p"])
    proj_s = _in_projection(xs, mod, norm1, w_in_b, **groups["s"])

    zero_state = jnp.zeros((n_p, 1, 2, HG_HEADS, HEAD_DIM, HEAD_DIM), F32)
    ohg_p, new_state = _hgrn(proj_p, lb, hg_gain, zero_state, n_batch=n_p, n_tok=t_p)
    ohg_s, _ = _hgrn(proj_s, lb, hg_gain, state_hgrn[:, layer:layer + 1].astype(F32), n_batch=n_s, n_tok=t_s)

    oatt_p = _context_attention(proj_p, sink, n_batch=n_p, n_tok=t_p)
    n_ctx = cache_k.shape[2]
    oatt_s = _window_attention(proj_s, cache_k[:, layer].reshape(n_s, n_ctx, KV_WIDTH),
                               cache_v[:, layer].reshape(n_s, n_ctx, KV_WIDTH), sink, cos, sin_signed,
                               n_batch=n_s, n_tok=t_s)

    xmid_p, h2_p = _out_projection(ohg_p, oatt_p, w_out_b, xp, mod, norm2, **groups["p"])
    xmid_s, h2_s = _out_projection(ohg_s, oatt_s, w_out_b, xs, mod, norm2, **groups["s"])

    cap_p = CAPACITY_FACTOR * t_p // N_EXPERTS
    cap_s = CAPACITY_FACTOR * t_s // N_EXPERTS
    off_s = n_p * cap_p // cap_s
    xg_p, gate_p, slot_p = _route_gather_requests(h2_p, w_router_t, n_batch=n_p, n_tok=t_p)
    slot_s, aff_s, bounds_s = _route_only(h2_s, w_router_t, n_batch=n_s, n_tok=t_s)
    xg, gate = _gather_windows(bounds_s, slot_s, aff_s, h2_s, xg_p, gate_p, n_batch=n_s, n_tok=t_s)
    bounds_p = jnp.tile(jnp.array([0, cap_p], jnp.int32), n_p * N_EXPERTS)

    y = _experts(xg, gate, w_gate[layer], w_up[layer], w_down[layer])

    y_prompt = _combine(bounds_p, y, slot_p, xmid_p, mod, final_g, n_batch=n_p, n_tok=t_p, row_block_off=0,
                        mod_base=0, mod_per_batch=0)
    y_sample = _combine(bounds_s, y, slot_s, xmid_s, mod, final_g, n_batch=n_s, n_tok=t_s, row_block_off=off_s,
                        mod_base=1, mod_per_batch=1)

    k_col = 5 * HG_WIDTH + ATT_WIDTH
    new_k = proj_p[:, k_col:k_col + KV_WIDTH].reshape(n_p, 1, t_p, ATT_KV_HEADS, HEAD_DIM)
    new_v = proj_p[:, k_col + KV_WIDTH:k_col + 2 * KV_WIDTH].reshape(n_p, 1, t_p, ATT_KV_HEADS, HEAD_DIM)
    return (y_prompt.reshape(n_p, t_p, D_MODEL), y_sample.reshape(n_s, t_s, D_MODEL), new_k, new_v, new_state)
```

```python
import functools

import jax
import jax.numpy as jnp
import numpy as np
from jax import lax
from jax.experimental import pallas as pl
from jax.experimental.pallas import tpu as pltpu

F32 = jnp.float32
BF16 = jnp.bfloat16

D_MODEL = 2048
HG_WIDTH = 1024
HG_HEADS = 8
HEAD_DIM = 128
ATT_HEADS = 8
ATT_KV_HEADS = 2
ATT_GROUP = ATT_HEADS // ATT_KV_HEADS
KV_WIDTH = ATT_KV_HEADS * HEAD_DIM
ATT_WIDTH = ATT_HEADS * HEAD_DIM
ATT_BLOCK = 128
GRID_W = 64
ROPE_BASE = 10000.0
ROPE_FREQS = HEAD_DIM // 4
N_EXPERTS = 16
CAPACITY_FACTOR = 2
EXPERT_FF = 5632
NORM_EPS = 1e-6
IN_WIDTH = 5 * HG_WIDTH + ATT_WIDTH + 2 * KV_WIDTH
N_MOD = 6
COND_ROWS = 16

HG_CHUNK = 128
HG_DIAG = 8
SEL_BLOCK = 256
TOK_BLOCK = 256
PIECE = 16

VMEM_LIMIT = 56 * 1024 * 1024


def _cparams(*sem):
    return pltpu.CompilerParams(dimension_semantics=sem, vmem_limit_bytes=VMEM_LIMIT)


def _sigmoid(x):
    return 1.0 / (1.0 + jnp.exp(-x))


def _silu(x):
    return x * _sigmoid(x)


def _dot(a, b):
    return jnp.dot(a, b, preferred_element_type=F32)


def _dot_nt(a, b):
    return lax.dot_general(a, b, (((1,), (1,)), ((), ())), preferred_element_type=F32)


def _dot_tn(a, b):
    return lax.dot_general(a, b, (((0,), (0,)), ((), ())), preferred_element_type=F32)


def _ada_kernel(c_ref, w_ref, b_ref, o_ref):
    s = _silu(c_ref[...]).astype(BF16)
    o_ref[...] = _dot(s, w_ref[...].astype(BF16)) + b_ref[...]


def _ada_modulation(cond, w_ada, b_ada):
    tn = 1024
    n = w_ada.shape[1]
    return pl.pallas_call(
        _ada_kernel,
        out_shape=jax.ShapeDtypeStruct((COND_ROWS, n), F32),
        grid=(n // tn,),
        in_specs=[pl.BlockSpec((COND_ROWS, D_MODEL), lambda j: (0, 0)),
                  pl.BlockSpec((D_MODEL, tn), lambda j: (0, j)),
                  pl.BlockSpec((1, tn), lambda j: (0, j))],
        out_specs=pl.BlockSpec((COND_ROWS, tn), lambda j: (0, j)),
        compiler_params=_cparams("arbitrary"),
        name="ada_modulation",
    )(cond, w_ada, b_ada.reshape(1, n))


def _norm_modulate(x, gain, shift, scale):
    var = jnp.mean(x * x, axis=-1, keepdims=True)
    return (x * lax.rsqrt(var + NORM_EPS) * gain) * (1.0 + scale) + shift


def _inproj_kernel(x_ref, mod_ref, g_ref, w_ref, o_ref, h_ref, *, rows):
    @pl.when(pl.program_id(1) == 0)
    def _():
        shift = mod_ref[0, 0:1, :]
        scale = mod_ref[0, 1:2, :]
        gain = g_ref[...]

        def body(r, carry):
            sl = pl.ds(pl.multiple_of(r * rows, rows), rows)
            h_ref[sl, :] = _norm_modulate(x_ref[sl, :], gain, shift, scale).astype(BF16)
            return carry

        lax.fori_loop(0, x_ref.shape[0] // rows, body, 0)

    w = w_ref[...].astype(BF16)
    chunk = 256
    for r in range(x_ref.shape[0] // chunk):
        sl = slice(r * chunk, (r + 1) * chunk)
        o_ref[sl, :] = _dot(h_ref[sl, :], w)


def _in_projection(x, mod, gain, w, *, mod_base, rows_per_mod):
    m = x.shape[0]
    tm, tn = 1024, 512
    return pl.pallas_call(
        functools.partial(_inproj_kernel, rows=128),
        out_shape=jax.ShapeDtypeStruct((m, IN_WIDTH), F32),
        grid=(m // tm, IN_WIDTH // tn),
        in_specs=[pl.BlockSpec((tm, D_MODEL), lambda i, j: (i, 0)),
                  pl.BlockSpec((1, N_MOD, D_MODEL), lambda i, j: (mod_base + (i * tm) // rows_per_mod, 0, 0)),
                  pl.BlockSpec((1, D_MODEL), lambda i, j: (0, 0)),
                  pl.BlockSpec((D_MODEL, tn), lambda i, j: (0, j))],
        out_specs=pl.BlockSpec((tm, tn), lambda i, j: (i, j)),
        scratch_shapes=[pltpu.VMEM((tm, D_MODEL), BF16)],
        compiler_params=_cparams("parallel", "arbitrary"),
        name="in_projection",
    )(x, mod, gain, w)


def _hgrn_codes(reverse):
    L = HG_CHUNK
    t = lax.broadcasted_iota(jnp.int32, (L, L), 0)
    s = lax.broadcasted_iota(jnp.int32, (L, L), 1)
    code = jnp.where(t == s, 1, 0)
    h = L // 2
    while h >= 1:
        same = (t & ~(2 * h - 1)) == (s & ~(2 * h - 1))
        t_hi = (t & h) != 0
        s_hi = (s & h) != 0
        pair = (s_hi & ~t_hi) if reverse else (t_hi & ~s_hi)
        code = jnp.where(same & pair, h * 16, code)
        h //= 2
    return code


def _cumsum_rows(tri_bf16, g):
    g1 = g.astype(BF16)
    r1 = g - g1.astype(F32)
    g2 = r1.astype(BF16)
    g3 = (r1 - g2.astype(F32)).astype(BF16)
    return _dot(tri_bf16, g1) + _dot(tri_bf16, g2) + _dot(tri_bf16, g3)


def _hgrn_intra(q, k, f, b, v_bf, code, reverse):
    L = HG_CHUNK
    G = HG_DIAG
    q_bf = q.astype(BF16)
    k_bf = k.astype(BF16)

    def level(h, ref, att):
        neg_abs = pltpu.bitcast(pltpu.bitcast(b - ref, jnp.int32) | jnp.int32(-2 ** 31), F32)
        e = jnp.exp2(neg_abs.astype(BF16))
        return jnp.where(code == h * 16, _dot_nt(q_bf * e, k_bf * e), att)

    att = jnp.where(code == 1, _dot_nt(q_bf, k_bf), 0.0)
    h = L // 2
    while h >= G:
        parts = []
        for p in range(L // (2 * h)):
            m = p * 2 * h + (h if reverse else h - 1)
            parts.append(jnp.broadcast_to(b[m:m + 1, :], (2 * h, HEAD_DIM)))
        att = level(h, parts[0] if len(parts) == 1 else jnp.concatenate(parts, axis=0), att)
        h //= 2

    b3 = b.reshape(L // G, G, HEAD_DIM)

    def group_row(r):
        return jnp.broadcast_to(b3[:, r:r + 1, :], (L // G, G, HEAD_DIM)).reshape(L, HEAD_DIM)

    row = lax.broadcasted_iota(jnp.int32, (L, HEAD_DIM), 0)
    att = level(4, group_row(4 if reverse else 3), att)
    lo, hi = (2, 6) if reverse else (1, 5)
    att = level(2, jnp.where((row & 4) == 0, group_row(lo), group_row(hi)), att)
    att = jnp.where(code == 16, _dot_nt(q_bf * f.astype(BF16), k_bf), att)
    return _dot(att.astype(BF16), v_bf)


def _hgrn_prepare(q, f, v, code, tri, reverse):
    L = HG_CHUNK
    k = 1.0 - f
    b = _cumsum_rows(tri, jnp.log2(f))
    b_tot = b[0:1, :] if reverse else b[L - 1:L, :]
    v_bf = v.astype(BF16)
    o_intra = _hgrn_intra(q, k, f, b, v_bf, code, reverse)
    q_in = (q * jnp.exp2(b)).astype(BF16)
    k_out = (k * jnp.exp2(b_tot - b)).astype(BF16)
    return o_intra, q_in, k_out, v_bf, jnp.exp2(b_tot)


def _hgrn_advance(st, prepared):
    o_intra, q_in, k_out, v_bf, decay = prepared
    return o_intra + _dot_nt(q_in, st.astype(BF16)), st * decay + _dot_tn(v_bf, k_out)


def _hgrn_kernel(*refs, n_tok, has_state):
    if has_state:
        (q_ref, ff_ref, fb_ref, v_ref, gate_ref, lb_ref, ng_ref, s0_ref,
         o_ref, sout_ref, of_ref, ob_ref, code_ref) = refs
    else:
        (q_ref, ff_ref, fb_ref, v_ref, gate_ref, lb_ref, ng_ref,
         o_ref, sout_ref, of_ref, ob_ref, code_ref) = refs
    L = HG_CHUNK
    nc = n_tok // L
    lb_f = lb_ref[0:1, :]
    lb_b = lb_ref[1:2, :]

    @pl.when((pl.program_id(0) == 0) & (pl.program_id(1) == 0))
    def _():
        code_ref[0] = _hgrn_codes(False)
        code_ref[1] = _hgrn_codes(True)

    code_f = code_ref[0]
    code_b = code_ref[1]
    ti = lax.broadcasted_iota(jnp.int32, (L, L), 0)
    si = lax.broadcasted_iota(jnp.int32, (L, L), 1)
    tri_f = jnp.where(si <= ti, 1.0, 0.0).astype(BF16)
    tri_b = jnp.where(si >= ti, 1.0, 0.0).astype(BF16)

    def prepare(row, f_ref, lb, code, tri, reverse):
        sl = pl.ds(row, L)
        q = _silu(q_ref[sl, :])
        f = lb + (1.0 - lb) * _sigmoid(f_ref[sl, :])
        return sl, _hgrn_prepare(q, f, v_ref[sl, :], code, tri, reverse)

    gain = ng_ref[...]

    def finish(sl, o):
        var = jnp.mean(o * o, axis=-1, keepdims=True)
        o_ref[sl, :] = ((o * lax.rsqrt(var + NORM_EPS) * gain) * _silu(gate_ref[sl, :])).astype(BF16)

    def make_body(other_direction):
        def body(c, states):
            st_f, st_b = states
            fwd, bwd = [], []
            for u in range(unroll):
                cu = c * unroll + u
                fwd.append(prepare(pl.multiple_of(cu * L, L), ff_ref, lb_f, code_f, tri_f, False))
                bwd.append(prepare(pl.multiple_of((nc - 1 - cu) * L, L), fb_ref, lb_b, code_b, tri_b, True))
            out_f, out_b = [], []
            for (sl_f, prep_f), (sl_b, prep_b) in zip(fwd, bwd):
                o_f, st_f = _hgrn_advance(st_f, prep_f)
                o_b, st_b = _hgrn_advance(st_b, prep_b)
                out_f.append((sl_f, o_f))
                out_b.append((sl_b, o_b))
            if other_direction == "later":
                for sl, o in out_f:
                    of_ref[sl, :] = o
                for sl, o in out_b:
                    ob_ref[sl, :] = o
            elif other_direction == "now":
                for u in range(unroll):
                    finish(out_f[u][0], out_f[u][1] + out_b[unroll - 1 - u][1])
            else:
                for sl, o in out_f:
                    finish(sl, o + ob_ref[sl, :])
                for sl, o in out_b:
                    finish(sl, of_ref[sl, :] + o)
            return st_f, st_b
        return body

    unroll = 2
    assert nc % unroll == 0
    n_iter = nc // unroll
    half = n_iter // 2
    if has_state:
        states = (s0_ref[0, 0, 0, 0].T, s0_ref[0, 0, 1, 0].T)
    else:
        states = (jnp.zeros((HEAD_DIM, HEAD_DIM), F32), jnp.zeros((HEAD_DIM, HEAD_DIM), F32))
    states = lax.fori_loop(0, half, make_body("later"), states)
    if n_iter % 2:
        states = make_body("now")(half, states)
    st_f, st_b = lax.fori_loop(half + n_iter % 2, n_iter, make_body("earlier"), states)
    sout_ref[0, 0, 0, 0] = st_f.T
    sout_ref[0, 0, 1, 0] = st_b.T


def _hgrn(proj, lb, norm_g, state, *, n_batch, n_tok):
    col = lambda k: (lambda b, h: (b, k * HG_HEADS + h))
    tok_spec = lambda k: pl.BlockSpec((n_tok, HEAD_DIM), col(k))
    st_spec = pl.BlockSpec((1, 1, 2, 1, HEAD_DIM, HEAD_DIM), lambda b, h: (b, 0, 0, h, 0, 0))
    has_state = state is not None
    return pl.pallas_call(
        functools.partial(_hgrn_kernel, n_tok=n_tok, has_state=has_state),
        out_shape=(jax.ShapeDtypeStruct((n_batch * n_tok, HG_WIDTH), BF16),
                   jax.ShapeDtypeStruct((n_batch, 1, 2, HG_HEADS, HEAD_DIM, HEAD_DIM), F32)),
        grid=(n_batch, HG_HEADS),
        in_specs=[tok_spec(0), tok_spec(1), tok_spec(2), tok_spec(3), tok_spec(4),
                  pl.BlockSpec((2, HEAD_DIM), lambda b, h: (0, h)),
                  pl.BlockSpec((1, HEAD_DIM), lambda b, h: (0, h))] + ([st_spec] if has_state else []),
        out_specs=(pl.BlockSpec((n_tok, HEAD_DIM), lambda b, h: (b, h)), st_spec),
        scratch_shapes=[pltpu.VMEM((n_tok, HEAD_DIM), F32), pltpu.VMEM((n_tok, HEAD_DIM), F32),
                        pltpu.VMEM((2, HG_CHUNK, HG_CHUNK), jnp.int32)],
        compiler_params=_cparams("arbitrary", "arbitrary"),
        name="hgrn2_scan",
    )(proj, proj, proj, proj, proj, lb, norm_g, *((state,) if has_state else ()))


def _stack_heads(x, kvh):
    return jnp.concatenate(
        [x[:, (kvh * ATT_GROUP + g) * HEAD_DIM:(kvh * ATT_GROUP + g + 1) * HEAD_DIM] for g in range(ATT_GROUP)],
        axis=0)


def _sink_column(sink_ref, kvh, rows):
    return jnp.concatenate(
        [jnp.full((rows, 1), sink_ref[kvh * ATT_GROUP + g], F32) for g in range(ATT_GROUP)],
        axis=0) * (1.0 / SOFTMAX_SCALE)


SOFTMAX_SCALE = HEAD_DIM ** -0.5
EXP2_SCALE = SOFTMAX_SCALE * 1.4426950408889634


def _ones_column(rows):
    lane = lax.broadcasted_iota(jnp.int32, (rows, HEAD_DIM), 1)
    return jnp.where(lane == 0, 1.0, 0.0).astype(BF16)


def _softmax_av(scores, values, sink_col):
    m = sink_col
    for s in scores:
        m = jnp.maximum(m, jnp.max(s, axis=-1, keepdims=True))
    acc = None
    for s, v in zip(scores, values):
        pv = _dot(jnp.exp2(((s - m) * EXP2_SCALE).astype(BF16)), v)
        acc = pv if acc is None else acc + pv
    denom = acc[:, HEAD_DIM:HEAD_DIM + 1] + jnp.exp2((sink_col - m) * EXP2_SCALE)
    return acc[:, :HEAD_DIM] / denom


def _ctx_attn_kernel(sink_ref, q_ref, k_ref, v_ref, o_ref):
    rows = q_ref.shape[0]
    q_all = q_ref[...]
    ones = _ones_column(k_ref.shape[0])
    for kvh in range(ATT_KV_HEADS):
        q = _stack_heads(q_all, kvh).astype(BF16)
        k = k_ref[:, kvh * HEAD_DIM:(kvh + 1) * HEAD_DIM].astype(BF16)
        v = jnp.concatenate([v_ref[:, kvh * HEAD_DIM:(kvh + 1) * HEAD_DIM].astype(BF16), ones], axis=1)
        o = _softmax_av([_dot_nt(q, k)], [v], _sink_column(sink_ref, kvh, rows))
        for g in range(ATT_GROUP):
            hd = kvh * ATT_GROUP + g
            o_ref[:, hd * HEAD_DIM:(hd + 1) * HEAD_DIM] = o[g * rows:(g + 1) * rows, :].astype(BF16)


def _context_attention(proj, sink, *, n_batch, n_tok):
    q_col = 5 * HG_WIDTH // ATT_WIDTH
    k_col = (5 * HG_WIDTH + ATT_WIDTH) // KV_WIDTH
    return pl.pallas_call(
        _ctx_attn_kernel,
        out_shape=jax.ShapeDtypeStruct((n_batch * n_tok, ATT_WIDTH), BF16),
        grid_spec=pltpu.PrefetchScalarGridSpec(
            num_scalar_prefetch=1,
            grid=(n_batch,),
            in_specs=[pl.BlockSpec((n_tok, ATT_WIDTH), lambda b, s: (b, q_col)),
                      pl.BlockSpec((n_tok, KV_WIDTH), lambda b, s: (b, k_col)),
                      pl.BlockSpec((n_tok, KV_WIDTH), lambda b, s: (b, k_col + 1))],
            out_specs=pl.BlockSpec((n_tok, ATT_WIDTH), lambda b, s: (b, 0))),
        compiler_params=_cparams("parallel"),
        name="context_attention",
    )(sink, proj, proj, proj)


def _rope(x, cos, sin_signed, even_group):
    partner = jnp.where(even_group, pltpu.roll(x, HEAD_DIM - ROPE_FREQS, 1), pltpu.roll(x, ROPE_FREQS, 1))
    return x * cos + partner * sin_signed


def _win_attn_kernel(sink_ref, q_ref, k_ref, v_ref, ck_ref, cv_ref, cos_ref, sin_ref, o_ref,
                     kpad_ref, vpad_ref, *, n_tok):
    blk = ATT_BLOCK
    nb = n_tok // blk
    i = pl.program_id(1)
    lane = lax.broadcasted_iota(jnp.int32, (blk, HEAD_DIM), 1)
    even_group = (lane & ROPE_FREQS) == 0

    @pl.when(i == 0)
    def _():
        kpad_ref[0:blk, :] = jnp.zeros((blk, KV_WIDTH), BF16)
        kpad_ref[blk + n_tok:2 * blk + n_tok, :] = jnp.zeros((blk, KV_WIDTH), BF16)
        vpad_ref[0:blk, :] = jnp.zeros((blk, 2 * KV_WIDTH), BF16)
        vpad_ref[blk + n_tok:2 * blk + n_tok, :] = jnp.zeros((blk, 2 * KV_WIDTH), BF16)
        ones = _ones_column(blk)

        def body(r, carry):
            src = pl.ds(pl.multiple_of(r * blk, blk), blk)
            dst = pl.ds(pl.multiple_of((r + 1) * blk, blk), blk)
            cos = cos_ref[src, :]
            sin = sin_ref[src, :]
            for kvh in range(ATT_KV_HEADS):
                cols = slice(kvh * HEAD_DIM, (kvh + 1) * HEAD_DIM)
                kpad_ref[dst, cols] = _rope(k_ref[src, cols], cos, sin, even_group).astype(BF16)
                vpad_ref[dst, 2 * kvh * HEAD_DIM:(2 * kvh + 1) * HEAD_DIM] = v_ref[src, cols].astype(BF16)
                vpad_ref[dst, (2 * kvh + 1) * HEAD_DIM:(2 * kvh + 2) * HEAD_DIM] = ones
            return carry

        lax.fori_loop(0, nb, body, 0)

    rows = pl.ds(pl.multiple_of(i * blk, blk), blk)
    cos = cos_ref[rows, :]
    sin = sin_ref[rows, :]
    band = pl.ds(pl.multiple_of(i * blk, blk), 3 * blk)
    r = lax.broadcasted_iota(jnp.int32, (blk, 3 * blk), 0)
    j = lax.broadcasted_iota(jnp.int32, (blk, 3 * blk), 1)
    kpos = j + (i - 1) * blk
    valid = (j >= r) & (j <= r + 2 * blk) & (kpos >= 0) & (kpos < n_tok)
    valid = jnp.concatenate([valid] * ATT_GROUP, axis=0)
    q_all = q_ref[...]
    ctx_ones = _ones_column(ck_ref.shape[1])
    for kvh in range(ATT_KV_HEADS):
        cols = slice(kvh * HEAD_DIM, (kvh + 1) * HEAD_DIM)
        q = jnp.concatenate(
            [_rope(q_all[:, (kvh * ATT_GROUP + g) * HEAD_DIM:(kvh * ATT_GROUP + g + 1) * HEAD_DIM],
                   cos, sin, even_group) for g in range(ATT_GROUP)], axis=0).astype(BF16)
        s_ctx = _dot_nt(q, ck_ref[0, :, cols].astype(BF16))
        s_loc = jnp.where(valid, _dot_nt(q, kpad_ref[band, cols]), -jnp.inf)
        v_ctx = jnp.concatenate([cv_ref[0, :, cols].astype(BF16), ctx_ones], axis=1)
        v_loc = vpad_ref[band, 2 * kvh * HEAD_DIM:(2 * kvh + 2) * HEAD_DIM]
        o = _softmax_av([s_ctx, s_loc], [v_ctx, v_loc], _sink_column(sink_ref, kvh, blk))
        for g in range(ATT_GROUP):
            hd = kvh * ATT_GROUP + g
            o_ref[:, hd * HEAD_DIM:(hd + 1) * HEAD_DIM] = o[g * blk:(g + 1) * blk, :].astype(BF16)


def _window_attention(proj, cache_k, cache_v, sink, cos, sin_signed, *, n_batch, n_tok):
    nb = n_tok // ATT_BLOCK
    n_ctx = cache_k.shape[1]
    q_col = 5 * HG_WIDTH // ATT_WIDTH
    k_col = (5 * HG_WIDTH + ATT_WIDTH) // KV_WIDTH
    table = pl.BlockSpec((n_tok, HEAD_DIM), lambda b, i, s: (0, 0))
    cache = pl.BlockSpec((1, n_ctx, KV_WIDTH), lambda b, i, s: (b, 0, 0))
    return pl.pallas_call(
        functools.partial(_win_attn_kernel, n_tok=n_tok),
        out_shape=jax.ShapeDtypeStruct((n_batch * n_tok, ATT_WIDTH), BF16),
        grid_spec=pltpu.PrefetchScalarGridSpec(
            num_scalar_prefetch=1,
            grid=(n_batch, nb),
            in_specs=[pl.BlockSpec((ATT_BLOCK, ATT_WIDTH), lambda b, i, s: (b * nb + i, q_col)),
                      pl.BlockSpec((n_tok, KV_WIDTH), lambda b, i, s: (b, k_col)),
                      pl.BlockSpec((n_tok, KV_WIDTH), lambda b, i, s: (b, k_col + 1)),
                      cache, cache, table, table],
            out_specs=pl.BlockSpec((ATT_BLOCK, ATT_WIDTH), lambda b, i, s: (b * nb + i, 0)),
            scratch_shapes=[pltpu.VMEM((n_tok + 2 * ATT_BLOCK, KV_WIDTH), BF16),
                            pltpu.VMEM((n_tok + 2 * ATT_BLOCK, 2 * KV_WIDTH), BF16)]),
        compiler_params=_cparams("parallel", "arbitrary"),
        name="window_attention",
    )(sink, proj, proj, proj, cache_k, cache_v, cos, sin_signed)


def _rope_tables(n_tok):
    rows = n_tok // GRID_W
    row = np.repeat(np.arange(rows), GRID_W).astype(np.float32)
    col = np.tile(np.arange(GRID_W), rows).astype(np.float32)
    inv = np.float32(ROPE_BASE) ** (-np.arange(ROPE_FREQS, dtype=np.float32) / np.float32(ROPE_FREQS))
    ar, ac = row[:, None] * inv, col[:, None] * inv
    cr, sr, cc, sc = np.cos(ar), np.sin(ar), np.cos(ac), np.sin(ac)
    return (jnp.asarray(np.concatenate([cr, cr, cc, cc], axis=1), F32),
            jnp.asarray(np.concatenate([-sr, sr, -sc, sc], axis=1), F32))


def _outproj_kernel(hg_ref, att_ref, w_ref, x_ref, mod_ref, g_ref, xo_ref, h_ref):
    chunk = 256
    for r in range(x_ref.shape[0] // chunk):
        sl = slice(r * chunk, (r + 1) * chunk)
        mix = _dot(hg_ref[sl, :], w_ref[0:HG_WIDTH, :]) + _dot(att_ref[sl, :], w_ref[HG_WIDTH:, :])
        x = x_ref[sl, :] + mod_ref[0, 2:3, :] * mix
        xo_ref[sl, :] = x
        h_ref[sl, :] = _norm_modulate(x, g_ref[...], mod_ref[0, 3:4, :], mod_ref[0, 4:5, :]).astype(BF16)


def _out_projection(o_hg, o_att, w_bf16, x, mod, gain, *, mod_base, rows_per_mod):
    m = x.shape[0]
    tm = 512
    row = lambda i: (i, 0)
    return pl.pallas_call(
        _outproj_kernel,
        out_shape=(jax.ShapeDtypeStruct((m, D_MODEL), F32), jax.ShapeDtypeStruct((m, D_MODEL), BF16)),
        grid=(m // tm,),
        in_specs=[pl.BlockSpec((tm, HG_WIDTH), row),
                  pl.BlockSpec((tm, ATT_WIDTH), row),
                  pl.BlockSpec((HG_WIDTH + ATT_WIDTH, D_MODEL), lambda i: (0, 0)),
                  pl.BlockSpec((tm, D_MODEL), row),
                  pl.BlockSpec((1, N_MOD, D_MODEL), lambda i: (mod_base + (i * tm) // rows_per_mod, 0, 0)),
                  pl.BlockSpec((1, D_MODEL), lambda i: (0, 0))],
        out_specs=(pl.BlockSpec((tm, D_MODEL), row), pl.BlockSpec((tm, D_MODEL), row)),
        compiler_params=_cparams("parallel"),
        name="out_projection",
    )(o_hg, o_att, w_bf16, x, mod, gain)


def _prefix_count(x):
    n = x.shape[1]
    i = lax.broadcasted_iota(jnp.int32, (SEL_BLOCK, SEL_BLOCK), 0)
    j = lax.broadcasted_iota(jnp.int32, (SEL_BLOCK, SEL_BLOCK), 1)
    upper = jnp.where(i < j, 1.0, 0.0).astype(BF16)
    off = jnp.zeros((x.shape[0], 1), F32)
    outs = []
    for blk in range(n // SEL_BLOCK):
        xb = x[:, blk * SEL_BLOCK:(blk + 1) * SEL_BLOCK]
        outs.append(_dot(xb.astype(BF16), upper) + off)
        off = off + jnp.sum(xb, axis=-1, keepdims=True)
    return outs[0] if len(outs) == 1 else jnp.concatenate(outs, axis=1)


def _route_select(h_ref, w_ref, cap):
    logits = _dot_nt(w_ref[...], h_ref[...])
    ex = jnp.exp(logits - jnp.max(logits, axis=0, keepdims=True))
    aff = ex / jnp.sum(ex, axis=0, keepdims=True)
    bits = pltpu.bitcast(aff, jnp.int32)

    def bisect(it, thr):
        cand = thr | jnp.left_shift(jnp.int32(1), 30 - it)
        cnt = jnp.sum(jnp.where(bits >= cand, 1.0, 0.0), axis=-1, keepdims=True)
        return jnp.where(cnt >= cap, cand, thr)

    thr = lax.fori_loop(0, 31, bisect, jnp.zeros((N_EXPERTS, 1), jnp.int32))
    above = jnp.where(bits > thr, 1.0, 0.0)
    tied = jnp.where(bits == thr, 1.0, 0.0)
    room = cap - jnp.sum(above, axis=-1, keepdims=True)
    sel = above + tied * jnp.where(_prefix_count(tied) < room, 1.0, 0.0)
    return jnp.where(sel > 0.0, _prefix_count(sel), -1.0), aff


def _one_hot_gather(slot_rows, aff_rows, h_ref, rows, cap):
    n_tok = h_ref.shape[0]
    c = (lax.broadcasted_iota(jnp.int32, (rows, n_tok), 0) & (cap - 1)).astype(F32)
    hit = c == slot_rows
    x = _dot(jnp.where(hit, 1.0, 0.0).astype(BF16), h_ref[...]).astype(BF16)
    return x, jnp.sum(jnp.where(hit, aff_rows, 0.0), axis=-1, keepdims=True)


def _route_request_kernel(h_ref, w_ref, x_ref, g_ref, slot_ref, *, cap):
    n_tok = h_ref.shape[0]
    slot, aff = _route_select(h_ref, w_ref, cap)
    slot_ref[0] = slot
    per_row = lambda a: jnp.concatenate(
        [jnp.broadcast_to(a[e:e + 1, :], (cap, n_tok)) for e in range(N_EXPERTS)], axis=0)
    x, g = _one_hot_gather(per_row(slot), per_row(aff), h_ref, N_EXPERTS * cap, cap)
    for e in range(N_EXPERTS):
        x_ref[e] = x[e * cap:(e + 1) * cap, :]
        g_ref[e] = g[e * cap:(e + 1) * cap, :]


def _route_only_kernel(h_ref, w_ref, slot_ref, aff_ref, bounds_ref, *, cap):
    n_tok = h_ref.shape[0]
    slot, aff = _route_select(h_ref, w_ref, cap)
    slot_ref[0] = slot
    aff_ref[0] = aff
    n = lax.broadcasted_iota(jnp.int32, (n_tok, 128), 0)
    t = lax.broadcasted_iota(jnp.int32, (n_tok, 128), 1)
    before = jnp.where(n < t * TOK_BLOCK, 1.0, 0.0).astype(BF16)
    chosen = jnp.where(slot >= 0.0, 1.0, 0.0).astype(BF16)
    bounds_ref[0] = _dot(chosen, before).astype(jnp.int32)


def _gather_window_kernel(bounds_ref, slot_ref, aff_ref, h_ref, xh_ref, gh_ref, x_ref, g_ref, xo_ref, go_ref, *,
                          n_batch, group):
    b = pl.program_id(0)
    eg = pl.program_id(1)
    n_tok = h_ref.shape[0]
    nt = n_tok // TOK_BLOCK
    cap = x_ref.shape[1]
    region = TOK_BLOCK // group

    @pl.when(b >= n_batch)
    def _():
        x_ref[...] = xh_ref[...]
        g_ref[...] = gh_ref[...]

    @pl.when(b < n_batch)
    def _():
        x_ref[...] = jnp.zeros(x_ref.shape, BF16)
        g_ref[...] = jnp.zeros(g_ref.shape, F32)
        local_i = lax.broadcasted_iota(jnp.int32, (region, TOK_BLOCK), 0).astype(F32)
        over_i = lax.broadcasted_iota(jnp.int32, (TOK_BLOCK, TOK_BLOCK), 0)

        def token_block(t, carry):
            toks = pl.ds(pl.multiple_of(t * TOK_BLOCK, TOK_BLOCK), TOK_BLOCK)
            hits, gates, firsts, n_overs, slots, affs = [], [], [], [], [], []
            for k in range(group):
                e = eg * group + k
                start, length = _window(bounds_ref, (b * N_EXPERTS + e) * (nt + 1), t)
                first = jnp.minimum(start, cap - region)
                s = slot_ref[0, pl.ds(e, 1), toks]
                a = aff_ref[0, pl.ds(e, 1), toks]
                hit = local_i == jnp.where(s >= 0.0, s - first.astype(F32), -1.0)
                hits.append(jnp.where(hit, 1.0, 0.0).astype(BF16))
                gates.append(jnp.sum(jnp.where(hit, a, 0.0), axis=-1, keepdims=True))
                firsts.append(first)
                n_overs.append(start + length - first - region)
                slots.append(s)
                affs.append(a)
            xc = _dot(jnp.concatenate(hits, axis=0), h_ref[toks, :]).astype(BF16)
            for k in range(group):
                dst = pl.ds(pl.multiple_of(firsts[k], PIECE), region)
                x_ref[k, dst, :] = x_ref[k, dst, :] + xc[k * region:(k + 1) * region, :]
                g_ref[k, dst, :] = g_ref[k, dst, :] + gates[k]
            for k in range(group):
                @pl.when(n_overs[k] > 0)
                def _(k=k):
                    base = firsts[k] + region
                    hit = (over_i + base).astype(F32) == slots[k]
                    xo_ref[...] = _dot(jnp.where(hit, 1.0, 0.0).astype(BF16), h_ref[toks, :]).astype(BF16)
                    go_ref[...] = jnp.sum(jnp.where(hit, affs[k], 0.0), axis=-1, keepdims=True)

                    def place(i, carry):
                        src = pl.ds(pl.multiple_of(i * PIECE, PIECE), PIECE)
                        dst = pl.ds(pl.multiple_of(base + i * PIECE, PIECE), PIECE)
                        x_ref[k, dst, :] = x_ref[k, dst, :] + xo_ref[src, :]
                        g_ref[k, dst, :] = g_ref[k, dst, :] + go_ref[src, :]
                        return carry

                    lax.fori_loop(0, n_overs[k] // PIECE, place, 0)
            return carry

        lax.fori_loop(0, nt, token_block, 0)


def _route_gather_requests(h, w_router_t, *, n_batch, n_tok):
    cap = CAPACITY_FACTOR * n_tok // N_EXPERTS
    out_block = lambda w: pl.BlockSpec((N_EXPERTS, cap, w), lambda b: (0, b, 0))
    return pl.pallas_call(
        functools.partial(_route_request_kernel, cap=cap),
        out_shape=(jax.ShapeDtypeStruct((N_EXPERTS, n_batch * cap, D_MODEL), BF16),
                   jax.ShapeDtypeStruct((N_EXPERTS, n_batch * cap, 1), F32),
                   jax.ShapeDtypeStruct((n_batch, N_EXPERTS, n_tok), F32)),
        grid=(n_batch,),
        in_specs=[pl.BlockSpec((n_tok, D_MODEL), lambda b: (b, 0)),
                  pl.BlockSpec((N_EXPERTS, D_MODEL), lambda b: (0, 0))],
        out_specs=(out_block(D_MODEL), out_block(1), pl.BlockSpec((1, N_EXPERTS, n_tok), lambda b: (b, 0, 0))),
        compiler_params=_cparams("parallel"),
        name="route_gather_requests",
    )(h, w_router_t)


def _route_only(h, w_router_t, *, n_batch, n_tok):
    cap = CAPACITY_FACTOR * n_tok // N_EXPERTS
    nt = n_tok // TOK_BLOCK
    expert_major = pl.BlockSpec((1, N_EXPERTS, n_tok), lambda b: (b, 0, 0))
    slot, aff, bounds = pl.pallas_call(
        functools.partial(_route_only_kernel, cap=cap),
        out_shape=(jax.ShapeDtypeStruct((n_batch, N_EXPERTS, n_tok), F32),
                   jax.ShapeDtypeStruct((n_batch, N_EXPERTS, n_tok), F32),
                   jax.ShapeDtypeStruct((n_batch, N_EXPERTS, 128), jnp.int32)),
        grid=(n_batch,),
        in_specs=[pl.BlockSpec((n_tok, D_MODEL), lambda b: (b, 0)),
                  pl.BlockSpec((N_EXPERTS, D_MODEL), lambda b: (0, 0))],
        out_specs=(expert_major, expert_major, pl.BlockSpec((1, N_EXPERTS, 128), lambda b: (b, 0, 0))),
        compiler_params=_cparams("parallel"),
        name="route_select",
    )(h, w_router_t)
    return slot, aff, bounds[:, :, :nt + 1].reshape(-1)


def _gather_windows(bounds, slot, aff, h, x_head, g_head, *, n_batch, n_tok):
    cap = CAPACITY_FACTOR * n_tok // N_EXPERTS
    group = 4
    head_rows = x_head.shape[1]
    assert head_rows % cap == 0 and N_EXPERTS % group == 0
    n_head = head_rows // cap
    rows_total = head_rows + n_batch * cap
    last = n_batch - 1
    request = lambda b: jnp.minimum(b, last)
    out_row = lambda b: jnp.where(b < n_batch, n_head + b, b - n_batch)
    head_row = lambda b: jnp.maximum(b - n_batch, 0)
    out_block = lambda w: pl.BlockSpec((group, cap, w), lambda b, g, s: (g, out_row(b), 0))
    head_block = lambda w: pl.BlockSpec((group, cap, w), lambda b, g, s: (g, head_row(b), 0))
    expert_major = pl.BlockSpec((1, N_EXPERTS, n_tok), lambda b, g, s: (request(b), 0, 0))
    return pl.pallas_call(
        functools.partial(_gather_window_kernel, n_batch=n_batch, group=group),
        out_shape=(jax.ShapeDtypeStruct((N_EXPERTS, rows_total, D_MODEL), BF16),
                   jax.ShapeDtypeStruct((N_EXPERTS, rows_total, 1), F32)),
        grid_spec=pltpu.PrefetchScalarGridSpec(
            num_scalar_prefetch=1,
            grid=(n_batch + n_head, N_EXPERTS // group),
            in_specs=[expert_major, expert_major,
                      pl.BlockSpec((n_tok, D_MODEL), lambda b, g, s: (request(b), 0)),
                      head_block(D_MODEL), head_block(1)],
            out_specs=(out_block(D_MODEL), out_block(1)),
            scratch_shapes=[pltpu.VMEM((TOK_BLOCK, D_MODEL), BF16), pltpu.VMEM((TOK_BLOCK, 1), F32)]),
        compiler_params=_cparams("arbitrary", "arbitrary"),
        name="gather_windows",
    )(bounds, slot, aff, h, x_head, g_head)


def _moe_kernel(x_ref, g_ref, wg_ref, wu_ref, wd_ref, y_ref, hid_ref, *, n_ff, rows):
    s = pl.program_id(2)
    tf = wg_ref.shape[2]

    @pl.when(s < n_ff)
    def _():
        wg = wg_ref[0].astype(BF16)
        wu = wu_ref[0].astype(BF16)
        cols = pl.ds(pl.multiple_of(s * tf, tf), tf)
        for r in range(x_ref.shape[1] // rows):
            sl = slice(r * rows, (r + 1) * rows)
            x = x_ref[0, sl, :]
            hid_ref[sl, cols] = (_silu(_dot(x, wg)) * _dot(x, wu)).astype(BF16)

    @pl.when(s >= n_ff)
    def _():
        y = _dot(hid_ref[...], wd_ref[0].astype(BF16))
        y_ref[0] = (y * g_ref[0]).astype(BF16)


def _experts(x, gate, w_gate, w_up, w_down):
    n_rows = x.shape[1]
    tr, tf, tn = n_rows // 2, 512, 256
    n_ff, n_out = EXPERT_FF // tf, D_MODEL // tn
    up_tile = lambda e, r, s: (e, 0, jnp.minimum(s, n_ff - 1))
    out_tile = lambda s: jnp.maximum(s - n_ff, 0)
    return pl.pallas_call(
        functools.partial(_moe_kernel, n_ff=n_ff, rows=256),
        out_shape=jax.ShapeDtypeStruct((N_EXPERTS, n_rows, D_MODEL), BF16),
        grid=(N_EXPERTS, n_rows // tr, n_ff + n_out),
        in_specs=[pl.BlockSpec((1, tr, D_MODEL), lambda e, r, s: (e, r, 0), pipeline_mode=pl.Buffered(1)),
                  pl.BlockSpec((1, tr, 1), lambda e, r, s: (e, r, 0)),
                  pl.BlockSpec((1, D_MODEL, tf), up_tile),
                  pl.BlockSpec((1, D_MODEL, tf), up_tile),
                  pl.BlockSpec((1, EXPERT_FF, tn), lambda e, r, s: (e, 0, out_tile(s)))],
        out_specs=pl.BlockSpec((1, tr, tn), lambda e, r, s: (e, r, out_tile(s))),
        scratch_shapes=[pltpu.VMEM((tr, EXPERT_FF), BF16)],
        compiler_params=_cparams("parallel", "parallel", "arbitrary"),
        name="expert_swiglu",
    )(x, gate, w_gate, w_up, w_down)


def _window(bounds_ref, base, t):
    p0 = bounds_ref[base + t]
    p1 = bounds_ref[base + t + 1]
    start = (p0 // PIECE) * PIECE
    return start, jnp.where(p1 > p0, ((p1 - start + PIECE - 1) // PIECE) * PIECE, 0)


def _combine_kernel(bounds_ref, y_ref, slot_ref, x_ref, mod_ref, g_ref, o_ref, ybuf_ref, obuf_ref, acc_ref, *,
                    nt, region):
    b = pl.program_id(0)
    t = pl.program_id(1)
    tt = x_ref.shape[0]
    cap = y_ref.shape[1]
    local_i = lax.broadcasted_iota(jnp.int32, (region, tt), 0).astype(F32)
    hits = []
    for e in range(N_EXPERTS):
        start, _ = _window(bounds_ref, (b * N_EXPERTS + e) * (nt + 1), t)
        first = jnp.minimum(start, cap - region)
        ybuf_ref[e * region:(e + 1) * region, :] = y_ref[e, pl.ds(pl.multiple_of(first, PIECE), region), :]
        s = slot_ref[0, e:e + 1, :]
        hit = local_i == jnp.where(s >= 0.0, s - first.astype(F32), -1.0)
        hits.append(jnp.where(hit, 1.0, 0.0).astype(BF16))
    acc_ref[...] = _dot_tn(jnp.concatenate(hits, axis=0), ybuf_ref[...])
    over_i = lax.broadcasted_iota(jnp.int32, (TOK_BLOCK, tt), 0)

    def overflow(e, carry):
        start, length = _window(bounds_ref, (b * N_EXPERTS + e) * (nt + 1), t)
        base = jnp.minimum(start, cap - region) + region
        n_over = start + length - base

        @pl.when(n_over > 0)
        def _():
            def copy(i, carry):
                obuf_ref[pl.ds(pl.multiple_of(i * PIECE, PIECE), PIECE), :] = (
                    y_ref[e, pl.ds(pl.multiple_of(base + i * PIECE, PIECE), PIECE), :])
                return carry

            def clear(i, carry):
                obuf_ref[pl.ds(pl.multiple_of(n_over + i * PIECE, PIECE), PIECE), :] = jnp.zeros((PIECE, D_MODEL), BF16)
                return carry

            lax.fori_loop(0, n_over // PIECE, copy, 0)
            lax.fori_loop(0, (TOK_BLOCK - n_over) // PIECE, clear, 0)
            hit = (over_i + base).astype(F32) == slot_ref[0, pl.ds(e, 1), :]
            acc_ref[...] += _dot_tn(jnp.where(hit, 1.0, 0.0).astype(BF16), obuf_ref[...])

        return carry

    lax.fori_loop(0, N_EXPERTS, overflow, 0)
    x = x_ref[...] + mod_ref[0, 5:6, :] * acc_ref[...]
    var = jnp.mean(x * x, axis=-1, keepdims=True)
    o_ref[...] = x * lax.rsqrt(var + NORM_EPS) * g_ref[...]


def _combine(bounds, y, slot, x_mid, mod, final_g, *, n_batch, n_tok, row_block_off, mod_base, mod_per_batch):
    cap = CAPACITY_FACTOR * n_tok // N_EXPERTS
    tt = TOK_BLOCK
    nt = n_tok // tt
    region = min(64, cap)
    assert cap - region <= TOK_BLOCK
    return pl.pallas_call(
        functools.partial(_combine_kernel, nt=nt, region=region),
        out_shape=jax.ShapeDtypeStruct((n_batch * n_tok, D_MODEL), F32),
        grid_spec=pltpu.PrefetchScalarGridSpec(
            num_scalar_prefetch=1,
            grid=(n_batch, nt),
            in_specs=[pl.BlockSpec((N_EXPERTS, cap, D_MODEL), lambda b, t, s: (0, row_block_off + b, 0)),
                      pl.BlockSpec((1, N_EXPERTS, tt), lambda b, t, s: (b, 0, t)),
                      pl.BlockSpec((tt, D_MODEL), lambda b, t, s: (b * nt + t, 0)),
                      pl.BlockSpec((1, N_MOD, D_MODEL), lambda b, t, s: (mod_base + b * mod_per_batch, 0, 0)),
                      pl.BlockSpec((1, D_MODEL), lambda b, t, s: (0, 0))],
            out_specs=pl.BlockSpec((tt, D_MODEL), lambda b, t, s: (b * nt + t, 0)),
            scratch_shapes=[pltpu.VMEM((N_EXPERTS * region, D_MODEL), BF16),
                            pltpu.VMEM((TOK_BLOCK, D_MODEL), BF16), pltpu.VMEM((tt, D_MODEL), F32)]),
        compiler_params=_cparams("arbitrary", "arbitrary"),
        name="combine_final_norm",
    )(bounds, y, slot, x_mid, mod, final_g)


def kernel(x_prompt, x_sample, cache_k, cache_v, state_hgrn, c, c_ctx, w_ada, b_ada, norm1_g, w_in, hg_lb,
           hg_norm_g, attn_sink, w_out, norm2_g, w_router, w_gate, w_up, w_down, final_norm_g):
    n_p, t_p, _ = x_prompt.shape
    n_s, t_s, _ = x_sample.shape
    assert w_ada.shape[0] == 1 and 1 + n_s <= COND_ROWS
    layer = 0

    cond = jnp.zeros((COND_ROWS, D_MODEL), F32).at[0].set(c_ctx).at[1:1 + n_s].set(c)
    mod = _ada_modulation(cond, w_ada[layer], b_ada[layer]).reshape(COND_ROWS, N_MOD, D_MODEL)
    lb = jnp.cumsum(jax.nn.softmax(hg_lb.astype(F32), axis=0), axis=0)[layer]
    w_in_l = w_in[layer]
    w_out_b = w_out[layer].astype(BF16)
    w_router_t = w_router[layer].T.astype(BF16)
    norm1 = norm1_g[layer].reshape(1, D_MODEL)
    norm2 = norm2_g[layer].reshape(1, D_MODEL)
    hg_gain = hg_norm_g[layer].reshape(1, HG_WIDTH)
    final_g = final_norm_g.reshape(1, D_MODEL)
    sink = attn_sink[layer]
    cos, sin_signed = _rope_tables(t_s)

    xp = x_prompt.reshape(n_p * t_p, D_MODEL)
    xs = x_sample.reshape(n_s * t_s, D_MODEL)
    groups = dict(p=dict(mod_base=0, rows_per_mod=n_p * t_p), s=dict(mod_base=1, rows_per_mod=t_s))

    proj_p = _in_projection(xp, mod, norm1, w_in_l, **groups["p"])
    proj_s = _in_projection(xs, mod, norm1, w_in_l, **groups["s"])

    ohg_p, new_state = _hgrn(proj_p, lb, hg_gain, None, n_batch=n_p, n_tok=t_p)
    ohg_s, _ = _hgrn(proj_s, lb, hg_gain, state_hgrn[:, layer:layer + 1].astype(F32), n_batch=n_s, n_tok=t_s)

    oatt_p = _context_attention(proj_p, sink, n_batch=n_p, n_tok=t_p)
    n_ctx = cache_k.shape[2]
    oatt_s = _window_attention(proj_s, cache_k[:, layer].reshape(n_s, n_ctx, KV_WIDTH),
                               cache_v[:, layer].reshape(n_s, n_ctx, KV_WIDTH), sink, cos, sin_signed,
                               n_batch=n_s, n_tok=t_s)

    xmid_p, h2_p = _out_projection(ohg_p, oatt_p, w_out_b, xp, mod, norm2, **groups["p"])
    xmid_s, h2_s = _out_projection(ohg_s, oatt_s, w_out_b, xs, mod, norm2, **groups["s"])

    cap_p = CAPACITY_FACTOR * t_p // N_EXPERTS
    cap_s = CAPACITY_FACTOR * t_s // N_EXPERTS
    off_s = n_p * cap_p // cap_s
    xg_p, gate_p, slot_p = _route_gather_requests(h2_p, w_router_t, n_batch=n_p, n_tok=t_p)
    slot_s, aff_s, bounds_s = _route_only(h2_s, w_router_t, n_batch=n_s, n_tok=t_s)
    xg, gate = _gather_windows(bounds_s, slot_s, aff_s, h2_s, xg_p, gate_p, n_batch=n_s, n_tok=t_s)
    bounds_p = jnp.tile(jnp.array([0, cap_p], jnp.int32), n_p * N_EXPERTS)

    y = _experts(xg, gate, w_gate[layer], w_up[layer], w_down[layer])

    y_prompt = _combine(bounds_p, y, slot_p, xmid_p, mod, final_g, n_batch=n_p, n_tok=t_p, row_block_off=0,
                        mod_base=0, mod_per_batch=0)
    y_sample = _combine(bounds_s, y, slot_s, xmid_s, mod, final_g, n_batch=n_s, n_tok=t_s, row_block_off=off_s,
                        mod_base=1, mod_per_batch=1)

    k_col = 5 * HG_WIDTH + ATT_WIDTH
    new_k = proj_p[:, k_col:k_col + KV_WIDTH].reshape(n_p, 1, t_p, ATT_KV_HEADS, HEAD_DIM)
    new_v = proj_p[:, k_col + KV_WIDTH:k_col + 2 * KV_WIDTH].reshape(n_p, 1, t_p, ATT_KV_HEADS, HEAD_DIM)
    return (y_prompt.reshape(n_p, t_p, D_MODEL), y_sample.reshape(n_s, t_s, D_MODEL), new_k, new_v, new_state)
```

```python
import functools

import jax
import jax.numpy as jnp
import numpy as np
from jax import lax
from jax.experimental import pallas as pl
from jax.experimental.pallas import tpu as pltpu

F32 = jnp.float32
BF16 = jnp.bfloat16

D_MODEL = 2048
HG_WIDTH = 1024
HG_HEADS = 8
HEAD_DIM = 128
ATT_HEADS = 8
ATT_KV_HEADS = 2
ATT_GROUP = ATT_HEADS // ATT_KV_HEADS
KV_WIDTH = ATT_KV_HEADS * HEAD_DIM
ATT_WIDTH = ATT_HEADS * HEAD_DIM
ATT_BLOCK = 128
GRID_W = 64
ROPE_BASE = 10000.0
ROPE_FREQS = HEAD_DIM // 4
N_EXPERTS = 16
CAPACITY_FACTOR = 2
EXPERT_FF = 5632
NORM_EPS = 1e-6
IN_WIDTH = 5 * HG_WIDTH + ATT_WIDTH + 2 * KV_WIDTH
N_MOD = 6
COND_ROWS = 16

HG_CHUNK = 128
HG_DIAG = 8
SEL_BLOCK = 256
TOK_BLOCK = 256
PIECE = 16

VMEM_LIMIT = 56 * 1024 * 1024


def _cparams(*sem):
    return pltpu.CompilerParams(dimension_semantics=sem, vmem_limit_bytes=VMEM_LIMIT)


def _sigmoid(x):
    return 1.0 / (1.0 + jnp.exp(-x))


def _silu(x):
    return x * _sigmoid(x)


def _dot(a, b):
    return jnp.dot(a, b, preferred_element_type=F32)


def _dot_nt(a, b):
    return lax.dot_general(a, b, (((1,), (1,)), ((), ())), preferred_element_type=F32)


def _dot_tn(a, b):
    return lax.dot_general(a, b, (((0,), (0,)), ((), ())), preferred_element_type=F32)


def _ada_kernel(c_ref, w_ref, b_ref, o_ref):
    s = _silu(c_ref[...]).astype(BF16)
    o_ref[...] = _dot(s, w_ref[...].astype(BF16)) + b_ref[...]


def _ada_modulation(cond, w_ada, b_ada):
    tn = 1024
    n = w_ada.shape[1]
    return pl.pallas_call(
        _ada_kernel,
        out_shape=jax.ShapeDtypeStruct((COND_ROWS, n), F32),
        grid=(n // tn,),
        in_specs=[pl.BlockSpec((COND_ROWS, D_MODEL), lambda j: (0, 0)),
                  pl.BlockSpec((D_MODEL, tn), lambda j: (0, j)),
                  pl.BlockSpec((1, tn), lambda j: (0, j))],
        out_specs=pl.BlockSpec((COND_ROWS, tn), lambda j: (0, j)),
        compiler_params=_cparams("arbitrary"),
        name="ada_modulation",
    )(cond, w_ada, b_ada.reshape(1, n))


def _norm_modulate(x, gain, shift, scale):
    var = jnp.mean(x * x, axis=-1, keepdims=True)
    return (x * lax.rsqrt(var + NORM_EPS) * gain) * (1.0 + scale) + shift


def _inproj_kernel(x_ref, mod_ref, g_ref, w_ref, o_ref, h_ref, *, rows):
    w = w_ref[...]
    chunks = [slice(r * rows, (r + 1) * rows) for r in range(x_ref.shape[0] // rows)]

    @pl.when(pl.program_id(1) == 0)
    def _():
        shift = mod_ref[0, 0:1, :]
        scale = mod_ref[0, 1:2, :]
        gain = g_ref[...]
        for sl in chunks:
            h = _norm_modulate(x_ref[sl, :], gain, shift, scale).astype(BF16)
            h_ref[sl, :] = h
            o_ref[sl, :] = _dot(h, w)

    @pl.when(pl.program_id(1) > 0)
    def _():
        for sl in chunks:
            o_ref[sl, :] = _dot(h_ref[sl, :], w)


def _in_projection(x, mod, gain, w, *, mod_base, rows_per_mod):
    m = x.shape[0]
    tm, tn = 1024, 512
    return pl.pallas_call(
        functools.partial(_inproj_kernel, rows=256),
        out_shape=jax.ShapeDtypeStruct((m, IN_WIDTH), F32),
        grid=(m // tm, IN_WIDTH // tn),
        in_specs=[pl.BlockSpec((tm, D_MODEL), lambda i, j: (i, 0)),
                  pl.BlockSpec((1, N_MOD, D_MODEL), lambda i, j: (mod_base + (i * tm) // rows_per_mod, 0, 0)),
                  pl.BlockSpec((1, D_MODEL), lambda i, j: (0, 0)),
                  pl.BlockSpec((D_MODEL, tn), lambda i, j: (0, j))],
        out_specs=pl.BlockSpec((tm, tn), lambda i, j: (i, j)),
        scratch_shapes=[pltpu.VMEM((tm, D_MODEL), BF16)],
        compiler_params=_cparams("parallel", "arbitrary"),
        name="in_projection",
    )(x, mod, gain, w)


def _hgrn_codes(reverse):
    L = HG_CHUNK
    t = lax.broadcasted_iota(jnp.int32, (L, L), 0)
    s = lax.broadcasted_iota(jnp.int32, (L, L), 1)
    code = jnp.where(t == s, 1, 0)
    h = L // 2
    while h >= 1:
        same = (t & ~(2 * h - 1)) == (s & ~(2 * h - 1))
        t_hi = (t & h) != 0
        s_hi = (s & h) != 0
        pair = (s_hi & ~t_hi) if reverse else (t_hi & ~s_hi)
        code = jnp.where(same & pair, h * 16, code)
        h //= 2
    return code


def _cumsum_rows(tri_bf16, g):
    g1 = g.astype(BF16)
    r1 = g - g1.astype(F32)
    g2 = r1.astype(BF16)
    g3 = (r1 - g2.astype(F32)).astype(BF16)
    return _dot(tri_bf16, g1) + _dot(tri_bf16, g2) + _dot(tri_bf16, g3)


def _hgrn_intra(q, k, f, b, v_bf, code, reverse):
    L = HG_CHUNK
    G = HG_DIAG
    q_bf = q.astype(BF16)
    k_bf = k.astype(BF16)

    def level(h, ref, att):
        neg_abs = pltpu.bitcast(pltpu.bitcast(b - ref, jnp.int32) | jnp.int32(-2 ** 31), F32)
        e = jnp.exp2(neg_abs.astype(BF16))
        return jnp.where(code == h * 16, _dot_nt(q_bf * e, k_bf * e), att)

    att = jnp.where(code == 1, _dot_nt(q_bf, k_bf), 0.0)
    h = L // 2
    while h >= G:
        parts = []
        for p in range(L // (2 * h)):
            m = p * 2 * h + (h if reverse else h - 1)
            parts.append(jnp.broadcast_to(b[m:m + 1, :], (2 * h, HEAD_DIM)))
        att = level(h, parts[0] if len(parts) == 1 else jnp.concatenate(parts, axis=0), att)
        h //= 2

    b3 = b.reshape(L // G, G, HEAD_DIM)

    def group_row(r):
        return jnp.broadcast_to(b3[:, r:r + 1, :], (L // G, G, HEAD_DIM)).reshape(L, HEAD_DIM)

    row = lax.broadcasted_iota(jnp.int32, (L, HEAD_DIM), 0)
    att = level(4, group_row(4 if reverse else 3), att)
    lo, hi = (2, 6) if reverse else (1, 5)
    att = level(2, jnp.where((row & 4) == 0, group_row(lo), group_row(hi)), att)
    att = jnp.where(code == 16, _dot_nt(q_bf * f.astype(BF16), k_bf), att)
    return _dot(att.astype(BF16), v_bf)


def _hgrn_prepare(q, f, v, code, tri, reverse):
    L = HG_CHUNK
    k = 1.0 - f
    b = _cumsum_rows(tri, jnp.log2(f))
    b_tot = b[0:1, :] if reverse else b[L - 1:L, :]
    v_bf = v.astype(BF16)
    o_intra = _hgrn_intra(q, k, f, b, v_bf, code, reverse)
    q_in = (q * jnp.exp2(b)).astype(BF16)
    k_out = (k * jnp.exp2(b_tot - b)).astype(BF16)
    return o_intra, q_in, k_out, v_bf, jnp.exp2(b_tot)


def _hgrn_advance(st, prepared):
    o_intra, q_in, k_out, v_bf, decay = prepared
    return o_intra + _dot_nt(q_in, st.astype(BF16)), st * decay + _dot_tn(v_bf, k_out)


def _hgrn_kernel(*refs, n_tok, has_state):
    if has_state:
        (q_ref, ff_ref, fb_ref, v_ref, gate_ref, lb_ref, ng_ref, s0_ref,
         o_ref, sout_ref, of_ref, ob_ref, code_ref) = refs
    else:
        (q_ref, ff_ref, fb_ref, v_ref, gate_ref, lb_ref, ng_ref,
         o_ref, sout_ref, of_ref, ob_ref, code_ref) = refs
    L = HG_CHUNK
    nc = n_tok // L
    lb_f = lb_ref[0:1, :]
    lb_b = lb_ref[1:2, :]

    @pl.when((pl.program_id(0) == 0) & (pl.program_id(1) == 0))
    def _():
        code_ref[0] = _hgrn_codes(False)
        code_ref[1] = _hgrn_codes(True)

    code_f = code_ref[0]
    code_b = code_ref[1]
    ti = lax.broadcasted_iota(jnp.int32, (L, L), 0)
    si = lax.broadcasted_iota(jnp.int32, (L, L), 1)
    tri_f = jnp.where(si <= ti, 1.0, 0.0).astype(BF16)
    tri_b = jnp.where(si >= ti, 1.0, 0.0).astype(BF16)

    def prepare(row, f_ref, lb, code, tri, reverse):
        sl = pl.ds(row, L)
        q = _silu(q_ref[sl, :])
        f = lb + (1.0 - lb) * _sigmoid(f_ref[sl, :])
        return sl, _hgrn_prepare(q, f, v_ref[sl, :], code, tri, reverse)

    gain = ng_ref[...]

    def finish(sl, o):
        var = jnp.mean(o * o, axis=-1, keepdims=True)
        o_ref[sl, :] = ((o * lax.rsqrt(var + NORM_EPS) * gain) * _silu(gate_ref[sl, :])).astype(BF16)

    def make_body(other_direction):
        def body(c, states):
            st_f, st_b = states
            fwd, bwd = [], []
            for u in range(unroll):
                cu = c * unroll + u
                fwd.append(prepare(pl.multiple_of(cu * L, L), ff_ref, lb_f, code_f, tri_f, False))
                bwd.append(prepare(pl.multiple_of((nc - 1 - cu) * L, L), fb_ref, lb_b, code_b, tri_b, True))
            out_f, out_b = [], []
            for (sl_f, prep_f), (sl_b, prep_b) in zip(fwd, bwd):
                o_f, st_f = _hgrn_advance(st_f, prep_f)
                o_b, st_b = _hgrn_advance(st_b, prep_b)
                out_f.append((sl_f, o_f))
                out_b.append((sl_b, o_b))
            if other_direction == "later":
                for sl, o in out_f:
                    of_ref[sl, :] = o
                for sl, o in out_b:
                    ob_ref[sl, :] = o
            elif other_direction == "now":
                for u in range(unroll):
                    finish(out_f[u][0], out_f[u][1] + out_b[unroll - 1 - u][1])
            else:
                for sl, o in out_f:
                    finish(sl, o + ob_ref[sl, :])
                for sl, o in out_b:
                    finish(sl, of_ref[sl, :] + o)
            return st_f, st_b
        return body

    unroll = 2
    assert nc % unroll == 0
    n_iter = nc // unroll
    half = n_iter // 2
    if has_state:
        states = (s0_ref[0, 0, 0, 0].T, s0_ref[0, 0, 1, 0].T)
    else:
        states = (jnp.zeros((HEAD_DIM, HEAD_DIM), F32), jnp.zeros((HEAD_DIM, HEAD_DIM), F32))
    states = lax.fori_loop(0, half, make_body("later"), states)
    if n_iter % 2:
        states = make_body("now")(half, states)
    st_f, st_b = lax.fori_loop(half + n_iter % 2, n_iter, make_body("earlier"), states)
    sout_ref[0, 0, 0, 0] = st_f.T
    sout_ref[0, 0, 1, 0] = st_b.T


def _hgrn(proj, lb, norm_g, state, *, n_batch, n_tok):
    col = lambda k: (lambda b, h: (b, k * HG_HEADS + h))
    tok_spec = lambda k: pl.BlockSpec((n_tok, HEAD_DIM), col(k))
    st_spec = pl.BlockSpec((1, 1, 2, 1, HEAD_DIM, HEAD_DIM), lambda b, h: (b, 0, 0, h, 0, 0))
    has_state = state is not None
    return pl.pallas_call(
        functools.partial(_hgrn_kernel, n_tok=n_tok, has_state=has_state),
        out_shape=(jax.ShapeDtypeStruct((n_batch * n_tok, HG_WIDTH), BF16),
                   jax.ShapeDtypeStruct((n_batch, 1, 2, HG_HEADS, HEAD_DIM, HEAD_DIM), F32)),
        grid=(n_batch, HG_HEADS),
        in_specs=[tok_spec(0), tok_spec(1), tok_spec(2), tok_spec(3), tok_spec(4),
                  pl.BlockSpec((2, HEAD_DIM), lambda b, h: (0, h)),
                  pl.BlockSpec((1, HEAD_DIM), lambda b, h: (0, h))] + ([st_spec] if has_state else []),
        out_specs=(pl.BlockSpec((n_tok, HEAD_DIM), lambda b, h: (b, h)), st_spec),
        scratch_shapes=[pltpu.VMEM((n_tok, HEAD_DIM), F32), pltpu.VMEM((n_tok, HEAD_DIM), F32),
                        pltpu.VMEM((2, HG_CHUNK, HG_CHUNK), jnp.int32)],
        compiler_params=_cparams("arbitrary", "arbitrary"),
        name="hgrn2_scan",
    )(proj, proj, proj, proj, proj, lb, norm_g, *((state,) if has_state else ()))


def _stack_heads(x, kvh):
    return jnp.concatenate(
        [x[:, (kvh * ATT_GROUP + g) * HEAD_DIM:(kvh * ATT_GROUP + g + 1) * HEAD_DIM] for g in range(ATT_GROUP)],
        axis=0)


def _sink_column(sink_ref, kvh, rows):
    return jnp.concatenate(
        [jnp.full((rows, 1), sink_ref[kvh * ATT_GROUP + g], F32) for g in range(ATT_GROUP)],
        axis=0) * (1.0 / SOFTMAX_SCALE)


SOFTMAX_SCALE = HEAD_DIM ** -0.5
EXP2_SCALE = SOFTMAX_SCALE * 1.4426950408889634


def _ones_column(rows):
    lane = lax.broadcasted_iota(jnp.int32, (rows, HEAD_DIM), 1)
    return jnp.where(lane == 0, 1.0, 0.0).astype(BF16)


def _softmax_av(scores, values, sink_col):
    m = sink_col
    for s in scores:
        m = jnp.maximum(m, jnp.max(s, axis=-1, keepdims=True))
    acc = None
    for s, v in zip(scores, values):
        pv = _dot(jnp.exp2(((s - m) * EXP2_SCALE).astype(BF16)), v)
        acc = pv if acc is None else acc + pv
    denom = acc[:, HEAD_DIM:HEAD_DIM + 1] + jnp.exp2((sink_col - m) * EXP2_SCALE)
    return acc[:, :HEAD_DIM] / denom


def _ctx_attn_kernel(sink_ref, q_ref, k_ref, v_ref, o_ref):
    rows = q_ref.shape[0]
    q_all = q_ref[...]
    ones = _ones_column(k_ref.shape[0])
    for kvh in range(ATT_KV_HEADS):
        q = _stack_heads(q_all, kvh).astype(BF16)
        k = k_ref[:, kvh * HEAD_DIM:(kvh + 1) * HEAD_DIM].astype(BF16)
        v = jnp.concatenate([v_ref[:, kvh * HEAD_DIM:(kvh + 1) * HEAD_DIM].astype(BF16), ones], axis=1)
        o = _softmax_av([_dot_nt(q, k)], [v], _sink_column(sink_ref, kvh, rows))
        for g in range(ATT_GROUP):
            hd = kvh * ATT_GROUP + g
            o_ref[:, hd * HEAD_DIM:(hd + 1) * HEAD_DIM] = o[g * rows:(g + 1) * rows, :].astype(BF16)


def _context_attention(proj, sink, *, n_batch, n_tok):
    q_col = 5 * HG_WIDTH // ATT_WIDTH
    k_col = (5 * HG_WIDTH + ATT_WIDTH) // KV_WIDTH
    return pl.pallas_call(
        _ctx_attn_kernel,
        out_shape=jax.ShapeDtypeStruct((n_batch * n_tok, ATT_WIDTH), BF16),
        grid_spec=pltpu.PrefetchScalarGridSpec(
            num_scalar_prefetch=1,
            grid=(n_batch,),
            in_specs=[pl.BlockSpec((n_tok, ATT_WIDTH), lambda b, s: (b, q_col)),
                      pl.BlockSpec((n_tok, KV_WIDTH), lambda b, s: (b, k_col)),
                      pl.BlockSpec((n_tok, KV_WIDTH), lambda b, s: (b, k_col + 1))],
            out_specs=pl.BlockSpec((n_tok, ATT_WIDTH), lambda b, s: (b, 0))),
        compiler_params=_cparams("parallel"),
        name="context_attention",
    )(sink, proj, proj, proj)


def _rope(x, cos, sin_signed, even_group):
    partner = jnp.where(even_group, pltpu.roll(x, HEAD_DIM - ROPE_FREQS, 1), pltpu.roll(x, ROPE_FREQS, 1))
    return x * cos + partner * sin_signed


def _win_attn_kernel(sink_ref, q_ref, k_ref, v_ref, ck_ref, cv_ref, cos_ref, sin_ref, o_ref,
                     kpad_ref, vpad_ref, *, n_tok):
    blk = ATT_BLOCK
    nb = n_tok // blk
    i = pl.program_id(1)
    lane = lax.broadcasted_iota(jnp.int32, (blk, HEAD_DIM), 1)
    even_group = (lane & ROPE_FREQS) == 0

    @pl.when(i == 0)
    def _():
        kpad_ref[0:blk, :] = jnp.zeros((blk, KV_WIDTH), BF16)
        kpad_ref[blk + n_tok:2 * blk + n_tok, :] = jnp.zeros((blk, KV_WIDTH), BF16)
        vpad_ref[0:blk, :] = jnp.zeros((blk, 2 * KV_WIDTH), BF16)
        vpad_ref[blk + n_tok:2 * blk + n_tok, :] = jnp.zeros((blk, 2 * KV_WIDTH), BF16)
        ones = _ones_column(blk)

        def body(r, carry):
            src = pl.ds(pl.multiple_of(r * blk, blk), blk)
            dst = pl.ds(pl.multiple_of((r + 1) * blk, blk), blk)
            cos = cos_ref[src, :]
            sin = sin_ref[src, :]
            for kvh in range(ATT_KV_HEADS):
                cols = slice(kvh * HEAD_DIM, (kvh + 1) * HEAD_DIM)
                kpad_ref[dst, cols] = _rope(k_ref[src, cols], cos, sin, even_group).astype(BF16)
                vpad_ref[dst, 2 * kvh * HEAD_DIM:(2 * kvh + 1) * HEAD_DIM] = v_ref[src, cols].astype(BF16)
                vpad_ref[dst, (2 * kvh + 1) * HEAD_DIM:(2 * kvh + 2) * HEAD_DIM] = ones
            return carry

        lax.fori_loop(0, nb, body, 0)

    rows = pl.ds(pl.multiple_of(i * blk, blk), blk)
    cos = cos_ref[rows, :]
    sin = sin_ref[rows, :]
    band = pl.ds(pl.multiple_of(i * blk, blk), 3 * blk)
    r = lax.broadcasted_iota(jnp.int32, (blk, 3 * blk), 0)
    j = lax.broadcasted_iota(jnp.int32, (blk, 3 * blk), 1)
    kpos = j + (i - 1) * blk
    valid = (j >= r) & (j <= r + 2 * blk) & (kpos >= 0) & (kpos < n_tok)
    valid = jnp.concatenate([valid] * ATT_GROUP, axis=0)
    q_all = q_ref[...]
    ctx_ones = _ones_column(ck_ref.shape[1])
    for kvh in range(ATT_KV_HEADS):
        cols = slice(kvh * HEAD_DIM, (kvh + 1) * HEAD_DIM)
        q = jnp.concatenate(
            [_rope(q_all[:, (kvh * ATT_GROUP + g) * HEAD_DIM:(kvh * ATT_GROUP + g + 1) * HEAD_DIM],
                   cos, sin, even_group) for g in range(ATT_GROUP)], axis=0).astype(BF16)
        s_ctx = _dot_nt(q, ck_ref[0, :, cols].astype(BF16))
        s_loc = jnp.where(valid, _dot_nt(q, kpad_ref[band, cols]), -jnp.inf)
        v_ctx = jnp.concatenate([cv_ref[0, :, cols].astype(BF16), ctx_ones], axis=1)
        v_loc = vpad_ref[band, 2 * kvh * HEAD_DIM:(2 * kvh + 2) * HEAD_DIM]
        o = _softmax_av([s_ctx, s_loc], [v_ctx, v_loc], _sink_column(sink_ref, kvh, blk))
        for g in range(ATT_GROUP):
            hd = kvh * ATT_GROUP + g
            o_ref[:, hd * HEAD_DIM:(hd + 1) * HEAD_DIM] = o[g * blk:(g + 1) * blk, :].astype(BF16)


def _window_attention(proj, cache_k, cache_v, sink, cos, sin_signed, *, n_batch, n_tok):
    nb = n_tok // ATT_BLOCK
    n_ctx = cache_k.shape[1]
    q_col = 5 * HG_WIDTH // ATT_WIDTH
    k_col = (5 * HG_WIDTH + ATT_WIDTH) // KV_WIDTH
    table = pl.BlockSpec((n_tok, HEAD_DIM), lambda b, i, s: (0, 0))
    cache = pl.BlockSpec((1, n_ctx, KV_WIDTH), lambda b, i, s: (b, 0, 0))
    return pl.pallas_call(
        functools.partial(_win_attn_kernel, n_tok=n_tok),
        out_shape=jax.ShapeDtypeStruct((n_batch * n_tok, ATT_WIDTH), BF16),
        grid_spec=pltpu.PrefetchScalarGridSpec(
            num_scalar_prefetch=1,
            grid=(n_batch, nb),
            in_specs=[pl.BlockSpec((ATT_BLOCK, ATT_WIDTH), lambda b, i, s: (b * nb + i, q_col)),
                      pl.BlockSpec((n_tok, KV_WIDTH), lambda b, i, s: (b, k_col)),
                      pl.BlockSpec((n_tok, KV_WIDTH), lambda b, i, s: (b, k_col + 1)),
                      cache, cache, table, table],
            out_specs=pl.BlockSpec((ATT_BLOCK, ATT_WIDTH), lambda b, i, s: (b * nb + i, 0)),
            scratch_shapes=[pltpu.VMEM((n_tok + 2 * ATT_BLOCK, KV_WIDTH), BF16),
                            pltpu.VMEM((n_tok + 2 * ATT_BLOCK, 2 * KV_WIDTH), BF16)]),
        compiler_params=_cparams("parallel", "arbitrary"),
        name="window_attention",
    )(sink, proj, proj, proj, cache_k, cache_v, cos, sin_signed)


def _rope_tables(n_tok):
    rows = n_tok // GRID_W
    row = np.repeat(np.arange(rows), GRID_W).astype(np.float32)
    col = np.tile(np.arange(GRID_W), rows).astype(np.float32)
    inv = np.float32(ROPE_BASE) ** (-np.arange(ROPE_FREQS, dtype=np.float32) / np.float32(ROPE_FREQS))
    ar, ac = row[:, None] * inv, col[:, None] * inv
    cr, sr, cc, sc = np.cos(ar), np.sin(ar), np.cos(ac), np.sin(ac)
    return (jnp.asarray(np.concatenate([cr, cr, cc, cc], axis=1), F32),
            jnp.asarray(np.concatenate([-sr, sr, -sc, sc], axis=1), F32))


def _outproj_kernel(hg_ref, att_ref, w_ref, x_ref, mod_ref, g_ref, xo_ref, h_ref):
    chunk = 256
    for r in range(x_ref.shape[0] // chunk):
        sl = slice(r * chunk, (r + 1) * chunk)
        mix = _dot(hg_ref[sl, :], w_ref[0:HG_WIDTH, :]) + _dot(att_ref[sl, :], w_ref[HG_WIDTH:, :])
        x = x_ref[sl, :] + mod_ref[0, 2:3, :] * mix
        xo_ref[sl, :] = x
        h_ref[sl, :] = _norm_modulate(x, g_ref[...], mod_ref[0, 3:4, :], mod_ref[0, 4:5, :]).astype(BF16)


def _out_projection(o_hg, o_att, w_bf16, x, mod, gain, *, mod_base, rows_per_mod):
    m = x.shape[0]
    tm = 512
    row = lambda i: (i, 0)
    return pl.pallas_call(
        _outproj_kernel,
        out_shape=(jax.ShapeDtypeStruct((m, D_MODEL), F32), jax.ShapeDtypeStruct((m, D_MODEL), BF16)),
        grid=(m // tm,),
        in_specs=[pl.BlockSpec((tm, HG_WIDTH), row),
                  pl.BlockSpec((tm, ATT_WIDTH), row),
                  pl.BlockSpec((HG_WIDTH + ATT_WIDTH, D_MODEL), lambda i: (0, 0)),
                  pl.BlockSpec((tm, D_MODEL), row),
                  pl.BlockSpec((1, N_MOD, D_MODEL), lambda i: (mod_base + (i * tm) // rows_per_mod, 0, 0)),
                  pl.BlockSpec((1, D_MODEL), lambda i: (0, 0))],
        out_specs=(pl.BlockSpec((tm, D_MODEL), row), pl.BlockSpec((tm, D_MODEL), row)),
        compiler_params=_cparams("parallel"),
        name="out_projection",
    )(o_hg, o_att, w_bf16, x, mod, gain)


def _prefix_count(x):
    n = x.shape[1]
    i = lax.broadcasted_iota(jnp.int32, (SEL_BLOCK, SEL_BLOCK), 0)
    j = lax.broadcasted_iota(jnp.int32, (SEL_BLOCK, SEL_BLOCK), 1)
    upper = jnp.where(i < j, 1.0, 0.0).astype(BF16)
    off = jnp.zeros((x.shape[0], 1), F32)
    outs = []
    for blk in range(n // SEL_BLOCK):
        xb = x[:, blk * SEL_BLOCK:(blk + 1) * SEL_BLOCK]
        outs.append(_dot(xb.astype(BF16), upper) + off)
        off = off + jnp.sum(xb, axis=-1, keepdims=True)
    return outs[0] if len(outs) == 1 else jnp.concatenate(outs, axis=1)


def _route_select(h_ref, w_ref, cap):
    logits = _dot_nt(w_ref[...], h_ref[...])
    ex = jnp.exp(logits - jnp.max(logits, axis=0, keepdims=True))
    aff = ex / jnp.sum(ex, axis=0, keepdims=True)
    bits = pltpu.bitcast(aff, jnp.int32)

    thr = jnp.zeros((N_EXPERTS, 1), jnp.int32)
    shift = 31
    while shift > 0:
        width = min(4, shift)
        shift -= width
        digit = jnp.zeros((N_EXPERTS, 1), jnp.int32)
        for j in range(1, 2 ** width):
            cnt = jnp.sum(jnp.where(bits >= (thr | (j << shift)), 1.0, 0.0), axis=-1, keepdims=True)
            digit = digit + jnp.where(cnt >= cap, 1, 0)
        thr = thr | (digit << shift)
    above = jnp.where(bits > thr, 1.0, 0.0)
    tied = jnp.where(bits == thr, 1.0, 0.0)
    room = cap - jnp.sum(above, axis=-1, keepdims=True)
    sel = above + tied * jnp.where(_prefix_count(tied) < room, 1.0, 0.0)
    return jnp.where(sel > 0.0, _prefix_count(sel), -1.0), aff


def _one_hot_gather(slot_rows, aff_rows, h_ref, rows, cap):
    n_tok = h_ref.shape[0]
    c = (lax.broadcasted_iota(jnp.int32, (rows, n_tok), 0) & (cap - 1)).astype(F32)
    hit = c == slot_rows
    x = _dot(jnp.where(hit, 1.0, 0.0).astype(BF16), h_ref[...]).astype(BF16)
    return x, jnp.sum(jnp.where(hit, aff_rows, 0.0), axis=-1, keepdims=True)


def _route_request_kernel(h_ref, w_ref, x_ref, g_ref, slot_ref, *, cap):
    n_tok = h_ref.shape[0]
    slot, aff = _route_select(h_ref, w_ref, cap)
    slot_ref[0] = slot
    per_row = lambda a: jnp.concatenate(
        [jnp.broadcast_to(a[e:e + 1, :], (cap, n_tok)) for e in range(N_EXPERTS)], axis=0)
    x, g = _one_hot_gather(per_row(slot), per_row(aff), h_ref, N_EXPERTS * cap, cap)
    for e in range(N_EXPERTS):
        x_ref[e] = x[e * cap:(e + 1) * cap, :]
        g_ref[e] = g[e * cap:(e + 1) * cap, :]


def _route_only_kernel(h_ref, w_ref, slot_ref, aff_ref, bounds_ref, *, cap):
    n_tok = h_ref.shape[0]
    slot, aff = _route_select(h_ref, w_ref, cap)
    slot_ref[0] = slot
    aff_ref[0] = aff
    n = lax.broadcasted_iota(jnp.int32, (n_tok, 128), 0)
    t = lax.broadcasted_iota(jnp.int32, (n_tok, 128), 1)
    before = jnp.where(n < t * TOK_BLOCK, 1.0, 0.0).astype(BF16)
    chosen = jnp.where(slot >= 0.0, 1.0, 0.0).astype(BF16)
    bounds_ref[0] = _dot(chosen, before).astype(jnp.int32)


def _gather_window_kernel(bounds_ref, slot_ref, aff_ref, h_ref, xh_ref, gh_ref, x_ref, g_ref, xo_ref, go_ref, *,
                          n_batch, group):
    b = pl.program_id(0)
    eg = pl.program_id(1)
    n_tok = h_ref.shape[0]
    nt = n_tok // TOK_BLOCK
    cap = x_ref.shape[1]
    region = TOK_BLOCK // group

    @pl.when(b >= n_batch)
    def _():
        x_ref[...] = xh_ref[...]
        g_ref[...] = gh_ref[...]

    @pl.when(b < n_batch)
    def _():
        x_ref[...] = jnp.zeros(x_ref.shape, BF16)
        g_ref[...] = jnp.zeros(g_ref.shape, F32)
        local_i = lax.broadcasted_iota(jnp.int32, (region, TOK_BLOCK), 0).astype(F32)
        over_i = lax.broadcasted_iota(jnp.int32, (TOK_BLOCK, TOK_BLOCK), 0)

        def token_block(t, carry):
            toks = pl.ds(pl.multiple_of(t * TOK_BLOCK, TOK_BLOCK), TOK_BLOCK)
            hits, gates, firsts, n_overs, slots, affs = [], [], [], [], [], []
            for k in range(group):
                e = eg * group + k
                start, length = _window(bounds_ref, (b * N_EXPERTS + e) * (nt + 1), t)
                first = jnp.minimum(start, cap - region)
                s = slot_ref[0, pl.ds(e, 1), toks]
                a = aff_ref[0, pl.ds(e, 1), toks]
                hit = local_i == jnp.where(s >= 0.0, s - first.astype(F32), -1.0)
                hits.append(jnp.where(hit, 1.0, 0.0).astype(BF16))
                gates.append(jnp.sum(jnp.where(hit, a, 0.0), axis=-1, keepdims=True))
                firsts.append(first)
                n_overs.append(start + length - first - region)
                slots.append(s)
                affs.append(a)
            xc = _dot(jnp.concatenate(hits, axis=0), h_ref[toks, :]).astype(BF16)
            for k in range(group):
                dst = pl.ds(pl.multiple_of(firsts[k], PIECE), region)
                x_ref[k, dst, :] = x_ref[k, dst, :] + xc[k * region:(k + 1) * region, :]
                g_ref[k, dst, :] = g_ref[k, dst, :] + gates[k]
            for k in range(group):
                @pl.when(n_overs[k] > 0)
                def _(k=k):
                    base = firsts[k] + region
                    hit = (over_i + base).astype(F32) == slots[k]
                    xo_ref[...] = _dot(jnp.where(hit, 1.0, 0.0).astype(BF16), h_ref[toks, :]).astype(BF16)
                    go_ref[...] = jnp.sum(jnp.where(hit, affs[k], 0.0), axis=-1, keepdims=True)

                    def place(i, carry):
                        src = pl.ds(pl.multiple_of(i * PIECE, PIECE), PIECE)
                        dst = pl.ds(pl.multiple_of(base + i * PIECE, PIECE), PIECE)
                        x_ref[k, dst, :] = x_ref[k, dst, :] + xo_ref[src, :]
                        g_ref[k, dst, :] = g_ref[k, dst, :] + go_ref[src, :]
                        return carry

                    lax.fori_loop(0, n_overs[k] // PIECE, place, 0)
            return carry

        lax.fori_loop(0, nt, token_block, 0)


def _route_gather_requests(h, w_router_t, *, n_batch, n_tok):
    cap = CAPACITY_FACTOR * n_tok // N_EXPERTS
    out_block = lambda w: pl.BlockSpec((N_EXPERTS, cap, w), lambda b: (0, b, 0))
    return pl.pallas_call(
        functools.partial(_route_request_kernel, cap=cap),
        out_shape=(jax.ShapeDtypeStruct((N_EXPERTS, n_batch * cap, D_MODEL), BF16),
                   jax.ShapeDtypeStruct((N_EXPERTS, n_batch * cap, 1), F32),
                   jax.ShapeDtypeStruct((n_batch, N_EXPERTS, n_tok), F32)),
        grid=(n_batch,),
        in_specs=[pl.BlockSpec((n_tok, D_MODEL), lambda b: (b, 0)),
                  pl.BlockSpec((N_EXPERTS, D_MODEL), lambda b: (0, 0))],
        out_specs=(out_block(D_MODEL), out_block(1), pl.BlockSpec((1, N_EXPERTS, n_tok), lambda b: (b, 0, 0))),
        compiler_params=_cparams("parallel"),
        name="route_gather_requests",
    )(h, w_router_t)


def _route_only(h, w_router_t, *, n_batch, n_tok):
    cap = CAPACITY_FACTOR * n_tok // N_EXPERTS
    nt = n_tok // TOK_BLOCK
    expert_major = pl.BlockSpec((1, N_EXPERTS, n_tok), lambda b: (b, 0, 0))
    slot, aff, bounds = pl.pallas_call(
        functools.partial(_route_only_kernel, cap=cap),
        out_shape=(jax.ShapeDtypeStruct((n_batch, N_EXPERTS, n_tok), F32),
                   jax.ShapeDtypeStruct((n_batch, N_EXPERTS, n_tok), F32),
                   jax.ShapeDtypeStruct((n_batch, N_EXPERTS, 128), jnp.int32)),
        grid=(n_batch,),
        in_specs=[pl.BlockSpec((n_tok, D_MODEL), lambda b: (b, 0)),
                  pl.BlockSpec((N_EXPERTS, D_MODEL), lambda b: (0, 0))],
        out_specs=(expert_major, expert_major, pl.BlockSpec((1, N_EXPERTS, 128), lambda b: (b, 0, 0))),
        compiler_params=_cparams("parallel"),
        name="route_select",
    )(h, w_router_t)
    return slot, aff, bounds[:, :, :nt + 1].reshape(-1)


def _gather_windows(bounds, slot, aff, h, x_head, g_head, *, n_batch, n_tok):
    cap = CAPACITY_FACTOR * n_tok // N_EXPERTS
    group = 4
    head_rows = x_head.shape[1]
    assert head_rows % cap == 0 and N_EXPERTS % group == 0
    n_head = head_rows // cap
    rows_total = head_rows + n_batch * cap
    last = n_batch - 1
    request = lambda b: jnp.minimum(b, last)
    out_row = lambda b: jnp.where(b < n_batch, n_head + b, b - n_batch)
    head_row = lambda b: jnp.maximum(b - n_batch, 0)
    out_block = lambda w: pl.BlockSpec((group, cap, w), lambda b, g, s: (g, out_row(b), 0))
    head_block = lambda w: pl.BlockSpec((group, cap, w), lambda b, g, s: (g, head_row(b), 0))
    expert_major = pl.BlockSpec((1, N_EXPERTS, n_tok), lambda b, g, s: (request(b), 0, 0))
    return pl.pallas_call(
        functools.partial(_gather_window_kernel, n_batch=n_batch, group=group),
        out_shape=(jax.ShapeDtypeStruct((N_EXPERTS, rows_total, D_MODEL), BF16),
                   jax.ShapeDtypeStruct((N_EXPERTS, rows_total, 1), F32)),
        grid_spec=pltpu.PrefetchScalarGridSpec(
            num_scalar_prefetch=1,
            grid=(n_batch + n_head, N_EXPERTS // group),
            in_specs=[expert_major, expert_major,
                      pl.BlockSpec((n_tok, D_MODEL), lambda b, g, s: (request(b), 0)),
                      head_block(D_MODEL), head_block(1)],
            out_specs=(out_block(D_MODEL), out_block(1)),
            scratch_shapes=[pltpu.VMEM((TOK_BLOCK, D_MODEL), BF16), pltpu.VMEM((TOK_BLOCK, 1), F32)]),
        compiler_params=_cparams("arbitrary", "arbitrary"),
        name="gather_windows",
    )(bounds, slot, aff, h, x_head, g_head)


def _moe_kernel(x_ref, g_ref, wg_ref, wu_ref, wd_ref, y_ref, hid_ref, *, n_ff, rows):
    s = pl.program_id(2)
    tf = wg_ref.shape[2]

    @pl.when(s < n_ff)
    def _():
        wg = wg_ref[0].astype(BF16)
        wu = wu_ref[0].astype(BF16)
        cols = pl.ds(pl.multiple_of(s * tf, tf), tf)
        for r in range(x_ref.shape[1] // rows):
            sl = slice(r * rows, (r + 1) * rows)
            x = x_ref[0, sl, :]
            hid_ref[sl, cols] = (_silu(_dot(x, wg)) * _dot(x, wu)).astype(BF16)

    @pl.when(s >= n_ff)
    def _():
        y = _dot(hid_ref[...], wd_ref[0].astype(BF16))
        y_ref[0] = (y * g_ref[0]).astype(BF16)


def _experts(x, gate, w_gate, w_up, w_down):
    n_rows = x.shape[1]
    tr, tf, tn = n_rows // 2, 512, 256
    n_ff, n_out = EXPERT_FF // tf, D_MODEL // tn
    up_tile = lambda e, r, s: (e, 0, jnp.minimum(s, n_ff - 1))
    out_tile = lambda s: jnp.maximum(s - n_ff, 0)
    return pl.pallas_call(
        functools.partial(_moe_kernel, n_ff=n_ff, rows=256),
        out_shape=jax.ShapeDtypeStruct((N_EXPERTS, n_rows, D_MODEL), BF16),
        grid=(N_EXPERTS, n_rows // tr, n_ff + n_out),
        in_specs=[pl.BlockSpec((1, tr, D_MODEL), lambda e, r, s: (e, r, 0), pipeline_mode=pl.Buffered(1)),
                  pl.BlockSpec((1, tr, 1), lambda e, r, s: (e, r, 0)),
                  pl.BlockSpec((1, D_MODEL, tf), up_tile),
                  pl.BlockSpec((1, D_MODEL, tf), up_tile),
                  pl.BlockSpec((1, EXPERT_FF, tn), lambda e, r, s: (e, 0, out_tile(s)))],
        out_specs=pl.BlockSpec((1, tr, tn), lambda e, r, s: (e, r, out_tile(s))),
        scratch_shapes=[pltpu.VMEM((tr, EXPERT_FF), BF16)],
        compiler_params=_cparams("parallel", "parallel", "arbitrary"),
        name="expert_swiglu",
    )(x, gate, w_gate, w_up, w_down)


def _window(bounds_ref, base, t):
    p0 = bounds_ref[base + t]
    p1 = bounds_ref[base + t + 1]
    start = (p0 // PIECE) * PIECE
    return start, jnp.where(p1 > p0, ((p1 - start + PIECE - 1) // PIECE) * PIECE, 0)


def _combine_kernel(bounds_ref, y_ref, slot_ref, x_ref, mod_ref, g_ref, o_ref, ybuf_ref, obuf_ref, acc_ref, *,
                    nt, region):
    b = pl.program_id(0)
    t = pl.program_id(1)
    tt = x_ref.shape[0]
    cap = y_ref.shape[1]
    local_i = lax.broadcasted_iota(jnp.int32, (region, tt), 0).astype(F32)
    hits = []
    for e in range(N_EXPERTS):
        start, _ = _window(bounds_ref, (b * N_EXPERTS + e) * (nt + 1), t)
        first = jnp.minimum(start, cap - region)
        ybuf_ref[e * region:(e + 1) * region, :] = y_ref[e, pl.ds(pl.multiple_of(first, PIECE), region), :]
        s = slot_ref[0, e:e + 1, :]
        hit = local_i == jnp.where(s >= 0.0, s - first.astype(F32), -1.0)
        hits.append(jnp.where(hit, 1.0, 0.0).astype(BF16))
    acc_ref[...] = _dot_tn(jnp.concatenate(hits, axis=0), ybuf_ref[...])
    over_i = lax.broadcasted_iota(jnp.int32, (TOK_BLOCK, tt), 0)

    def overflow(e, carry):
        start, length = _window(bounds_ref, (b * N_EXPERTS + e) * (nt + 1), t)
        base = jnp.minimum(start, cap - region) + region
        n_over = start + length - base

        @pl.when(n_over > 0)
        def _():
            def copy(i, carry):
                obuf_ref[pl.ds(pl.multiple_of(i * PIECE, PIECE), PIECE), :] = (
                    y_ref[e, pl.ds(pl.multiple_of(base + i * PIECE, PIECE), PIECE), :])
                return carry

            def clear(i, carry):
                obuf_ref[pl.ds(pl.multiple_of(n_over + i * PIECE, PIECE), PIECE), :] = jnp.zeros((PIECE, D_MODEL), BF16)
                return carry

            lax.fori_loop(0, n_over // PIECE, copy, 0)
            lax.fori_loop(0, (TOK_BLOCK - n_over) // PIECE, clear, 0)
            hit = (over_i + base).astype(F32) == slot_ref[0, pl.ds(e, 1), :]
            acc_ref[...] += _dot_tn(jnp.where(hit, 1.0, 0.0).astype(BF16), obuf_ref[...])

        return carry

    lax.fori_loop(0, N_EXPERTS, overflow, 0)
    x = x_ref[...] + mod_ref[0, 5:6, :] * acc_ref[...]
    var = jnp.mean(x * x, axis=-1, keepdims=True)
    o_ref[...] = x * lax.rsqrt(var + NORM_EPS) * g_ref[...]


def _combine(bounds, y, slot, x_mid, mod, final_g, *, n_batch, n_tok, row_block_off, mod_base, mod_per_batch):
    cap = CAPACITY_FACTOR * n_tok // N_EXPERTS
    tt = TOK_BLOCK
    nt = n_tok // tt
    region = min(64, cap)
    assert cap - region <= TOK_BLOCK
    return pl.pallas_call(
        functools.partial(_combine_kernel, nt=nt, region=region),
        out_shape=jax.ShapeDtypeStruct((n_batch * n_tok, D_MODEL), F32),
        grid_spec=pltpu.PrefetchScalarGridSpec(
            num_scalar_prefetch=1,
            grid=(n_batch, nt),
            in_specs=[pl.BlockSpec((N_EXPERTS, cap, D_MODEL), lambda b, t, s: (0, row_block_off + b, 0)),
                      pl.BlockSpec((1, N_EXPERTS, tt), lambda b, t, s: (b, 0, t)),
                      pl.BlockSpec((tt, D_MODEL), lambda b, t, s: (b * nt + t, 0)),
                      pl.BlockSpec((1, N_MOD, D_MODEL), lambda b, t, s: (mod_base + b * mod_per_batch, 0, 0)),
                      pl.BlockSpec((1, D_MODEL), lambda b, t, s: (0, 0))],
            out_specs=pl.BlockSpec((tt, D_MODEL), lambda b, t, s: (b * nt + t, 0)),
            scratch_shapes=[pltpu.VMEM((N_EXPERTS * region, D_MODEL), BF16),
                            pltpu.VMEM((TOK_BLOCK, D_MODEL), BF16), pltpu.VMEM((tt, D_MODEL), F32)]),
        compiler_params=_cparams("arbitrary", "arbitrary"),
        name="combine_final_norm",
    )(bounds, y, slot, x_mid, mod, final_g)


def kernel(x_prompt, x_sample, cache_k, cache_v, state_hgrn, c, c_ctx, w_ada, b_ada, norm1_g, w_in, hg_lb,
           hg_norm_g, attn_sink, w_out, norm2_g, w_router, w_gate, w_up, w_down, final_norm_g):
    n_p, t_p, _ = x_prompt.shape
    n_s, t_s, _ = x_sample.shape
    assert w_ada.shape[0] == 1 and 1 + n_s <= COND_ROWS
    layer = 0

    cond = jnp.zeros((COND_ROWS, D_MODEL), F32).at[0].set(c_ctx).at[1:1 + n_s].set(c)
    mod = _ada_modulation(cond, w_ada[layer], b_ada[layer]).reshape(COND_ROWS, N_MOD, D_MODEL)
    lb = jnp.cumsum(jax.nn.softmax(hg_lb.astype(F32), axis=0), axis=0)[layer]
    w_in_l = w_in[layer].astype(BF16)
    w_out_b = w_out[layer].astype(BF16)
    w_router_t = w_router[layer].T.astype(BF16)
    norm1 = norm1_g[layer].reshape(1, D_MODEL)
    norm2 = norm2_g[layer].reshape(1, D_MODEL)
    hg_gain = hg_norm_g[layer].reshape(1, HG_WIDTH)
    final_g = final_norm_g.reshape(1, D_MODEL)
    sink = attn_sink[layer]
    cos, sin_signed = _rope_tables(t_s)

    xp = x_prompt.reshape(n_p * t_p, D_MODEL)
    xs = x_sample.reshape(n_s * t_s, D_MODEL)
    groups = dict(p=dict(mod_base=0, rows_per_mod=n_p * t_p), s=dict(mod_base=1, rows_per_mod=t_s))

    proj_p = _in_projection(xp, mod, norm1, w_in_l, **groups["p"])
    proj_s = _in_projection(xs, mod, norm1, w_in_l, **groups["s"])

    ohg_p, new_state = _hgrn(proj_p, lb, hg_gain, None, n_batch=n_p, n_tok=t_p)
    ohg_s, _ = _hgrn(proj_s, lb, hg_gain, state_hgrn[:, layer:layer + 1].astype(F32), n_batch=n_s, n_tok=t_s)

    oatt_p = _context_attention(proj_p, sink, n_batch=n_p, n_tok=t_p)
    n_ctx = cache_k.shape[2]
    oatt_s = _window_attention(proj_s, cache_k[:, layer].reshape(n_s, n_ctx, KV_WIDTH),
                               cache_v[:, layer].reshape(n_s, n_ctx, KV_WIDTH), sink, cos, sin_signed,
                               n_batch=n_s, n_tok=t_s)

    xmid_p, h2_p = _out_projection(ohg_p, oatt_p, w_out_b, xp, mod, norm2, **groups["p"])
    xmid_s, h2_s = _out_projection(ohg_s, oatt_s, w_out_b, xs, mod, norm2, **groups["s"])

    cap_p = CAPACITY_FACTOR * t_p // N_EXPERTS
    cap_s = CAPACITY_FACTOR * t_s // N_EXPERTS
    off_s = n_p * cap_p // cap_s
    xg_p, gate_p, slot_p = _route_gather_requests(h2_p, w_router_t, n_batch=n_p, n_tok=t_p)
    slot_s, aff_s, bounds_s = _route_only(h2_s, w_router_t, n_batch=n_s, n_tok=t_s)
    xg, gate = _gather_windows(bounds_s, slot_s, aff_s, h2_s, xg_p, gate_p, n_batch=n_s, n_tok=t_s)
    bounds_p = jnp.tile(jnp.array([0, cap_p], jnp.int32), n_p * N_EXPERTS)

    y = _experts(xg, gate, w_gate[layer], w_up[layer], w_down[layer])

    y_prompt = _combine(bounds_p, y, slot_p, xmid_p, mod, final_g, n_batch=n_p, n_tok=t_p, row_block_off=0,
                        mod_base=0, mod_per_batch=0)
    y_sample = _combine(bounds_s, y, slot_s, xmid_s, mod, final_g, n_batch=n_s, n_tok=t_s, row_block_off=off_s,
                        mod_base=1, mod_per_batch=1)

    k_col = 5 * HG_WIDTH + ATT_WIDTH
    new_k = proj_p[:, k_col:k_col + KV_WIDTH].reshape(n_p, 1, t_p, ATT_KV_HEADS, HEAD_DIM)
    new_v = proj_p[:, k_col + KV_WIDTH:k_col + 2 * KV_WIDTH].reshape(n_p, 1, t_p, ATT_KV_HEADS, HEAD_DIM)
    return (y_prompt.reshape(n_p, t_p, D_MODEL), y_sample.reshape(n_s, t_s, D_MODEL), new_k, new_v, new_state)
```

```python
import functools

import jax
import jax.numpy as jnp
import numpy as np
from jax import lax
from jax.experimental import pallas as pl
from jax.experimental.pallas import tpu as pltpu

F32 = jnp.float32
BF16 = jnp.bfloat16

D_MODEL = 2048
HG_WIDTH = 1024
HG_HEADS = 8
HEAD_DIM = 128
ATT_HEADS = 8
ATT_KV_HEADS = 2
ATT_GROUP = ATT_HEADS // ATT_KV_HEADS
KV_WIDTH = ATT_KV_HEADS * HEAD_DIM
ATT_WIDTH = ATT_HEADS * HEAD_DIM
ATT_BLOCK = 128
GRID_W = 64
ROPE_BASE = 10000.0
ROPE_FREQS = HEAD_DIM // 4
N_EXPERTS = 16
CAPACITY_FACTOR = 2
EXPERT_FF = 5632
NORM_EPS = 1e-6
IN_WIDTH = 5 * HG_WIDTH + ATT_WIDTH + 2 * KV_WIDTH
N_MOD = 6
COND_ROWS = 16

HG_CHUNK = 128
HG_DIAG = 8
SEL_BLOCK = 256
TOK_BLOCK = 256
PIECE = 16

V7X_VMEM_BYTES = 64 * 1024 * 1024
VMEM_LIMIT = V7X_VMEM_BYTES - 8 * 1024 * 1024
VMEM_LIMIT_EXPERTS = V7X_VMEM_BYTES - 4 * 1024 * 1024


def _cparams(*sem, vmem_limit=VMEM_LIMIT):
    return pltpu.CompilerParams(dimension_semantics=sem, vmem_limit_bytes=vmem_limit)


def _sigmoid(x):
    return 1.0 / (1.0 + jnp.exp(-x))


def _silu(x):
    return x * _sigmoid(x)


def _dot(a, b):
    return jnp.dot(a, b, preferred_element_type=F32)


def _dot_nt(a, b):
    return lax.dot_general(a, b, (((1,), (1,)), ((), ())), preferred_element_type=F32)


def _dot_tn(a, b):
    return lax.dot_general(a, b, (((0,), (0,)), ((), ())), preferred_element_type=F32)


def _ada_kernel(c_ref, w_ref, b_ref, o_ref):
    s = _silu(c_ref[...]).astype(BF16)
    o_ref[...] = _dot(s, w_ref[...].astype(BF16)) + b_ref[...]


def _ada_modulation(cond, w_ada, b_ada):
    tn = 1024
    n = w_ada.shape[1]
    return pl.pallas_call(
        _ada_kernel,
        out_shape=jax.ShapeDtypeStruct((COND_ROWS, n), F32),
        grid=(n // tn,),
        in_specs=[pl.BlockSpec((COND_ROWS, D_MODEL), lambda j: (0, 0)),
                  pl.BlockSpec((D_MODEL, tn), lambda j: (0, j)),
                  pl.BlockSpec((1, tn), lambda j: (0, j))],
        out_specs=pl.BlockSpec((COND_ROWS, tn), lambda j: (0, j)),
        compiler_params=_cparams("arbitrary"),
        name="ada_modulation",
    )(cond, w_ada, b_ada.reshape(1, n))


def _norm_modulate(x, gain, shift, scale):
    var = jnp.mean(x * x, axis=-1, keepdims=True)
    return (x * lax.rsqrt(var + NORM_EPS) * gain) * (1.0 + scale) + shift


def _inproj_kernel(x_ref, mod_ref, g_ref, w_ref, o_ref, h_ref, *, rows):
    w = w_ref[...]
    chunks = [slice(r * rows, (r + 1) * rows) for r in range(x_ref.shape[0] // rows)]

    @pl.when(pl.program_id(1) == 0)
    def _():
        shift = mod_ref[0, 0:1, :]
        scale = mod_ref[0, 1:2, :]
        gain = g_ref[...]
        for sl in chunks:
            h = _norm_modulate(x_ref[sl, :], gain, shift, scale).astype(BF16)
            h_ref[sl, :] = h
            o_ref[sl, :] = _dot(h, w)

    @pl.when(pl.program_id(1) > 0)
    def _():
        for sl in chunks:
            o_ref[sl, :] = _dot(h_ref[sl, :], w)


def _in_projection(x, mod, gain, w, *, mod_base, rows_per_mod):
    m = x.shape[0]
    tm, tn = 1024, 512
    return pl.pallas_call(
        functools.partial(_inproj_kernel, rows=256),
        out_shape=jax.ShapeDtypeStruct((m, IN_WIDTH), F32),
        grid=(m // tm, IN_WIDTH // tn),
        in_specs=[pl.BlockSpec((tm, D_MODEL), lambda i, j: (i, 0)),
                  pl.BlockSpec((1, N_MOD, D_MODEL), lambda i, j: (mod_base + (i * tm) // rows_per_mod, 0, 0)),
                  pl.BlockSpec((1, D_MODEL), lambda i, j: (0, 0)),
                  pl.BlockSpec((D_MODEL, tn), lambda i, j: (0, j))],
        out_specs=pl.BlockSpec((tm, tn), lambda i, j: (i, j)),
        scratch_shapes=[pltpu.VMEM((tm, D_MODEL), BF16)],
        compiler_params=_cparams("parallel", "arbitrary"),
        name="in_projection",
    )(x, mod, gain, w)


def _hgrn_codes(reverse):
    L = HG_CHUNK
    t = lax.broadcasted_iota(jnp.int32, (L, L), 0)
    s = lax.broadcasted_iota(jnp.int32, (L, L), 1)
    code = jnp.where(t == s, 1, 0)
    h = L // 2
    while h >= 1:
        same = (t & ~(2 * h - 1)) == (s & ~(2 * h - 1))
        t_hi = (t & h) != 0
        s_hi = (s & h) != 0
        pair = (s_hi & ~t_hi) if reverse else (t_hi & ~s_hi)
        code = jnp.where(same & pair, h * 16, code)
        h //= 2
    return code


def _cumsum_rows(tri_bf16, g):
    g1 = g.astype(BF16)
    g2 = (g - g1.astype(F32)).astype(BF16)
    s = _dot(tri_bf16, jnp.concatenate([g1, g2], axis=1))
    return s[:, :HEAD_DIM] + s[:, HEAD_DIM:]


def _hgrn_intra(q, k, f, b, v_bf, code, reverse):
    L = HG_CHUNK
    G = HG_DIAG
    q_bf = q.astype(BF16)
    k_bf = k.astype(BF16)

    def level(h, ref, att):
        neg_abs = pltpu.bitcast(pltpu.bitcast(b - ref, jnp.int32) | jnp.int32(-2 ** 31), F32)
        e = jnp.exp2(neg_abs.astype(BF16))
        return jnp.where(code == h * 16, _dot_nt(q_bf * e, k_bf * e), att)

    att = jnp.where(code == 1, _dot_nt(q_bf, k_bf), 0.0)
    h = L // 2
    while h >= G:
        parts = []
        for p in range(L // (2 * h)):
            m = p * 2 * h + (h if reverse else h - 1)
            parts.append(jnp.broadcast_to(b[m:m + 1, :], (2 * h, HEAD_DIM)))
        att = level(h, parts[0] if len(parts) == 1 else jnp.concatenate(parts, axis=0), att)
        h //= 2

    b3 = b.reshape(L // G, G, HEAD_DIM)

    def group_row(r):
        return jnp.broadcast_to(b3[:, r:r + 1, :], (L // G, G, HEAD_DIM)).reshape(L, HEAD_DIM)

    row = lax.broadcasted_iota(jnp.int32, (L, HEAD_DIM), 0)
    att = level(4, group_row(4 if reverse else 3), att)
    lo, hi = (2, 6) if reverse else (1, 5)
    att = level(2, jnp.where((row & 4) == 0, group_row(lo), group_row(hi)), att)
    att = jnp.where(code == 16, _dot_nt(q_bf * f.astype(BF16), k_bf), att)
    return _dot(att.astype(BF16), v_bf)


def _hgrn_prepare(q, f, v, code, tri, reverse):
    L = HG_CHUNK
    k = 1.0 - f
    b = _cumsum_rows(tri, jnp.log2(f))
    b_tot = b[0:1, :] if reverse else b[L - 1:L, :]
    v_bf = v.astype(BF16)
    o_intra = _hgrn_intra(q, k, f, b, v_bf, code, reverse)
    q_in = (q * jnp.exp2(b)).astype(BF16)
    k_out = (k * jnp.exp2(b_tot - b)).astype(BF16)
    return o_intra, q_in, k_out, v_bf, jnp.exp2(b_tot)


def _hgrn_advance(st, prepared):
    o_intra, q_in, k_out, v_bf, decay = prepared
    return o_intra + _dot_nt(q_in, st.astype(BF16)), st * decay + _dot_tn(v_bf, k_out)


def _hgrn_kernel(*refs, n_tok, has_state):
    if has_state:
        (q_ref, ff_ref, fb_ref, v_ref, gate_ref, lb_ref, ng_ref, s0_ref,
         o_ref, sout_ref, of_ref, ob_ref, code_ref) = refs
    else:
        (q_ref, ff_ref, fb_ref, v_ref, gate_ref, lb_ref, ng_ref,
         o_ref, sout_ref, of_ref, ob_ref, code_ref) = refs
    L = HG_CHUNK
    nc = n_tok // L
    lb_f = lb_ref[0:1, :]
    lb_b = lb_ref[1:2, :]

    @pl.when((pl.program_id(0) == 0) & (pl.program_id(1) == 0))
    def _():
        code_ref[0] = _hgrn_codes(False)
        code_ref[1] = _hgrn_codes(True)

    code_f = code_ref[0]
    code_b = code_ref[1]
    ti = lax.broadcasted_iota(jnp.int32, (L, L), 0)
    si = lax.broadcasted_iota(jnp.int32, (L, L), 1)
    tri_f = jnp.where(si <= ti, 1.0, 0.0).astype(BF16)
    tri_b = jnp.where(si >= ti, 1.0, 0.0).astype(BF16)

    def prepare(row, f_ref, lb, code, tri, reverse):
        sl = pl.ds(row, L)
        q = _silu(q_ref[sl, :])
        f = lb + (1.0 - lb) * _sigmoid(f_ref[sl, :])
        return sl, _hgrn_prepare(q, f, v_ref[sl, :], code, tri, reverse)

    gain = ng_ref[...]

    def finish(sl, o):
        var = jnp.mean(o * o, axis=-1, keepdims=True)
        o_ref[sl, :] = ((o * lax.rsqrt(var + NORM_EPS) * gain) * _silu(gate_ref[sl, :])).astype(BF16)

    def make_body(other_direction):
        def body(c, states):
            st_f, st_b = states
            fwd, bwd = [], []
            for u in range(unroll):
                cu = c * unroll + u
                fwd.append(prepare(pl.multiple_of(cu * L, L), ff_ref, lb_f, code_f, tri_f, False))
                bwd.append(prepare(pl.multiple_of((nc - 1 - cu) * L, L), fb_ref, lb_b, code_b, tri_b, True))
            out_f, out_b = [], []
            for (sl_f, prep_f), (sl_b, prep_b) in zip(fwd, bwd):
                o_f, st_f = _hgrn_advance(st_f, prep_f)
                o_b, st_b = _hgrn_advance(st_b, prep_b)
                out_f.append((sl_f, o_f))
                out_b.append((sl_b, o_b))
            if other_direction == "later":
                for sl, o in out_f:
                    of_ref[sl, :] = o
                for sl, o in out_b:
                    ob_ref[sl, :] = o
            elif other_direction == "now":
                for u in range(unroll):
                    finish(out_f[u][0], out_f[u][1] + out_b[unroll - 1 - u][1])
            else:
                for sl, o in out_f:
                    finish(sl, o + ob_ref[sl, :])
                for sl, o in out_b:
                    finish(sl, of_ref[sl, :] + o)
            return st_f, st_b
        return body

    unroll = 2
    assert nc % unroll == 0
    n_iter = nc // unroll
    half = n_iter // 2
    if has_state:
        states = (s0_ref[0, 0, 0, 0].T, s0_ref[0, 0, 1, 0].T)
    else:
        states = (jnp.zeros((HEAD_DIM, HEAD_DIM), F32), jnp.zeros((HEAD_DIM, HEAD_DIM), F32))
    states = lax.fori_loop(0, half, make_body("later"), states)
    if n_iter % 2:
        states = make_body("now")(half, states)
    st_f, st_b = lax.fori_loop(half + n_iter % 2, n_iter, make_body("earlier"), states)
    sout_ref[0, 0, 0, 0] = st_f.T
    sout_ref[0, 0, 1, 0] = st_b.T


def _hgrn(proj, lb, norm_g, state, *, n_batch, n_tok):
    col = lambda k: (lambda b, h: (b, k * HG_HEADS + h))
    tok_spec = lambda k: pl.BlockSpec((n_tok, HEAD_DIM), col(k))
    st_spec = pl.BlockSpec((1, 1, 2, 1, HEAD_DIM, HEAD_DIM), lambda b, h: (b, 0, 0, h, 0, 0))
    has_state = state is not None
    return pl.pallas_call(
        functools.partial(_hgrn_kernel, n_tok=n_tok, has_state=has_state),
        out_shape=(jax.ShapeDtypeStruct((n_batch * n_tok, HG_WIDTH), BF16),
                   jax.ShapeDtypeStruct((n_batch, 1, 2, HG_HEADS, HEAD_DIM, HEAD_DIM), F32)),
        grid=(n_batch, HG_HEADS),
        in_specs=[tok_spec(0), tok_spec(1), tok_spec(2), tok_spec(3), tok_spec(4),
                  pl.BlockSpec((2, HEAD_DIM), lambda b, h: (0, h)),
                  pl.BlockSpec((1, HEAD_DIM), lambda b, h: (0, h))] + ([st_spec] if has_state else []),
        out_specs=(pl.BlockSpec((n_tok, HEAD_DIM), lambda b, h: (b, h)), st_spec),
        scratch_shapes=[pltpu.VMEM((n_tok, HEAD_DIM), F32), pltpu.VMEM((n_tok, HEAD_DIM), F32),
                        pltpu.VMEM((2, HG_CHUNK, HG_CHUNK), jnp.int32)],
        compiler_params=_cparams("arbitrary", "arbitrary"),
        name="hgrn2_scan",
    )(proj, proj, proj, proj, proj, lb, norm_g, *((state,) if has_state else ()))


def _stack_heads(x, kvh):
    return jnp.concatenate(
        [x[:, (kvh * ATT_GROUP + g) * HEAD_DIM:(kvh * ATT_GROUP + g + 1) * HEAD_DIM] for g in range(ATT_GROUP)],
        axis=0)


def _sink_column(sink_ref, kvh, rows):
    return jnp.concatenate(
        [jnp.full((rows, 1), sink_ref[kvh * ATT_GROUP + g], F32) for g in range(ATT_GROUP)],
        axis=0) * (1.0 / SOFTMAX_SCALE)


SOFTMAX_SCALE = HEAD_DIM ** -0.5
EXP2_SCALE = SOFTMAX_SCALE * 1.4426950408889634


def _ones_column(rows):
    lane = lax.broadcasted_iota(jnp.int32, (rows, HEAD_DIM), 1)
    return jnp.where(lane == 0, 1.0, 0.0).astype(BF16)


def _softmax_av(scores, values, sink_col):
    m = sink_col
    for s in scores:
        m = jnp.maximum(m, jnp.max(s, axis=-1, keepdims=True))
    acc = None
    for s, v in zip(scores, values):
        pv = _dot(jnp.exp2(((s - m) * EXP2_SCALE).astype(BF16)), v)
        acc = pv if acc is None else acc + pv
    denom = acc[:, HEAD_DIM:HEAD_DIM + 1] + jnp.exp2((sink_col - m) * EXP2_SCALE)
    return acc[:, :HEAD_DIM] / denom


def _ctx_attn_kernel(sink_ref, q_ref, k_ref, v_ref, o_ref):
    rows = q_ref.shape[0]
    q_all = q_ref[...]
    ones = _ones_column(k_ref.shape[0])
    for kvh in range(ATT_KV_HEADS):
        q = _stack_heads(q_all, kvh).astype(BF16)
        k = k_ref[:, kvh * HEAD_DIM:(kvh + 1) * HEAD_DIM].astype(BF16)
        v = jnp.concatenate([v_ref[:, kvh * HEAD_DIM:(kvh + 1) * HEAD_DIM].astype(BF16), ones], axis=1)
        o = _softmax_av([_dot_nt(q, k)], [v], _sink_column(sink_ref, kvh, rows))
        for g in range(ATT_GROUP):
            hd = kvh * ATT_GROUP + g
            o_ref[:, hd * HEAD_DIM:(hd + 1) * HEAD_DIM] = o[g * rows:(g + 1) * rows, :].astype(BF16)


def _context_attention(proj, sink, *, n_batch, n_tok):
    q_col = 5 * HG_WIDTH // ATT_WIDTH
    k_col = (5 * HG_WIDTH + ATT_WIDTH) // KV_WIDTH
    return pl.pallas_call(
        _ctx_attn_kernel,
        out_shape=jax.ShapeDtypeStruct((n_batch * n_tok, ATT_WIDTH), BF16),
        grid_spec=pltpu.PrefetchScalarGridSpec(
            num_scalar_prefetch=1,
            grid=(n_batch,),
            in_specs=[pl.BlockSpec((n_tok, ATT_WIDTH), lambda b, s: (b, q_col)),
                      pl.BlockSpec((n_tok, KV_WIDTH), lambda b, s: (b, k_col)),
                      pl.BlockSpec((n_tok, KV_WIDTH), lambda b, s: (b, k_col + 1))],
            out_specs=pl.BlockSpec((n_tok, ATT_WIDTH), lambda b, s: (b, 0))),
        compiler_params=_cparams("parallel"),
        name="context_attention",
    )(sink, proj, proj, proj)


def _rope(x, cos, sin_signed, even_group):
    partner = jnp.where(even_group, pltpu.roll(x, HEAD_DIM - ROPE_FREQS, 1), pltpu.roll(x, ROPE_FREQS, 1))
    return x * cos + partner * sin_signed


def _win_attn_kernel(sink_ref, q_ref, k_ref, v_ref, ck_ref, cv_ref, cos_ref, sin_ref, o_ref,
                     kpad_ref, vpad_ref, *, n_tok):
    blk = ATT_BLOCK
    nb = n_tok // blk
    i = pl.program_id(1)
    lane = lax.broadcasted_iota(jnp.int32, (blk, HEAD_DIM), 1)
    even_group = (lane & ROPE_FREQS) == 0

    @pl.when(i == 0)
    def _():
        kpad_ref[0:blk, :] = jnp.zeros((blk, KV_WIDTH), BF16)
        kpad_ref[blk + n_tok:2 * blk + n_tok, :] = jnp.zeros((blk, KV_WIDTH), BF16)
        vpad_ref[0:blk, :] = jnp.zeros((blk, 2 * KV_WIDTH), BF16)
        vpad_ref[blk + n_tok:2 * blk + n_tok, :] = jnp.zeros((blk, 2 * KV_WIDTH), BF16)
        ones = _ones_column(blk)

        def body(r, carry):
            src = pl.ds(pl.multiple_of(r * blk, blk), blk)
            dst = pl.ds(pl.multiple_of((r + 1) * blk, blk), blk)
            cos = cos_ref[src, :]
            sin = sin_ref[src, :]
            for kvh in range(ATT_KV_HEADS):
                cols = slice(kvh * HEAD_DIM, (kvh + 1) * HEAD_DIM)
                kpad_ref[dst, cols] = _rope(k_ref[src, cols], cos, sin, even_group).astype(BF16)
                vpad_ref[dst, 2 * kvh * HEAD_DIM:(2 * kvh + 1) * HEAD_DIM] = v_ref[src, cols].astype(BF16)
                vpad_ref[dst, (2 * kvh + 1) * HEAD_DIM:(2 * kvh + 2) * HEAD_DIM] = ones
            return carry

        lax.fori_loop(0, nb, body, 0)

    rows = pl.ds(pl.multiple_of(i * blk, blk), blk)
    cos = cos_ref[rows, :]
    sin = sin_ref[rows, :]
    band = pl.ds(pl.multiple_of(i * blk, blk), 3 * blk)
    r = lax.broadcasted_iota(jnp.int32, (blk, 3 * blk), 0)
    j = lax.broadcasted_iota(jnp.int32, (blk, 3 * blk), 1)
    kpos = j + (i - 1) * blk
    valid = (j >= r) & (j <= r + 2 * blk) & (kpos >= 0) & (kpos < n_tok)
    valid = jnp.concatenate([valid] * ATT_GROUP, axis=0)
    q_all = q_ref[...]
    ctx_ones = _ones_column(ck_ref.shape[1])
    for kvh in range(ATT_KV_HEADS):
        cols = slice(kvh * HEAD_DIM, (kvh + 1) * HEAD_DIM)
        q = jnp.concatenate(
            [_rope(q_all[:, (kvh * ATT_GROUP + g) * HEAD_DIM:(kvh * ATT_GROUP + g + 1) * HEAD_DIM],
                   cos, sin, even_group) for g in range(ATT_GROUP)], axis=0).astype(BF16)
        s_ctx = _dot_nt(q, ck_ref[0, :, cols].astype(BF16))
        s_loc = jnp.where(valid, _dot_nt(q, kpad_ref[band, cols]), -jnp.inf)
        v_ctx = jnp.concatenate([cv_ref[0, :, cols].astype(BF16), ctx_ones], axis=1)
        v_loc = vpad_ref[band, 2 * kvh * HEAD_DIM:(2 * kvh + 2) * HEAD_DIM]
        o = _softmax_av([s_ctx, s_loc], [v_ctx, v_loc], _sink_column(sink_ref, kvh, blk))
        for g in range(ATT_GROUP):
            hd = kvh * ATT_GROUP + g
            o_ref[:, hd * HEAD_DIM:(hd + 1) * HEAD_DIM] = o[g * blk:(g + 1) * blk, :].astype(BF16)


def _window_attention(proj, cache_k, cache_v, sink, cos, sin_signed, *, n_batch, n_tok):
    nb = n_tok // ATT_BLOCK
    n_ctx = cache_k.shape[1]
    q_col = 5 * HG_WIDTH // ATT_WIDTH
    k_col = (5 * HG_WIDTH + ATT_WIDTH) // KV_WIDTH
    table = pl.BlockSpec((n_tok, HEAD_DIM), lambda b, i, s: (0, 0))
    cache = pl.BlockSpec((1, n_ctx, KV_WIDTH), lambda b, i, s: (b, 0, 0))
    return pl.pallas_call(
        functools.partial(_win_attn_kernel, n_tok=n_tok),
        out_shape=jax.ShapeDtypeStruct((n_batch * n_tok, ATT_WIDTH), BF16),
        grid_spec=pltpu.PrefetchScalarGridSpec(
            num_scalar_prefetch=1,
            grid=(n_batch, nb),
            in_specs=[pl.BlockSpec((ATT_BLOCK, ATT_WIDTH), lambda b, i, s: (b * nb + i, q_col)),
                      pl.BlockSpec((n_tok, KV_WIDTH), lambda b, i, s: (b, k_col)),
                      pl.BlockSpec((n_tok, KV_WIDTH), lambda b, i, s: (b, k_col + 1)),
                      cache, cache, table, table],
            out_specs=pl.BlockSpec((ATT_BLOCK, ATT_WIDTH), lambda b, i, s: (b * nb + i, 0)),
            scratch_shapes=[pltpu.VMEM((n_tok + 2 * ATT_BLOCK, KV_WIDTH), BF16),
                            pltpu.VMEM((n_tok + 2 * ATT_BLOCK, 2 * KV_WIDTH), BF16)]),
        compiler_params=_cparams("parallel", "arbitrary"),
        name="window_attention",
    )(sink, proj, proj, proj, cache_k, cache_v, cos, sin_signed)


def _rope_tables(n_tok):
    rows = n_tok // GRID_W
    row = np.repeat(np.arange(rows), GRID_W).astype(np.float32)
    col = np.tile(np.arange(GRID_W), rows).astype(np.float32)
    inv = np.float32(ROPE_BASE) ** (-np.arange(ROPE_FREQS, dtype=np.float32) / np.float32(ROPE_FREQS))
    ar, ac = row[:, None] * inv, col[:, None] * inv
    cr, sr, cc, sc = np.cos(ar), np.sin(ar), np.cos(ac), np.sin(ac)
    return (jnp.asarray(np.concatenate([cr, cr, cc, cc], axis=1), F32),
            jnp.asarray(np.concatenate([-sr, sr, -sc, sc], axis=1), F32))


def _outproj_kernel(hg_ref, att_ref, w_ref, x_ref, mod_ref, g_ref, xo_ref, h_ref):
    chunk = 256
    for r in range(x_ref.shape[0] // chunk):
        sl = slice(r * chunk, (r + 1) * chunk)
        mix = _dot(hg_ref[sl, :], w_ref[0:HG_WIDTH, :]) + _dot(att_ref[sl, :], w_ref[HG_WIDTH:, :])
        x = x_ref[sl, :] + mod_ref[0, 2:3, :] * mix
        xo_ref[sl, :] = x
        h_ref[sl, :] = _norm_modulate(x, g_ref[...], mod_ref[0, 3:4, :], mod_ref[0, 4:5, :]).astype(BF16)


def _out_projection(o_hg, o_att, w_bf16, x, mod, gain, *, mod_base, rows_per_mod):
    m = x.shape[0]
    tm = 512
    row = lambda i: (i, 0)
    return pl.pallas_call(
        _outproj_kernel,
        out_shape=(jax.ShapeDtypeStruct((m, D_MODEL), F32), jax.ShapeDtypeStruct((m, D_MODEL), BF16)),
        grid=(m // tm,),
        in_specs=[pl.BlockSpec((tm, HG_WIDTH), row),
                  pl.BlockSpec((tm, ATT_WIDTH), row),
                  pl.BlockSpec((HG_WIDTH + ATT_WIDTH, D_MODEL), lambda i: (0, 0)),
                  pl.BlockSpec((tm, D_MODEL), row),
                  pl.BlockSpec((1, N_MOD, D_MODEL), lambda i: (mod_base + (i * tm) // rows_per_mod, 0, 0)),
                  pl.BlockSpec((1, D_MODEL), lambda i: (0, 0))],
        out_specs=(pl.BlockSpec((tm, D_MODEL), row), pl.BlockSpec((tm, D_MODEL), row)),
        compiler_params=_cparams("parallel"),
        name="out_projection",
    )(o_hg, o_att, w_bf16, x, mod, gain)


def _prefix_count(x):
    n = x.shape[1]
    i = lax.broadcasted_iota(jnp.int32, (SEL_BLOCK, SEL_BLOCK), 0)
    j = lax.broadcasted_iota(jnp.int32, (SEL_BLOCK, SEL_BLOCK), 1)
    upper = jnp.where(i < j, 1.0, 0.0).astype(BF16)
    off = jnp.zeros((x.shape[0], 1), F32)
    outs = []
    for blk in range(n // SEL_BLOCK):
        xb = x[:, blk * SEL_BLOCK:(blk + 1) * SEL_BLOCK]
        outs.append(_dot(xb.astype(BF16), upper) + off)
        off = off + jnp.sum(xb, axis=-1, keepdims=True)
    return outs[0] if len(outs) == 1 else jnp.concatenate(outs, axis=1)


def _route_select(h_ref, w_ref, cap):
    logits = _dot_nt(w_ref[...], h_ref[...])
    ex = jnp.exp(logits - jnp.max(logits, axis=0, keepdims=True))
    aff = ex / jnp.sum(ex, axis=0, keepdims=True)
    bits = pltpu.bitcast(aff, jnp.int32)

    thr = jnp.zeros((N_EXPERTS, 1), jnp.int32)
    shift = 31
    while shift > 0:
        width = min(4, shift)
        shift -= width
        digit = jnp.zeros((N_EXPERTS, 1), jnp.int32)
        for j in range(1, 2 ** width):
            cnt = jnp.sum(jnp.where(bits >= (thr | (j << shift)), 1.0, 0.0), axis=-1, keepdims=True)
            digit = digit + jnp.where(cnt >= cap, 1, 0)
        thr = thr | (digit << shift)
    above = jnp.where(bits > thr, 1.0, 0.0)
    tied = jnp.where(bits == thr, 1.0, 0.0)
    room = cap - jnp.sum(above, axis=-1, keepdims=True)
    sel = above + tied * jnp.where(_prefix_count(tied) < room, 1.0, 0.0)
    return jnp.where(sel > 0.0, _prefix_count(sel), -1.0), aff


def _one_hot_gather(slot_rows, aff_rows, h_ref, rows, cap):
    n_tok = h_ref.shape[0]
    c = (lax.broadcasted_iota(jnp.int32, (rows, n_tok), 0) & (cap - 1)).astype(F32)
    hit = c == slot_rows
    x = _dot(jnp.where(hit, 1.0, 0.0).astype(BF16), h_ref[...]).astype(BF16)
    return x, jnp.sum(jnp.where(hit, aff_rows, 0.0), axis=-1, keepdims=True)


def _route_request_kernel(h_ref, w_ref, x_ref, g_ref, slot_ref, *, cap):
    n_tok = h_ref.shape[0]
    slot, aff = _route_select(h_ref, w_ref, cap)
    slot_ref[0] = slot
    per_row = lambda a: jnp.concatenate(
        [jnp.broadcast_to(a[e:e + 1, :], (cap, n_tok)) for e in range(N_EXPERTS)], axis=0)
    x, g = _one_hot_gather(per_row(slot), per_row(aff), h_ref, N_EXPERTS * cap, cap)
    for e in range(N_EXPERTS):
        x_ref[e] = x[e * cap:(e + 1) * cap, :]
        g_ref[e] = g[e * cap:(e + 1) * cap, :]


def _route_only_kernel(h_ref, w_ref, slot_ref, aff_ref, bounds_ref, *, cap):
    n_tok = h_ref.shape[0]
    slot, aff = _route_select(h_ref, w_ref, cap)
    slot_ref[0] = slot
    aff_ref[0] = aff
    n = lax.broadcasted_iota(jnp.int32, (n_tok, 128), 0)
    t = lax.broadcasted_iota(jnp.int32, (n_tok, 128), 1)
    before = jnp.where(n < t * TOK_BLOCK, 1.0, 0.0).astype(BF16)
    chosen = jnp.where(slot >= 0.0, 1.0, 0.0).astype(BF16)
    bounds_ref[0] = _dot(chosen, before).astype(jnp.int32)


def _gather_window_kernel(bounds_ref, slot_ref, aff_ref, h_ref, xh_ref, gh_ref, x_ref, g_ref, xo_ref, go_ref, *,
                          n_batch, group):
    b = pl.program_id(0)
    eg = pl.program_id(1)
    n_tok = h_ref.shape[0]
    nt = n_tok // TOK_BLOCK
    cap = x_ref.shape[1]
    region = TOK_BLOCK // group

    @pl.when(b >= n_batch)
    def _():
        x_ref[...] = xh_ref[...]
        g_ref[...] = gh_ref[...]

    @pl.when(b < n_batch)
    def _():
        x_ref[...] = jnp.zeros(x_ref.shape, BF16)
        g_ref[...] = jnp.zeros(g_ref.shape, F32)
        local_i = lax.broadcasted_iota(jnp.int32, (region, TOK_BLOCK), 0).astype(F32)
        over_i = lax.broadcasted_iota(jnp.int32, (TOK_BLOCK, TOK_BLOCK), 0)

        def token_block(t, carry):
            toks = pl.ds(pl.multiple_of(t * TOK_BLOCK, TOK_BLOCK), TOK_BLOCK)
            hits, gates, firsts, n_overs, slots, affs = [], [], [], [], [], []
            for k in range(group):
                e = eg * group + k
                start, length = _window(bounds_ref, (b * N_EXPERTS + e) * (nt + 1), t)
                first = jnp.minimum(start, cap - region)
                s = slot_ref[0, pl.ds(e, 1), toks]
                a = aff_ref[0, pl.ds(e, 1), toks]
                hit = local_i == jnp.where(s >= 0.0, s - first.astype(F32), -1.0)
                hits.append(jnp.where(hit, 1.0, 0.0).astype(BF16))
                gates.append(jnp.sum(jnp.where(hit, a, 0.0), axis=-1, keepdims=True))
                firsts.append(first)
                n_overs.append(start + length - first - region)
                slots.append(s)
                affs.append(a)
            xc = _dot(jnp.concatenate(hits, axis=0), h_ref[toks, :]).astype(BF16)
            for k in range(group):
                dst = pl.ds(pl.multiple_of(firsts[k], PIECE), region)
                x_ref[k, dst, :] = x_ref[k, dst, :] + xc[k * region:(k + 1) * region, :]
                g_ref[k, dst, :] = g_ref[k, dst, :] + gates[k]
            for k in range(group):
                @pl.when(n_overs[k] > 0)
                def _(k=k):
                    base = firsts[k] + region
                    hit = (over_i + base).astype(F32) == slots[k]
                    xo_ref[...] = _dot(jnp.where(hit, 1.0, 0.0).astype(BF16), h_ref[toks, :]).astype(BF16)
                    go_ref[...] = jnp.sum(jnp.where(hit, affs[k], 0.0), axis=-1, keepdims=True)

                    def place(i, carry):
                        src = pl.ds(pl.multiple_of(i * PIECE, PIECE), PIECE)
                        dst = pl.ds(pl.multiple_of(base + i * PIECE, PIECE), PIECE)
                        x_ref[k, dst, :] = x_ref[k, dst, :] + xo_ref[src, :]
                        g_ref[k, dst, :] = g_ref[k, dst, :] + go_ref[src, :]
                        return carry

                    lax.fori_loop(0, n_overs[k] // PIECE, place, 0)
            return carry

        lax.fori_loop(0, nt, token_block, 0)


def _route_gather_requests(h, w_router_t, *, n_batch, n_tok):
    cap = CAPACITY_FACTOR * n_tok // N_EXPERTS
    out_block = lambda w: pl.BlockSpec((N_EXPERTS, cap, w), lambda b: (0, b, 0))
    return pl.pallas_call(
        functools.partial(_route_request_kernel, cap=cap),
        out_shape=(jax.ShapeDtypeStruct((N_EXPERTS, n_batch * cap, D_MODEL), BF16),
                   jax.ShapeDtypeStruct((N_EXPERTS, n_batch * cap, 1), F32),
                   jax.ShapeDtypeStruct((n_batch, N_EXPERTS, n_tok), F32)),
        grid=(n_batch,),
        in_specs=[pl.BlockSpec((n_tok, D_MODEL), lambda b: (b, 0)),
                  pl.BlockSpec((N_EXPERTS, D_MODEL), lambda b: (0, 0))],
        out_specs=(out_block(D_MODEL), out_block(1), pl.BlockSpec((1, N_EXPERTS, n_tok), lambda b: (b, 0, 0))),
        compiler_params=_cparams("parallel"),
        name="route_gather_requests",
    )(h, w_router_t)


def _route_only(h, w_router_t, *, n_batch, n_tok):
    cap = CAPACITY_FACTOR * n_tok // N_EXPERTS
    nt = n_tok // TOK_BLOCK
    expert_major = pl.BlockSpec((1, N_EXPERTS, n_tok), lambda b: (b, 0, 0))
    slot, aff, bounds = pl.pallas_call(
        functools.partial(_route_only_kernel, cap=cap),
        out_shape=(jax.ShapeDtypeStruct((n_batch, N_EXPERTS, n_tok), F32),
                   jax.ShapeDtypeStruct((n_batch, N_EXPERTS, n_tok), F32),
                   jax.ShapeDtypeStruct((n_batch, N_EXPERTS, 128), jnp.int32)),
        grid=(n_batch,),
        in_specs=[pl.BlockSpec((n_tok, D_MODEL), lambda b: (b, 0)),
                  pl.BlockSpec((N_EXPERTS, D_MODEL), lambda b: (0, 0))],
        out_specs=(expert_major, expert_major, pl.BlockSpec((1, N_EXPERTS, 128), lambda b: (b, 0, 0))),
        compiler_params=_cparams("parallel"),
        name="route_select",
    )(h, w_router_t)
    return slot, aff, bounds[:, :, :nt + 1].reshape(-1)


def _gather_windows(bounds, slot, aff, h, x_head, g_head, *, n_batch, n_tok):
    cap = CAPACITY_FACTOR * n_tok // N_EXPERTS
    group = 4
    head_rows = x_head.shape[1]
    assert head_rows % cap == 0 and N_EXPERTS % group == 0
    n_head = head_rows // cap
    rows_total = head_rows + n_batch * cap
    last = n_batch - 1
    request = lambda b: jnp.minimum(b, last)
    out_row = lambda b: jnp.where(b < n_batch, n_head + b, b - n_batch)
    head_row = lambda b: jnp.maximum(b - n_batch, 0)
    out_block = lambda w: pl.BlockSpec((group, cap, w), lambda b, g, s: (g, out_row(b), 0))
    head_block = lambda w: pl.BlockSpec((group, cap, w), lambda b, g, s: (g, head_row(b), 0))
    expert_major = pl.BlockSpec((1, N_EXPERTS, n_tok), lambda b, g, s: (request(b), 0, 0))
    return pl.pallas_call(
        functools.partial(_gather_window_kernel, n_batch=n_batch, group=group),
        out_shape=(jax.ShapeDtypeStruct((N_EXPERTS, rows_total, D_MODEL), BF16),
                   jax.ShapeDtypeStruct((N_EXPERTS, rows_total, 1), F32)),
        grid_spec=pltpu.PrefetchScalarGridSpec(
            num_scalar_prefetch=1,
            grid=(n_batch + n_head, N_EXPERTS // group),
            in_specs=[expert_major, expert_major,
                      pl.BlockSpec((n_tok, D_MODEL), lambda b, g, s: (request(b), 0)),
                      head_block(D_MODEL), head_block(1)],
            out_specs=(out_block(D_MODEL), out_block(1)),
            scratch_shapes=[pltpu.VMEM((TOK_BLOCK, D_MODEL), BF16), pltpu.VMEM((TOK_BLOCK, 1), F32)]),
        compiler_params=_cparams("arbitrary", "arbitrary"),
        name="gather_windows",
    )(bounds, slot, aff, h, x_head, g_head)


def _moe_kernel(x_ref, g_ref, wg_ref, wu_ref, wd_ref, y_ref, hid_ref, *, n_ff, rows):
    s = pl.program_id(2)
    tf = wg_ref.shape[2]

    @pl.when(s < n_ff)
    def _():
        wg = wg_ref[0].astype(BF16)
        wu = wu_ref[0].astype(BF16)
        cols = pl.ds(pl.multiple_of(s * tf, tf), tf)
        for r in range(x_ref.shape[1] // rows):
            sl = slice(r * rows, (r + 1) * rows)
            x = x_ref[0, sl, :]
            hid_ref[sl, cols] = (_silu(_dot(x, wg)) * _dot(x, wu)).astype(BF16)

    @pl.when(s >= n_ff)
    def _():
        y = _dot(hid_ref[...], wd_ref[0].astype(BF16))
        y_ref[0] = (y * g_ref[0]).astype(BF16)


def _experts(x, gate, w_gate, w_up, w_down):
    n_rows = x.shape[1]
    tr, tf, tn = n_rows // 2, 512, 256
    n_ff, n_out = EXPERT_FF // tf, D_MODEL // tn
    up_tile = lambda e, r, s: (e, 0, jnp.minimum(s, n_ff - 1))
    out_tile = lambda s: jnp.maximum(s - n_ff, 0)
    return pl.pallas_call(
        functools.partial(_moe_kernel, n_ff=n_ff, rows=256),
        out_shape=jax.ShapeDtypeStruct((N_EXPERTS, n_rows, D_MODEL), BF16),
        grid=(N_EXPERTS, n_rows // tr, n_ff + n_out),
        in_specs=[pl.BlockSpec((1, tr, D_MODEL), lambda e, r, s: (e, r, 0)),
                  pl.BlockSpec((1, tr, 1), lambda e, r, s: (e, r, 0), pipeline_mode=pl.Buffered(1)),
                  pl.BlockSpec((1, D_MODEL, tf), up_tile),
                  pl.BlockSpec((1, D_MODEL, tf), up_tile),
                  pl.BlockSpec((1, EXPERT_FF, tn), lambda e, r, s: (e, 0, out_tile(s)))],
        out_specs=pl.BlockSpec((1, tr, tn), lambda e, r, s: (e, r, out_tile(s))),
        scratch_shapes=[pltpu.VMEM((tr, EXPERT_FF), BF16)],
        compiler_params=_cparams("parallel", "parallel", "arbitrary", vmem_limit=VMEM_LIMIT_EXPERTS),
        name="expert_swiglu",
    )(x, gate, w_gate, w_up, w_down)


def _window(bounds_ref, base, t):
    p0 = bounds_ref[base + t]
    p1 = bounds_ref[base + t + 1]
    start = (p0 // PIECE) * PIECE
    return start, jnp.where(p1 > p0, ((p1 - start + PIECE - 1) // PIECE) * PIECE, 0)


def _combine_kernel(bounds_ref, y_ref, slot_ref, x_ref, mod_ref, g_ref, o_ref, ybuf_ref, obuf_ref, acc_ref, *,
                    nt, region):
    b = pl.program_id(0)
    t = pl.program_id(1)
    tt = x_ref.shape[0]
    cap = y_ref.shape[1]
    local_i = lax.broadcasted_iota(jnp.int32, (region, tt), 0).astype(F32)
    hits = []
    for e in range(N_EXPERTS):
        start, _ = _window(bounds_ref, (b * N_EXPERTS + e) * (nt + 1), t)
        first = jnp.minimum(start, cap - region)
        ybuf_ref[e * region:(e + 1) * region, :] = y_ref[e, pl.ds(pl.multiple_of(first, PIECE), region), :]
        s = slot_ref[0, e:e + 1, :]
        hit = local_i == jnp.where(s >= 0.0, s - first.astype(F32), -1.0)
        hits.append(jnp.where(hit, 1.0, 0.0).astype(BF16))
    acc_ref[...] = _dot_tn(jnp.concatenate(hits, axis=0), ybuf_ref[...])
    over_i = lax.broadcasted_iota(jnp.int32, (TOK_BLOCK, tt), 0)

    def overflow(e, carry):
        start, length = _window(bounds_ref, (b * N_EXPERTS + e) * (nt + 1), t)
        base = jnp.minimum(start, cap - region) + region
        n_over = start + length - base

        @pl.when(n_over > 0)
        def _():
            def copy(i, carry):
                obuf_ref[pl.ds(pl.multiple_of(i * PIECE, PIECE), PIECE), :] = (
                    y_ref[e, pl.ds(pl.multiple_of(base + i * PIECE, PIECE), PIECE), :])
                return carry

            def clear(i, carry):
                obuf_ref[pl.ds(pl.multiple_of(n_over + i * PIECE, PIECE), PIECE), :] = jnp.zeros((PIECE, D_MODEL), BF16)
                return carry

            lax.fori_loop(0, n_over // PIECE, copy, 0)
            lax.fori_loop(0, (TOK_BLOCK - n_over) // PIECE, clear, 0)
            hit = (over_i + base).astype(F32) == slot_ref[0, pl.ds(e, 1), :]
            acc_ref[...] += _dot_tn(jnp.where(hit, 1.0, 0.0).astype(BF16), obuf_ref[...])

        return carry

    lax.fori_loop(0, N_EXPERTS, overflow, 0)
    x = x_ref[...] + mod_ref[0, 5:6, :] * acc_ref[...]
    var = jnp.mean(x * x, axis=-1, keepdims=True)
    o_ref[...] = x * lax.rsqrt(var + NORM_EPS) * g_ref[...]


def _combine(bounds, y, slot, x_mid, mod, final_g, *, n_batch, n_tok, row_block_off, mod_base, mod_per_batch):
    cap = CAPACITY_FACTOR * n_tok // N_EXPERTS
    tt = TOK_BLOCK
    nt = n_tok // tt
    region = min(64, cap)
    assert cap - region <= TOK_BLOCK
    return pl.pallas_call(
        functools.partial(_combine_kernel, nt=nt, region=region),
        out_shape=jax.ShapeDtypeStruct((n_batch * n_tok, D_MODEL), F32),
        grid_spec=pltpu.PrefetchScalarGridSpec(
            num_scalar_prefetch=1,
            grid=(n_batch, nt),
            in_specs=[pl.BlockSpec((N_EXPERTS, cap, D_MODEL), lambda b, t, s: (0, row_block_off + b, 0)),
                      pl.BlockSpec((1, N_EXPERTS, tt), lambda b, t, s: (b, 0, t)),
                      pl.BlockSpec((tt, D_MODEL), lambda b, t, s: (b * nt + t, 0)),
                      pl.BlockSpec((1, N_MOD, D_MODEL), lambda b, t, s: (mod_base + b * mod_per_batch, 0, 0)),
                      pl.BlockSpec((1, D_MODEL), lambda b, t, s: (0, 0))],
            out_specs=pl.BlockSpec((tt, D_MODEL), lambda b, t, s: (b * nt + t, 0)),
            scratch_shapes=[pltpu.VMEM((N_EXPERTS * region, D_MODEL), BF16),
                            pltpu.VMEM((TOK_BLOCK, D_MODEL), BF16), pltpu.VMEM((tt, D_MODEL), F32)]),
        compiler_params=_cparams("arbitrary", "arbitrary"),
        name="combine_final_norm",
    )(bounds, y, slot, x_mid, mod, final_g)


def kernel(x_prompt, x_sample, cache_k, cache_v, state_hgrn, c, c_ctx, w_ada, b_ada, norm1_g, w_in, hg_lb,
           hg_norm_g, attn_sink, w_out, norm2_g, w_router, w_gate, w_up, w_down, final_norm_g):
    n_p, t_p, _ = x_prompt.shape
    n_s, t_s, _ = x_sample.shape
    assert w_ada.shape[0] == 1 and 1 + n_s <= COND_ROWS
    layer = 0

    cond = jnp.zeros((COND_ROWS, D_MODEL), F32).at[0].set(c_ctx).at[1:1 + n_s].set(c)
    mod = _ada_modulation(cond, w_ada[layer], b_ada[layer]).reshape(COND_ROWS, N_MOD, D_MODEL)
    lb = jnp.cumsum(jax.nn.softmax(hg_lb.astype(F32), axis=0), axis=0)[layer]
    w_in_l = w_in[layer].astype(BF16)
    w_out_b = w_out[layer].astype(BF16)
    w_router_t = w_router[layer].T.astype(BF16)
    norm1 = norm1_g[layer].reshape(1, D_MODEL)
    norm2 = norm2_g[layer].reshape(1, D_MODEL)
    hg_gain = hg_norm_g[layer].reshape(1, HG_WIDTH)
    final_g = final_norm_g.reshape(1, D_MODEL)
    sink = attn_sink[layer]
    cos, sin_signed = _rope_tables(t_s)

    xp = x_prompt.reshape(n_p * t_p, D_MODEL)
    xs = x_sample.reshape(n_s * t_s, D_MODEL)
    groups = dict(p=dict(mod_base=0, rows_per_mod=n_p * t_p), s=dict(mod_base=1, rows_per_mod=t_s))

    proj_p = _in_projection(xp, mod, norm1, w_in_l, **groups["p"])
    proj_s = _in_projection(xs, mod, norm1, w_in_l, **groups["s"])

    ohg_p, new_state = _hgrn(proj_p, lb, hg_gain, None, n_batch=n_p, n_tok=t_p)
    ohg_s, _ = _hgrn(proj_s, lb, hg_gain, state_hgrn[:, layer:layer + 1].astype(F32), n_batch=n_s, n_tok=t_s)

    oatt_p = _context_attention(proj_p, sink, n_batch=n_p, n_tok=t_p)
    n_ctx = cache_k.shape[2]
    oatt_s = _window_attention(proj_s, cache_k[:, layer].reshape(n_s, n_ctx, KV_WIDTH),
                               cache_v[:, layer].reshape(n_s, n_ctx, KV_WIDTH), sink, cos, sin_signed,
                               n_batch=n_s, n_tok=t_s)

    xmid_p, h2_p = _out_projection(ohg_p, oatt_p, w_out_b, xp, mod, norm2, **groups["p"])
    xmid_s, h2_s = _out_projection(ohg_s, oatt_s, w_out_b, xs, mod, norm2, **groups["s"])

    cap_p = CAPACITY_FACTOR * t_p // N_EXPERTS
    cap_s = CAPACITY_FACTOR * t_s // N_EXPERTS
    off_s = n_p * cap_p // cap_s
    xg_p, gate_p, slot_p = _route_gather_requests(h2_p, w_router_t, n_batch=n_p, n_tok=t_p)
    slot_s, aff_s, bounds_s = _route_only(h2_s, w_router_t, n_batch=n_s, n_tok=t_s)
    xg, gate = _gather_windows(bounds_s, slot_s, aff_s, h2_s, xg_p, gate_p, n_batch=n_s, n_tok=t_s)
    bounds_p = jnp.tile(jnp.array([0, cap_p], jnp.int32), n_p * N_EXPERTS)

    y = _experts(xg, gate, w_gate[layer], w_up[layer], w_down[layer])

    y_prompt = _combine(bounds_p, y, slot_p, xmid_p, mod, final_g, n_batch=n_p, n_tok=t_p, row_block_off=0,
                        mod_base=0, mod_per_batch=0)
    y_sample = _combine(bounds_s, y, slot_s, xmid_s, mod, final_g, n_batch=n_s, n_tok=t_s, row_block_off=off_s,
                        mod_base=1, mod_per_batch=1)

    k_col = 5 * HG_WIDTH + ATT_WIDTH
    new_k = proj_p[:, k_col:k_col + KV_WIDTH].reshape(n_p, 1, t_p, ATT_KV_HEADS, HEAD_DIM)
    new_v = proj_p[:, k_col + KV_WIDTH:k_col + 2 * KV_WIDTH].reshape(n_p, 1, t_p, ATT_KV_HEADS, HEAD_DIM)
    return (y_prompt.reshape(n_p, t_p, D_MODEL), y_sample.reshape(n_s, t_s, D_MODEL), new_k, new_v, new_state)
```

```python
import functools

import jax
import jax.numpy as jnp
import numpy as np
from jax import lax
from jax.experimental import pallas as pl
from jax.experimental.pallas import tpu as pltpu

F32 = jnp.float32
BF16 = jnp.bfloat16

D_MODEL = 2048
HG_WIDTH = 1024
HG_HEADS = 8
HEAD_DIM = 128
ATT_HEADS = 8
ATT_KV_HEADS = 2
ATT_GROUP = ATT_HEADS // ATT_KV_HEADS
KV_WIDTH = ATT_KV_HEADS * HEAD_DIM
ATT_WIDTH = ATT_HEADS * HEAD_DIM
ATT_BLOCK = 128
GRID_W = 64
ROPE_BASE = 10000.0
ROPE_FREQS = HEAD_DIM // 4
N_EXPERTS = 16
CAPACITY_FACTOR = 2
EXPERT_FF = 5632
NORM_EPS = 1e-6
IN_WIDTH = 5 * HG_WIDTH + ATT_WIDTH + 2 * KV_WIDTH
N_MOD = 6
COND_ROWS = 16

HG_CHUNK = 128
HG_DIAG = 8
SEL_BLOCK = 256
TOK_BLOCK = 256
PIECE = 16

V7X_VMEM_BYTES = 64 * 1024 * 1024
VMEM_LIMIT = V7X_VMEM_BYTES - 8 * 1024 * 1024


def _cparams(*sem):
    return pltpu.CompilerParams(dimension_semantics=sem, vmem_limit_bytes=VMEM_LIMIT)


def _sigmoid(x):
    return 1.0 / (1.0 + jnp.exp(-x))


def _silu(x):
    return x * _sigmoid(x)


def _dot(a, b):
    return jnp.dot(a, b, preferred_element_type=F32)


def _dot_nt(a, b):
    return lax.dot_general(a, b, (((1,), (1,)), ((), ())), preferred_element_type=F32)


def _dot_tn(a, b):
    return lax.dot_general(a, b, (((0,), (0,)), ((), ())), preferred_element_type=F32)


def _ada_kernel(c_ref, w_ref, b_ref, o_ref):
    s = _silu(c_ref[...]).astype(BF16)
    o_ref[...] = _dot(s, w_ref[...].astype(BF16)) + b_ref[...]


def _ada_modulation(cond, w_ada, b_ada):
    tn = 1024
    n = w_ada.shape[1]
    return pl.pallas_call(
        _ada_kernel,
        out_shape=jax.ShapeDtypeStruct((COND_ROWS, n), F32),
        grid=(n // tn,),
        in_specs=[pl.BlockSpec((COND_ROWS, D_MODEL), lambda j: (0, 0)),
                  pl.BlockSpec((D_MODEL, tn), lambda j: (0, j)),
                  pl.BlockSpec((1, tn), lambda j: (0, j))],
        out_specs=pl.BlockSpec((COND_ROWS, tn), lambda j: (0, j)),
        compiler_params=_cparams("arbitrary"),
        name="ada_modulation",
    )(cond, w_ada, b_ada.reshape(1, n))


def _norm_modulate(x, gain, shift, scale):
    var = jnp.mean(x * x, axis=-1, keepdims=True)
    return (x * lax.rsqrt(var + NORM_EPS) * gain) * (1.0 + scale) + shift


def _inproj_kernel(x_ref, mod_ref, g_ref, w_ref, o_ref, h_ref, *, rows):
    w = w_ref[...]
    chunks = [slice(r * rows, (r + 1) * rows) for r in range(x_ref.shape[0] // rows)]

    @pl.when(pl.program_id(1) == 0)
    def _():
        shift = mod_ref[0, 0:1, :]
        scale = mod_ref[0, 1:2, :]
        gain = g_ref[...]
        for sl in chunks:
            h = _norm_modulate(x_ref[sl, :], gain, shift, scale).astype(BF16)
            h_ref[sl, :] = h
            o_ref[sl, :] = _dot(h, w)

    @pl.when(pl.program_id(1) > 0)
    def _():
        for sl in chunks:
            o_ref[sl, :] = _dot(h_ref[sl, :], w)


def _in_projection(x, mod, gain, w, *, mod_base, rows_per_mod):
    m = x.shape[0]
    tm, tn = 1024, 512
    return pl.pallas_call(
        functools.partial(_inproj_kernel, rows=256),
        out_shape=jax.ShapeDtypeStruct((m, IN_WIDTH), F32),
        grid=(m // tm, IN_WIDTH // tn),
        in_specs=[pl.BlockSpec((tm, D_MODEL), lambda i, j: (i, 0)),
                  pl.BlockSpec((1, N_MOD, D_MODEL), lambda i, j: (mod_base + (i * tm) // rows_per_mod, 0, 0)),
                  pl.BlockSpec((1, D_MODEL), lambda i, j: (0, 0)),
                  pl.BlockSpec((D_MODEL, tn), lambda i, j: (0, j))],
        out_specs=pl.BlockSpec((tm, tn), lambda i, j: (i, j)),
        scratch_shapes=[pltpu.VMEM((tm, D_MODEL), BF16)],
        compiler_params=_cparams("parallel", "arbitrary"),
        name="in_projection",
    )(x, mod, gain, w)


def _hgrn_codes(reverse):
    L = HG_CHUNK
    t = lax.broadcasted_iota(jnp.int32, (L, L), 0)
    s = lax.broadcasted_iota(jnp.int32, (L, L), 1)
    code = jnp.where(t == s, 1, 0)
    h = L // 2
    while h >= 1:
        same = (t & ~(2 * h - 1)) == (s & ~(2 * h - 1))
        t_hi = (t & h) != 0
        s_hi = (s & h) != 0
        pair = (s_hi & ~t_hi) if reverse else (t_hi & ~s_hi)
        code = jnp.where(same & pair, h * 16, code)
        h //= 2
    return code


def _cumsum_rows(tri_bf16, g):
    g1 = g.astype(BF16)
    g2 = (g - g1.astype(F32)).astype(BF16)
    s = _dot(tri_bf16, jnp.concatenate([g1, g2], axis=1))
    return s[:, :HEAD_DIM] + s[:, HEAD_DIM:]


def _hgrn_intra(q, k, f, b, v_bf, code, reverse):
    L = HG_CHUNK
    G = HG_DIAG
    q_bf = q.astype(BF16)
    k_bf = k.astype(BF16)

    def level(h, ref, att):
        neg_abs = pltpu.bitcast(pltpu.bitcast(b - ref, jnp.int32) | jnp.int32(-2 ** 31), F32)
        e = jnp.exp2(neg_abs.astype(BF16))
        return jnp.where(code == h * 16, _dot_nt(q_bf * e, k_bf * e), att)

    att = jnp.where(code == 1, _dot_nt(q_bf, k_bf), 0.0)
    h = L // 2
    while h >= G:
        parts = []
        for p in range(L // (2 * h)):
            m = p * 2 * h + (h if reverse else h - 1)
            parts.append(jnp.broadcast_to(b[m:m + 1, :], (2 * h, HEAD_DIM)))
        att = level(h, parts[0] if len(parts) == 1 else jnp.concatenate(parts, axis=0), att)
        h //= 2

    b3 = b.reshape(L // G, G, HEAD_DIM)

    def group_row(r):
        return jnp.broadcast_to(b3[:, r:r + 1, :], (L // G, G, HEAD_DIM)).reshape(L, HEAD_DIM)

    row = lax.broadcasted_iota(jnp.int32, (L, HEAD_DIM), 0)
    att = level(4, group_row(4 if reverse else 3), att)
    lo, hi = (2, 6) if reverse else (1, 5)
    att = level(2, jnp.where((row & 4) == 0, group_row(lo), group_row(hi)), att)
    att = jnp.where(code == 16, _dot_nt(q_bf * f.astype(BF16), k_bf), att)
    return _dot(att.astype(BF16), v_bf)


def _hgrn_prepare(q, f, v, code, tri, reverse):
    L = HG_CHUNK
    k = 1.0 - f
    b = _cumsum_rows(tri, jnp.log2(f))
    b_tot = b[0:1, :] if reverse else b[L - 1:L, :]
    v_bf = v.astype(BF16)
    o_intra = _hgrn_intra(q, k, f, b, v_bf, code, reverse)
    q_in = (q * jnp.exp2(b)).astype(BF16)
    k_out = (k * jnp.exp2(b_tot - b)).astype(BF16)
    return o_intra, q_in, k_out, v_bf, jnp.exp2(b_tot)


def _hgrn_advance(st, prepared):
    o_intra, q_in, k_out, v_bf, decay = prepared
    return o_intra + _dot_nt(q_in, st.astype(BF16)), st * decay + _dot_tn(v_bf, k_out)


def _hgrn_kernel(*refs, n_tok, has_state):
    if has_state:
        (q_ref, ff_ref, fb_ref, v_ref, gate_ref, lb_ref, ng_ref, s0_ref,
         o_ref, sout_ref, of_ref, ob_ref, code_ref) = refs
    else:
        (q_ref, ff_ref, fb_ref, v_ref, gate_ref, lb_ref, ng_ref,
         o_ref, sout_ref, of_ref, ob_ref, code_ref) = refs
    L = HG_CHUNK
    nc = n_tok // L
    lb_f = lb_ref[0:1, :]
    lb_b = lb_ref[1:2, :]

    @pl.when((pl.program_id(0) == 0) & (pl.program_id(1) == 0))
    def _():
        code_ref[0] = _hgrn_codes(False)
        code_ref[1] = _hgrn_codes(True)

    code_f = code_ref[0]
    code_b = code_ref[1]
    ti = lax.broadcasted_iota(jnp.int32, (L, L), 0)
    si = lax.broadcasted_iota(jnp.int32, (L, L), 1)
    tri_f = jnp.where(si <= ti, 1.0, 0.0).astype(BF16)
    tri_b = jnp.where(si >= ti, 1.0, 0.0).astype(BF16)

    def prepare(row, f_ref, lb, code, tri, reverse):
        sl = pl.ds(row, L)
        q = _silu(q_ref[sl, :])
        f = lb + (1.0 - lb) * _sigmoid(f_ref[sl, :])
        return sl, _hgrn_prepare(q, f, v_ref[sl, :], code, tri, reverse)

    gain = ng_ref[...]

    def finish(sl, o):
        var = jnp.mean(o * o, axis=-1, keepdims=True)
        o_ref[sl, :] = ((o * lax.rsqrt(var + NORM_EPS) * gain) * _silu(gate_ref[sl, :])).astype(BF16)

    def make_body(other_direction):
        def body(c, states):
            st_f, st_b = states
            fwd, bwd = [], []
            for u in range(unroll):
                cu = c * unroll + u
                fwd.append(prepare(pl.multiple_of(cu * L, L), ff_ref, lb_f, code_f, tri_f, False))
                bwd.append(prepare(pl.multiple_of((nc - 1 - cu) * L, L), fb_ref, lb_b, code_b, tri_b, True))
            out_f, out_b = [], []
            for (sl_f, prep_f), (sl_b, prep_b) in zip(fwd, bwd):
                o_f, st_f = _hgrn_advance(st_f, prep_f)
                o_b, st_b = _hgrn_advance(st_b, prep_b)
                out_f.append((sl_f, o_f))
                out_b.append((sl_b, o_b))
            if other_direction == "later":
                for sl, o in out_f:
                    of_ref[sl, :] = o
                for sl, o in out_b:
                    ob_ref[sl, :] = o
            elif other_direction == "now":
                for u in range(unroll):
                    finish(out_f[u][0], out_f[u][1] + out_b[unroll - 1 - u][1])
            else:
                for sl, o in out_f:
                    finish(sl, o + ob_ref[sl, :])
                for sl, o in out_b:
                    finish(sl, of_ref[sl, :] + o)
            return st_f, st_b
        return body

    unroll = 4 if nc % 8 == 0 else 2
    assert nc % unroll == 0
    n_iter = nc // unroll
    half = n_iter // 2
    if has_state:
        states = (s0_ref[0, 0, 0, 0].T, s0_ref[0, 0, 1, 0].T)
    else:
        states = (jnp.zeros((HEAD_DIM, HEAD_DIM), F32), jnp.zeros((HEAD_DIM, HEAD_DIM), F32))
    states = lax.fori_loop(0, half, make_body("later"), states)
    if n_iter % 2:
        states = make_body("now")(half, states)
    st_f, st_b = lax.fori_loop(half + n_iter % 2, n_iter, make_body("earlier"), states)
    sout_ref[0, 0, 0, 0] = st_f.T
    sout_ref[0, 0, 1, 0] = st_b.T


def _hgrn(proj, lb, norm_g, state, *, n_batch, n_tok):
    col = lambda k: (lambda b, h: (b, k * HG_HEADS + h))
    tok_spec = lambda k: pl.BlockSpec((n_tok, HEAD_DIM), col(k))
    st_spec = pl.BlockSpec((1, 1, 2, 1, HEAD_DIM, HEAD_DIM), lambda b, h: (b, 0, 0, h, 0, 0))
    has_state = state is not None
    return pl.pallas_call(
        functools.partial(_hgrn_kernel, n_tok=n_tok, has_state=has_state),
        out_shape=(jax.ShapeDtypeStruct((n_batch * n_tok, HG_WIDTH), BF16),
                   jax.ShapeDtypeStruct((n_batch, 1, 2, HG_HEADS, HEAD_DIM, HEAD_DIM), F32)),
        grid=(n_batch, HG_HEADS),
        in_specs=[tok_spec(0), tok_spec(1), tok_spec(2), tok_spec(3), tok_spec(4),
                  pl.BlockSpec((2, HEAD_DIM), lambda b, h: (0, h)),
                  pl.BlockSpec((1, HEAD_DIM), lambda b, h: (0, h))] + ([st_spec] if has_state else []),
        out_specs=(pl.BlockSpec((n_tok, HEAD_DIM), lambda b, h: (b, h)), st_spec),
        scratch_shapes=[pltpu.VMEM((n_tok, HEAD_DIM), F32), pltpu.VMEM((n_tok, HEAD_DIM), F32),
                        pltpu.VMEM((2, HG_CHUNK, HG_CHUNK), jnp.int32)],
        compiler_params=_cparams("arbitrary", "arbitrary"),
        name="hgrn2_scan",
    )(proj, proj, proj, proj, proj, lb, norm_g, *((state,) if has_state else ()))


def _stack_heads(x, kvh):
    return jnp.concatenate(
        [x[:, (kvh * ATT_GROUP + g) * HEAD_DIM:(kvh * ATT_GROUP + g + 1) * HEAD_DIM] for g in range(ATT_GROUP)],
        axis=0)


def _sink_column(sink_ref, kvh, rows):
    return jnp.concatenate(
        [jnp.full((rows, 1), sink_ref[kvh * ATT_GROUP + g], F32) for g in range(ATT_GROUP)],
        axis=0) * (1.0 / SOFTMAX_SCALE)


SOFTMAX_SCALE = HEAD_DIM ** -0.5
EXP2_SCALE = SOFTMAX_SCALE * 1.4426950408889634


def _ones_column(rows):
    lane = lax.broadcasted_iota(jnp.int32, (rows, HEAD_DIM), 1)
    return jnp.where(lane == 0, 1.0, 0.0).astype(BF16)


def _softmax_av(scores, values, sink_col):
    m = sink_col
    for s in scores:
        m = jnp.maximum(m, jnp.max(s, axis=-1, keepdims=True))
    acc = None
    for s, v in zip(scores, values):
        pv = _dot(jnp.exp2(((s - m) * EXP2_SCALE).astype(BF16)), v)
        acc = pv if acc is None else acc + pv
    denom = acc[:, HEAD_DIM:HEAD_DIM + 1] + jnp.exp2((sink_col - m) * EXP2_SCALE)
    return acc[:, :HEAD_DIM] / denom


def _ctx_attn_kernel(sink_ref, q_ref, k_ref, v_ref, o_ref):
    rows = q_ref.shape[0]
    q_all = q_ref[...]
    ones = _ones_column(k_ref.shape[0])
    for kvh in range(ATT_KV_HEADS):
        q = _stack_heads(q_all, kvh).astype(BF16)
        k = k_ref[:, kvh * HEAD_DIM:(kvh + 1) * HEAD_DIM].astype(BF16)
        v = jnp.concatenate([v_ref[:, kvh * HEAD_DIM:(kvh + 1) * HEAD_DIM].astype(BF16), ones], axis=1)
        o = _softmax_av([_dot_nt(q, k)], [v], _sink_column(sink_ref, kvh, rows))
        for g in range(ATT_GROUP):
            hd = kvh * ATT_GROUP + g
            o_ref[:, hd * HEAD_DIM:(hd + 1) * HEAD_DIM] = o[g * rows:(g + 1) * rows, :].astype(BF16)


def _context_attention(proj, sink, *, n_batch, n_tok):
    q_col = 5 * HG_WIDTH // ATT_WIDTH
    k_col = (5 * HG_WIDTH + ATT_WIDTH) // KV_WIDTH
    return pl.pallas_call(
        _ctx_attn_kernel,
        out_shape=jax.ShapeDtypeStruct((n_batch * n_tok, ATT_WIDTH), BF16),
        grid_spec=pltpu.PrefetchScalarGridSpec(
            num_scalar_prefetch=1,
            grid=(n_batch,),
            in_specs=[pl.BlockSpec((n_tok, ATT_WIDTH), lambda b, s: (b, q_col)),
                      pl.BlockSpec((n_tok, KV_WIDTH), lambda b, s: (b, k_col)),
                      pl.BlockSpec((n_tok, KV_WIDTH), lambda b, s: (b, k_col + 1))],
            out_specs=pl.BlockSpec((n_tok, ATT_WIDTH), lambda b, s: (b, 0))),
        compiler_params=_cparams("parallel"),
        name="context_attention",
    )(sink, proj, proj, proj)


def _rope(x, cos, sin_signed, even_group):
    partner = jnp.where(even_group, pltpu.roll(x, HEAD_DIM - ROPE_FREQS, 1), pltpu.roll(x, ROPE_FREQS, 1))
    return x * cos + partner * sin_signed


def _win_attn_kernel(sink_ref, q_ref, k_ref, v_ref, ck_ref, cv_ref, cos_ref, sin_ref, o_ref,
                     kpad_ref, vpad_ref, *, n_tok):
    blk = ATT_BLOCK
    nb = n_tok // blk
    i = pl.program_id(1)
    lane = lax.broadcasted_iota(jnp.int32, (blk, HEAD_DIM), 1)
    even_group = (lane & ROPE_FREQS) == 0

    @pl.when(i == 0)
    def _():
        kpad_ref[0:blk, :] = jnp.zeros((blk, KV_WIDTH), BF16)
        kpad_ref[blk + n_tok:2 * blk + n_tok, :] = jnp.zeros((blk, KV_WIDTH), BF16)
        vpad_ref[0:blk, :] = jnp.zeros((blk, 2 * KV_WIDTH), BF16)
        vpad_ref[blk + n_tok:2 * blk + n_tok, :] = jnp.zeros((blk, 2 * KV_WIDTH), BF16)
        ones = _ones_column(blk)

        def body(r, carry):
            src = pl.ds(pl.multiple_of(r * blk, blk), blk)
            dst = pl.ds(pl.multiple_of((r + 1) * blk, blk), blk)
            cos = cos_ref[src, :]
            sin = sin_ref[src, :]
            for kvh in range(ATT_KV_HEADS):
                cols = slice(kvh * HEAD_DIM, (kvh + 1) * HEAD_DIM)
                kpad_ref[dst, cols] = _rope(k_ref[src, cols], cos, sin, even_group).astype(BF16)
                vpad_ref[dst, 2 * kvh * HEAD_DIM:(2 * kvh + 1) * HEAD_DIM] = v_ref[src, cols].astype(BF16)
                vpad_ref[dst, (2 * kvh + 1) * HEAD_DIM:(2 * kvh + 2) * HEAD_DIM] = ones
            return carry

        lax.fori_loop(0, nb, body, 0)

    rows = pl.ds(pl.multiple_of(i * blk, blk), blk)
    cos = cos_ref[rows, :]
    sin = sin_ref[rows, :]
    band = pl.ds(pl.multiple_of(i * blk, blk), 3 * blk)
    r = lax.broadcasted_iota(jnp.int32, (blk, 3 * blk), 0)
    j = lax.broadcasted_iota(jnp.int32, (blk, 3 * blk), 1)
    kpos = j + (i - 1) * blk
    valid = (j >= r) & (j <= r + 2 * blk) & (kpos >= 0) & (kpos < n_tok)
    valid = jnp.concatenate([valid] * ATT_GROUP, axis=0)
    q_all = q_ref[...]
    ctx_ones = _ones_column(ck_ref.shape[1])
    for kvh in range(ATT_KV_HEADS):
        cols = slice(kvh * HEAD_DIM, (kvh + 1) * HEAD_DIM)
        q = jnp.concatenate(
            [_rope(q_all[:, (kvh * ATT_GROUP + g) * HEAD_DIM:(kvh * ATT_GROUP + g + 1) * HEAD_DIM],
                   cos, sin, even_group) for g in range(ATT_GROUP)], axis=0).astype(BF16)
        s_ctx = _dot_nt(q, ck_ref[0, :, cols].astype(BF16))
        s_loc = jnp.where(valid, _dot_nt(q, kpad_ref[band, cols]), -jnp.inf)
        v_ctx = jnp.concatenate([cv_ref[0, :, cols].astype(BF16), ctx_ones], axis=1)
        v_loc = vpad_ref[band, 2 * kvh * HEAD_DIM:(2 * kvh + 2) * HEAD_DIM]
        o = _softmax_av([s_ctx, s_loc], [v_ctx, v_loc], _sink_column(sink_ref, kvh, blk))
        for g in range(ATT_GROUP):
            hd = kvh * ATT_GROUP + g
            o_ref[:, hd * HEAD_DIM:(hd + 1) * HEAD_DIM] = o[g * blk:(g + 1) * blk, :].astype(BF16)


def _window_attention(proj, cache_k, cache_v, sink, cos, sin_signed, *, n_batch, n_tok):
    nb = n_tok // ATT_BLOCK
    n_ctx = cache_k.shape[1]
    q_col = 5 * HG_WIDTH // ATT_WIDTH
    k_col = (5 * HG_WIDTH + ATT_WIDTH) // KV_WIDTH
    table = pl.BlockSpec((n_tok, HEAD_DIM), lambda b, i, s: (0, 0))
    cache = pl.BlockSpec((1, n_ctx, KV_WIDTH), lambda b, i, s: (b, 0, 0))
    return pl.pallas_call(
        functools.partial(_win_attn_kernel, n_tok=n_tok),
        out_shape=jax.ShapeDtypeStruct((n_batch * n_tok, ATT_WIDTH), BF16),
        grid_spec=pltpu.PrefetchScalarGridSpec(
            num_scalar_prefetch=1,
            grid=(n_batch, nb),
            in_specs=[pl.BlockSpec((ATT_BLOCK, ATT_WIDTH), lambda b, i, s: (b * nb + i, q_col)),
                      pl.BlockSpec((n_tok, KV_WIDTH), lambda b, i, s: (b, k_col)),
                      pl.BlockSpec((n_tok, KV_WIDTH), lambda b, i, s: (b, k_col + 1)),
                      cache, cache, table, table],
            out_specs=pl.BlockSpec((ATT_BLOCK, ATT_WIDTH), lambda b, i, s: (b * nb + i, 0)),
            scratch_shapes=[pltpu.VMEM((n_tok + 2 * ATT_BLOCK, KV_WIDTH), BF16),
                            pltpu.VMEM((n_tok + 2 * ATT_BLOCK, 2 * KV_WIDTH), BF16)]),
        compiler_params=_cparams("parallel", "arbitrary"),
        name="window_attention",
    )(sink, proj, proj, proj, cache_k, cache_v, cos, sin_signed)


def _rope_tables(n_tok):
    rows = n_tok // GRID_W
    row = np.repeat(np.arange(rows), GRID_W).astype(np.float32)
    col = np.tile(np.arange(GRID_W), rows).astype(np.float32)
    inv = np.float32(ROPE_BASE) ** (-np.arange(ROPE_FREQS, dtype=np.float32) / np.float32(ROPE_FREQS))
    ar, ac = row[:, None] * inv, col[:, None] * inv
    cr, sr, cc, sc = np.cos(ar), np.sin(ar), np.cos(ac), np.sin(ac)
    return (jnp.asarray(np.concatenate([cr, cr, cc, cc], axis=1), F32),
            jnp.asarray(np.concatenate([-sr, sr, -sc, sc], axis=1), F32))


def _outproj_kernel(hg_ref, att_ref, w_ref, x_ref, mod_ref, g_ref, xo_ref, h_ref):
    chunk = 256
    for r in range(x_ref.shape[0] // chunk):
        sl = slice(r * chunk, (r + 1) * chunk)
        mix = _dot(hg_ref[sl, :], w_ref[0:HG_WIDTH, :]) + _dot(att_ref[sl, :], w_ref[HG_WIDTH:, :])
        x = x_ref[sl, :] + mod_ref[0, 2:3, :] * mix
        xo_ref[sl, :] = x
        h_ref[sl, :] = _norm_modulate(x, g_ref[...], mod_ref[0, 3:4, :], mod_ref[0, 4:5, :]).astype(BF16)


def _out_projection(o_hg, o_att, w_bf16, x, mod, gain, *, mod_base, rows_per_mod):
    m = x.shape[0]
    tm = 512
    row = lambda i: (i, 0)
    return pl.pallas_call(
        _outproj_kernel,
        out_shape=(jax.ShapeDtypeStruct((m, D_MODEL), F32), jax.ShapeDtypeStruct((m, D_MODEL), BF16)),
        grid=(m // tm,),
        in_specs=[pl.BlockSpec((tm, HG_WIDTH), row),
                  pl.BlockSpec((tm, ATT_WIDTH), row),
                  pl.BlockSpec((HG_WIDTH + ATT_WIDTH, D_MODEL), lambda i: (0, 0)),
                  pl.BlockSpec((tm, D_MODEL), row),
                  pl.BlockSpec((1, N_MOD, D_MODEL), lambda i: (mod_base + (i * tm) // rows_per_mod, 0, 0)),
                  pl.BlockSpec((1, D_MODEL), lambda i: (0, 0))],
        out_specs=(pl.BlockSpec((tm, D_MODEL), row), pl.BlockSpec((tm, D_MODEL), row)),
        compiler_params=_cparams("parallel"),
        name="out_projection",
    )(o_hg, o_att, w_bf16, x, mod, gain)


def _prefix_count(x):
    n = x.shape[1]
    i = lax.broadcasted_iota(jnp.int32, (SEL_BLOCK, SEL_BLOCK), 0)
    j = lax.broadcasted_iota(jnp.int32, (SEL_BLOCK, SEL_BLOCK), 1)
    upper = jnp.where(i < j, 1.0, 0.0).astype(BF16)
    off = jnp.zeros((x.shape[0], 1), F32)
    outs = []
    for blk in range(n // SEL_BLOCK):
        xb = x[:, blk * SEL_BLOCK:(blk + 1) * SEL_BLOCK]
        outs.append(_dot(xb.astype(BF16), upper) + off)
        off = off + jnp.sum(xb, axis=-1, keepdims=True)
    return outs[0] if len(outs) == 1 else jnp.concatenate(outs, axis=1)


def _route_select(h_ref, w_ref, cap):
    logits = _dot_nt(w_ref[...], h_ref[...])
    ex = jnp.exp(logits - jnp.max(logits, axis=0, keepdims=True))
    aff = ex / jnp.sum(ex, axis=0, keepdims=True)
    bits = pltpu.bitcast(aff, jnp.int32)

    thr = jnp.zeros((N_EXPERTS, 1), jnp.int32)
    shift = 31
    while shift > 0:
        width = min(4, shift)
        shift -= width
        digit = jnp.zeros((N_EXPERTS, 1), jnp.int32)
        for j in range(1, 2 ** width):
            cnt = jnp.sum(jnp.where(bits >= (thr | (j << shift)), 1.0, 0.0), axis=-1, keepdims=True)
            digit = digit + jnp.where(cnt >= cap, 1, 0)
        thr = thr | (digit << shift)
    above = jnp.where(bits > thr, 1.0, 0.0)
    tied = jnp.where(bits == thr, 1.0, 0.0)
    room = cap - jnp.sum(above, axis=-1, keepdims=True)
    sel = above + tied * jnp.where(_prefix_count(tied) < room, 1.0, 0.0)
    return jnp.where(sel > 0.0, _prefix_count(sel), -1.0), aff


def _one_hot_gather(slot_rows, aff_rows, h_ref, rows, cap):
    n_tok = h_ref.shape[0]
    c = (lax.broadcasted_iota(jnp.int32, (rows, n_tok), 0) & (cap - 1)).astype(F32)
    hit = c == slot_rows
    x = _dot(jnp.where(hit, 1.0, 0.0).astype(BF16), h_ref[...]).astype(BF16)
    return x, jnp.sum(jnp.where(hit, aff_rows, 0.0), axis=-1, keepdims=True)


def _route_request_kernel(h_ref, w_ref, x_ref, g_ref, slot_ref, *, cap):
    n_tok = h_ref.shape[0]
    slot, aff = _route_select(h_ref, w_ref, cap)
    slot_ref[0] = slot
    per_row = lambda a: jnp.concatenate(
        [jnp.broadcast_to(a[e:e + 1, :], (cap, n_tok)) for e in range(N_EXPERTS)], axis=0)
    x, g = _one_hot_gather(per_row(slot), per_row(aff), h_ref, N_EXPERTS * cap, cap)
    for e in range(N_EXPERTS):
        x_ref[e] = x[e * cap:(e + 1) * cap, :]
        g_ref[e] = g[e * cap:(e + 1) * cap, :]


def _route_only_kernel(h_ref, w_ref, slot_ref, aff_ref, bounds_ref, *, cap):
    n_tok = h_ref.shape[0]
    slot, aff = _route_select(h_ref, w_ref, cap)
    slot_ref[0] = slot
    aff_ref[0] = aff
    n = lax.broadcasted_iota(jnp.int32, (n_tok, 128), 0)
    t = lax.broadcasted_iota(jnp.int32, (n_tok, 128), 1)
    before = jnp.where(n < t * TOK_BLOCK, 1.0, 0.0).astype(BF16)
    chosen = jnp.where(slot >= 0.0, 1.0, 0.0).astype(BF16)
    bounds_ref[0] = _dot(chosen, before).astype(jnp.int32)


def _gather_window_kernel(bounds_ref, slot_ref, aff_ref, h_ref, xh_ref, gh_ref, x_ref, g_ref, xo_ref, go_ref, *,
                          n_batch, group):
    b = pl.program_id(0)
    eg = pl.program_id(1)
    n_tok = h_ref.shape[0]
    nt = n_tok // TOK_BLOCK
    cap = x_ref.shape[1]
    region = TOK_BLOCK // group

    @pl.when(b >= n_batch)
    def _():
        x_ref[...] = xh_ref[...]
        g_ref[...] = gh_ref[...]

    @pl.when(b < n_batch)
    def _():
        x_ref[...] = jnp.zeros(x_ref.shape, BF16)
        g_ref[...] = jnp.zeros(g_ref.shape, F32)
        local_i = lax.broadcasted_iota(jnp.int32, (region, TOK_BLOCK), 0).astype(F32)
        over_i = lax.broadcasted_iota(jnp.int32, (TOK_BLOCK, TOK_BLOCK), 0)

        def token_block(t, carry):
            toks = pl.ds(pl.multiple_of(t * TOK_BLOCK, TOK_BLOCK), TOK_BLOCK)
            hits, gates, firsts, n_overs, slots, affs = [], [], [], [], [], []
            for k in range(group):
                e = eg * group + k
                start, length = _window(bounds_ref, (b * N_EXPERTS + e) * (nt + 1), t)
                first = jnp.minimum(start, cap - region)
                s = slot_ref[0, pl.ds(e, 1), toks]
                a = aff_ref[0, pl.ds(e, 1), toks]
                hit = local_i == jnp.where(s >= 0.0, s - first.astype(F32), -1.0)
                hits.append(jnp.where(hit, 1.0, 0.0).astype(BF16))
                gates.append(jnp.sum(jnp.where(hit, a, 0.0), axis=-1, keepdims=True))
                firsts.append(first)
                n_overs.append(start + length - first - region)
                slots.append(s)
                affs.append(a)
            xc = _dot(jnp.concatenate(hits, axis=0), h_ref[toks, :]).astype(BF16)
            for k in range(group):
                dst = pl.ds(pl.multiple_of(firsts[k], PIECE), region)
                x_ref[k, dst, :] = x_ref[k, dst, :] + xc[k * region:(k + 1) * region, :]
                g_ref[k, dst, :] = g_ref[k, dst, :] + gates[k]
            for k in range(group):
                @pl.when(n_overs[k] > 0)
                def _(k=k):
                    base = firsts[k] + region
                    hit = (over_i + base).astype(F32) == slots[k]
                    xo_ref[...] = _dot(jnp.where(hit, 1.0, 0.0).astype(BF16), h_ref[toks, :]).astype(BF16)
                    go_ref[...] = jnp.sum(jnp.where(hit, affs[k], 0.0), axis=-1, keepdims=True)

                    def place(i, carry):
                        src = pl.ds(pl.multiple_of(i * PIECE, PIECE), PIECE)
                        dst = pl.ds(pl.multiple_of(base + i * PIECE, PIECE), PIECE)
                        x_ref[k, dst, :] = x_ref[k, dst, :] + xo_ref[src, :]
                        g_ref[k, dst, :] = g_ref[k, dst, :] + go_ref[src, :]
                        return carry

                    lax.fori_loop(0, n_overs[k] // PIECE, place, 0)
            return carry

        lax.fori_loop(0, nt, token_block, 0)


def _route_gather_requests(h, w_router_t, *, n_batch, n_tok):
    cap = CAPACITY_FACTOR * n_tok // N_EXPERTS
    out_block = lambda w: pl.BlockSpec((N_EXPERTS, cap, w), lambda b: (0, b, 0))
    return pl.pallas_call(
        functools.partial(_route_request_kernel, cap=cap),
        out_shape=(jax.ShapeDtypeStruct((N_EXPERTS, n_batch * cap, D_MODEL), BF16),
                   jax.ShapeDtypeStruct((N_EXPERTS, n_batch * cap, 1), F32),
                   jax.ShapeDtypeStruct((n_batch, N_EXPERTS, n_tok), F32)),
        grid=(n_batch,),
        in_specs=[pl.BlockSpec((n_tok, D_MODEL), lambda b: (b, 0)),
                  pl.BlockSpec((N_EXPERTS, D_MODEL), lambda b: (0, 0))],
        out_specs=(out_block(D_MODEL), out_block(1), pl.BlockSpec((1, N_EXPERTS, n_tok), lambda b: (b, 0, 0))),
        compiler_params=_cparams("parallel"),
        name="route_gather_requests",
    )(h, w_router_t)


def _route_only(h, w_router_t, *, n_batch, n_tok):
    cap = CAPACITY_FACTOR * n_tok // N_EXPERTS
    nt = n_tok // TOK_BLOCK
    expert_major = pl.BlockSpec((1, N_EXPERTS, n_tok), lambda b: (b, 0, 0))
    slot, aff, bounds = pl.pallas_call(
        functools.partial(_route_only_kernel, cap=cap),
        out_shape=(jax.ShapeDtypeStruct((n_batch, N_EXPERTS, n_tok), F32),
                   jax.ShapeDtypeStruct((n_batch, N_EXPERTS, n_tok), F32),
                   jax.ShapeDtypeStruct((n_batch, N_EXPERTS, 128), jnp.int32)),
        grid=(n_batch,),
        in_specs=[pl.BlockSpec((n_tok, D_MODEL), lambda b: (b, 0)),
                  pl.BlockSpec((N_EXPERTS, D_MODEL), lambda b: (0, 0))],
        out_specs=(expert_major, expert_major, pl.BlockSpec((1, N_EXPERTS, 128), lambda b: (b, 0, 0))),
        compiler_params=_cparams("parallel"),
        name="route_select",
    )(h, w_router_t)
    return slot, aff, bounds[:, :, :nt + 1].reshape(-1)


def _gather_windows(bounds, slot, aff, h, x_head, g_head, *, n_batch, n_tok):
    cap = CAPACITY_FACTOR * n_tok // N_EXPERTS
    group = 4
    head_rows = x_head.shape[1]
    assert head_rows % cap == 0 and N_EXPERTS % group == 0
    n_head = head_rows // cap
    rows_total = head_rows + n_batch * cap
    last = n_batch - 1
    request = lambda b: jnp.minimum(b, last)
    out_row = lambda b: jnp.where(b < n_batch, n_head + b, b - n_batch)
    head_row = lambda b: jnp.maximum(b - n_batch, 0)
    out_block = lambda w: pl.BlockSpec((group, cap, w), lambda b, g, s: (g, out_row(b), 0))
    head_block = lambda w: pl.BlockSpec((group, cap, w), lambda b, g, s: (g, head_row(b), 0))
    expert_major = pl.BlockSpec((1, N_EXPERTS, n_tok), lambda b, g, s: (request(b), 0, 0))
    return pl.pallas_call(
        functools.partial(_gather_window_kernel, n_batch=n_batch, group=group),
        out_shape=(jax.ShapeDtypeStruct((N_EXPERTS, rows_total, D_MODEL), BF16),
                   jax.ShapeDtypeStruct((N_EXPERTS, rows_total, 1), F32)),
        grid_spec=pltpu.PrefetchScalarGridSpec(
            num_scalar_prefetch=1,
            grid=(n_batch + n_head, N_EXPERTS // group),
            in_specs=[expert_major, expert_major,
                      pl.BlockSpec((n_tok, D_MODEL), lambda b, g, s: (request(b), 0)),
                      head_block(D_MODEL), head_block(1)],
            out_specs=(out_block(D_MODEL), out_block(1)),
            scratch_shapes=[pltpu.VMEM((TOK_BLOCK, D_MODEL), BF16), pltpu.VMEM((TOK_BLOCK, 1), F32)]),
        compiler_params=_cparams("arbitrary", "arbitrary"),
        name="gather_windows",
    )(bounds, slot, aff, h, x_head, g_head)


def _moe_kernel(x_ref, g_ref, wg_ref, wu_ref, wd_ref, y_ref, hid_ref, *, n_ff, rows):
    s = pl.program_id(2)
    tf = wg_ref.shape[2]

    @pl.when(s < n_ff)
    def _():
        wg = wg_ref[0].astype(BF16)
        wu = wu_ref[0].astype(BF16)
        cols = pl.ds(pl.multiple_of(s * tf, tf), tf)
        for r in range(x_ref.shape[1] // rows):
            sl = slice(r * rows, (r + 1) * rows)
            x = x_ref[0, sl, :]
            hid_ref[sl, cols] = (_silu(_dot(x, wg)) * _dot(x, wu)).astype(BF16)

    @pl.when(s >= n_ff)
    def _():
        y = _dot(hid_ref[...], wd_ref[0].astype(BF16))
        y_ref[0] = (y * g_ref[0]).astype(BF16)


def _experts(x, gate, w_gate, w_up, w_down):
    n_rows = x.shape[1]
    tr, tf, tn = n_rows // 2, 512, 256
    n_ff, n_out = EXPERT_FF // tf, D_MODEL // tn
    up_tile = lambda e, r, s: (e, 0, jnp.minimum(s, n_ff - 1))
    out_tile = lambda s: jnp.maximum(s - n_ff, 0)
    return pl.pallas_call(
        functools.partial(_moe_kernel, n_ff=n_ff, rows=256),
        out_shape=jax.ShapeDtypeStruct((N_EXPERTS, n_rows, D_MODEL), BF16),
        grid=(N_EXPERTS, n_rows // tr, n_ff + n_out),
        in_specs=[pl.BlockSpec((1, tr, D_MODEL), lambda e, r, s: (e, r, 0), pipeline_mode=pl.Buffered(1)),
                  pl.BlockSpec((1, tr, 1), lambda e, r, s: (e, r, 0)),
                  pl.BlockSpec((1, D_MODEL, tf), up_tile),
                  pl.BlockSpec((1, D_MODEL, tf), up_tile),
                  pl.BlockSpec((1, EXPERT_FF, tn), lambda e, r, s: (e, 0, out_tile(s)))],
        out_specs=pl.BlockSpec((1, tr, tn), lambda e, r, s: (e, r, out_tile(s))),
        scratch_shapes=[pltpu.VMEM((tr, EXPERT_FF), BF16)],
        compiler_params=_cparams("parallel", "parallel", "arbitrary"),
        name="expert_swiglu",
    )(x, gate, w_gate, w_up, w_down)


def _window(bounds_ref, base, t):
    p0 = bounds_ref[base + t]
    p1 = bounds_ref[base + t + 1]
    start = (p0 // PIECE) * PIECE
    return start, jnp.where(p1 > p0, ((p1 - start + PIECE - 1) // PIECE) * PIECE, 0)


def _combine_kernel(bounds_ref, y_ref, slot_ref, x_ref, mod_ref, g_ref, o_ref, ybuf_ref, obuf_ref, acc_ref, *,
                    nt, region):
    b = pl.program_id(0)
    t = pl.program_id(1)
    tt = x_ref.shape[0]
    cap = y_ref.shape[1]
    local_i = lax.broadcasted_iota(jnp.int32, (region, tt), 0).astype(F32)
    hits = []
    for e in range(N_EXPERTS):
        start, _ = _window(bounds_ref, (b * N_EXPERTS + e) * (nt + 1), t)
        first = jnp.minimum(start, cap - region)
        ybuf_ref[e * region:(e + 1) * region, :] = y_ref[e, pl.ds(pl.multiple_of(first, PIECE), region), :]
        s = slot_ref[0, e:e + 1, :]
        hit = local_i == jnp.where(s >= 0.0, s - first.astype(F32), -1.0)
        hits.append(jnp.where(hit, 1.0, 0.0).astype(BF16))
    acc_ref[...] = _dot_tn(jnp.concatenate(hits, axis=0), ybuf_ref[...])
    over_i = lax.broadcasted_iota(jnp.int32, (TOK_BLOCK, tt), 0)

    def overflow(e, carry):
        start, length = _window(bounds_ref, (b * N_EXPERTS + e) * (nt + 1), t)
        base = jnp.minimum(start, cap - region) + region
        n_over = start + length - base

        @pl.when(n_over > 0)
        def _():
            def copy(i, carry):
                obuf_ref[pl.ds(pl.multiple_of(i * PIECE, PIECE), PIECE), :] = (
                    y_ref[e, pl.ds(pl.multiple_of(base + i * PIECE, PIECE), PIECE), :])
                return carry

            def clear(i, carry):
                obuf_ref[pl.ds(pl.multiple_of(n_over + i * PIECE, PIECE), PIECE), :] = jnp.zeros((PIECE, D_MODEL), BF16)
                return carry

            lax.fori_loop(0, n_over // PIECE, copy, 0)
            lax.fori_loop(0, (TOK_BLOCK - n_over) // PIECE, clear, 0)
            hit = (over_i + base).astype(F32) == slot_ref[0, pl.ds(e, 1), :]
            acc_ref[...] += _dot_tn(jnp.where(hit, 1.0, 0.0).astype(BF16), obuf_ref[...])

        return carry

    lax.fori_loop(0, N_EXPERTS, overflow, 0)
    x = x_ref[...] + mod_ref[0, 5:6, :] * acc_ref[...]
    var = jnp.mean(x * x, axis=-1, keepdims=True)
    o_ref[...] = x * lax.rsqrt(var + NORM_EPS) * g_ref[...]


def _combine(bounds, y, slot, x_mid, mod, final_g, *, n_batch, n_tok, row_block_off, mod_base, mod_per_batch):
    cap = CAPACITY_FACTOR * n_tok // N_EXPERTS
    tt = TOK_BLOCK
    nt = n_tok // tt
    region = min(64, cap)
    assert cap - region <= TOK_BLOCK
    return pl.pallas_call(
        functools.partial(_combine_kernel, nt=nt, region=region),
        out_shape=jax.ShapeDtypeStruct((n_batch * n_tok, D_MODEL), F32),
        grid_spec=pltpu.PrefetchScalarGridSpec(
            num_scalar_prefetch=1,
            grid=(n_batch, nt),
            in_specs=[pl.BlockSpec((N_EXPERTS, cap, D_MODEL), lambda b, t, s: (0, row_block_off + b, 0)),
                      pl.BlockSpec((1, N_EXPERTS, tt), lambda b, t, s: (b, 0, t)),
                      pl.BlockSpec((tt, D_MODEL), lambda b, t, s: (b * nt + t, 0)),
                      pl.BlockSpec((1, N_MOD, D_MODEL), lambda b, t, s: (mod_base + b * mod_per_batch, 0, 0)),
                      pl.BlockSpec((1, D_MODEL), lambda b, t, s: (0, 0))],
            out_specs=pl.BlockSpec((tt, D_MODEL), lambda b, t, s: (b * nt + t, 0)),
            scratch_shapes=[pltpu.VMEM((N_EXPERTS * region, D_MODEL), BF16),
                            pltpu.VMEM((TOK_BLOCK, D_MODEL), BF16), pltpu.VMEM((tt, D_MODEL), F32)]),
        compiler_params=_cparams("arbitrary", "arbitrary"),
        name="combine_final_norm",
    )(bounds, y, slot, x_mid, mod, final_g)


def kernel(x_prompt, x_sample, cache_k, cache_v, state_hgrn, c, c_ctx, w_ada, b_ada, norm1_g, w_in, hg_lb,
           hg_norm_g, attn_sink, w_out, norm2_g, w_router, w_gate, w_up, w_down, final_norm_g):
    n_p, t_p, _ = x_prompt.shape
    n_s, t_s, _ = x_sample.shape
    assert w_ada.shape[0] == 1 and 1 + n_s <= COND_ROWS
    layer = 0

    cond = jnp.zeros((COND_ROWS, D_MODEL), F32).at[0].set(c_ctx).at[1:1 + n_s].set(c)
    mod = _ada_modulation(cond, w_ada[layer], b_ada[layer]).reshape(COND_ROWS, N_MOD, D_MODEL)
    lb = jnp.cumsum(jax.nn.softmax(hg_lb.astype(F32), axis=0), axis=0)[layer]
    w_in_l = w_in[layer].astype(BF16)
    w_out_b = w_out[layer].astype(BF16)
    w_router_t = w_router[layer].T.astype(BF16)
    norm1 = norm1_g[layer].reshape(1, D_MODEL)
    norm2 = norm2_g[layer].reshape(1, D_MODEL)
    hg_gain = hg_norm_g[layer].reshape(1, HG_WIDTH)
    final_g = final_norm_g.reshape(1, D_MODEL)
    sink = attn_sink[layer]
    cos, sin_signed = _rope_tables(t_s)

    xp = x_prompt.reshape(n_p * t_p, D_MODEL)
    xs = x_sample.reshape(n_s * t_s, D_MODEL)
    groups = dict(p=dict(mod_base=0, rows_per_mod=n_p * t_p), s=dict(mod_base=1, rows_per_mod=t_s))

    proj_p = _in_projection(xp, mod, norm1, w_in_l, **groups["p"])
    proj_s = _in_projection(xs, mod, norm1, w_in_l, **groups["s"])

    ohg_p, new_state = _hgrn(proj_p, lb, hg_gain, None, n_batch=n_p, n_tok=t_p)
    ohg_s, _ = _hgrn(proj_s, lb, hg_gain, state_hgrn[:, layer:layer + 1].astype(F32), n_batch=n_s, n_tok=t_s)

    oatt_p = _context_attention(proj_p, sink, n_batch=n_p, n_tok=t_p)
    n_ctx = cache_k.shape[2]
    oatt_s = _window_attention(proj_s, cache_k[:, layer].reshape(n_s, n_ctx, KV_WIDTH),
                               cache_v[:, layer].reshape(n_s, n_ctx, KV_WIDTH), sink, cos, sin_signed,
                               n_batch=n_s, n_tok=t_s)

    xmid_p, h2_p = _out_projection(ohg_p, oatt_p, w_out_b, xp, mod, norm2, **groups["p"])
    xmid_s, h2_s = _out_projection(ohg_s, oatt_s, w_out_b, xs, mod, norm2, **groups["s"])

    cap_p = CAPACITY_FACTOR * t_p // N_EXPERTS
    cap_s = CAPACITY_FACTOR * t_s // N_EXPERTS
    off_s = n_p * cap_p // cap_s
    xg_p, gate_p, slot_p = _route_gather_requests(h2_p, w_router_t, n_batch=n_p, n_tok=t_p)
    slot_s, aff_s, bounds_s = _route_only(h2_s, w_router_t, n_batch=n_s, n_tok=t_s)
    xg, gate = _gather_windows(bounds_s, slot_s, aff_s, h2_s, xg_p, gate_p, n_batch=n_s, n_tok=t_s)
    bounds_p = jnp.tile(jnp.array([0, cap_p], jnp.int32), n_p * N_EXPERTS)

    y = _experts(xg, gate, w_gate[layer], w_up[layer], w_down[layer])

    y_prompt = _combine(bounds_p, y, slot_p, xmid_p, mod, final_g, n_batch=n_p, n_tok=t_p, row_block_off=0,
                        mod_base=0, mod_per_batch=0)
    y_sample = _combine(bounds_s, y, slot_s, xmid_s, mod, final_g, n_batch=n_s, n_tok=t_s, row_block_off=off_s,
                        mod_base=1, mod_per_batch=1)

    k_col = 5 * HG_WIDTH + ATT_WIDTH
    new_k = proj_p[:, k_col:k_col + KV_WIDTH].reshape(n_p, 1, t_p, ATT_KV_HEADS, HEAD_DIM)
    new_v = proj_p[:, k_col + KV_WIDTH:k_col + 2 * KV_WIDTH].reshape(n_p, 1, t_p, ATT_KV_HEADS, HEAD_DIM)
    return (y_prompt.reshape(n_p, t_p, D_MODEL), y_sample.reshape(n_s, t_s, D_MODEL), new_k, new_v, new_state)
```

```python
import functools

import jax
import jax.numpy as jnp
import numpy as np
from jax import lax
from jax.experimental import pallas as pl
from jax.experimental.pallas import tpu as pltpu

F32 = jnp.float32
BF16 = jnp.bfloat16

D_MODEL = 2048
HG_WIDTH = 1024
HG_HEADS = 8
HEAD_DIM = 128
ATT_HEADS = 8
ATT_KV_HEADS = 2
ATT_GROUP = ATT_HEADS // ATT_KV_HEADS
KV_WIDTH = ATT_KV_HEADS * HEAD_DIM
ATT_WIDTH = ATT_HEADS * HEAD_DIM
ATT_BLOCK = 128
GRID_W = 64
ROPE_BASE = 10000.0
ROPE_FREQS = HEAD_DIM // 4
N_EXPERTS = 16
CAPACITY_FACTOR = 2
EXPERT_FF = 5632
NORM_EPS = 1e-6
IN_WIDTH = 5 * HG_WIDTH + ATT_WIDTH + 2 * KV_WIDTH
N_MOD = 6
COND_ROWS = 16

HG_CHUNK = 128
HG_DIAG = 8
SEL_BLOCK = 256
TOK_BLOCK = 256
PIECE = 16

V7X_VMEM_BYTES = 64 * 1024 * 1024
VMEM_LIMIT = V7X_VMEM_BYTES - 8 * 1024 * 1024


def _cparams(*sem):
    return pltpu.CompilerParams(dimension_semantics=sem, vmem_limit_bytes=VMEM_LIMIT)


def _sigmoid(x):
    return 1.0 / (1.0 + jnp.exp(-x))


def _silu(x):
    return x * _sigmoid(x)


def _dot(a, b):
    return jnp.dot(a, b, preferred_element_type=F32)


def _dot_nt(a, b):
    return lax.dot_general(a, b, (((1,), (1,)), ((), ())), preferred_element_type=F32)


def _dot_tn(a, b):
    return lax.dot_general(a, b, (((0,), (0,)), ((), ())), preferred_element_type=F32)


def _ada_kernel(c_ref, w_ref, b_ref, o_ref):
    s = _silu(c_ref[...]).astype(BF16)
    o_ref[...] = _dot(s, w_ref[...].astype(BF16)) + b_ref[...]


def _ada_modulation(cond, w_ada, b_ada):
    tn = 1024
    n = w_ada.shape[1]
    return pl.pallas_call(
        _ada_kernel,
        out_shape=jax.ShapeDtypeStruct((COND_ROWS, n), F32),
        grid=(n // tn,),
        in_specs=[pl.BlockSpec((COND_ROWS, D_MODEL), lambda j: (0, 0)),
                  pl.BlockSpec((D_MODEL, tn), lambda j: (0, j)),
                  pl.BlockSpec((1, tn), lambda j: (0, j))],
        out_specs=pl.BlockSpec((COND_ROWS, tn), lambda j: (0, j)),
        compiler_params=_cparams("arbitrary"),
        name="ada_modulation",
    )(cond, w_ada, b_ada.reshape(1, n))


def _norm_modulate(x, gain, shift, scale):
    var = jnp.mean(x * x, axis=-1, keepdims=True)
    return (x * lax.rsqrt(var + NORM_EPS) * gain) * (1.0 + scale) + shift


def _inproj_kernel(x_ref, mod_ref, g_ref, w_ref, o_ref, h_ref, *, rows):
    w = w_ref[...]
    chunks = [slice(r * rows, (r + 1) * rows) for r in range(x_ref.shape[0] // rows)]

    @pl.when(pl.program_id(1) == 0)
    def _():
        shift = mod_ref[0, 0:1, :]
        scale = mod_ref[0, 1:2, :]
        gain = g_ref[...]
        for sl in chunks:
            h = _norm_modulate(x_ref[sl, :], gain, shift, scale).astype(BF16)
            h_ref[sl, :] = h
            o_ref[sl, :] = _dot(h, w)

    @pl.when(pl.program_id(1) > 0)
    def _():
        for sl in chunks:
            o_ref[sl, :] = _dot(h_ref[sl, :], w)


def _in_projection(x, mod, gain, w, *, mod_base, rows_per_mod):
    m = x.shape[0]
    tm, tn = 1024, 512
    return pl.pallas_call(
        functools.partial(_inproj_kernel, rows=256),
        out_shape=jax.ShapeDtypeStruct((m, IN_WIDTH), F32),
        grid=(m // tm, IN_WIDTH // tn),
        in_specs=[pl.BlockSpec((tm, D_MODEL), lambda i, j: (i, 0)),
                  pl.BlockSpec((1, N_MOD, D_MODEL), lambda i, j: (mod_base + (i * tm) // rows_per_mod, 0, 0)),
                  pl.BlockSpec((1, D_MODEL), lambda i, j: (0, 0)),
                  pl.BlockSpec((D_MODEL, tn), lambda i, j: (0, j))],
        out_specs=pl.BlockSpec((tm, tn), lambda i, j: (i, j)),
        scratch_shapes=[pltpu.VMEM((tm, D_MODEL), BF16)],
        compiler_params=_cparams("parallel", "arbitrary"),
        name="in_projection",
    )(x, mod, gain, w)


def _hgrn_codes(reverse):
    L = HG_CHUNK
    t = lax.broadcasted_iota(jnp.int32, (L, L), 0)
    s = lax.broadcasted_iota(jnp.int32, (L, L), 1)
    code = jnp.where(t == s, 1, 0)
    h = L // 2
    while h >= 1:
        same = (t & ~(2 * h - 1)) == (s & ~(2 * h - 1))
        t_hi = (t & h) != 0
        s_hi = (s & h) != 0
        pair = (s_hi & ~t_hi) if reverse else (t_hi & ~s_hi)
        code = jnp.where(same & pair, h * 16, code)
        h //= 2
    return code


def _cumsum_rows(tri_bf16, g):
    g1 = g.astype(BF16)
    g2 = (g - g1.astype(F32)).astype(BF16)
    s = _dot(tri_bf16, jnp.concatenate([g1, g2], axis=1))
    return s[:, :HEAD_DIM] + s[:, HEAD_DIM:]


def _hgrn_intra(q, k, f, b, v_bf, code, reverse):
    L = HG_CHUNK
    G = HG_DIAG
    q_bf = q.astype(BF16)
    k_bf = k.astype(BF16)

    def level(h, ref, att):
        neg_abs = pltpu.bitcast(pltpu.bitcast(b - ref, jnp.int32) | jnp.int32(-2 ** 31), F32)
        e = jnp.exp2(neg_abs.astype(BF16))
        return jnp.where(code == h * 16, _dot_nt(q_bf * e, k_bf * e), att)

    att = jnp.where(code == 1, _dot_nt(q_bf, k_bf), 0.0)
    h = L // 2
    while h >= G:
        parts = []
        for p in range(L // (2 * h)):
            m = p * 2 * h + (h if reverse else h - 1)
            parts.append(jnp.broadcast_to(b[m:m + 1, :], (2 * h, HEAD_DIM)))
        att = level(h, parts[0] if len(parts) == 1 else jnp.concatenate(parts, axis=0), att)
        h //= 2

    b3 = b.reshape(L // G, G, HEAD_DIM)

    def group_row(r):
        return jnp.broadcast_to(b3[:, r:r + 1, :], (L // G, G, HEAD_DIM)).reshape(L, HEAD_DIM)

    row = lax.broadcasted_iota(jnp.int32, (L, HEAD_DIM), 0)
    att = level(4, group_row(4 if reverse else 3), att)
    lo, hi = (2, 6) if reverse else (1, 5)
    att = level(2, jnp.where((row & 4) == 0, group_row(lo), group_row(hi)), att)
    att = jnp.where(code == 16, _dot_nt(q_bf * f.astype(BF16), k_bf), att)
    return _dot(att.astype(BF16), v_bf)


def _hgrn_prepare(q, f, v, code, tri, reverse):
    L = HG_CHUNK
    k = 1.0 - f
    b = _cumsum_rows(tri, jnp.log2(f))
    b_tot = b[0:1, :] if reverse else b[L - 1:L, :]
    v_bf = v.astype(BF16)
    o_intra = _hgrn_intra(q, k, f, b, v_bf, code, reverse)
    q_in = (q * jnp.exp2(b)).astype(BF16)
    k_out = (k * jnp.exp2(b_tot - b)).astype(BF16)
    return o_intra, q_in, k_out, v_bf, jnp.exp2(b_tot)


def _hgrn_advance(st, prepared):
    o_intra, q_in, k_out, v_bf, decay = prepared
    return o_intra + _dot_nt(q_in, st.astype(BF16)), st * decay + _dot_tn(v_bf, k_out)


def _hgrn_kernel(*refs, n_tok, has_state):
    if has_state:
        (q_ref, ff_ref, fb_ref, v_ref, gate_ref, lb_ref, ng_ref, s0_ref,
         o_ref, sout_ref, of_ref, ob_ref, code_ref) = refs
    else:
        (q_ref, ff_ref, fb_ref, v_ref, gate_ref, lb_ref, ng_ref,
         o_ref, sout_ref, of_ref, ob_ref, code_ref) = refs
    L = HG_CHUNK
    nc = n_tok // L
    lb_f = lb_ref[0:1, :]
    lb_b = lb_ref[1:2, :]

    @pl.when((pl.program_id(0) == 0) & (pl.program_id(1) == 0))
    def _():
        code_ref[0] = _hgrn_codes(False)
        code_ref[1] = _hgrn_codes(True)

    code_f = code_ref[0]
    code_b = code_ref[1]
    ti = lax.broadcasted_iota(jnp.int32, (L, L), 0)
    si = lax.broadcasted_iota(jnp.int32, (L, L), 1)
    tri_f = jnp.where(si <= ti, 1.0, 0.0).astype(BF16)
    tri_b = jnp.where(si >= ti, 1.0, 0.0).astype(BF16)

    def prepare(row, f_ref, lb, code, tri, reverse):
        sl = pl.ds(row, L)
        q = _silu(q_ref[sl, :])
        f = lb + (1.0 - lb) * _sigmoid(f_ref[sl, :])
        return sl, _hgrn_prepare(q, f, v_ref[sl, :], code, tri, reverse)

    gain = ng_ref[...]

    def finish(sl, o):
        var = jnp.mean(o * o, axis=-1, keepdims=True)
        o_ref[sl, :] = ((o * lax.rsqrt(var + NORM_EPS) * gain) * _silu(gate_ref[sl, :])).astype(BF16)

    def make_body(other_direction):
        def body(c, states):
            st_f, st_b = states
            fwd, bwd = [], []
            for u in range(unroll):
                cu = c * unroll + u
                fwd.append(prepare(pl.multiple_of(cu * L, L), ff_ref, lb_f, code_f, tri_f, False))
                bwd.append(prepare(pl.multiple_of((nc - 1 - cu) * L, L), fb_ref, lb_b, code_b, tri_b, True))
            out_f, out_b = [], []
            for (sl_f, prep_f), (sl_b, prep_b) in zip(fwd, bwd):
                o_f, st_f = _hgrn_advance(st_f, prep_f)
                o_b, st_b = _hgrn_advance(st_b, prep_b)
                out_f.append((sl_f, o_f))
                out_b.append((sl_b, o_b))
            if other_direction == "later":
                for sl, o in out_f:
                    of_ref[sl, :] = o
                for sl, o in out_b:
                    ob_ref[sl, :] = o
            elif other_direction == "now":
                for u in range(unroll):
                    finish(out_f[u][0], out_f[u][1] + out_b[unroll - 1 - u][1])
            else:
                for sl, o in out_f:
                    finish(sl, o + ob_ref[sl, :])
                for sl, o in out_b:
                    finish(sl, of_ref[sl, :] + o)
            return st_f, st_b
        return body

    unroll = 8 if nc % 16 == 0 else 2
    assert nc % unroll == 0
    n_iter = nc // unroll
    half = n_iter // 2
    if has_state:
        states = (s0_ref[0, 0, 0, 0].T, s0_ref[0, 0, 1, 0].T)
    else:
        states = (jnp.zeros((HEAD_DIM, HEAD_DIM), F32), jnp.zeros((HEAD_DIM, HEAD_DIM), F32))
    states = lax.fori_loop(0, half, make_body("later"), states)
    if n_iter % 2:
        states = make_body("now")(half, states)
    st_f, st_b = lax.fori_loop(half + n_iter % 2, n_iter, make_body("earlier"), states)
    sout_ref[0, 0, 0, 0] = st_f.T
    sout_ref[0, 0, 1, 0] = st_b.T


def _hgrn(proj, lb, norm_g, state, *, n_batch, n_tok):
    col = lambda k: (lambda b, h: (b, k * HG_HEADS + h))
    tok_spec = lambda k: pl.BlockSpec((n_tok, HEAD_DIM), col(k))
    st_spec = pl.BlockSpec((1, 1, 2, 1, HEAD_DIM, HEAD_DIM), lambda b, h: (b, 0, 0, h, 0, 0))
    has_state = state is not None
    return pl.pallas_call(
        functools.partial(_hgrn_kernel, n_tok=n_tok, has_state=has_state),
        out_shape=(jax.ShapeDtypeStruct((n_batch * n_tok, HG_WIDTH), BF16),
                   jax.ShapeDtypeStruct((n_batch, 1, 2, HG_HEADS, HEAD_DIM, HEAD_DIM), F32)),
        grid=(n_batch, HG_HEADS),
        in_specs=[tok_spec(0), tok_spec(1), tok_spec(2), tok_spec(3), tok_spec(4),
                  pl.BlockSpec((2, HEAD_DIM), lambda b, h: (0, h)),
                  pl.BlockSpec((1, HEAD_DIM), lambda b, h: (0, h))] + ([st_spec] if has_state else []),
        out_specs=(pl.BlockSpec((n_tok, HEAD_DIM), lambda b, h: (b, h)), st_spec),
        scratch_shapes=[pltpu.VMEM((n_tok, HEAD_DIM), F32), pltpu.VMEM((n_tok, HEAD_DIM), F32),
                        pltpu.VMEM((2, HG_CHUNK, HG_CHUNK), jnp.int32)],
        compiler_params=_cparams("arbitrary", "arbitrary"),
        name="hgrn2_scan",
    )(proj, proj, proj, proj, proj, lb, norm_g, *((state,) if has_state else ()))


def _stack_heads(x, kvh):
    return jnp.concatenate(
        [x[:, (kvh * ATT_GROUP + g) * HEAD_DIM:(kvh * ATT_GROUP + g + 1) * HEAD_DIM] for g in range(ATT_GROUP)],
        axis=0)


def _sink_column(sink_ref, kvh, rows):
    return jnp.concatenate(
        [jnp.full((rows, 1), sink_ref[kvh * ATT_GROUP + g], F32) for g in range(ATT_GROUP)],
        axis=0) * (1.0 / SOFTMAX_SCALE)


SOFTMAX_SCALE = HEAD_DIM ** -0.5
EXP2_SCALE = SOFTMAX_SCALE * 1.4426950408889634


def _ones_column(rows):
    lane = lax.broadcasted_iota(jnp.int32, (rows, HEAD_DIM), 1)
    return jnp.where(lane == 0, 1.0, 0.0).astype(BF16)


def _softmax_av(scores, values, sink_col):
    m = sink_col
    for s in scores:
        m = jnp.maximum(m, jnp.max(s, axis=-1, keepdims=True))
    acc = None
    for s, v in zip(scores, values):
        pv = _dot(jnp.exp2(((s - m) * EXP2_SCALE).astype(BF16)), v)
        acc = pv if acc is None else acc + pv
    denom = acc[:, HEAD_DIM:HEAD_DIM + 1] + jnp.exp2((sink_col - m) * EXP2_SCALE)
    return acc[:, :HEAD_DIM] / denom


def _ctx_attn_kernel(sink_ref, q_ref, k_ref, v_ref, o_ref):
    rows = q_ref.shape[0]
    q_all = q_ref[...]
    ones = _ones_column(k_ref.shape[0])
    for kvh in range(ATT_KV_HEADS):
        q = _stack_heads(q_all, kvh).astype(BF16)
        k = k_ref[:, kvh * HEAD_DIM:(kvh + 1) * HEAD_DIM].astype(BF16)
        v = jnp.concatenate([v_ref[:, kvh * HEAD_DIM:(kvh + 1) * HEAD_DIM].astype(BF16), ones], axis=1)
        o = _softmax_av([_dot_nt(q, k)], [v], _sink_column(sink_ref, kvh, rows))
        for g in range(ATT_GROUP):
            hd = kvh * ATT_GROUP + g
            o_ref[:, hd * HEAD_DIM:(hd + 1) * HEAD_DIM] = o[g * rows:(g + 1) * rows, :].astype(BF16)


def _context_attention(proj, sink, *, n_batch, n_tok):
    q_col = 5 * HG_WIDTH // ATT_WIDTH
    k_col = (5 * HG_WIDTH + ATT_WIDTH) // KV_WIDTH
    return pl.pallas_call(
        _ctx_attn_kernel,
        out_shape=jax.ShapeDtypeStruct((n_batch * n_tok, ATT_WIDTH), BF16),
        grid_spec=pltpu.PrefetchScalarGridSpec(
            num_scalar_prefetch=1,
            grid=(n_batch,),
            in_specs=[pl.BlockSpec((n_tok, ATT_WIDTH), lambda b, s: (b, q_col)),
                      pl.BlockSpec((n_tok, KV_WIDTH), lambda b, s: (b, k_col)),
                      pl.BlockSpec((n_tok, KV_WIDTH), lambda b, s: (b, k_col + 1))],
            out_specs=pl.BlockSpec((n_tok, ATT_WIDTH), lambda b, s: (b, 0))),
        compiler_params=_cparams("parallel"),
        name="context_attention",
    )(sink, proj, proj, proj)


def _rope(x, cos, sin_signed, even_group):
    partner = jnp.where(even_group, pltpu.roll(x, HEAD_DIM - ROPE_FREQS, 1), pltpu.roll(x, ROPE_FREQS, 1))
    return x * cos + partner * sin_signed


def _win_attn_kernel(sink_ref, q_ref, k_ref, v_ref, ck_ref, cv_ref, cos_ref, sin_ref, o_ref,
                     kpad_ref, vpad_ref, *, n_tok):
    blk = ATT_BLOCK
    nb = n_tok // blk
    i = pl.program_id(1)
    lane = lax.broadcasted_iota(jnp.int32, (blk, HEAD_DIM), 1)
    even_group = (lane & ROPE_FREQS) == 0

    @pl.when(i == 0)
    def _():
        kpad_ref[0:blk, :] = jnp.zeros((blk, KV_WIDTH), BF16)
        kpad_ref[blk + n_tok:2 * blk + n_tok, :] = jnp.zeros((blk, KV_WIDTH), BF16)
        vpad_ref[0:blk, :] = jnp.zeros((blk, 2 * KV_WIDTH), BF16)
        vpad_ref[blk + n_tok:2 * blk + n_tok, :] = jnp.zeros((blk, 2 * KV_WIDTH), BF16)
        ones = _ones_column(blk)

        def body(r, carry):
            src = pl.ds(pl.multiple_of(r * blk, blk), blk)
            dst = pl.ds(pl.multiple_of((r + 1) * blk, blk), blk)
            cos = cos_ref[src, :]
            sin = sin_ref[src, :]
            for kvh in range(ATT_KV_HEADS):
                cols = slice(kvh * HEAD_DIM, (kvh + 1) * HEAD_DIM)
                kpad_ref[dst, cols] = _rope(k_ref[src, cols], cos, sin, even_group).astype(BF16)
                vpad_ref[dst, 2 * kvh * HEAD_DIM:(2 * kvh + 1) * HEAD_DIM] = v_ref[src, cols].astype(BF16)
                vpad_ref[dst, (2 * kvh + 1) * HEAD_DIM:(2 * kvh + 2) * HEAD_DIM] = ones
            return carry

        lax.fori_loop(0, nb, body, 0)

    ctx_ones = _ones_column(ck_ref.shape[1])
    r = lax.broadcasted_iota(jnp.int32, (blk, 3 * blk), 0)
    j = lax.broadcasted_iota(jnp.int32, (blk, 3 * blk), 1)
    for u in range(q_ref.shape[0] // blk):
        qb = i * (q_ref.shape[0] // blk) + u
        rows = pl.ds(pl.multiple_of(qb * blk, blk), blk)
        cos = cos_ref[rows, :]
        sin = sin_ref[rows, :]
        band = pl.ds(pl.multiple_of(qb * blk, blk), 3 * blk)
        kpos = j + (qb - 1) * blk
        valid = (j >= r) & (j <= r + 2 * blk) & (kpos >= 0) & (kpos < n_tok)
        valid = jnp.concatenate([valid] * ATT_GROUP, axis=0)
        q_all = q_ref[u * blk:(u + 1) * blk, :]
        for kvh in range(ATT_KV_HEADS):
            cols = slice(kvh * HEAD_DIM, (kvh + 1) * HEAD_DIM)
            q = jnp.concatenate(
                [_rope(q_all[:, (kvh * ATT_GROUP + g) * HEAD_DIM:(kvh * ATT_GROUP + g + 1) * HEAD_DIM],
                       cos, sin, even_group) for g in range(ATT_GROUP)], axis=0).astype(BF16)
            s_ctx = _dot_nt(q, ck_ref[0, :, cols].astype(BF16))
            s_loc = jnp.where(valid, _dot_nt(q, kpad_ref[band, cols]), -jnp.inf)
            v_ctx = jnp.concatenate([cv_ref[0, :, cols].astype(BF16), ctx_ones], axis=1)
            v_loc = vpad_ref[band, 2 * kvh * HEAD_DIM:(2 * kvh + 2) * HEAD_DIM]
            o = _softmax_av([s_ctx, s_loc], [v_ctx, v_loc], _sink_column(sink_ref, kvh, blk))
            for g in range(ATT_GROUP):
                hd = kvh * ATT_GROUP + g
                o_ref[u * blk:(u + 1) * blk, hd * HEAD_DIM:(hd + 1) * HEAD_DIM] = (
                    o[g * blk:(g + 1) * blk, :].astype(BF16))


def _window_attention(proj, cache_k, cache_v, sink, cos, sin_signed, *, n_batch, n_tok):
    tq = ATT_BLOCK
    steps = n_tok // tq
    n_ctx = cache_k.shape[1]
    q_col = 5 * HG_WIDTH // ATT_WIDTH
    k_col = (5 * HG_WIDTH + ATT_WIDTH) // KV_WIDTH
    table = pl.BlockSpec((n_tok, HEAD_DIM), lambda b, i, s: (0, 0))
    cache = pl.BlockSpec((1, n_ctx, KV_WIDTH), lambda b, i, s: (b, 0, 0))
    return pl.pallas_call(
        functools.partial(_win_attn_kernel, n_tok=n_tok),
        out_shape=jax.ShapeDtypeStruct((n_batch * n_tok, ATT_WIDTH), BF16),
        grid_spec=pltpu.PrefetchScalarGridSpec(
            num_scalar_prefetch=1,
            grid=(n_batch, steps),
            in_specs=[pl.BlockSpec((tq, ATT_WIDTH), lambda b, i, s: (b * steps + i, q_col)),
                      pl.BlockSpec((n_tok, KV_WIDTH), lambda b, i, s: (b, k_col)),
                      pl.BlockSpec((n_tok, KV_WIDTH), lambda b, i, s: (b, k_col + 1)),
                      cache, cache, table, table],
            out_specs=pl.BlockSpec((tq, ATT_WIDTH), lambda b, i, s: (b * steps + i, 0)),
            scratch_shapes=[pltpu.VMEM((n_tok + 2 * ATT_BLOCK, KV_WIDTH), BF16),
                            pltpu.VMEM((n_tok + 2 * ATT_BLOCK, 2 * KV_WIDTH), BF16)]),
        compiler_params=_cparams("parallel", "arbitrary"),
        name="window_attention",
    )(sink, proj, proj, proj, cache_k, cache_v, cos, sin_signed)


def _rope_tables(n_tok):
    rows = n_tok // GRID_W
    row = np.repeat(np.arange(rows), GRID_W).astype(np.float32)
    col = np.tile(np.arange(GRID_W), rows).astype(np.float32)
    inv = np.float32(ROPE_BASE) ** (-np.arange(ROPE_FREQS, dtype=np.float32) / np.float32(ROPE_FREQS))
    ar, ac = row[:, None] * inv, col[:, None] * inv
    cr, sr, cc, sc = np.cos(ar), np.sin(ar), np.cos(ac), np.sin(ac)
    return (jnp.asarray(np.concatenate([cr, cr, cc, cc], axis=1), F32),
            jnp.asarray(np.concatenate([-sr, sr, -sc, sc], axis=1), F32))


def _outproj_kernel(hg_ref, att_ref, w_ref, x_ref, mod_ref, g_ref, xo_ref, h_ref):
    chunk = 256
    for r in range(x_ref.shape[0] // chunk):
        sl = slice(r * chunk, (r + 1) * chunk)
        mix = _dot(hg_ref[sl, :], w_ref[0:HG_WIDTH, :]) + _dot(att_ref[sl, :], w_ref[HG_WIDTH:, :])
        x = x_ref[sl, :] + mod_ref[0, 2:3, :] * mix
        xo_ref[sl, :] = x
        h_ref[sl, :] = _norm_modulate(x, g_ref[...], mod_ref[0, 3:4, :], mod_ref[0, 4:5, :]).astype(BF16)


def _out_projection(o_hg, o_att, w_bf16, x, mod, gain, *, mod_base, rows_per_mod):
    m = x.shape[0]
    tm = 512
    row = lambda i: (i, 0)
    return pl.pallas_call(
        _outproj_kernel,
        out_shape=(jax.ShapeDtypeStruct((m, D_MODEL), F32), jax.ShapeDtypeStruct((m, D_MODEL), BF16)),
        grid=(m // tm,),
        in_specs=[pl.BlockSpec((tm, HG_WIDTH), row),
                  pl.BlockSpec((tm, ATT_WIDTH), row),
                  pl.BlockSpec((HG_WIDTH + ATT_WIDTH, D_MODEL), lambda i: (0, 0)),
                  pl.BlockSpec((tm, D_MODEL), row),
                  pl.BlockSpec((1, N_MOD, D_MODEL), lambda i: (mod_base + (i * tm) // rows_per_mod, 0, 0)),
                  pl.BlockSpec((1, D_MODEL), lambda i: (0, 0))],
        out_specs=(pl.BlockSpec((tm, D_MODEL), row), pl.BlockSpec((tm, D_MODEL), row)),
        compiler_params=_cparams("parallel"),
        name="out_projection",
    )(o_hg, o_att, w_bf16, x, mod, gain)


def _prefix_count(x):
    n = x.shape[1]
    i = lax.broadcasted_iota(jnp.int32, (SEL_BLOCK, SEL_BLOCK), 0)
    j = lax.broadcasted_iota(jnp.int32, (SEL_BLOCK, SEL_BLOCK), 1)
    upper = jnp.where(i < j, 1.0, 0.0).astype(BF16)
    off = jnp.zeros((x.shape[0], 1), F32)
    outs = []
    for blk in range(n // SEL_BLOCK):
        xb = x[:, blk * SEL_BLOCK:(blk + 1) * SEL_BLOCK]
        outs.append(_dot(xb.astype(BF16), upper) + off)
        off = off + jnp.sum(xb, axis=-1, keepdims=True)
    return outs[0] if len(outs) == 1 else jnp.concatenate(outs, axis=1)


def _route_select(h_ref, w_ref, cap):
    logits = _dot_nt(w_ref[...], h_ref[...])
    ex = jnp.exp(logits - jnp.max(logits, axis=0, keepdims=True))
    aff = ex / jnp.sum(ex, axis=0, keepdims=True)
    bits = pltpu.bitcast(aff, jnp.int32)

    thr = jnp.zeros((N_EXPERTS, 1), jnp.int32)
    shift = 31
    while shift > 0:
        width = min(4, shift)
        shift -= width
        digit = jnp.zeros((N_EXPERTS, 1), jnp.int32)
        for j in range(1, 2 ** width):
            cnt = jnp.sum(jnp.where(bits >= (thr | (j << shift)), 1.0, 0.0), axis=-1, keepdims=True)
            digit = digit + jnp.where(cnt >= cap, 1, 0)
        thr = thr | (digit << shift)
    above = jnp.where(bits > thr, 1.0, 0.0)
    tied = jnp.where(bits == thr, 1.0, 0.0)
    room = cap - jnp.sum(above, axis=-1, keepdims=True)
    sel = above + tied * jnp.where(_prefix_count(tied) < room, 1.0, 0.0)
    return jnp.where(sel > 0.0, _prefix_count(sel), -1.0), aff


def _one_hot_gather(slot_rows, aff_rows, h_ref, rows, cap):
    n_tok = h_ref.shape[0]
    c = (lax.broadcasted_iota(jnp.int32, (rows, n_tok), 0) & (cap - 1)).astype(F32)
    hit = c == slot_rows
    x = _dot(jnp.where(hit, 1.0, 0.0).astype(BF16), h_ref[...]).astype(BF16)
    return x, jnp.sum(jnp.where(hit, aff_rows, 0.0), axis=-1, keepdims=True)


def _route_request_kernel(h_ref, w_ref, x_ref, g_ref, slot_ref, *, cap):
    n_tok = h_ref.shape[0]
    slot, aff = _route_select(h_ref, w_ref, cap)
    slot_ref[0] = slot
    per_row = lambda a: jnp.concatenate(
        [jnp.broadcast_to(a[e:e + 1, :], (cap, n_tok)) for e in range(N_EXPERTS)], axis=0)
    x, g = _one_hot_gather(per_row(slot), per_row(aff), h_ref, N_EXPERTS * cap, cap)
    for e in range(N_EXPERTS):
        x_ref[e] = x[e * cap:(e + 1) * cap, :]
        g_ref[e] = g[e * cap:(e + 1) * cap, :]


def _route_only_kernel(h_ref, w_ref, slot_ref, aff_ref, bounds_ref, *, cap):
    n_tok = h_ref.shape[0]
    slot, aff = _route_select(h_ref, w_ref, cap)
    slot_ref[0] = slot
    aff_ref[0] = aff
    n = lax.broadcasted_iota(jnp.int32, (n_tok, 128), 0)
    t = lax.broadcasted_iota(jnp.int32, (n_tok, 128), 1)
    before = jnp.where(n < t * TOK_BLOCK, 1.0, 0.0).astype(BF16)
    chosen = jnp.where(slot >= 0.0, 1.0, 0.0).astype(BF16)
    bounds_ref[0] = _dot(chosen, before).astype(jnp.int32)


def _gather_window_kernel(bounds_ref, slot_ref, aff_ref, h_ref, xh_ref, gh_ref, x_ref, g_ref, xo_ref, go_ref, *,
                          n_batch, group):
    b = pl.program_id(0)
    eg = pl.program_id(1)
    n_tok = h_ref.shape[0]
    nt = n_tok // TOK_BLOCK
    cap = x_ref.shape[1]
    region = TOK_BLOCK // group

    @pl.when(b >= n_batch)
    def _():
        x_ref[...] = xh_ref[...]
        g_ref[...] = gh_ref[...]

    @pl.when(b < n_batch)
    def _():
        x_ref[...] = jnp.zeros(x_ref.shape, BF16)
        g_ref[...] = jnp.zeros(g_ref.shape, F32)
        local_i = lax.broadcasted_iota(jnp.int32, (region, TOK_BLOCK), 0).astype(F32)
        over_i = lax.broadcasted_iota(jnp.int32, (TOK_BLOCK, TOK_BLOCK), 0)

        def token_block(t, carry):
            toks = pl.ds(pl.multiple_of(t * TOK_BLOCK, TOK_BLOCK), TOK_BLOCK)
            hits, gates, firsts, n_overs, slots, affs = [], [], [], [], [], []
            for k in range(group):
                e = eg * group + k
                start, length = _window(bounds_ref, (b * N_EXPERTS + e) * (nt + 1), t)
                first = jnp.minimum(start, cap - region)
                s = slot_ref[0, pl.ds(e, 1), toks]
                a = aff_ref[0, pl.ds(e, 1), toks]
                hit = local_i == jnp.where(s >= 0.0, s - first.astype(F32), -1.0)
                hits.append(jnp.where(hit, 1.0, 0.0).astype(BF16))
                gates.append(jnp.sum(jnp.where(hit, a, 0.0), axis=-1, keepdims=True))
                firsts.append(first)
                n_overs.append(start + length - first - region)
                slots.append(s)
                affs.append(a)
            xc = _dot(jnp.concatenate(hits, axis=0), h_ref[toks, :]).astype(BF16)
            for k in range(group):
                dst = pl.ds(pl.multiple_of(firsts[k], PIECE), region)
                x_ref[k, dst, :] = x_ref[k, dst, :] + xc[k * region:(k + 1) * region, :]
                g_ref[k, dst, :] = g_ref[k, dst, :] + gates[k]
            for k in range(group):
                @pl.when(n_overs[k] > 0)
                def _(k=k):
                    base = firsts[k] + region
                    hit = (over_i + base).astype(F32) == slots[k]
                    xo_ref[...] = _dot(jnp.where(hit, 1.0, 0.0).astype(BF16), h_ref[toks, :]).astype(BF16)
                    go_ref[...] = jnp.sum(jnp.where(hit, affs[k], 0.0), axis=-1, keepdims=True)

                    def place(i, carry):
                        src = pl.ds(pl.multiple_of(i * PIECE, PIECE), PIECE)
                        dst = pl.ds(pl.multiple_of(base + i * PIECE, PIECE), PIECE)
                        x_ref[k, dst, :] = x_ref[k, dst, :] + xo_ref[src, :]
                        g_ref[k, dst, :] = g_ref[k, dst, :] + go_ref[src, :]
                        return carry

                    lax.fori_loop(0, n_overs[k] // PIECE, place, 0)
            return carry

        lax.fori_loop(0, nt, token_block, 0)


def _route_gather_requests(h, w_router_t, *, n_batch, n_tok):
    cap = CAPACITY_FACTOR * n_tok // N_EXPERTS
    out_block = lambda w: pl.BlockSpec((N_EXPERTS, cap, w), lambda b: (0, b, 0))
    return pl.pallas_call(
        functools.partial(_route_request_kernel, cap=cap),
        out_shape=(jax.ShapeDtypeStruct((N_EXPERTS, n_batch * cap, D_MODEL), BF16),
                   jax.ShapeDtypeStruct((N_EXPERTS, n_batch * cap, 1), F32),
                   jax.ShapeDtypeStruct((n_batch, N_EXPERTS, n_tok), F32)),
        grid=(n_batch,),
        in_specs=[pl.BlockSpec((n_tok, D_MODEL), lambda b: (b, 0)),
                  pl.BlockSpec((N_EXPERTS, D_MODEL), lambda b: (0, 0))],
        out_specs=(out_block(D_MODEL), out_block(1), pl.BlockSpec((1, N_EXPERTS, n_tok), lambda b: (b, 0, 0))),
        compiler_params=_cparams("parallel"),
        name="route_gather_requests",
    )(h, w_router_t)


def _route_only(h, w_router_t, *, n_batch, n_tok):
    cap = CAPACITY_FACTOR * n_tok // N_EXPERTS
    nt = n_tok // TOK_BLOCK
    expert_major = pl.BlockSpec((1, N_EXPERTS, n_tok), lambda b: (b, 0, 0))
    slot, aff, bounds = pl.pallas_call(
        functools.partial(_route_only_kernel, cap=cap),
        out_shape=(jax.ShapeDtypeStruct((n_batch, N_EXPERTS, n_tok), F32),
                   jax.ShapeDtypeStruct((n_batch, N_EXPERTS, n_tok), F32),
                   jax.ShapeDtypeStruct((n_batch, N_EXPERTS, 128), jnp.int32)),
        grid=(n_batch,),
        in_specs=[pl.BlockSpec((n_tok, D_MODEL), lambda b: (b, 0)),
                  pl.BlockSpec((N_EXPERTS, D_MODEL), lambda b: (0, 0))],
        out_specs=(expert_major, expert_major, pl.BlockSpec((1, N_EXPERTS, 128), lambda b: (b, 0, 0))),
        compiler_params=_cparams("parallel"),
        name="route_select",
    )(h, w_router_t)
    return slot, aff, bounds[:, :, :nt + 1].reshape(-1)


def _gather_windows(bounds, slot, aff, h, x_head, g_head, *, n_batch, n_tok):
    cap = CAPACITY_FACTOR * n_tok // N_EXPERTS
    group = 4
    head_rows = x_head.shape[1]
    assert head_rows % cap == 0 and N_EXPERTS % group == 0
    n_head = head_rows // cap
    rows_total = head_rows + n_batch * cap
    last = n_batch - 1
    request = lambda b: jnp.minimum(b, last)
    out_row = lambda b: jnp.where(b < n_batch, n_head + b, b - n_batch)
    head_row = lambda b: jnp.maximum(b - n_batch, 0)
    out_block = lambda w: pl.BlockSpec((group, cap, w), lambda b, g, s: (g, out_row(b), 0))
    head_block = lambda w: pl.BlockSpec((group, cap, w), lambda b, g, s: (g, head_row(b), 0))
    expert_major = pl.BlockSpec((1, N_EXPERTS, n_tok), lambda b, g, s: (request(b), 0, 0))
    return pl.pallas_call(
        functools.partial(_gather_window_kernel, n_batch=n_batch, group=group),
        out_shape=(jax.ShapeDtypeStruct((N_EXPERTS, rows_total, D_MODEL), BF16),
                   jax.ShapeDtypeStruct((N_EXPERTS, rows_total, 1), F32)),
        grid_spec=pltpu.PrefetchScalarGridSpec(
            num_scalar_prefetch=1,
            grid=(n_batch + n_head, N_EXPERTS // group),
            in_specs=[expert_major, expert_major,
                      pl.BlockSpec((n_tok, D_MODEL), lambda b, g, s: (request(b), 0)),
                      head_block(D_MODEL), head_block(1)],
            out_specs=(out_block(D_MODEL), out_block(1)),
            scratch_shapes=[pltpu.VMEM((TOK_BLOCK, D_MODEL), BF16), pltpu.VMEM((TOK_BLOCK, 1), F32)]),
        compiler_params=_cparams("arbitrary", "arbitrary"),
        name="gather_windows",
    )(bounds, slot, aff, h, x_head, g_head)


def _moe_kernel(x_ref, g_ref, wg_ref, wu_ref, wd_ref, y_ref, hid_ref, *, n_ff, rows):
    s = pl.program_id(2)
    tf = wg_ref.shape[2]

    @pl.when(s < n_ff)
    def _():
        wg = wg_ref[0].astype(BF16)
        wu = wu_ref[0].astype(BF16)
        cols = pl.ds(pl.multiple_of(s * tf, tf), tf)
        for r in range(x_ref.shape[1] // rows):
            sl = slice(r * rows, (r + 1) * rows)
            x = x_ref[0, sl, :]
            hid_ref[sl, cols] = (_silu(_dot(x, wg)) * _dot(x, wu)).astype(BF16)

    @pl.when(s >= n_ff)
    def _():
        y = _dot(hid_ref[...], wd_ref[0].astype(BF16))
        y_ref[0] = (y * g_ref[0]).astype(BF16)


def _experts(x, gate, w_gate, w_up, w_down):
    n_rows = x.shape[1]
    tr, tf, tn = n_rows // 2, 512, 256
    n_ff, n_out = EXPERT_FF // tf, D_MODEL // tn
    up_tile = lambda e, r, s: (e, 0, jnp.minimum(s, n_ff - 1))
    out_tile = lambda s: jnp.maximum(s - n_ff, 0)
    return pl.pallas_call(
        functools.partial(_moe_kernel, n_ff=n_ff, rows=256),
        out_shape=jax.ShapeDtypeStruct((N_EXPERTS, n_rows, D_MODEL), BF16),
        grid=(N_EXPERTS, n_rows // tr, n_ff + n_out),
        in_specs=[pl.BlockSpec((1, tr, D_MODEL), lambda e, r, s: (e, r, 0), pipeline_mode=pl.Buffered(1)),
                  pl.BlockSpec((1, tr, 1), lambda e, r, s: (e, r, 0)),
                  pl.BlockSpec((1, D_MODEL, tf), up_tile),
                  pl.BlockSpec((1, D_MODEL, tf), up_tile),
                  pl.BlockSpec((1, EXPERT_FF, tn), lambda e, r, s: (e, 0, out_tile(s)))],
        out_specs=pl.BlockSpec((1, tr, tn), lambda e, r, s: (e, r, out_tile(s))),
        scratch_shapes=[pltpu.VMEM((tr, EXPERT_FF), BF16)],
        compiler_params=_cparams("parallel", "parallel", "arbitrary"),
        name="expert_swiglu",
    )(x, gate, w_gate, w_up, w_down)


def _window(bounds_ref, base, t):
    p0 = bounds_ref[base + t]
    p1 = bounds_ref[base + t + 1]
    start = (p0 // PIECE) * PIECE
    return start, jnp.where(p1 > p0, ((p1 - start + PIECE - 1) // PIECE) * PIECE, 0)


def _combine_kernel(bounds_ref, y_ref, slot_ref, x_ref, mod_ref, g_ref, o_ref, ybuf_ref, obuf_ref, acc_ref, *,
                    nt, region):
    b = pl.program_id(0)
    t = pl.program_id(1)
    tt = x_ref.shape[0]
    cap = y_ref.shape[1]
    local_i = lax.broadcasted_iota(jnp.int32, (region, tt), 0).astype(F32)
    hits = []
    for e in range(N_EXPERTS):
        start, _ = _window(bounds_ref, (b * N_EXPERTS + e) * (nt + 1), t)
        first = jnp.minimum(start, cap - region)
        ybuf_ref[e * region:(e + 1) * region, :] = y_ref[e, pl.ds(pl.multiple_of(first, PIECE), region), :]
        s = slot_ref[0, e:e + 1, :]
        hit = local_i == jnp.where(s >= 0.0, s - first.astype(F32), -1.0)
        hits.append(jnp.where(hit, 1.0, 0.0).astype(BF16))
    acc_ref[...] = _dot_tn(jnp.concatenate(hits, axis=0), ybuf_ref[...])
    over_i = lax.broadcasted_iota(jnp.int32, (TOK_BLOCK, tt), 0)

    def overflow(e, carry):
        start, length = _window(bounds_ref, (b * N_EXPERTS + e) * (nt + 1), t)
        base = jnp.minimum(start, cap - region) + region
        n_over = start + length - base

        @pl.when(n_over > 0)
        def _():
            def copy(i, carry):
                obuf_ref[pl.ds(pl.multiple_of(i * PIECE, PIECE), PIECE), :] = (
                    y_ref[e, pl.ds(pl.multiple_of(base + i * PIECE, PIECE), PIECE), :])
                return carry

            def clear(i, carry):
                obuf_ref[pl.ds(pl.multiple_of(n_over + i * PIECE, PIECE), PIECE), :] = jnp.zeros((PIECE, D_MODEL), BF16)
                return carry

            lax.fori_loop(0, n_over // PIECE, copy, 0)
            lax.fori_loop(0, (TOK_BLOCK - n_over) // PIECE, clear, 0)
            hit = (over_i + base).astype(F32) == slot_ref[0, pl.ds(e, 1), :]
            acc_ref[...] += _dot_tn(jnp.where(hit, 1.0, 0.0).astype(BF16), obuf_ref[...])

        return carry

    lax.fori_loop(0, N_EXPERTS, overflow, 0)
    x = x_ref[...] + mod_ref[0, 5:6, :] * acc_ref[...]
    var = jnp.mean(x * x, axis=-1, keepdims=True)
    o_ref[...] = x * lax.rsqrt(var + NORM_EPS) * g_ref[...]


def _combine(bounds, y, slot, x_mid, mod, final_g, *, n_batch, n_tok, row_block_off, mod_base, mod_per_batch):
    cap = CAPACITY_FACTOR * n_tok // N_EXPERTS
    tt = TOK_BLOCK
    nt = n_tok // tt
    region = min(64, cap)
    assert cap - region <= TOK_BLOCK
    return pl.pallas_call(
        functools.partial(_combine_kernel, nt=nt, region=region),
        out_shape=jax.ShapeDtypeStruct((n_batch * n_tok, D_MODEL), F32),
        grid_spec=pltpu.PrefetchScalarGridSpec(
            num_scalar_prefetch=1,
            grid=(n_batch, nt),
            in_specs=[pl.BlockSpec((N_EXPERTS, cap, D_MODEL), lambda b, t, s: (0, row_block_off + b, 0)),
                      pl.BlockSpec((1, N_EXPERTS, tt), lambda b, t, s: (b, 0, t)),
                      pl.BlockSpec((tt, D_MODEL), lambda b, t, s: (b * nt + t, 0)),
                      pl.BlockSpec((1, N_MOD, D_MODEL), lambda b, t, s: (mod_base + b * mod_per_batch, 0, 0)),
                      pl.BlockSpec((1, D_MODEL), lambda b, t, s: (0, 0))],
            out_specs=pl.BlockSpec((tt, D_MODEL), lambda b, t, s: (b * nt + t, 0)),
            scratch_shapes=[pltpu.VMEM((N_EXPERTS * region, D_MODEL), BF16),
                            pltpu.VMEM((TOK_BLOCK, D_MODEL), BF16), pltpu.VMEM((tt, D_MODEL), F32)]),
        compiler_params=_cparams("arbitrary", "arbitrary"),
        name="combine_final_norm",
    )(bounds, y, slot, x_mid, mod, final_g)


def kernel(x_prompt, x_sample, cache_k, cache_v, state_hgrn, c, c_ctx, w_ada, b_ada, norm1_g, w_in, hg_lb,
           hg_norm_g, attn_sink, w_out, norm2_g, w_router, w_gate, w_up, w_down, final_norm_g):
    n_p, t_p, _ = x_prompt.shape
    n_s, t_s, _ = x_sample.shape
    assert w_ada.shape[0] == 1 and 1 + n_s <= COND_ROWS
    layer = 0

    cond = jnp.zeros((COND_ROWS, D_MODEL), F32).at[0].set(c_ctx).at[1:1 + n_s].set(c)
    mod = _ada_modulation(cond, w_ada[layer], b_ada[layer]).reshape(COND_ROWS, N_MOD, D_MODEL)
    lb = jnp.cumsum(jax.nn.softmax(hg_lb.astype(F32), axis=0), axis=0)[layer]
    w_in_l = w_in[layer].astype(BF16)
    w_out_b = w_out[layer].astype(BF16)
    w_router_t = w_router[layer].T.astype(BF16)
    norm1 = norm1_g[layer].reshape(1, D_MODEL)
    norm2 = norm2_g[layer].reshape(1, D_MODEL)
    hg_gain = hg_norm_g[layer].reshape(1, HG_WIDTH)
    final_g = final_norm_g.reshape(1, D_MODEL)
    sink = attn_sink[layer]
    cos, sin_signed = _rope_tables(t_s)

    xp = x_prompt.reshape(n_p * t_p, D_MODEL)
    xs = x_sample.reshape(n_s * t_s, D_MODEL)
    groups = dict(p=dict(mod_base=0, rows_per_mod=n_p * t_p), s=dict(mod_base=1, rows_per_mod=t_s))

    proj_p = _in_projection(xp, mod, norm1, w_in_l, **groups["p"])
    proj_s = _in_projection(xs, mod, norm1, w_in_l, **groups["s"])

    ohg_p, new_state = _hgrn(proj_p, lb, hg_gain, None, n_batch=n_p, n_tok=t_p)
    ohg_s, _ = _hgrn(proj_s, lb, hg_gain, state_hgrn[:, layer:layer + 1].astype(F32), n_batch=n_s, n_tok=t_s)

    oatt_p = _context_attention(proj_p, sink, n_batch=n_p, n_tok=t_p)
    n_ctx = cache_k.shape[2]
    oatt_s = _window_attention(proj_s, cache_k[:, layer].reshape(n_s, n_ctx, KV_WIDTH),
                               cache_v[:, layer].reshape(n_s, n_ctx, KV_WIDTH), sink, cos, sin_signed,
                               n_batch=n_s, n_tok=t_s)

    xmid_p, h2_p = _out_projection(ohg_p, oatt_p, w_out_b, xp, mod, norm2, **groups["p"])
    xmid_s, h2_s = _out_projection(ohg_s, oatt_s, w_out_b, xs, mod, norm2, **groups["s"])

    cap_p = CAPACITY_FACTOR * t_p // N_EXPERTS
    cap_s = CAPACITY_FACTOR * t_s // N_EXPERTS
    off_s = n_p * cap_p // cap_s
    xg_p, gate_p, slot_p = _route_gather_requests(h2_p, w_router_t, n_batch=n_p, n_tok=t_p)
    slot_s, aff_s, bounds_s = _route_only(h2_s, w_router_t, n_batch=n_s, n_tok=t_s)
    xg, gate = _gather_windows(bounds_s, slot_s, aff_s, h2_s, xg_p, gate_p, n_batch=n_s, n_tok=t_s)
    bounds_p = jnp.tile(jnp.array([0, cap_p], jnp.int32), n_p * N_EXPERTS)

    y = _experts(xg, gate, w_gate[layer], w_up[layer], w_down[layer])

    y_prompt = _combine(bounds_p, y, slot_p, xmid_p, mod, final_g, n_batch=n_p, n_tok=t_p, row_block_off=0,
                        mod_base=0, mod_per_batch=0)
    y_sample = _combine(bounds_s, y, slot_s, xmid_s, mod, final_g, n_batch=n_s, n_tok=t_s, row_block_off=off_s,
                        mod_base=1, mod_per_batch=1)

    k_col = 5 * HG_WIDTH + ATT_WIDTH
    new_k = proj_p[:, k_col:k_col + KV_WIDTH].reshape(n_p, 1, t_p, ATT_KV_HEADS, HEAD_DIM)
    new_v = proj_p[:, k_col + KV_WIDTH:k_col + 2 * KV_WIDTH].reshape(n_p, 1, t_p, ATT_KV_HEADS, HEAD_DIM)
    return (y_prompt.reshape(n_p, t_p, D_MODEL), y_sample.reshape(n_s, t_s, D_MODEL), new_k, new_v, new_state)
```

```python
import functools

import jax
import jax.numpy as jnp
import numpy as np
from jax import lax
from jax.experimental import pallas as pl
from jax.experimental.pallas import tpu as pltpu

F32 = jnp.float32
BF16 = jnp.bfloat16

D_MODEL = 2048
HG_WIDTH = 1024
HG_HEADS = 8
HEAD_DIM = 128
ATT_HEADS = 8
ATT_KV_HEADS = 2
ATT_GROUP = ATT_HEADS // ATT_KV_HEADS
KV_WIDTH = ATT_KV_HEADS * HEAD_DIM
ATT_WIDTH = ATT_HEADS * HEAD_DIM
ATT_BLOCK = 128
GRID_W = 64
ROPE_BASE = 10000.0
ROPE_FREQS = HEAD_DIM // 4
N_EXPERTS = 16
CAPACITY_FACTOR = 2
EXPERT_FF = 5632
NORM_EPS = 1e-6
IN_WIDTH = 5 * HG_WIDTH + ATT_WIDTH + 2 * KV_WIDTH
N_MOD = 6
COND_ROWS = 16

HG_CHUNK = 128
HG_DIAG = 8
SEL_BLOCK = 256
TOK_BLOCK = 256
PIECE = 16
SLOT_REGION = 64
LANES = 128

V7X_VMEM_BYTES = 64 * 1024 * 1024
VMEM_LIMIT = V7X_VMEM_BYTES - 8 * 1024 * 1024


def _cparams(*sem):
    return pltpu.CompilerParams(dimension_semantics=sem, vmem_limit_bytes=VMEM_LIMIT)


def _sigmoid(x):
    return 1.0 / (1.0 + jnp.exp(-x))


def _silu(x):
    return x * _sigmoid(x)


def _dot(a, b):
    return jnp.dot(a, b, preferred_element_type=F32)


def _dot_nt(a, b):
    return lax.dot_general(a, b, (((1,), (1,)), ((), ())), preferred_element_type=F32)


def _dot_tn(a, b):
    return lax.dot_general(a, b, (((0,), (0,)), ((), ())), preferred_element_type=F32)


def _ada_kernel(c_ref, w_ref, b_ref, o_ref):
    s = _silu(c_ref[...]).astype(BF16)
    o_ref[...] = _dot(s, w_ref[...].astype(BF16)) + b_ref[...]


def _ada_modulation(cond, w_ada, b_ada):
    tn = 1024
    n = w_ada.shape[1]
    return pl.pallas_call(
        _ada_kernel,
        out_shape=jax.ShapeDtypeStruct((COND_ROWS, n), F32),
        grid=(n // tn,),
        in_specs=[pl.BlockSpec((COND_ROWS, D_MODEL), lambda j: (0, 0)),
                  pl.BlockSpec((D_MODEL, tn), lambda j: (0, j)),
                  pl.BlockSpec((1, tn), lambda j: (0, j))],
        out_specs=pl.BlockSpec((COND_ROWS, tn), lambda j: (0, j)),
        compiler_params=_cparams("arbitrary"),
        name="ada_modulation",
    )(cond, w_ada, b_ada.reshape(1, n))


def _norm_modulate(x, gain, shift, scale):
    var = jnp.mean(x * x, axis=-1, keepdims=True)
    return (x * lax.rsqrt(var + NORM_EPS) * gain) * (1.0 + scale) + shift


def _inproj_kernel(x_ref, mod_ref, g_ref, w_ref, o_ref, h_ref, *, rows):
    w = w_ref[...]
    chunks = [slice(r * rows, (r + 1) * rows) for r in range(x_ref.shape[0] // rows)]

    @pl.when(pl.program_id(1) == 0)
    def _():
        shift = mod_ref[0, 0:1, :]
        scale = mod_ref[0, 1:2, :]
        gain = g_ref[...]
        for sl in chunks:
            h = _norm_modulate(x_ref[sl, :], gain, shift, scale).astype(BF16)
            h_ref[sl, :] = h
            o_ref[sl, :] = _dot(h, w)

    @pl.when(pl.program_id(1) > 0)
    def _():
        for sl in chunks:
            o_ref[sl, :] = _dot(h_ref[sl, :], w)


def _in_projection(x, mod, gain, w, *, mod_base, rows_per_mod):
    m = x.shape[0]
    tm, tn = 1024, 512
    return pl.pallas_call(
        functools.partial(_inproj_kernel, rows=256),
        out_shape=jax.ShapeDtypeStruct((m, IN_WIDTH), F32),
        grid=(m // tm, IN_WIDTH // tn),
        in_specs=[pl.BlockSpec((tm, D_MODEL), lambda i, j: (i, 0)),
                  pl.BlockSpec((1, N_MOD, D_MODEL), lambda i, j: (mod_base + (i * tm) // rows_per_mod, 0, 0)),
                  pl.BlockSpec((1, D_MODEL), lambda i, j: (0, 0)),
                  pl.BlockSpec((D_MODEL, tn), lambda i, j: (0, j))],
        out_specs=pl.BlockSpec((tm, tn), lambda i, j: (i, j)),
        scratch_shapes=[pltpu.VMEM((tm, D_MODEL), BF16)],
        compiler_params=_cparams("parallel", "arbitrary"),
        name="in_projection",
    )(x, mod, gain, w)


def _hgrn_codes(reverse):
    L = HG_CHUNK
    t = lax.broadcasted_iota(jnp.int32, (L, L), 0)
    s = lax.broadcasted_iota(jnp.int32, (L, L), 1)
    code = jnp.where(t == s, 1, 0)
    h = L // 2
    while h >= 1:
        same = (t & ~(2 * h - 1)) == (s & ~(2 * h - 1))
        t_hi = (t & h) != 0
        s_hi = (s & h) != 0
        pair = (s_hi & ~t_hi) if reverse else (t_hi & ~s_hi)
        code = jnp.where(same & pair, h * 16, code)
        h //= 2
    return code


def _cumsum_rows(tri_bf16, g):
    g1 = g.astype(BF16)
    g2 = (g - g1.astype(F32)).astype(BF16)
    s = _dot(tri_bf16, jnp.concatenate([g1, g2], axis=1))
    return s[:, :HEAD_DIM] + s[:, HEAD_DIM:]


def _hgrn_intra(q, k, f, b, v_bf, code, reverse):
    L = HG_CHUNK
    G = HG_DIAG
    q_bf = q.astype(BF16)
    k_bf = k.astype(BF16)

    def level(h, ref, att):
        neg_abs = pltpu.bitcast(pltpu.bitcast(b - ref, jnp.int32) | jnp.int32(-2 ** 31), F32)
        e = jnp.exp2(neg_abs.astype(BF16))
        return jnp.where(code == h * 16, _dot_nt(q_bf * e, k_bf * e), att)

    att = jnp.where(code == 1, jnp.sum(q * k, axis=-1, keepdims=True), 0.0)
    h = L // 2
    while h >= G:
        parts = []
        for p in range(L // (2 * h)):
            m = p * 2 * h + (h if reverse else h - 1)
            parts.append(jnp.broadcast_to(b[m:m + 1, :], (2 * h, HEAD_DIM)))
        att = level(h, parts[0] if len(parts) == 1 else jnp.concatenate(parts, axis=0), att)
        h //= 2

    b3 = b.reshape(L // G, G, HEAD_DIM)

    def group_row(r):
        return jnp.broadcast_to(b3[:, r:r + 1, :], (L // G, G, HEAD_DIM)).reshape(L, HEAD_DIM)

    row = lax.broadcasted_iota(jnp.int32, (L, HEAD_DIM), 0)
    att = level(4, group_row(4 if reverse else 3), att)
    lo, hi = (2, 6) if reverse else (1, 5)
    att = level(2, jnp.where((row & 4) == 0, group_row(lo), group_row(hi)), att)
    att = jnp.where(code == 16, _dot_nt(q_bf * f.astype(BF16), k_bf), att)
    return _dot(att.astype(BF16), v_bf)


def _hgrn_prepare(q, f, v, code, tri, reverse):
    L = HG_CHUNK
    k = 1.0 - f
    b = _cumsum_rows(tri, jnp.log2(f))
    b_tot = b[0:1, :] if reverse else b[L - 1:L, :]
    v_bf = v.astype(BF16)
    o_intra = _hgrn_intra(q, k, f, b, v_bf, code, reverse)
    q_in = (q * jnp.exp2(b)).astype(BF16)
    k_out = (k * jnp.exp2(b_tot - b)).astype(BF16)
    return o_intra, q_in, k_out, v_bf, jnp.exp2(b_tot)


def _hgrn_advance(st, prepared):
    o_intra, q_in, k_out, v_bf, decay = prepared
    return o_intra + _dot_nt(q_in, st.astype(BF16)), st * decay + _dot_tn(v_bf, k_out)


def _hgrn_kernel(*refs, n_tok, has_state):
    if has_state:
        (q_ref, ff_ref, fb_ref, v_ref, gate_ref, lb_ref, ng_ref, s0_ref,
         o_ref, sout_ref, of_ref, ob_ref, code_ref) = refs
    else:
        (q_ref, ff_ref, fb_ref, v_ref, gate_ref, lb_ref, ng_ref,
         o_ref, sout_ref, of_ref, ob_ref, code_ref) = refs
    L = HG_CHUNK
    nc = n_tok // L
    lb_f = lb_ref[0:1, :]
    lb_b = lb_ref[1:2, :]

    @pl.when((pl.program_id(0) == 0) & (pl.program_id(1) == 0))
    def _():
        code_ref[0] = _hgrn_codes(False)
        code_ref[1] = _hgrn_codes(True)

    code_f = code_ref[0]
    code_b = code_ref[1]
    ti = lax.broadcasted_iota(jnp.int32, (L, L), 0)
    si = lax.broadcasted_iota(jnp.int32, (L, L), 1)
    tri_f = jnp.where(si <= ti, 1.0, 0.0).astype(BF16)
    tri_b = jnp.where(si >= ti, 1.0, 0.0).astype(BF16)

    def prepare(row, f_ref, lb, code, tri, reverse):
        sl = pl.ds(row, L)
        q = _silu(q_ref[sl, :])
        f = lb + (1.0 - lb) * _sigmoid(f_ref[sl, :])
        return sl, _hgrn_prepare(q, f, v_ref[sl, :], code, tri, reverse)

    gain = ng_ref[...]

    def finish(sl, o):
        var = jnp.mean(o * o, axis=-1, keepdims=True)
        o_ref[sl, :] = ((o * lax.rsqrt(var + NORM_EPS) * gain) * _silu(gate_ref[sl, :])).astype(BF16)

    def make_body(other_direction):
        def body(c, states):
            st_f, st_b = states
            fwd, bwd = [], []
            for u in range(unroll):
                cu = c * unroll + u
                fwd.append(prepare(pl.multiple_of(cu * L, L), ff_ref, lb_f, code_f, tri_f, False))
                bwd.append(prepare(pl.multiple_of((nc - 1 - cu) * L, L), fb_ref, lb_b, code_b, tri_b, True))
            out_f, out_b = [], []
            for (sl_f, prep_f), (sl_b, prep_b) in zip(fwd, bwd):
                o_f, st_f = _hgrn_advance(st_f, prep_f)
                o_b, st_b = _hgrn_advance(st_b, prep_b)
                out_f.append((sl_f, o_f))
                out_b.append((sl_b, o_b))
            if other_direction == "later":
                for sl, o in out_f:
                    of_ref[sl, :] = o
                for sl, o in out_b:
                    ob_ref[sl, :] = o
            elif other_direction == "now":
                for u in range(unroll):
                    finish(out_f[u][0], out_f[u][1] + out_b[unroll - 1 - u][1])
            else:
                for sl, o in out_f:
                    finish(sl, o + ob_ref[sl, :])
                for sl, o in out_b:
                    finish(sl, of_ref[sl, :] + o)
            return st_f, st_b
        return body

    unroll = 8 if nc % 16 == 0 else 2
    assert nc % unroll == 0
    n_iter = nc // unroll
    half = n_iter // 2
    if has_state:
        states = (s0_ref[0, 0, 0, 0].T, s0_ref[0, 0, 1, 0].T)
    else:
        states = (jnp.zeros((HEAD_DIM, HEAD_DIM), F32), jnp.zeros((HEAD_DIM, HEAD_DIM), F32))
    states = lax.fori_loop(0, half, make_body("later"), states)
    if n_iter % 2:
        states = make_body("now")(half, states)
    st_f, st_b = lax.fori_loop(half + n_iter % 2, n_iter, make_body("earlier"), states)
    sout_ref[0, 0, 0, 0] = st_f.T
    sout_ref[0, 0, 1, 0] = st_b.T


def _hgrn(proj, lb, norm_g, state, *, n_batch, n_tok):
    col = lambda k: (lambda b, h: (b, k * HG_HEADS + h))
    tok_spec = lambda k: pl.BlockSpec((n_tok, HEAD_DIM), col(k))
    st_spec = pl.BlockSpec((1, 1, 2, 1, HEAD_DIM, HEAD_DIM), lambda b, h: (b, 0, 0, h, 0, 0))
    has_state = state is not None
    return pl.pallas_call(
        functools.partial(_hgrn_kernel, n_tok=n_tok, has_state=has_state),
        out_shape=(jax.ShapeDtypeStruct((n_batch * n_tok, HG_WIDTH), BF16),
                   jax.ShapeDtypeStruct((n_batch, 1, 2, HG_HEADS, HEAD_DIM, HEAD_DIM), F32)),
        grid=(n_batch, HG_HEADS),
        in_specs=[tok_spec(0), tok_spec(1), tok_spec(2), tok_spec(3), tok_spec(4),
                  pl.BlockSpec((2, HEAD_DIM), lambda b, h: (0, h)),
                  pl.BlockSpec((1, HEAD_DIM), lambda b, h: (0, h))] + ([st_spec] if has_state else []),
        out_specs=(pl.BlockSpec((n_tok, HEAD_DIM), lambda b, h: (b, h)), st_spec),
        scratch_shapes=[pltpu.VMEM((n_tok, HEAD_DIM), F32), pltpu.VMEM((n_tok, HEAD_DIM), F32),
                        pltpu.VMEM((2, HG_CHUNK, HG_CHUNK), jnp.int32)],
        compiler_params=_cparams("arbitrary", "arbitrary"),
        name="hgrn2_scan",
    )(proj, proj, proj, proj, proj, lb, norm_g, *((state,) if has_state else ()))


def _stack_heads(x, kvh):
    return jnp.concatenate(
        [x[:, (kvh * ATT_GROUP + g) * HEAD_DIM:(kvh * ATT_GROUP + g + 1) * HEAD_DIM] for g in range(ATT_GROUP)],
        axis=0)


def _sink_column(sink_ref, kvh, rows):
    return jnp.concatenate(
        [jnp.full((rows, 1), sink_ref[kvh * ATT_GROUP + g], F32) for g in range(ATT_GROUP)],
        axis=0) * (1.0 / SOFTMAX_SCALE)


SOFTMAX_SCALE = HEAD_DIM ** -0.5
EXP2_SCALE = SOFTMAX_SCALE * 1.4426950408889634


def _ones_column(rows):
    lane = lax.broadcasted_iota(jnp.int32, (rows, HEAD_DIM), 1)
    return jnp.where(lane == 0, 1.0, 0.0).astype(BF16)


def _softmax_av(scores, values, sink_col):
    m = sink_col
    for s in scores:
        m = jnp.maximum(m, jnp.max(s, axis=-1, keepdims=True))
    acc = None
    for s, v in zip(scores, values):
        pv = _dot(jnp.exp2(((s - m) * EXP2_SCALE).astype(BF16)), v)
        acc = pv if acc is None else acc + pv
    denom = acc[:, HEAD_DIM:HEAD_DIM + 1] + jnp.exp2((sink_col - m) * EXP2_SCALE)
    return acc[:, :HEAD_DIM] / denom


def _ctx_attn_kernel(sink_ref, q_ref, k_ref, v_ref, o_ref):
    rows = q_ref.shape[0]
    q_all = q_ref[...]
    ones = _ones_column(k_ref.shape[0])
    for kvh in range(ATT_KV_HEADS):
        q = _stack_heads(q_all, kvh).astype(BF16)
        k = k_ref[:, kvh * HEAD_DIM:(kvh + 1) * HEAD_DIM].astype(BF16)
        v = jnp.concatenate([v_ref[:, kvh * HEAD_DIM:(kvh + 1) * HEAD_DIM].astype(BF16), ones], axis=1)
        o = _softmax_av([_dot_nt(q, k)], [v], _sink_column(sink_ref, kvh, rows))
        for g in range(ATT_GROUP):
            hd = kvh * ATT_GROUP + g
            o_ref[:, hd * HEAD_DIM:(hd + 1) * HEAD_DIM] = o[g * rows:(g + 1) * rows, :].astype(BF16)


def _context_attention(proj, sink, *, n_batch, n_tok):
    q_col = 5 * HG_WIDTH // ATT_WIDTH
    k_col = (5 * HG_WIDTH + ATT_WIDTH) // KV_WIDTH
    return pl.pallas_call(
        _ctx_attn_kernel,
        out_shape=jax.ShapeDtypeStruct((n_batch * n_tok, ATT_WIDTH), BF16),
        grid_spec=pltpu.PrefetchScalarGridSpec(
            num_scalar_prefetch=1,
            grid=(n_batch,),
            in_specs=[pl.BlockSpec((n_tok, ATT_WIDTH), lambda b, s: (b, q_col)),
                      pl.BlockSpec((n_tok, KV_WIDTH), lambda b, s: (b, k_col)),
                      pl.BlockSpec((n_tok, KV_WIDTH), lambda b, s: (b, k_col + 1))],
            out_specs=pl.BlockSpec((n_tok, ATT_WIDTH), lambda b, s: (b, 0))),
        compiler_params=_cparams("parallel"),
        name="context_attention",
    )(sink, proj, proj, proj)


def _rope(x, cos, sin_signed, even_group):
    partner = jnp.where(even_group, pltpu.roll(x, HEAD_DIM - ROPE_FREQS, 1), pltpu.roll(x, ROPE_FREQS, 1))
    return x * cos + partner * sin_signed


def _win_attn_kernel(sink_ref, q_ref, k_ref, v_ref, ck_ref, cv_ref, cos_ref, sin_ref, o_ref,
                     kpad_ref, vpad_ref, *, n_tok):
    blk = ATT_BLOCK
    nb = n_tok // blk
    i = pl.program_id(1)
    lane = lax.broadcasted_iota(jnp.int32, (blk, HEAD_DIM), 1)
    even_group = (lane & ROPE_FREQS) == 0

    @pl.when(i == 0)
    def _():
        kpad_ref[0:blk, :] = jnp.zeros((blk, KV_WIDTH), BF16)
        kpad_ref[blk + n_tok:2 * blk + n_tok, :] = jnp.zeros((blk, KV_WIDTH), BF16)
        vpad_ref[0:blk, :] = jnp.zeros((blk, 2 * KV_WIDTH), BF16)
        vpad_ref[blk + n_tok:2 * blk + n_tok, :] = jnp.zeros((blk, 2 * KV_WIDTH), BF16)
        ones = _ones_column(blk)

        def body(r, carry):
            src = pl.ds(pl.multiple_of(r * blk, blk), blk)
            dst = pl.ds(pl.multiple_of((r + 1) * blk, blk), blk)
            cos = cos_ref[src, :]
            sin = sin_ref[src, :]
            for kvh in range(ATT_KV_HEADS):
                cols = slice(kvh * HEAD_DIM, (kvh + 1) * HEAD_DIM)
                kpad_ref[dst, cols] = _rope(k_ref[src, cols], cos, sin, even_group).astype(BF16)
                vpad_ref[dst, 2 * kvh * HEAD_DIM:(2 * kvh + 1) * HEAD_DIM] = v_ref[src, cols].astype(BF16)
                vpad_ref[dst, (2 * kvh + 1) * HEAD_DIM:(2 * kvh + 2) * HEAD_DIM] = ones
            return carry

        lax.fori_loop(0, nb, body, 0)

    ctx_ones = _ones_column(ck_ref.shape[1])
    r = lax.broadcasted_iota(jnp.int32, (blk, 3 * blk), 0)
    j = lax.broadcasted_iota(jnp.int32, (blk, 3 * blk), 1)
    for u in range(q_ref.shape[0] // blk):
        qb = i * (q_ref.shape[0] // blk) + u
        rows = pl.ds(pl.multiple_of(qb * blk, blk), blk)
        cos = cos_ref[rows, :]
        sin = sin_ref[rows, :]
        band = pl.ds(pl.multiple_of(qb * blk, blk), 3 * blk)
        kpos = j + (qb - 1) * blk
        valid = (j >= r) & (j <= r + 2 * blk) & (kpos >= 0) & (kpos < n_tok)
        valid = jnp.concatenate([valid] * ATT_GROUP, axis=0)
        q_all = q_ref[u * blk:(u + 1) * blk, :]
        for kvh in range(ATT_KV_HEADS):
            cols = slice(kvh * HEAD_DIM, (kvh + 1) * HEAD_DIM)
            q = jnp.concatenate(
                [_rope(q_all[:, (kvh * ATT_GROUP + g) * HEAD_DIM:(kvh * ATT_GROUP + g + 1) * HEAD_DIM],
                       cos, sin, even_group) for g in range(ATT_GROUP)], axis=0).astype(BF16)
            s_ctx = _dot_nt(q, ck_ref[0, :, cols].astype(BF16))
            s_loc = jnp.where(valid, _dot_nt(q, kpad_ref[band, cols]), -jnp.inf)
            v_ctx = jnp.concatenate([cv_ref[0, :, cols].astype(BF16), ctx_ones], axis=1)
            v_loc = vpad_ref[band, 2 * kvh * HEAD_DIM:(2 * kvh + 2) * HEAD_DIM]
            o = _softmax_av([s_ctx, s_loc], [v_ctx, v_loc], _sink_column(sink_ref, kvh, blk))
            for g in range(ATT_GROUP):
                hd = kvh * ATT_GROUP + g
                o_ref[u * blk:(u + 1) * blk, hd * HEAD_DIM:(hd + 1) * HEAD_DIM] = (
                    o[g * blk:(g + 1) * blk, :].astype(BF16))


def _window_attention(proj, cache_k, cache_v, sink, cos, sin_signed, *, n_batch, n_tok):
    tq = ATT_BLOCK
    steps = n_tok // tq
    n_ctx = cache_k.shape[1]
    q_col = 5 * HG_WIDTH // ATT_WIDTH
    k_col = (5 * HG_WIDTH + ATT_WIDTH) // KV_WIDTH
    table = pl.BlockSpec((n_tok, HEAD_DIM), lambda b, i, s: (0, 0))
    cache = pl.BlockSpec((1, n_ctx, KV_WIDTH), lambda b, i, s: (b, 0, 0))
    return pl.pallas_call(
        functools.partial(_win_attn_kernel, n_tok=n_tok),
        out_shape=jax.ShapeDtypeStruct((n_batch * n_tok, ATT_WIDTH), BF16),
        grid_spec=pltpu.PrefetchScalarGridSpec(
            num_scalar_prefetch=1,
            grid=(n_batch, steps),
            in_specs=[pl.BlockSpec((tq, ATT_WIDTH), lambda b, i, s: (b * steps + i, q_col)),
                      pl.BlockSpec((n_tok, KV_WIDTH), lambda b, i, s: (b, k_col)),
                      pl.BlockSpec((n_tok, KV_WIDTH), lambda b, i, s: (b, k_col + 1)),
                      cache, cache, table, table],
            out_specs=pl.BlockSpec((tq, ATT_WIDTH), lambda b, i, s: (b * steps + i, 0)),
            scratch_shapes=[pltpu.VMEM((n_tok + 2 * ATT_BLOCK, KV_WIDTH), BF16),
                            pltpu.VMEM((n_tok + 2 * ATT_BLOCK, 2 * KV_WIDTH), BF16)]),
        compiler_params=_cparams("parallel", "arbitrary"),
        name="window_attention",
    )(sink, proj, proj, proj, cache_k, cache_v, cos, sin_signed)


def _rope_tables(n_tok):
    rows = n_tok // GRID_W
    row = np.repeat(np.arange(rows), GRID_W).astype(np.float32)
    col = np.tile(np.arange(GRID_W), rows).astype(np.float32)
    inv = np.float32(ROPE_BASE) ** (-np.arange(ROPE_FREQS, dtype=np.float32) / np.float32(ROPE_FREQS))
    ar, ac = row[:, None] * inv, col[:, None] * inv
    cr, sr, cc, sc = np.cos(ar), np.sin(ar), np.cos(ac), np.sin(ac)
    return (jnp.asarray(np.concatenate([cr, cr, cc, cc], axis=1), F32),
            jnp.asarray(np.concatenate([-sr, sr, -sc, sc], axis=1), F32))


def _outproj_kernel(hg_ref, att_ref, w_ref, x_ref, mod_ref, g_ref, xo_ref, h_ref):
    chunk = 256
    for r in range(x_ref.shape[0] // chunk):
        sl = slice(r * chunk, (r + 1) * chunk)
        mix = _dot(hg_ref[sl, :], w_ref[0:HG_WIDTH, :]) + _dot(att_ref[sl, :], w_ref[HG_WIDTH:, :])
        x = x_ref[sl, :] + mod_ref[0, 2:3, :] * mix
        xo_ref[sl, :] = x
        h_ref[sl, :] = _norm_modulate(x, g_ref[...], mod_ref[0, 3:4, :], mod_ref[0, 4:5, :]).astype(BF16)


def _out_projection(o_hg, o_att, w_bf16, x, mod, gain, *, mod_base, rows_per_mod):
    m = x.shape[0]
    tm = 512
    row = lambda i: (i, 0)
    return pl.pallas_call(
        _outproj_kernel,
        out_shape=(jax.ShapeDtypeStruct((m, D_MODEL), F32), jax.ShapeDtypeStruct((m, D_MODEL), BF16)),
        grid=(m // tm,),
        in_specs=[pl.BlockSpec((tm, HG_WIDTH), row),
                  pl.BlockSpec((tm, ATT_WIDTH), row),
                  pl.BlockSpec((HG_WIDTH + ATT_WIDTH, D_MODEL), lambda i: (0, 0)),
                  pl.BlockSpec((tm, D_MODEL), row),
                  pl.BlockSpec((1, N_MOD, D_MODEL), lambda i: (mod_base + (i * tm) // rows_per_mod, 0, 0)),
                  pl.BlockSpec((1, D_MODEL), lambda i: (0, 0))],
        out_specs=(pl.BlockSpec((tm, D_MODEL), row), pl.BlockSpec((tm, D_MODEL), row)),
        compiler_params=_cparams("parallel"),
        name="out_projection",
    )(o_hg, o_att, w_bf16, x, mod, gain)


def _prefix_count(x):
    n = x.shape[1]
    i = lax.broadcasted_iota(jnp.int32, (SEL_BLOCK, SEL_BLOCK), 0)
    j = lax.broadcasted_iota(jnp.int32, (SEL_BLOCK, SEL_BLOCK), 1)
    upper = jnp.where(i < j, 1.0, 0.0).astype(BF16)
    off = jnp.zeros((x.shape[0], 1), F32)
    outs = []
    for blk in range(n // SEL_BLOCK):
        xb = x[:, blk * SEL_BLOCK:(blk + 1) * SEL_BLOCK]
        outs.append(_dot(xb.astype(BF16), upper) + off)
        off = off + jnp.sum(xb, axis=-1, keepdims=True)
    return outs[0] if len(outs) == 1 else jnp.concatenate(outs, axis=1)


def _route_select(h_ref, w_ref, cap):
    logits = _dot_nt(w_ref[...], h_ref[...])
    ex = jnp.exp(logits - jnp.max(logits, axis=0, keepdims=True))
    aff = ex / jnp.sum(ex, axis=0, keepdims=True)
    bits = pltpu.bitcast(aff, jnp.int32)

    thr = jnp.zeros((N_EXPERTS, 1), jnp.int32)
    shift = 31
    while shift > 0:
        width = min(4, shift)
        shift -= width
        digit = jnp.zeros((N_EXPERTS, 1), jnp.int32)
        for j in range(1, 2 ** width):
            cnt = jnp.sum(jnp.where(bits >= (thr | (j << shift)), 1.0, 0.0), axis=-1, keepdims=True)
            digit = digit + jnp.where(cnt >= cap, 1, 0)
        thr = thr | (digit << shift)
    above = jnp.where(bits > thr, 1.0, 0.0)
    tied = jnp.where(bits == thr, 1.0, 0.0)
    room = cap - jnp.sum(above, axis=-1, keepdims=True)
    sel = above + tied * jnp.where(_prefix_count(tied) < room, 1.0, 0.0)
    return jnp.where(sel > 0.0, _prefix_count(sel), -1.0), aff


def _one_hot_gather(slot_rows, aff_rows, h_ref, rows, cap):
    n_tok = h_ref.shape[0]
    c = (lax.broadcasted_iota(jnp.int32, (rows, n_tok), 0) & (cap - 1)).astype(F32)
    hit = c == slot_rows
    x = _dot(jnp.where(hit, 1.0, 0.0).astype(BF16), h_ref[...]).astype(BF16)
    return x, jnp.sum(jnp.where(hit, aff_rows, 0.0), axis=-1, keepdims=True)


def _route_request_kernel(h_ref, w_ref, x_ref, g_ref, slot_ref, *, cap):
    n_tok = h_ref.shape[0]
    slot, aff = _route_select(h_ref, w_ref, cap)
    slot_ref[0] = slot
    per_row = lambda a: jnp.concatenate(
        [jnp.broadcast_to(a[e:e + 1, :], (cap, n_tok)) for e in range(N_EXPERTS)], axis=0)
    x, g = _one_hot_gather(per_row(slot), per_row(aff), h_ref, N_EXPERTS * cap, cap)
    for e in range(N_EXPERTS):
        x_ref[e] = x[e * cap:(e + 1) * cap, :]
        g_ref[e] = g[e * cap:(e + 1) * cap, :]


def _route_only_kernel(h_ref, w_ref, slot_ref, aff_ref, bounds_ref, *, cap):
    n_tok = h_ref.shape[0]
    slot, aff = _route_select(h_ref, w_ref, cap)
    slot_ref[0] = slot
    aff_ref[0] = aff
    n = lax.broadcasted_iota(jnp.int32, (n_tok, LANES), 0)
    t = lax.broadcasted_iota(jnp.int32, (n_tok, LANES), 1)
    before = jnp.where(n < t * TOK_BLOCK, 1.0, 0.0).astype(BF16)
    chosen = jnp.where(slot >= 0.0, 1.0, 0.0).astype(BF16)
    bounds_ref[0] = _dot(chosen, before).astype(jnp.int32)


def _gather_window_kernel(bounds_ref, slot_ref, aff_ref, h_ref, xh_ref, gh_ref, x_ref, g_ref, xo_ref, go_ref, *,
                          n_batch, group):
    b = pl.program_id(0)
    eg = pl.program_id(1)
    n_tok = h_ref.shape[0]
    nt = n_tok // TOK_BLOCK
    cap = x_ref.shape[1]
    region = TOK_BLOCK // group

    @pl.when(b >= n_batch)
    def _():
        x_ref[...] = xh_ref[...]
        g_ref[...] = gh_ref[...]

    @pl.when(b < n_batch)
    def _():
        x_ref[...] = jnp.zeros(x_ref.shape, BF16)
        g_ref[...] = jnp.zeros(g_ref.shape, F32)
        local_i = lax.broadcasted_iota(jnp.int32, (region, TOK_BLOCK), 0).astype(F32)
        over_i = lax.broadcasted_iota(jnp.int32, (TOK_BLOCK, TOK_BLOCK), 0)

        def token_block(t, carry):
            toks = pl.ds(pl.multiple_of(t * TOK_BLOCK, TOK_BLOCK), TOK_BLOCK)
            hits, gates, firsts, n_overs, slots, affs = [], [], [], [], [], []
            for k in range(group):
                e = eg * group + k
                start, length = _window(bounds_ref, (b * N_EXPERTS + e) * (nt + 1), t)
                first = jnp.minimum(start, cap - region)
                s = slot_ref[0, pl.ds(e, 1), toks]
                a = aff_ref[0, pl.ds(e, 1), toks]
                hit = local_i == jnp.where(s >= 0.0, s - first.astype(F32), -1.0)
                hits.append(jnp.where(hit, 1.0, 0.0).astype(BF16))
                gates.append(jnp.sum(jnp.where(hit, a, 0.0), axis=-1, keepdims=True))
                firsts.append(first)
                n_overs.append(start + length - first - region)
                slots.append(s)
                affs.append(a)
            xc = _dot(jnp.concatenate(hits, axis=0), h_ref[toks, :]).astype(BF16)
            for k in range(group):
                dst = pl.ds(pl.multiple_of(firsts[k], PIECE), region)
                x_ref[k, dst, :] = x_ref[k, dst, :] + xc[k * region:(k + 1) * region, :]
                g_ref[k, dst, :] = g_ref[k, dst, :] + gates[k]
            for k in range(group):
                @pl.when(n_overs[k] > 0)
                def _(k=k):
                    base = firsts[k] + region
                    hit = (over_i + base).astype(F32) == slots[k]
                    xo_ref[...] = _dot(jnp.where(hit, 1.0, 0.0).astype(BF16), h_ref[toks, :]).astype(BF16)
                    go_ref[...] = jnp.sum(jnp.where(hit, affs[k], 0.0), axis=-1, keepdims=True)

                    def place(i, carry):
                        src = pl.ds(pl.multiple_of(i * PIECE, PIECE), PIECE)
                        dst = pl.ds(pl.multiple_of(base + i * PIECE, PIECE), PIECE)
                        x_ref[k, dst, :] = x_ref[k, dst, :] + xo_ref[src, :]
                        g_ref[k, dst, :] = g_ref[k, dst, :] + go_ref[src, :]
                        return carry

                    lax.fori_loop(0, n_overs[k] // PIECE, place, 0)
            return carry

        lax.fori_loop(0, nt, token_block, 0)


def _route_gather_requests(h, w_router_t, *, n_batch, n_tok):
    cap = CAPACITY_FACTOR * n_tok // N_EXPERTS
    out_block = lambda w: pl.BlockSpec((N_EXPERTS, cap, w), lambda b: (0, b, 0))
    return pl.pallas_call(
        functools.partial(_route_request_kernel, cap=cap),
        out_shape=(jax.ShapeDtypeStruct((N_EXPERTS, n_batch * cap, D_MODEL), BF16),
                   jax.ShapeDtypeStruct((N_EXPERTS, n_batch * cap, 1), F32),
                   jax.ShapeDtypeStruct((n_batch, N_EXPERTS, n_tok), F32)),
        grid=(n_batch,),
        in_specs=[pl.BlockSpec((n_tok, D_MODEL), lambda b: (b, 0)),
                  pl.BlockSpec((N_EXPERTS, D_MODEL), lambda b: (0, 0))],
        out_specs=(out_block(D_MODEL), out_block(1), pl.BlockSpec((1, N_EXPERTS, n_tok), lambda b: (b, 0, 0))),
        compiler_params=_cparams("parallel"),
        name="route_gather_requests",
    )(h, w_router_t)


def _route_only(h, w_router_t, *, n_batch, n_tok):
    cap = CAPACITY_FACTOR * n_tok // N_EXPERTS
    nt = n_tok // TOK_BLOCK
    expert_major = pl.BlockSpec((1, N_EXPERTS, n_tok), lambda b: (b, 0, 0))
    slot, aff, bounds = pl.pallas_call(
        functools.partial(_route_only_kernel, cap=cap),
        out_shape=(jax.ShapeDtypeStruct((n_batch, N_EXPERTS, n_tok), F32),
                   jax.ShapeDtypeStruct((n_batch, N_EXPERTS, n_tok), F32),
                   jax.ShapeDtypeStruct((n_batch, N_EXPERTS, LANES), jnp.int32)),
        grid=(n_batch,),
        in_specs=[pl.BlockSpec((n_tok, D_MODEL), lambda b: (b, 0)),
                  pl.BlockSpec((N_EXPERTS, D_MODEL), lambda b: (0, 0))],
        out_specs=(expert_major, expert_major, pl.BlockSpec((1, N_EXPERTS, LANES), lambda b: (b, 0, 0))),
        compiler_params=_cparams("parallel"),
        name="route_select",
    )(h, w_router_t)
    return slot, aff, bounds[:, :, :nt + 1].reshape(-1)


def _gather_windows(bounds, slot, aff, h, x_head, g_head, *, n_batch, n_tok):
    cap = CAPACITY_FACTOR * n_tok // N_EXPERTS
    group = TOK_BLOCK // SLOT_REGION
    head_rows = x_head.shape[1]
    assert head_rows % cap == 0 and N_EXPERTS % group == 0
    n_head = head_rows // cap
    rows_total = head_rows + n_batch * cap
    last = n_batch - 1
    request = lambda b: jnp.minimum(b, last)
    out_row = lambda b: jnp.where(b < n_batch, n_head + b, b - n_batch)
    head_row = lambda b: jnp.maximum(b - n_batch, 0)
    out_block = lambda w: pl.BlockSpec((group, cap, w), lambda b, g, s: (g, out_row(b), 0))
    head_block = lambda w: pl.BlockSpec((group, cap, w), lambda b, g, s: (g, head_row(b), 0))
    expert_major = pl.BlockSpec((1, N_EXPERTS, n_tok), lambda b, g, s: (request(b), 0, 0))
    return pl.pallas_call(
        functools.partial(_gather_window_kernel, n_batch=n_batch, group=group),
        out_shape=(jax.ShapeDtypeStruct((N_EXPERTS, rows_total, D_MODEL), BF16),
                   jax.ShapeDtypeStruct((N_EXPERTS, rows_total, 1), F32)),
        grid_spec=pltpu.PrefetchScalarGridSpec(
            num_scalar_prefetch=1,
            grid=(n_batch + n_head, N_EXPERTS // group),
            in_specs=[expert_major, expert_major,
                      pl.BlockSpec((n_tok, D_MODEL), lambda b, g, s: (request(b), 0)),
                      head_block(D_MODEL), head_block(1)],
            out_specs=(out_block(D_MODEL), out_block(1)),
            scratch_shapes=[pltpu.VMEM((TOK_BLOCK, D_MODEL), BF16), pltpu.VMEM((TOK_BLOCK, 1), F32)]),
        compiler_params=_cparams("arbitrary", "arbitrary"),
        name="gather_windows",
    )(bounds, slot, aff, h, x_head, g_head)


def _moe_kernel(x_ref, g_ref, wg_ref, wu_ref, wd_ref, y_ref, hid_ref, *, n_ff, rows):
    s = pl.program_id(2)
    tf = wg_ref.shape[2]

    @pl.when(s < n_ff)
    def _():
        wg = wg_ref[0].astype(BF16)
        wu = wu_ref[0].astype(BF16)
        cols = pl.ds(pl.multiple_of(s * tf, tf), tf)
        for r in range(x_ref.shape[1] // rows):
            sl = slice(r * rows, (r + 1) * rows)
            x = x_ref[0, sl, :]
            hid_ref[sl, cols] = (_silu(_dot(x, wg)) * _dot(x, wu)).astype(BF16)

    @pl.when(s >= n_ff)
    def _():
        y = _dot(hid_ref[...], wd_ref[0].astype(BF16))
        y_ref[0] = (y * g_ref[0]).astype(BF16)


def _experts(x, gate, w_gate, w_up, w_down):
    n_rows = x.shape[1]
    tr, tf, tn = n_rows // 2, 512, 256
    n_ff, n_out = EXPERT_FF // tf, D_MODEL // tn
    up_tile = lambda e, r, s: (e, 0, jnp.minimum(s, n_ff - 1))
    out_tile = lambda s: jnp.maximum(s - n_ff, 0)
    return pl.pallas_call(
        functools.partial(_moe_kernel, n_ff=n_ff, rows=256),
        out_shape=jax.ShapeDtypeStruct((N_EXPERTS, n_rows, D_MODEL), BF16),
        grid=(N_EXPERTS, n_rows // tr, n_ff + n_out),
        in_specs=[pl.BlockSpec((1, tr, D_MODEL), lambda e, r, s: (e, r, 0), pipeline_mode=pl.Buffered(1)),
                  pl.BlockSpec((1, tr, 1), lambda e, r, s: (e, r, 0)),
                  pl.BlockSpec((1, D_MODEL, tf), up_tile),
                  pl.BlockSpec((1, D_MODEL, tf), up_tile),
                  pl.BlockSpec((1, EXPERT_FF, tn), lambda e, r, s: (e, 0, out_tile(s)))],
        out_specs=pl.BlockSpec((1, tr, tn), lambda e, r, s: (e, r, out_tile(s))),
        scratch_shapes=[pltpu.VMEM((tr, EXPERT_FF), BF16)],
        compiler_params=_cparams("parallel", "parallel", "arbitrary"),
        name="expert_swiglu",
    )(x, gate, w_gate, w_up, w_down)


def _window(bounds_ref, base, t):
    p0 = bounds_ref[base + t]
    p1 = bounds_ref[base + t + 1]
    start = (p0 // PIECE) * PIECE
    return start, jnp.where(p1 > p0, ((p1 - start + PIECE - 1) // PIECE) * PIECE, 0)


def _combine_kernel(bounds_ref, y_ref, slot_ref, x_ref, mod_ref, g_ref, o_ref, ybuf_ref, obuf_ref, acc_ref, *,
                    nt, region):
    b = pl.program_id(0)
    t = pl.program_id(1)
    tt = x_ref.shape[0]
    cap = y_ref.shape[1]
    local_i = lax.broadcasted_iota(jnp.int32, (region, tt), 0).astype(F32)
    hits = []
    for e in range(N_EXPERTS):
        start, _ = _window(bounds_ref, (b * N_EXPERTS + e) * (nt + 1), t)
        first = jnp.minimum(start, cap - region)
        ybuf_ref[e * region:(e + 1) * region, :] = y_ref[e, pl.ds(pl.multiple_of(first, PIECE), region), :]
        s = slot_ref[0, e:e + 1, :]
        hit = local_i == jnp.where(s >= 0.0, s - first.astype(F32), -1.0)
        hits.append(jnp.where(hit, 1.0, 0.0).astype(BF16))
    acc_ref[...] = _dot_tn(jnp.concatenate(hits, axis=0), ybuf_ref[...])
    over_i = lax.broadcasted_iota(jnp.int32, (TOK_BLOCK, tt), 0)

    def overflow(e, carry):
        start, length = _window(bounds_ref, (b * N_EXPERTS + e) * (nt + 1), t)
        base = jnp.minimum(start, cap - region) + region
        n_over = start + length - base

        @pl.when(n_over > 0)
        def _():
            def copy(i, carry):
                obuf_ref[pl.ds(pl.multiple_of(i * PIECE, PIECE), PIECE), :] = (
                    y_ref[e, pl.ds(pl.multiple_of(base + i * PIECE, PIECE), PIECE), :])
                return carry

            def clear(i, carry):
                obuf_ref[pl.ds(pl.multiple_of(n_over + i * PIECE, PIECE), PIECE), :] = jnp.zeros((PIECE, D_MODEL), BF16)
                return carry

            lax.fori_loop(0, n_over // PIECE, copy, 0)
            lax.fori_loop(0, (TOK_BLOCK - n_over) // PIECE, clear, 0)
            hit = (over_i + base).astype(F32) == slot_ref[0, pl.ds(e, 1), :]
            acc_ref[...] += _dot_tn(jnp.where(hit, 1.0, 0.0).astype(BF16), obuf_ref[...])

        return carry

    lax.fori_loop(0, N_EXPERTS, overflow, 0)
    x = x_ref[...] + mod_ref[0, 5:6, :] * acc_ref[...]
    var = jnp.mean(x * x, axis=-1, keepdims=True)
    o_ref[...] = x * lax.rsqrt(var + NORM_EPS) * g_ref[...]


def _combine(bounds, y, slot, x_mid, mod, final_g, *, n_batch, n_tok, row_block_off, mod_base, mod_per_batch):
    cap = CAPACITY_FACTOR * n_tok // N_EXPERTS
    tt = TOK_BLOCK
    nt = n_tok // tt
    region = min(SLOT_REGION, cap)
    assert cap - region <= TOK_BLOCK
    return pl.pallas_call(
        functools.partial(_combine_kernel, nt=nt, region=region),
        out_shape=jax.ShapeDtypeStruct((n_batch * n_tok, D_MODEL), F32),
        grid_spec=pltpu.PrefetchScalarGridSpec(
            num_scalar_prefetch=1,
            grid=(n_batch, nt),
            in_specs=[pl.BlockSpec((N_EXPERTS, cap, D_MODEL), lambda b, t, s: (0, row_block_off + b, 0)),
                      pl.BlockSpec((1, N_EXPERTS, tt), lambda b, t, s: (b, 0, t)),
                      pl.BlockSpec((tt, D_MODEL), lambda b, t, s: (b * nt + t, 0)),
                      pl.BlockSpec((1, N_MOD, D_MODEL), lambda b, t, s: (mod_base + b * mod_per_batch, 0, 0)),
                      pl.BlockSpec((1, D_MODEL), lambda b, t, s: (0, 0))],
            out_specs=pl.BlockSpec((tt, D_MODEL), lambda b, t, s: (b * nt + t, 0)),
            scratch_shapes=[pltpu.VMEM((N_EXPERTS * region, D_MODEL), BF16),
                            pltpu.VMEM((TOK_BLOCK, D_MODEL), BF16), pltpu.VMEM((tt, D_MODEL), F32)]),
        compiler_params=_cparams("arbitrary", "arbitrary"),
        name="combine_final_norm",
    )(bounds, y, slot, x_mid, mod, final_g)


def kernel(x_prompt, x_sample, cache_k, cache_v, state_hgrn, c, c_ctx, w_ada, b_ada, norm1_g, w_in, hg_lb,
           hg_norm_g, attn_sink, w_out, norm2_g, w_router, w_gate, w_up, w_down, final_norm_g):
    n_p, t_p, _ = x_prompt.shape
    n_s, t_s, _ = x_sample.shape
    assert w_ada.shape[0] == 1 and 1 + n_s <= COND_ROWS
    layer = 0

    cond = jnp.zeros((COND_ROWS, D_MODEL), F32).at[0].set(c_ctx).at[1:1 + n_s].set(c)
    mod = _ada_modulation(cond, w_ada[layer], b_ada[layer]).reshape(COND_ROWS, N_MOD, D_MODEL)
    lb = jnp.cumsum(jax.nn.softmax(hg_lb.astype(F32), axis=0), axis=0)[layer]
    w_in_l = w_in[layer].astype(BF16)
    w_out_b = w_out[layer].astype(BF16)
    w_router_t = w_router[layer].T.astype(BF16)
    norm1 = norm1_g[layer].reshape(1, D_MODEL)
    norm2 = norm2_g[layer].reshape(1, D_MODEL)
    hg_gain = hg_norm_g[layer].reshape(1, HG_WIDTH)
    final_g = final_norm_g.reshape(1, D_MODEL)
    sink = attn_sink[layer]
    cos, sin_signed = _rope_tables(t_s)

    xp = x_prompt.reshape(n_p * t_p, D_MODEL)
    xs = x_sample.reshape(n_s * t_s, D_MODEL)
    groups = dict(p=dict(mod_base=0, rows_per_mod=n_p * t_p), s=dict(mod_base=1, rows_per_mod=t_s))

    proj_p = _in_projection(xp, mod, norm1, w_in_l, **groups["p"])
    proj_s = _in_projection(xs, mod, norm1, w_in_l, **groups["s"])

    ohg_p, new_state = _hgrn(proj_p, lb, hg_gain, None, n_batch=n_p, n_tok=t_p)
    ohg_s, _ = _hgrn(proj_s, lb, hg_gain, state_hgrn[:, layer:layer + 1].astype(F32), n_batch=n_s, n_tok=t_s)

    oatt_p = _context_attention(proj_p, sink, n_batch=n_p, n_tok=t_p)
    n_ctx = cache_k.shape[2]
    oatt_s = _window_attention(proj_s, cache_k[:, layer].reshape(n_s, n_ctx, KV_WIDTH),
                               cache_v[:, layer].reshape(n_s, n_ctx, KV_WIDTH), sink, cos, sin_signed,
                               n_batch=n_s, n_tok=t_s)

    xmid_p, h2_p = _out_projection(ohg_p, oatt_p, w_out_b, xp, mod, norm2, **groups["p"])
    xmid_s, h2_s = _out_projection(ohg_s, oatt_s, w_out_b, xs, mod, norm2, **groups["s"])

    cap_p = CAPACITY_FACTOR * t_p // N_EXPERTS
    cap_s = CAPACITY_FACTOR * t_s // N_EXPERTS
    off_s = n_p * cap_p // cap_s
    xg_p, gate_p, slot_p = _route_gather_requests(h2_p, w_router_t, n_batch=n_p, n_tok=t_p)
    slot_s, aff_s, bounds_s = _route_only(h2_s, w_router_t, n_batch=n_s, n_tok=t_s)
    xg, gate = _gather_windows(bounds_s, slot_s, aff_s, h2_s, xg_p, gate_p, n_batch=n_s, n_tok=t_s)
    bounds_p = jnp.tile(jnp.array([0, cap_p], jnp.int32), n_p * N_EXPERTS)

    y = _experts(xg, gate, w_gate[layer], w_up[layer], w_down[layer])

    y_prompt = _combine(bounds_p, y, slot_p, xmid_p, mod, final_g, n_batch=n_p, n_tok=t_p, row_block_off=0,
                        mod_base=0, mod_per_batch=0)
    y_sample = _combine(bounds_s, y, slot_s, xmid_s, mod, final_g, n_batch=n_s, n_tok=t_s, row_block_off=off_s,
                        mod_base=1, mod_per_batch=1)

    k_col = 5 * HG_WIDTH + ATT_WIDTH
    new_k = proj_p[:, k_col:k_col + KV_WIDTH].reshape(n_p, 1, t_p, ATT_KV_HEADS, HEAD_DIM)
    new_v = proj_p[:, k_col + KV_WIDTH:k_col + 2 * KV_WIDTH].reshape(n_p, 1, t_p, ATT_KV_HEADS, HEAD_DIM)
    return (y_prompt.reshape(n_p, t_p, D_MODEL), y_sample.reshape(n_s, t_s, D_MODEL), new_k, new_v, new_state)
```

```python
import functools

import jax
import jax.numpy as jnp
import numpy as np
from jax import lax
from jax.experimental import pallas as pl
from jax.experimental.pallas import tpu as pltpu

F32 = jnp.float32
BF16 = jnp.bfloat16

D_MODEL = 2048
HG_WIDTH = 1024
HG_HEADS = 8
HEAD_DIM = 128
ATT_HEADS = 8
ATT_KV_HEADS = 2
ATT_GROUP = ATT_HEADS // ATT_KV_HEADS
KV_WIDTH = ATT_KV_HEADS * HEAD_DIM
ATT_WIDTH = ATT_HEADS * HEAD_DIM
ATT_BLOCK = 128
GRID_W = 64
ROPE_BASE = 10000.0
ROPE_FREQS = HEAD_DIM // 4
N_EXPERTS = 16
CAPACITY_FACTOR = 2
EXPERT_FF = 5632
NORM_EPS = 1e-6
IN_WIDTH = 5 * HG_WIDTH + ATT_WIDTH + 2 * KV_WIDTH
N_MOD = 6
COND_ROWS = 16

HG_CHUNK = 128
HG_DIAG = 8
SEL_BLOCK = 256
TOK_BLOCK = 256
PIECE = 16
SLOT_REGION = 64
LANES = 128

V7X_VMEM_BYTES = 64 * 1024 * 1024
VMEM_LIMIT = V7X_VMEM_BYTES - 8 * 1024 * 1024


def _cparams(*sem):
    return pltpu.CompilerParams(dimension_semantics=sem, vmem_limit_bytes=VMEM_LIMIT)


def _sigmoid(x):
    return 1.0 / (1.0 + jnp.exp(-x))


def _silu(x):
    return x * _sigmoid(x)


def _dot(a, b):
    return jnp.dot(a, b, preferred_element_type=F32)


def _dot_nt(a, b):
    return lax.dot_general(a, b, (((1,), (1,)), ((), ())), preferred_element_type=F32)


def _dot_tn(a, b):
    return lax.dot_general(a, b, (((0,), (0,)), ((), ())), preferred_element_type=F32)


def _ada_kernel(c_ref, w_ref, b_ref, o_ref):
    s = _silu(c_ref[...]).astype(BF16)
    o_ref[...] = _dot(s, w_ref[...].astype(BF16)) + b_ref[...]


def _ada_modulation(cond, w_ada, b_ada):
    tn = 1024
    n = w_ada.shape[1]
    return pl.pallas_call(
        _ada_kernel,
        out_shape=jax.ShapeDtypeStruct((COND_ROWS, n), F32),
        grid=(n // tn,),
        in_specs=[pl.BlockSpec((COND_ROWS, D_MODEL), lambda j: (0, 0)),
                  pl.BlockSpec((D_MODEL, tn), lambda j: (0, j)),
                  pl.BlockSpec((1, tn), lambda j: (0, j))],
        out_specs=pl.BlockSpec((COND_ROWS, tn), lambda j: (0, j)),
        compiler_params=_cparams("arbitrary"),
        name="ada_modulation",
    )(cond, w_ada, b_ada.reshape(1, n))


def _norm_modulate(x, gain, shift, scale):
    var = jnp.mean(x * x, axis=-1, keepdims=True)
    return (x * lax.rsqrt(var + NORM_EPS) * gain) * (1.0 + scale) + shift


def _inproj_kernel(x_ref, mod_ref, g_ref, w_ref, o_ref, h_ref, *, rows):
    w = w_ref[...]
    chunks = [slice(r * rows, (r + 1) * rows) for r in range(x_ref.shape[0] // rows)]

    @pl.when(pl.program_id(1) == 0)
    def _():
        shift = mod_ref[0, 0:1, :]
        scale = mod_ref[0, 1:2, :]
        gain = g_ref[...]
        for sl in chunks:
            h = _norm_modulate(x_ref[sl, :], gain, shift, scale).astype(BF16)
            h_ref[sl, :] = h
            o_ref[sl, :] = _dot(h, w)

    @pl.when(pl.program_id(1) > 0)
    def _():
        for sl in chunks:
            o_ref[sl, :] = _dot(h_ref[sl, :], w)


def _in_projection(x, mod, gain, w, *, mod_base, rows_per_mod):
    m = x.shape[0]
    tm, tn = 1024, 512
    return pl.pallas_call(
        functools.partial(_inproj_kernel, rows=256),
        out_shape=jax.ShapeDtypeStruct((m, IN_WIDTH), F32),
        grid=(m // tm, IN_WIDTH // tn),
        in_specs=[pl.BlockSpec((tm, D_MODEL), lambda i, j: (i, 0)),
                  pl.BlockSpec((1, N_MOD, D_MODEL), lambda i, j: (mod_base + (i * tm) // rows_per_mod, 0, 0)),
                  pl.BlockSpec((1, D_MODEL), lambda i, j: (0, 0)),
                  pl.BlockSpec((D_MODEL, tn), lambda i, j: (0, j))],
        out_specs=pl.BlockSpec((tm, tn), lambda i, j: (i, j)),
        scratch_shapes=[pltpu.VMEM((tm, D_MODEL), BF16)],
        compiler_params=_cparams("parallel", "arbitrary"),
        name="in_projection",
    )(x, mod, gain, w)


def _hgrn_codes(reverse):
    L = HG_CHUNK
    t = lax.broadcasted_iota(jnp.int32, (L, L), 0)
    s = lax.broadcasted_iota(jnp.int32, (L, L), 1)
    code = jnp.where(t == s, 1, 0)
    h = L // 2
    while h >= 1:
        same = (t & ~(2 * h - 1)) == (s & ~(2 * h - 1))
        t_hi = (t & h) != 0
        s_hi = (s & h) != 0
        pair = (s_hi & ~t_hi) if reverse else (t_hi & ~s_hi)
        code = jnp.where(same & pair, h * 16, code)
        h //= 2
    return code


def _cumsum_rows(tri_bf16, g):
    g1 = g.astype(BF16)
    g2 = (g - g1.astype(F32)).astype(BF16)
    s = _dot(tri_bf16, jnp.concatenate([g1, g2], axis=1))
    return s[:, :HEAD_DIM] + s[:, HEAD_DIM:]


def _hgrn_intra(q, k, f, b, v_bf, code, reverse):
    L = HG_CHUNK
    G = HG_DIAG
    q_bf = q.astype(BF16)
    k_bf = k.astype(BF16)

    def level(h, ref, att):
        neg_abs = pltpu.bitcast(pltpu.bitcast(b - ref, jnp.int32) | jnp.int32(-2 ** 31), F32)
        e = jnp.exp2(neg_abs.astype(BF16))
        return jnp.where(code == h * 16, _dot_nt(q_bf * e, k_bf * e), att)

    att = jnp.where(code == 1, _dot_nt(q_bf, k_bf), 0.0)
    h = L // 2
    while h >= G:
        parts = []
        for p in range(L // (2 * h)):
            m = p * 2 * h + (h if reverse else h - 1)
            parts.append(jnp.broadcast_to(b[m:m + 1, :], (2 * h, HEAD_DIM)))
        att = level(h, parts[0] if len(parts) == 1 else jnp.concatenate(parts, axis=0), att)
        h //= 2

    b3 = b.reshape(L // G, G, HEAD_DIM)

    def group_row(r):
        return jnp.broadcast_to(b3[:, r:r + 1, :], (L // G, G, HEAD_DIM)).reshape(L, HEAD_DIM)

    row = lax.broadcasted_iota(jnp.int32, (L, HEAD_DIM), 0)
    att = level(4, group_row(4 if reverse else 3), att)
    lo, hi = (2, 6) if reverse else (1, 5)
    att = level(2, jnp.where((row & 4) == 0, group_row(lo), group_row(hi)), att)
    att = jnp.where(code == 16, _dot_nt(q_bf * f.astype(BF16), k_bf), att)
    return _dot(att.astype(BF16), v_bf)


def _hgrn_prepare(q, f, v, code, tri, reverse):
    L = HG_CHUNK
    k = 1.0 - f
    b = _cumsum_rows(tri, jnp.log2(f))
    b_tot = b[0:1, :] if reverse else b[L - 1:L, :]
    v_bf = v.astype(BF16)
    o_intra = _hgrn_intra(q, k, f, b, v_bf, code, reverse)
    q_in = (q * jnp.exp2(b)).astype(BF16)
    k_out = (k * jnp.exp2(b_tot - b)).astype(BF16)
    return o_intra, q_in, k_out, v_bf, jnp.exp2(b_tot)


def _hgrn_advance(st, prepared):
    o_intra, q_in, k_out, v_bf, decay = prepared
    return o_intra + _dot_nt(q_in, st.astype(BF16)), st * decay + _dot_tn(v_bf, k_out)


def _hgrn_kernel(*refs, n_tok, has_state):
    if has_state:
        (q_ref, ff_ref, fb_ref, v_ref, gate_ref, lb_ref, ng_ref, s0_ref,
         o_ref, sout_ref, of_ref, ob_ref, code_ref) = refs
    else:
        (q_ref, ff_ref, fb_ref, v_ref, gate_ref, lb_ref, ng_ref,
         o_ref, sout_ref, of_ref, ob_ref, code_ref) = refs
    L = HG_CHUNK
    nc = n_tok // L
    lb_f = lb_ref[0:1, :]
    lb_b = lb_ref[1:2, :]

    @pl.when((pl.program_id(0) == 0) & (pl.program_id(1) == 0))
    def _():
        code_ref[0] = _hgrn_codes(False)
        code_ref[1] = _hgrn_codes(True)

    code_f = code_ref[0]
    code_b = code_ref[1]
    ti = lax.broadcasted_iota(jnp.int32, (L, L), 0)
    si = lax.broadcasted_iota(jnp.int32, (L, L), 1)
    tri_f = jnp.where(si <= ti, 1.0, 0.0).astype(BF16)
    tri_b = jnp.where(si >= ti, 1.0, 0.0).astype(BF16)

    def prepare(row, f_ref, lb, code, tri, reverse):
        sl = pl.ds(row, L)
        q = _silu(q_ref[sl, :])
        f = lb + (1.0 - lb) * _sigmoid(f_ref[sl, :])
        return sl, _hgrn_prepare(q, f, v_ref[sl, :], code, tri, reverse)

    gain = ng_ref[...]

    def finish(sl, o):
        var = jnp.mean(o * o, axis=-1, keepdims=True)
        o_ref[sl, :] = ((o * lax.rsqrt(var + NORM_EPS) * gain) * _silu(gate_ref[sl, :])).astype(BF16)

    def make_body(other_direction):
        def body(c, states):
            st_f, st_b = states
            fwd, bwd = [], []
            for u in range(unroll):
                cu = c * unroll + u
                fwd.append(prepare(pl.multiple_of(cu * L, L), ff_ref, lb_f, code_f, tri_f, False))
                bwd.append(prepare(pl.multiple_of((nc - 1 - cu) * L, L), fb_ref, lb_b, code_b, tri_b, True))
            out_f, out_b = [], []
            for (sl_f, prep_f), (sl_b, prep_b) in zip(fwd, bwd):
                o_f, st_f = _hgrn_advance(st_f, prep_f)
                o_b, st_b = _hgrn_advance(st_b, prep_b)
                out_f.append((sl_f, o_f))
                out_b.append((sl_b, o_b))
            if other_direction == "later":
                for sl, o in out_f:
                    of_ref[sl, :] = o
                for sl, o in out_b:
                    ob_ref[sl, :] = o
            elif other_direction == "now":
                for u in range(unroll):
                    finish(out_f[u][0], out_f[u][1] + out_b[unroll - 1 - u][1])
            else:
                for sl, o in out_f:
                    finish(sl, o + ob_ref[sl, :])
                for sl, o in out_b:
                    finish(sl, of_ref[sl, :] + o)
            return st_f, st_b
        return body

    unroll = 8 if nc % 16 == 0 else 2
    assert nc % unroll == 0
    n_iter = nc // unroll
    half = n_iter // 2
    if has_state:
        states = (s0_ref[0, 0, 0, 0].T, s0_ref[0, 0, 1, 0].T)
    else:
        states = (jnp.zeros((HEAD_DIM, HEAD_DIM), F32), jnp.zeros((HEAD_DIM, HEAD_DIM), F32))
    states = lax.fori_loop(0, half, make_body("later"), states)
    if n_iter % 2:
        states = make_body("now")(half, states)
    st_f, st_b = lax.fori_loop(half + n_iter % 2, n_iter, make_body("earlier"), states)
    sout_ref[0, 0, 0, 0] = st_f.T
    sout_ref[0, 0, 1, 0] = st_b.T


def _hgrn(proj, lb, norm_g, state, *, n_batch, n_tok):
    col = lambda k: (lambda b, h: (b, k * HG_HEADS + h))
    tok_spec = lambda k: pl.BlockSpec((n_tok, HEAD_DIM), col(k))
    st_spec = pl.BlockSpec((1, 1, 2, 1, HEAD_DIM, HEAD_DIM), lambda b, h: (b, 0, 0, h, 0, 0))
    has_state = state is not None
    return pl.pallas_call(
        functools.partial(_hgrn_kernel, n_tok=n_tok, has_state=has_state),
        out_shape=(jax.ShapeDtypeStruct((n_batch * n_tok, HG_WIDTH), BF16),
                   jax.ShapeDtypeStruct((n_batch, 1, 2, HG_HEADS, HEAD_DIM, HEAD_DIM), F32)),
        grid=(n_batch, HG_HEADS),
        in_specs=[tok_spec(0), tok_spec(1), tok_spec(2), tok_spec(3), tok_spec(4),
                  pl.BlockSpec((2, HEAD_DIM), lambda b, h: (0, h)),
                  pl.BlockSpec((1, HEAD_DIM), lambda b, h: (0, h))] + ([st_spec] if has_state else []),
        out_specs=(pl.BlockSpec((n_tok, HEAD_DIM), lambda b, h: (b, h)), st_spec),
        scratch_shapes=[pltpu.VMEM((n_tok, HEAD_DIM), F32), pltpu.VMEM((n_tok, HEAD_DIM), F32),
                        pltpu.VMEM((2, HG_CHUNK, HG_CHUNK), jnp.int32)],
        compiler_params=_cparams("arbitrary", "arbitrary"),
        name="hgrn2_scan",
    )(proj, proj, proj, proj, proj, lb, norm_g, *((state,) if has_state else ()))


def _stack_heads(x, kvh):
    return jnp.concatenate(
        [x[:, (kvh * ATT_GROUP + g) * HEAD_DIM:(kvh * ATT_GROUP + g + 1) * HEAD_DIM] for g in range(ATT_GROUP)],
        axis=0)


def _sink_column(sink_ref, kvh, rows):
    return jnp.concatenate(
        [jnp.full((rows, 1), sink_ref[kvh * ATT_GROUP + g], F32) for g in range(ATT_GROUP)],
        axis=0) * (1.0 / SOFTMAX_SCALE)


SOFTMAX_SCALE = HEAD_DIM ** -0.5
EXP2_SCALE = SOFTMAX_SCALE * 1.4426950408889634


def _ones_column(rows):
    lane = lax.broadcasted_iota(jnp.int32, (rows, HEAD_DIM), 1)
    return jnp.where(lane == 0, 1.0, 0.0).astype(BF16)


def _softmax_av(scores, values, sink_col):
    m = sink_col
    for s in scores:
        m = jnp.maximum(m, jnp.max(s, axis=-1, keepdims=True))
    acc = None
    for s, v in zip(scores, values):
        pv = _dot(jnp.exp2(((s - m) * EXP2_SCALE).astype(BF16)), v)
        acc = pv if acc is None else acc + pv
    denom = acc[:, HEAD_DIM:HEAD_DIM + 1] + jnp.exp2((sink_col - m) * EXP2_SCALE)
    return acc[:, :HEAD_DIM] / denom


def _ctx_attn_kernel(sink_ref, q_ref, k_ref, v_ref, o_ref):
    rows = q_ref.shape[0]
    q_all = q_ref[...]
    ones = _ones_column(k_ref.shape[0])
    for kvh in range(ATT_KV_HEADS):
        q = _stack_heads(q_all, kvh).astype(BF16)
        k = k_ref[:, kvh * HEAD_DIM:(kvh + 1) * HEAD_DIM].astype(BF16)
        v = jnp.concatenate([v_ref[:, kvh * HEAD_DIM:(kvh + 1) * HEAD_DIM].astype(BF16), ones], axis=1)
        o = _softmax_av([_dot_nt(q, k)], [v], _sink_column(sink_ref, kvh, rows))
        for g in range(ATT_GROUP):
            hd = kvh * ATT_GROUP + g
            o_ref[:, hd * HEAD_DIM:(hd + 1) * HEAD_DIM] = o[g * rows:(g + 1) * rows, :].astype(BF16)


def _context_attention(proj, sink, *, n_batch, n_tok):
    q_col = 5 * HG_WIDTH // ATT_WIDTH
    k_col = (5 * HG_WIDTH + ATT_WIDTH) // KV_WIDTH
    return pl.pallas_call(
        _ctx_attn_kernel,
        out_shape=jax.ShapeDtypeStruct((n_batch * n_tok, ATT_WIDTH), BF16),
        grid_spec=pltpu.PrefetchScalarGridSpec(
            num_scalar_prefetch=1,
            grid=(n_batch,),
            in_specs=[pl.BlockSpec((n_tok, ATT_WIDTH), lambda b, s: (b, q_col)),
                      pl.BlockSpec((n_tok, KV_WIDTH), lambda b, s: (b, k_col)),
                      pl.BlockSpec((n_tok, KV_WIDTH), lambda b, s: (b, k_col + 1))],
            out_specs=pl.BlockSpec((n_tok, ATT_WIDTH), lambda b, s: (b, 0))),
        compiler_params=_cparams("parallel"),
        name="context_attention",
    )(sink, proj, proj, proj)


def _rope(x, cos, sin_signed, even_group):
    partner = jnp.where(even_group, pltpu.roll(x, HEAD_DIM - ROPE_FREQS, 1), pltpu.roll(x, ROPE_FREQS, 1))
    return x * cos + partner * sin_signed


def _win_attn_kernel(sink_ref, q_ref, k_ref, v_ref, ck_ref, cv_ref, cos_ref, sin_ref, o_ref,
                     kpad_ref, vpad_ref, *, n_tok):
    blk = ATT_BLOCK
    nb = n_tok // blk
    i = pl.program_id(1)
    lane = lax.broadcasted_iota(jnp.int32, (blk, HEAD_DIM), 1)
    even_group = (lane & ROPE_FREQS) == 0

    @pl.when(i == 0)
    def _():
        kpad_ref[0:blk, :] = jnp.zeros((blk, KV_WIDTH), BF16)
        kpad_ref[blk + n_tok:2 * blk + n_tok, :] = jnp.zeros((blk, KV_WIDTH), BF16)
        vpad_ref[0:blk, :] = jnp.zeros((blk, 2 * KV_WIDTH), BF16)
        vpad_ref[blk + n_tok:2 * blk + n_tok, :] = jnp.zeros((blk, 2 * KV_WIDTH), BF16)
        ones = _ones_column(blk)

        def body(r, carry):
            src = pl.ds(pl.multiple_of(r * blk, blk), blk)
            dst = pl.ds(pl.multiple_of((r + 1) * blk, blk), blk)
            cos = cos_ref[src, :]
            sin = sin_ref[src, :]
            for kvh in range(ATT_KV_HEADS):
                cols = slice(kvh * HEAD_DIM, (kvh + 1) * HEAD_DIM)
                kpad_ref[dst, cols] = _rope(k_ref[src, cols], cos, sin, even_group).astype(BF16)
                vpad_ref[dst, 2 * kvh * HEAD_DIM:(2 * kvh + 1) * HEAD_DIM] = v_ref[src, cols].astype(BF16)
                vpad_ref[dst, (2 * kvh + 1) * HEAD_DIM:(2 * kvh + 2) * HEAD_DIM] = ones
            return carry

        lax.fori_loop(0, nb, body, 0)

    ctx_ones = _ones_column(ck_ref.shape[1])
    r = lax.broadcasted_iota(jnp.int32, (blk, 3 * blk), 0)
    j = lax.broadcasted_iota(jnp.int32, (blk, 3 * blk), 1)
    for u in range(q_ref.shape[0] // blk):
        qb = i * (q_ref.shape[0] // blk) + u
        rows = pl.ds(pl.multiple_of(qb * blk, blk), blk)
        cos = cos_ref[rows, :]
        sin = sin_ref[rows, :]
        band = pl.ds(pl.multiple_of(qb * blk, blk), 3 * blk)
        kpos = j + (qb - 1) * blk
        valid = (j >= r) & (j <= r + 2 * blk) & (kpos >= 0) & (kpos < n_tok)
        valid = jnp.concatenate([valid] * ATT_GROUP, axis=0)
        q_all = q_ref[u * blk:(u + 1) * blk, :]
        for kvh in range(ATT_KV_HEADS):
            cols = slice(kvh * HEAD_DIM, (kvh + 1) * HEAD_DIM)
            q = jnp.concatenate(
                [_rope(q_all[:, (kvh * ATT_GROUP + g) * HEAD_DIM:(kvh * ATT_GROUP + g + 1) * HEAD_DIM],
                       cos, sin, even_group) for g in range(ATT_GROUP)], axis=0).astype(BF16)
            s_ctx = _dot_nt(q, ck_ref[0, :, cols].astype(BF16))
            s_loc = jnp.where(valid, _dot_nt(q, kpad_ref[band, cols]), -jnp.inf)
            v_ctx = jnp.concatenate([cv_ref[0, :, cols].astype(BF16), ctx_ones], axis=1)
            v_loc = vpad_ref[band, 2 * kvh * HEAD_DIM:(2 * kvh + 2) * HEAD_DIM]
            o = _softmax_av([s_ctx, s_loc], [v_ctx, v_loc], _sink_column(sink_ref, kvh, blk))
            for g in range(ATT_GROUP):
                hd = kvh * ATT_GROUP + g
                o_ref[u * blk:(u + 1) * blk, hd * HEAD_DIM:(hd + 1) * HEAD_DIM] = (
                    o[g * blk:(g + 1) * blk, :].astype(BF16))


def _window_attention(proj, cache_k, cache_v, sink, cos, sin_signed, *, n_batch, n_tok):
    tq = ATT_BLOCK
    steps = n_tok // tq
    n_ctx = cache_k.shape[1]
    q_col = 5 * HG_WIDTH // ATT_WIDTH
    k_col = (5 * HG_WIDTH + ATT_WIDTH) // KV_WIDTH
    table = pl.BlockSpec((n_tok, HEAD_DIM), lambda b, i, s: (0, 0))
    cache = pl.BlockSpec((1, n_ctx, KV_WIDTH), lambda b, i, s: (b, 0, 0))
    return pl.pallas_call(
        functools.partial(_win_attn_kernel, n_tok=n_tok),
        out_shape=jax.ShapeDtypeStruct((n_batch * n_tok, ATT_WIDTH), BF16),
        grid_spec=pltpu.PrefetchScalarGridSpec(
            num_scalar_prefetch=1,
            grid=(n_batch, steps),
            in_specs=[pl.BlockSpec((tq, ATT_WIDTH), lambda b, i, s: (b * steps + i, q_col)),
                      pl.BlockSpec((n_tok, KV_WIDTH), lambda b, i, s: (b, k_col)),
                      pl.BlockSpec((n_tok, KV_WIDTH), lambda b, i, s: (b, k_col + 1)),
                      cache, cache, table, table],
            out_specs=pl.BlockSpec((tq, ATT_WIDTH), lambda b, i, s: (b * steps + i, 0)),
            scratch_shapes=[pltpu.VMEM((n_tok + 2 * ATT_BLOCK, KV_WIDTH), BF16),
                            pltpu.VMEM((n_tok + 2 * ATT_BLOCK, 2 * KV_WIDTH), BF16)]),
        compiler_params=_cparams("parallel", "arbitrary"),
        name="window_attention",
    )(sink, proj, proj, proj, cache_k, cache_v, cos, sin_signed)


def _rope_tables(n_tok):
    rows = n_tok // GRID_W
    row = np.repeat(np.arange(rows), GRID_W).astype(np.float32)
    col = np.tile(np.arange(GRID_W), rows).astype(np.float32)
    inv = np.float32(ROPE_BASE) ** (-np.arange(ROPE_FREQS, dtype=np.float32) / np.float32(ROPE_FREQS))
    ar, ac = row[:, None] * inv, col[:, None] * inv
    cr, sr, cc, sc = np.cos(ar), np.sin(ar), np.cos(ac), np.sin(ac)
    return (jnp.asarray(np.concatenate([cr, cr, cc, cc], axis=1), F32),
            jnp.asarray(np.concatenate([-sr, sr, -sc, sc], axis=1), F32))


def _outproj_kernel(hg_ref, att_ref, w_ref, x_ref, mod_ref, g_ref, xo_ref, h_ref):
    chunk = 256
    for r in range(x_ref.shape[0] // chunk):
        sl = slice(r * chunk, (r + 1) * chunk)
        mix = _dot(hg_ref[sl, :], w_ref[0:HG_WIDTH, :]) + _dot(att_ref[sl, :], w_ref[HG_WIDTH:, :])
        x = x_ref[sl, :] + mod_ref[0, 2:3, :] * mix
        xo_ref[sl, :] = x
        h_ref[sl, :] = _norm_modulate(x, g_ref[...], mod_ref[0, 3:4, :], mod_ref[0, 4:5, :]).astype(BF16)


def _out_projection(o_hg, o_att, w_bf16, x, mod, gain, *, mod_base, rows_per_mod):
    m = x.shape[0]
    tm = 512
    row = lambda i: (i, 0)
    return pl.pallas_call(
        _outproj_kernel,
        out_shape=(jax.ShapeDtypeStruct((m, D_MODEL), F32), jax.ShapeDtypeStruct((m, D_MODEL), BF16)),
        grid=(m // tm,),
        in_specs=[pl.BlockSpec((tm, HG_WIDTH), row),
                  pl.BlockSpec((tm, ATT_WIDTH), row),
                  pl.BlockSpec((HG_WIDTH + ATT_WIDTH, D_MODEL), lambda i: (0, 0)),
                  pl.BlockSpec((tm, D_MODEL), row),
                  pl.BlockSpec((1, N_MOD, D_MODEL), lambda i: (mod_base + (i * tm) // rows_per_mod, 0, 0)),
                  pl.BlockSpec((1, D_MODEL), lambda i: (0, 0))],
        out_specs=(pl.BlockSpec((tm, D_MODEL), row), pl.BlockSpec((tm, D_MODEL), row)),
        compiler_params=_cparams("parallel"),
        name="out_projection",
    )(o_hg, o_att, w_bf16, x, mod, gain)


def _prefix_count(x):
    n = x.shape[1]
    i = lax.broadcasted_iota(jnp.int32, (SEL_BLOCK, SEL_BLOCK), 0)
    j = lax.broadcasted_iota(jnp.int32, (SEL_BLOCK, SEL_BLOCK), 1)
    upper = jnp.where(i < j, 1.0, 0.0).astype(BF16)
    off = jnp.zeros((x.shape[0], 1), F32)
    outs = []
    for blk in range(n // SEL_BLOCK):
        xb = x[:, blk * SEL_BLOCK:(blk + 1) * SEL_BLOCK]
        outs.append(_dot(xb.astype(BF16), upper) + off)
        off = off + jnp.sum(xb, axis=-1, keepdims=True)
    return outs[0] if len(outs) == 1 else jnp.concatenate(outs, axis=1)


def _route_select(h_ref, w_ref, cap):
    logits = _dot_nt(w_ref[...], h_ref[...])
    ex = jnp.exp(logits - jnp.max(logits, axis=0, keepdims=True))
    aff = ex / jnp.sum(ex, axis=0, keepdims=True)
    bits = pltpu.bitcast(aff, jnp.int32)

    thr = jnp.zeros((N_EXPERTS, 1), jnp.int32)
    shift = 31
    while shift > 0:
        width = min(4, shift)
        shift -= width
        digit = jnp.zeros((N_EXPERTS, 1), jnp.int32)
        for j in range(1, 2 ** width):
            cnt = jnp.sum(jnp.where(bits >= (thr | (j << shift)), 1.0, 0.0), axis=-1, keepdims=True)
            digit = digit + jnp.where(cnt >= cap, 1, 0)
        thr = thr | (digit << shift)
    above = jnp.where(bits > thr, 1.0, 0.0)
    tied = jnp.where(bits == thr, 1.0, 0.0)
    room = cap - jnp.sum(above, axis=-1, keepdims=True)
    sel = above + tied * jnp.where(_prefix_count(tied) < room, 1.0, 0.0)
    return jnp.where(sel > 0.0, _prefix_count(sel), -1.0), aff


def _one_hot_gather(slot_rows, aff_rows, h_ref, rows, cap):
    n_tok = h_ref.shape[0]
    c = (lax.broadcasted_iota(jnp.int32, (rows, n_tok), 0) & (cap - 1)).astype(F32)
    hit = c == slot_rows
    x = _dot(jnp.where(hit, 1.0, 0.0).astype(BF16), h_ref[...]).astype(BF16)
    return x, jnp.sum(jnp.where(hit, aff_rows, 0.0), axis=-1, keepdims=True)


def _route_request_kernel(h_ref, w_ref, x_ref, g_ref, slot_ref, *, cap):
    n_tok = h_ref.shape[0]
    slot, aff = _route_select(h_ref, w_ref, cap)
    slot_ref[0] = slot
    per_row = lambda a: jnp.concatenate(
        [jnp.broadcast_to(a[e:e + 1, :], (cap, n_tok)) for e in range(N_EXPERTS)], axis=0)
    x, g = _one_hot_gather(per_row(slot), per_row(aff), h_ref, N_EXPERTS * cap, cap)
    for e in range(N_EXPERTS):
        x_ref[e] = x[e * cap:(e + 1) * cap, :]
        g_ref[e] = g[e * cap:(e + 1) * cap, :]


def _route_only_kernel(h_ref, w_ref, slot_ref, aff_ref, bounds_ref, *, cap):
    n_tok = h_ref.shape[0]
    slot, aff = _route_select(h_ref, w_ref, cap)
    slot_ref[0] = slot
    aff_ref[0] = aff
    n = lax.broadcasted_iota(jnp.int32, (n_tok, LANES), 0)
    t = lax.broadcasted_iota(jnp.int32, (n_tok, LANES), 1)
    before = jnp.where(n < t * TOK_BLOCK, 1.0, 0.0).astype(BF16)
    chosen = jnp.where(slot >= 0.0, 1.0, 0.0).astype(BF16)
    bounds_ref[0] = _dot(chosen, before).astype(jnp.int32)


def _gather_window_kernel(bounds_ref, slot_ref, aff_ref, h_ref, xh_ref, gh_ref, x_ref, g_ref, xo_ref, go_ref, *,
                          n_batch, group):
    b = pl.program_id(0)
    eg = pl.program_id(1)
    n_tok = h_ref.shape[0]
    nt = n_tok // TOK_BLOCK
    cap = x_ref.shape[1]
    region = TOK_BLOCK // group

    @pl.when(b >= n_batch)
    def _():
        x_ref[...] = xh_ref[...]
        g_ref[...] = gh_ref[...]

    @pl.when(b < n_batch)
    def _():
        x_ref[...] = jnp.zeros(x_ref.shape, BF16)
        g_ref[...] = jnp.zeros(g_ref.shape, F32)
        local_i = lax.broadcasted_iota(jnp.int32, (region, TOK_BLOCK), 0).astype(F32)
        over_i = lax.broadcasted_iota(jnp.int32, (TOK_BLOCK, TOK_BLOCK), 0)

        def token_block(t, carry):
            toks = pl.ds(pl.multiple_of(t * TOK_BLOCK, TOK_BLOCK), TOK_BLOCK)
            hits, gates, firsts, n_overs, slots, affs = [], [], [], [], [], []
            for k in range(group):
                e = eg * group + k
                start, length = _window(bounds_ref, (b * N_EXPERTS + e) * (nt + 1), t)
                first = jnp.minimum(start, cap - region)
                s = slot_ref[0, pl.ds(e, 1), toks]
                a = aff_ref[0, pl.ds(e, 1), toks]
                hit = local_i == jnp.where(s >= 0.0, s - first.astype(F32), -1.0)
                hits.append(jnp.where(hit, 1.0, 0.0).astype(BF16))
                gates.append(jnp.sum(jnp.where(hit, a, 0.0), axis=-1, keepdims=True))
                firsts.append(first)
                n_overs.append(start + length - first - region)
                slots.append(s)
                affs.append(a)
            xc = _dot(jnp.concatenate(hits, axis=0), h_ref[toks, :]).astype(BF16)
            for k in range(group):
                dst = pl.ds(pl.multiple_of(firsts[k], PIECE), region)
                x_ref[k, dst, :] = x_ref[k, dst, :] + xc[k * region:(k + 1) * region, :]
                g_ref[k, dst, :] = g_ref[k, dst, :] + gates[k]
            for k in range(group):
                @pl.when(n_overs[k] > 0)
                def _(k=k):
                    base = firsts[k] + region
                    hit = (over_i + base).astype(F32) == slots[k]
                    xo_ref[...] = _dot(jnp.where(hit, 1.0, 0.0).astype(BF16), h_ref[toks, :]).astype(BF16)
                    go_ref[...] = jnp.sum(jnp.where(hit, affs[k], 0.0), axis=-1, keepdims=True)

                    def place(i, carry):
                        src = pl.ds(pl.multiple_of(i * PIECE, PIECE), PIECE)
                        dst = pl.ds(pl.multiple_of(base + i * PIECE, PIECE), PIECE)
                        x_ref[k, dst, :] = x_ref[k, dst, :] + xo_ref[src, :]
                        g_ref[k, dst, :] = g_ref[k, dst, :] + go_ref[src, :]
                        return carry

                    lax.fori_loop(0, n_overs[k] // PIECE, place, 0)
            return carry

        lax.fori_loop(0, nt, token_block, 0)


def _route_gather_requests(h, w_router_t, *, n_batch, n_tok):
    cap = CAPACITY_FACTOR * n_tok // N_EXPERTS
    out_block = lambda w: pl.BlockSpec((N_EXPERTS, cap, w), lambda b: (0, b, 0))
    return pl.pallas_call(
        functools.partial(_route_request_kernel, cap=cap),
        out_shape=(jax.ShapeDtypeStruct((N_EXPERTS, n_batch * cap, D_MODEL), BF16),
                   jax.ShapeDtypeStruct((N_EXPERTS, n_batch * cap, 1), F32),
                   jax.ShapeDtypeStruct((n_batch, N_EXPERTS, n_tok), F32)),
        grid=(n_batch,),
        in_specs=[pl.BlockSpec((n_tok, D_MODEL), lambda b: (b, 0)),
                  pl.BlockSpec((N_EXPERTS, D_MODEL), lambda b: (0, 0))],
        out_specs=(out_block(D_MODEL), out_block(1), pl.BlockSpec((1, N_EXPERTS, n_tok), lambda b: (b, 0, 0))),
        compiler_params=_cparams("parallel"),
        name="route_gather_requests",
    )(h, w_router_t)


def _route_only(h, w_router_t, *, n_batch, n_tok):
    cap = CAPACITY_FACTOR * n_tok // N_EXPERTS
    nt = n_tok // TOK_BLOCK
    expert_major = pl.BlockSpec((1, N_EXPERTS, n_tok), lambda b: (b, 0, 0))
    slot, aff, bounds = pl.pallas_call(
        functools.partial(_route_only_kernel, cap=cap),
        out_shape=(jax.ShapeDtypeStruct((n_batch, N_EXPERTS, n_tok), F32),
                   jax.ShapeDtypeStruct((n_batch, N_EXPERTS, n_tok), F32),
                   jax.ShapeDtypeStruct((n_batch, N_EXPERTS, LANES), jnp.int32)),
        grid=(n_batch,),
        in_specs=[pl.BlockSpec((n_tok, D_MODEL), lambda b: (b, 0)),
                  pl.BlockSpec((N_EXPERTS, D_MODEL), lambda b: (0, 0))],
        out_specs=(expert_major, expert_major, pl.BlockSpec((1, N_EXPERTS, LANES), lambda b: (b, 0, 0))),
        compiler_params=_cparams("parallel"),
        name="route_select",
    )(h, w_router_t)
    return slot, aff, bounds[:, :, :nt + 1].reshape(-1)


def _gather_windows(bounds, slot, aff, h, x_head, g_head, *, n_batch, n_tok):
    cap = CAPACITY_FACTOR * n_tok // N_EXPERTS
    group = TOK_BLOCK // SLOT_REGION
    head_rows = x_head.shape[1]
    assert head_rows % cap == 0 and N_EXPERTS % group == 0
    n_head = head_rows // cap
    rows_total = head_rows + n_batch * cap
    last = n_batch - 1
    request = lambda b: jnp.minimum(b, last)
    out_row = lambda b: jnp.where(b < n_batch, n_head + b, b - n_batch)
    head_row = lambda b: jnp.maximum(b - n_batch, 0)
    out_block = lambda w: pl.BlockSpec((group, cap, w), lambda b, g, s: (g, out_row(b), 0))
    head_block = lambda w: pl.BlockSpec((group, cap, w), lambda b, g, s: (g, head_row(b), 0))
    expert_major = pl.BlockSpec((1, N_EXPERTS, n_tok), lambda b, g, s: (request(b), 0, 0))
    return pl.pallas_call(
        functools.partial(_gather_window_kernel, n_batch=n_batch, group=group),
        out_shape=(jax.ShapeDtypeStruct((N_EXPERTS, rows_total, D_MODEL), BF16),
                   jax.ShapeDtypeStruct((N_EXPERTS, rows_total, 1), F32)),
        grid_spec=pltpu.PrefetchScalarGridSpec(
            num_scalar_prefetch=1,
            grid=(n_batch + n_head, N_EXPERTS // group),
            in_specs=[expert_major, expert_major,
                      pl.BlockSpec((n_tok, D_MODEL), lambda b, g, s: (request(b), 0)),
                      head_block(D_MODEL), head_block(1)],
            out_specs=(out_block(D_MODEL), out_block(1)),
            scratch_shapes=[pltpu.VMEM((TOK_BLOCK, D_MODEL), BF16), pltpu.VMEM((TOK_BLOCK, 1), F32)]),
        compiler_params=_cparams("arbitrary", "arbitrary"),
        name="gather_windows",
    )(bounds, slot, aff, h, x_head, g_head)


def _moe_kernel(x_ref, g_ref, wg_ref, wu_ref, wd_ref, y_ref, hid_ref, *, n_ff, rows):
    s = pl.program_id(2)
    tf = wg_ref.shape[2]

    @pl.when(s < n_ff)
    def _():
        wg = wg_ref[0].astype(BF16)
        wu = wu_ref[0].astype(BF16)
        cols = pl.ds(pl.multiple_of(s * tf, tf), tf)
        for r in range(x_ref.shape[1] // rows):
            sl = slice(r * rows, (r + 1) * rows)
            x = x_ref[0, sl, :]
            hid_ref[sl, cols] = (_silu(_dot(x, wg)) * _dot(x, wu)).astype(BF16)

    @pl.when(s >= n_ff)
    def _():
        y = _dot(hid_ref[...], wd_ref[0].astype(BF16))
        y_ref[0] = (y * g_ref[0]).astype(BF16)


def _experts(x, gate, w_gate, w_up, w_down):
    n_rows = x.shape[1]
    tr, tf, tn = n_rows // 2, 512, 256
    n_ff, n_out = EXPERT_FF // tf, D_MODEL // tn
    up_tile = lambda e, r, s: (e, 0, jnp.minimum(s, n_ff - 1))
    out_tile = lambda s: jnp.maximum(s - n_ff, 0)
    return pl.pallas_call(
        functools.partial(_moe_kernel, n_ff=n_ff, rows=256),
        out_shape=jax.ShapeDtypeStruct((N_EXPERTS, n_rows, D_MODEL), BF16),
        grid=(N_EXPERTS, n_rows // tr, n_ff + n_out),
        in_specs=[pl.BlockSpec((1, tr, D_MODEL), lambda e, r, s: (e, r, 0), pipeline_mode=pl.Buffered(1)),
                  pl.BlockSpec((1, tr, 1), lambda e, r, s: (e, r, 0)),
                  pl.BlockSpec((1, D_MODEL, tf), up_tile),
                  pl.BlockSpec((1, D_MODEL, tf), up_tile),
                  pl.BlockSpec((1, EXPERT_FF, tn), lambda e, r, s: (e, 0, out_tile(s)))],
        out_specs=pl.BlockSpec((1, tr, tn), lambda e, r, s: (e, r, out_tile(s))),
        scratch_shapes=[pltpu.VMEM((tr, EXPERT_FF), BF16)],
        compiler_params=_cparams("parallel", "parallel", "arbitrary"),
        name="expert_swiglu",
    )(x, gate, w_gate, w_up, w_down)


def _window(bounds_ref, base, t):
    p0 = bounds_ref[base + t]
    p1 = bounds_ref[base + t + 1]
    start = (p0 // PIECE) * PIECE
    return start, jnp.where(p1 > p0, ((p1 - start + PIECE - 1) // PIECE) * PIECE, 0)


def _combine_kernel(bounds_ref, y_ref, slot_ref, x_ref, mod_ref, g_ref, o_ref, ybuf_ref, obuf_ref, acc_ref, *,
                    nt, region):
    b = pl.program_id(0)
    t = pl.program_id(1)
    tt = x_ref.shape[0]
    cap = y_ref.shape[1]
    local_i = lax.broadcasted_iota(jnp.int32, (region, tt), 0).astype(F32)
    hits = []
    for e in range(N_EXPERTS):
        start, _ = _window(bounds_ref, (b * N_EXPERTS + e) * (nt + 1), t)
        first = jnp.minimum(start, cap - region)
        ybuf_ref[e * region:(e + 1) * region, :] = y_ref[e, pl.ds(pl.multiple_of(first, PIECE), region), :]
        s = slot_ref[0, e:e + 1, :]
        hit = local_i == jnp.where(s >= 0.0, s - first.astype(F32), -1.0)
        hits.append(jnp.where(hit, 1.0, 0.0).astype(BF16))
    acc_ref[...] = _dot_tn(jnp.concatenate(hits, axis=0), ybuf_ref[...])
    over_i = lax.broadcasted_iota(jnp.int32, (TOK_BLOCK, tt), 0)

    def overflow(e, carry):
        start, length = _window(bounds_ref, (b * N_EXPERTS + e) * (nt + 1), t)
        base = jnp.minimum(start, cap - region) + region
        n_over = start + length - base

        @pl.when(n_over > 0)
        def _():
            def copy(i, carry):
                obuf_ref[pl.ds(pl.multiple_of(i * PIECE, PIECE), PIECE), :] = (
                    y_ref[e, pl.ds(pl.multiple_of(base + i * PIECE, PIECE), PIECE), :])
                return carry

            def clear(i, carry):
                obuf_ref[pl.ds(pl.multiple_of(n_over + i * PIECE, PIECE), PIECE), :] = jnp.zeros((PIECE, D_MODEL), BF16)
                return carry

            lax.fori_loop(0, n_over // PIECE, copy, 0)
            lax.fori_loop(0, (TOK_BLOCK - n_over) // PIECE, clear, 0)
            hit = (over_i + base).astype(F32) == slot_ref[0, pl.ds(e, 1), :]
            acc_ref[...] += _dot_tn(jnp.where(hit, 1.0, 0.0).astype(BF16), obuf_ref[...])

        return carry

    lax.fori_loop(0, N_EXPERTS, overflow, 0)
    x = x_ref[...] + mod_ref[0, 5:6, :] * acc_ref[...]
    var = jnp.mean(x * x, axis=-1, keepdims=True)
    o_ref[...] = x * lax.rsqrt(var + NORM_EPS) * g_ref[...]


def _combine(bounds, y, slot, x_mid, mod, final_g, *, n_batch, n_tok, row_block_off, mod_base, mod_per_batch):
    cap = CAPACITY_FACTOR * n_tok // N_EXPERTS
    tt = TOK_BLOCK
    nt = n_tok // tt
    region = min(SLOT_REGION, cap)
    assert cap - region <= TOK_BLOCK
    return pl.pallas_call(
        functools.partial(_combine_kernel, nt=nt, region=region),
        out_shape=jax.ShapeDtypeStruct((n_batch * n_tok, D_MODEL), F32),
        grid_spec=pltpu.PrefetchScalarGridSpec(
            num_scalar_prefetch=1,
            grid=(n_batch, nt),
            in_specs=[pl.BlockSpec((N_EXPERTS, cap, D_MODEL), lambda b, t, s: (0, row_block_off + b, 0)),
                      pl.BlockSpec((1, N_EXPERTS, tt), lambda b, t, s: (b, 0, t)),
                      pl.BlockSpec((tt, D_MODEL), lambda b, t, s: (b * nt + t, 0)),
                      pl.BlockSpec((1, N_MOD, D_MODEL), lambda b, t, s: (mod_base + b * mod_per_batch, 0, 0)),
                      pl.BlockSpec((1, D_MODEL), lambda b, t, s: (0, 0))],
            out_specs=pl.BlockSpec((tt, D_MODEL), lambda b, t, s: (b * nt + t, 0)),
            scratch_shapes=[pltpu.VMEM((N_EXPERTS * region, D_MODEL), BF16),
                            pltpu.VMEM((TOK_BLOCK, D_MODEL), BF16), pltpu.VMEM((tt, D_MODEL), F32)]),
        compiler_params=_cparams("arbitrary", "arbitrary"),
        name="combine_final_norm",
    )(bounds, y, slot, x_mid, mod, final_g)


def kernel(x_prompt, x_sample, cache_k, cache_v, state_hgrn, c, c_ctx, w_ada, b_ada, norm1_g, w_in, hg_lb,
           hg_norm_g, attn_sink, w_out, norm2_g, w_router, w_gate, w_up, w_down, final_norm_g):
    n_p, t_p, _ = x_prompt.shape
    n_s, t_s, _ = x_sample.shape
    assert w_ada.shape[0] == 1 and 1 + n_s <= COND_ROWS
    layer = 0

    cond = jnp.zeros((COND_ROWS, D_MODEL), F32).at[0].set(c_ctx).at[1:1 + n_s].set(c)
    mod = _ada_modulation(cond, w_ada[layer], b_ada[layer]).reshape(COND_ROWS, N_MOD, D_MODEL)
    lb = jnp.cumsum(jax.nn.softmax(hg_lb.astype(F32), axis=0), axis=0)[layer]
    w_in_l = w_in[layer].astype(BF16)
    w_out_b = w_out[layer].astype(BF16)
    w_router_t = w_router[layer].T.astype(BF16)
    norm1 = norm1_g[layer].reshape(1, D_MODEL)
    norm2 = norm2_g[layer].reshape(1, D_MODEL)
    hg_gain = hg_norm_g[layer].reshape(1, HG_WIDTH)
    final_g = final_norm_g.reshape(1, D_MODEL)
    sink = attn_sink[layer]
    cos, sin_signed = _rope_tables(t_s)

    xp = x_prompt.reshape(n_p * t_p, D_MODEL)
    xs = x_sample.reshape(n_s * t_s, D_MODEL)
    groups = dict(p=dict(mod_base=0, rows_per_mod=n_p * t_p), s=dict(mod_base=1, rows_per_mod=t_s))

    proj_p = _in_projection(xp, mod, norm1, w_in_l, **groups["p"])
    proj_s = _in_projection(xs, mod, norm1, w_in_l, **groups["s"])

    ohg_p, new_state = _hgrn(proj_p, lb, hg_gain, None, n_batch=n_p, n_tok=t_p)
    ohg_s, _ = _hgrn(proj_s, lb, hg_gain, state_hgrn[:, layer:layer + 1].astype(F32), n_batch=n_s, n_tok=t_s)

    oatt_p = _context_attention(proj_p, sink, n_batch=n_p, n_tok=t_p)
    n_ctx = cache_k.shape[2]
    oatt_s = _window_attention(proj_s, cache_k[:, layer].reshape(n_s, n_ctx, KV_WIDTH),
                               cache_v[:, layer].reshape(n_s, n_ctx, KV_WIDTH), sink, cos, sin_signed,
                               n_batch=n_s, n_tok=t_s)

    xmid_p, h2_p = _out_projection(ohg_p, oatt_p, w_out_b, xp, mod, norm2, **groups["p"])
    xmid_s, h2_s = _out_projection(ohg_s, oatt_s, w_out_b, xs, mod, norm2, **groups["s"])

    cap_p = CAPACITY_FACTOR * t_p // N_EXPERTS
    cap_s = CAPACITY_FACTOR * t_s // N_EXPERTS
    off_s = n_p * cap_p // cap_s
    xg_p, gate_p, slot_p = _route_gather_requests(h2_p, w_router_t, n_batch=n_p, n_tok=t_p)
    slot_s, aff_s, bounds_s = _route_only(h2_s, w_router_t, n_batch=n_s, n_tok=t_s)
    xg, gate = _gather_windows(bounds_s, slot_s, aff_s, h2_s, xg_p, gate_p, n_batch=n_s, n_tok=t_s)
    bounds_p = jnp.tile(jnp.array([0, cap_p], jnp.int32), n_p * N_EXPERTS)

    y = _experts(xg, gate, w_gate[layer], w_up[layer], w_down[layer])

    y_prompt = _combine(bounds_p, y, slot_p, xmid_p, mod, final_g, n_batch=n_p, n_tok=t_p, row_block_off=0,
                        mod_base=0, mod_per_batch=0)
    y_sample = _combine(bounds_s, y, slot_s, xmid_s, mod, final_g, n_batch=n_s, n_tok=t_s, row_block_off=off_s,
                        mod_base=1, mod_per_batch=1)

    k_col = 5 * HG_WIDTH + ATT_WIDTH
    new_k = proj_p[:, k_col:k_col + KV_WIDTH].reshape(n_p, 1, t_p, ATT_KV_HEADS, HEAD_DIM)
    new_v = proj_p[:, k_col + KV_WIDTH:k_col + 2 * KV_WIDTH].reshape(n_p, 1, t_p, ATT_KV_HEADS, HEAD_DIM)
    return (y_prompt.reshape(n_p, t_p, D_MODEL), y_sample.reshape(n_s, t_s, D_MODEL), new_k, new_v, new_state)
```

```python
import functools

import jax
import jax.numpy as jnp
import numpy as np
from jax import lax
from jax.experimental import pallas as pl
from jax.experimental.pallas import tpu as pltpu

F32 = jnp.float32
BF16 = jnp.bfloat16

D_MODEL = 2048
HG_WIDTH = 1024
HG_HEADS = 8
HEAD_DIM = 128
ATT_HEADS = 8
ATT_KV_HEADS = 2
ATT_GROUP = ATT_HEADS // ATT_KV_HEADS
KV_WIDTH = ATT_KV_HEADS * HEAD_DIM
ATT_WIDTH = ATT_HEADS * HEAD_DIM
ATT_BLOCK = 128
GRID_W = 64
ROPE_BASE = 10000.0
ROPE_FREQS = HEAD_DIM // 4
N_EXPERTS = 16
CAPACITY_FACTOR = 2
EXPERT_FF = 5632
NORM_EPS = 1e-6
IN_WIDTH = 5 * HG_WIDTH + ATT_WIDTH + 2 * KV_WIDTH
N_MOD = 6
COND_ROWS = 16

HG_CHUNK = 128
HG_DIAG = 8
SEL_BLOCK = 256
TOK_BLOCK = 256
PIECE = 16
SLOT_REGION = 64
LANES = 128

V7X_VMEM_BYTES = 64 * 1024 * 1024
VMEM_LIMIT = V7X_VMEM_BYTES - 8 * 1024 * 1024


def _cparams(*sem):
    return pltpu.CompilerParams(dimension_semantics=sem, vmem_limit_bytes=VMEM_LIMIT)


def _sigmoid(x):
    return 1.0 / (1.0 + jnp.exp(-x))


def _silu(x):
    return x * _sigmoid(x)


def _dot(a, b):
    return jnp.dot(a, b, preferred_element_type=F32)


def _dot_nt(a, b):
    return lax.dot_general(a, b, (((1,), (1,)), ((), ())), preferred_element_type=F32)


def _dot_tn(a, b):
    return lax.dot_general(a, b, (((0,), (0,)), ((), ())), preferred_element_type=F32)


def _ada_kernel(c_ref, w_ref, b_ref, o_ref):
    s = _silu(c_ref[...]).astype(BF16)
    o_ref[...] = _dot(s, w_ref[...].astype(BF16)) + b_ref[...]


def _ada_modulation(cond, w_ada, b_ada):
    tn = 1024
    n = w_ada.shape[1]
    return pl.pallas_call(
        _ada_kernel,
        out_shape=jax.ShapeDtypeStruct((COND_ROWS, n), F32),
        grid=(n // tn,),
        in_specs=[pl.BlockSpec((COND_ROWS, D_MODEL), lambda j: (0, 0)),
                  pl.BlockSpec((D_MODEL, tn), lambda j: (0, j)),
                  pl.BlockSpec((1, tn), lambda j: (0, j))],
        out_specs=pl.BlockSpec((COND_ROWS, tn), lambda j: (0, j)),
        compiler_params=_cparams("arbitrary"),
        name="ada_modulation",
    )(cond, w_ada, b_ada.reshape(1, n))


def _norm_modulate(x, gain, shift, scale):
    var = jnp.mean(x * x, axis=-1, keepdims=True)
    return (x * lax.rsqrt(var + NORM_EPS) * gain) * (1.0 + scale) + shift


def _inproj_kernel(x_ref, mod_ref, g_ref, w_ref, o_ref, h_ref, *, rows):
    w = w_ref[...]
    chunks = [slice(r * rows, (r + 1) * rows) for r in range(x_ref.shape[0] // rows)]

    @pl.when(pl.program_id(1) == 0)
    def _():
        shift = mod_ref[0, 0:1, :]
        scale = mod_ref[0, 1:2, :]
        gain = g_ref[...]
        for sl in chunks:
            h = _norm_modulate(x_ref[sl, :], gain, shift, scale).astype(BF16)
            h_ref[sl, :] = h
            o_ref[sl, :] = _dot(h, w)

    @pl.when(pl.program_id(1) > 0)
    def _():
        for sl in chunks:
            o_ref[sl, :] = _dot(h_ref[sl, :], w)


def _in_projection(x, mod, gain, w, *, mod_base, rows_per_mod):
    m = x.shape[0]
    tm, tn = 1024, 512
    return pl.pallas_call(
        functools.partial(_inproj_kernel, rows=256),
        out_shape=jax.ShapeDtypeStruct((m, IN_WIDTH), F32),
        grid=(m // tm, IN_WIDTH // tn),
        in_specs=[pl.BlockSpec((tm, D_MODEL), lambda i, j: (i, 0)),
                  pl.BlockSpec((1, N_MOD, D_MODEL), lambda i, j: (mod_base + (i * tm) // rows_per_mod, 0, 0)),
                  pl.BlockSpec((1, D_MODEL), lambda i, j: (0, 0)),
                  pl.BlockSpec((D_MODEL, tn), lambda i, j: (0, j))],
        out_specs=pl.BlockSpec((tm, tn), lambda i, j: (i, j)),
        scratch_shapes=[pltpu.VMEM((tm, D_MODEL), BF16)],
        compiler_params=_cparams("parallel", "arbitrary"),
        name="in_projection",
    )(x, mod, gain, w)


def _hgrn_codes(reverse):
    L = HG_CHUNK
    t = lax.broadcasted_iota(jnp.int32, (L, L), 0)
    s = lax.broadcasted_iota(jnp.int32, (L, L), 1)
    code = jnp.where(t == s, 1, 0)
    h = L // 2
    while h >= 1:
        same = (t & ~(2 * h - 1)) == (s & ~(2 * h - 1))
        t_hi = (t & h) != 0
        s_hi = (s & h) != 0
        pair = (s_hi & ~t_hi) if reverse else (t_hi & ~s_hi)
        code = jnp.where(same & pair, h * 16, code)
        h //= 2
    return code


def _cumsum_rows(tri_bf16, g):
    g1 = g.astype(BF16)
    g2 = (g - g1.astype(F32)).astype(BF16)
    s = _dot(tri_bf16, jnp.concatenate([g1, g2], axis=1))
    return s[:, :HEAD_DIM] + s[:, HEAD_DIM:]


def _hgrn_intra(q, k, f, b, v_bf, code, reverse):
    L = HG_CHUNK
    G = HG_DIAG
    q_bf = q.astype(BF16)
    k_bf = k.astype(BF16)

    def level(h, ref, att):
        neg_abs = pltpu.bitcast(pltpu.bitcast(b - ref, jnp.int32) | jnp.int32(-2 ** 31), F32)
        e = jnp.exp2(neg_abs.astype(BF16))
        return jnp.where(code == h * 16, _dot_nt(q_bf * e, k_bf * e), att)

    att = jnp.where(code == 1, _dot_nt(q_bf, k_bf), 0.0)
    h = L // 2
    while h >= G:
        parts = []
        for p in range(L // (2 * h)):
            m = p * 2 * h + (h if reverse else h - 1)
            parts.append(jnp.broadcast_to(b[m:m + 1, :], (2 * h, HEAD_DIM)))
        att = level(h, parts[0] if len(parts) == 1 else jnp.concatenate(parts, axis=0), att)
        h //= 2

    b3 = b.reshape(L // G, G, HEAD_DIM)

    def group_row(r):
        return jnp.broadcast_to(b3[:, r:r + 1, :], (L // G, G, HEAD_DIM)).reshape(L, HEAD_DIM)

    row = lax.broadcasted_iota(jnp.int32, (L, HEAD_DIM), 0)
    att = level(4, group_row(4 if reverse else 3), att)
    lo, hi = (2, 6) if reverse else (1, 5)
    att = level(2, jnp.where((row & 4) == 0, group_row(lo), group_row(hi)), att)
    att = jnp.where(code == 16, _dot_nt(q_bf * f.astype(BF16), k_bf), att)
    return _dot(att.astype(BF16), v_bf)


def _hgrn_prepare(q, f, v, code, tri, reverse):
    L = HG_CHUNK
    k = 1.0 - f
    b = _cumsum_rows(tri, jnp.log2(f))
    b_tot = b[0:1, :] if reverse else b[L - 1:L, :]
    v_bf = v.astype(BF16)
    o_intra = _hgrn_intra(q, k, f, b, v_bf, code, reverse)
    q_in = (q * jnp.exp2(b)).astype(BF16)
    k_out = (k * jnp.exp2(b_tot - b)).astype(BF16)
    return o_intra, q_in, k_out, v_bf, jnp.exp2(b_tot)


def _hgrn_advance(st, prepared):
    o_intra, q_in, k_out, v_bf, decay = prepared
    return o_intra + _dot_nt(q_in, st.astype(BF16)), st * decay + _dot_tn(v_bf, k_out)


def _hgrn_kernel(*refs, n_tok, has_state, heads):
    if has_state:
        (q_ref, ff_ref, fb_ref, v_ref, gate_ref, lb_ref, ng_ref, s0_ref,
         o_ref, sout_ref, of_ref, ob_ref, code_ref) = refs
    else:
        (q_ref, ff_ref, fb_ref, v_ref, gate_ref, lb_ref, ng_ref,
         o_ref, sout_ref, of_ref, ob_ref, code_ref) = refs
    L = HG_CHUNK
    nc = n_tok // L

    @pl.when((pl.program_id(0) == 0) & (pl.program_id(1) == 0))
    def _():
        code_ref[0] = _hgrn_codes(False)
        code_ref[1] = _hgrn_codes(True)

    code_f = code_ref[0]
    code_b = code_ref[1]
    ti = lax.broadcasted_iota(jnp.int32, (L, L), 0)
    si = lax.broadcasted_iota(jnp.int32, (L, L), 1)
    tri_f = jnp.where(si <= ti, 1.0, 0.0).astype(BF16)
    tri_b = jnp.where(si >= ti, 1.0, 0.0).astype(BF16)

    def one_head(hd, carry):
        cols = pl.ds(pl.multiple_of(hd * HEAD_DIM, HEAD_DIM), HEAD_DIM)
        lb_f = lb_ref[0:1, cols]
        lb_b = lb_ref[1:2, cols]

        def prepare(row, f_ref, lb, code, tri, reverse):
            sl = pl.ds(row, L)
            q = _silu(q_ref[sl, cols])
            f = lb + (1.0 - lb) * _sigmoid(f_ref[sl, cols])
            return sl, _hgrn_prepare(q, f, v_ref[sl, cols], code, tri, reverse)

        gain = ng_ref[:, cols]

        def finish(sl, o):
            var = jnp.mean(o * o, axis=-1, keepdims=True)
            o = (o * lax.rsqrt(var + NORM_EPS) * gain) * _silu(gate_ref[sl, cols])
            o_ref[sl, cols] = o.astype(BF16)

        def make_body(other_direction):
            def body(c, states):
                st_f, st_b = states
                fwd, bwd = [], []
                for u in range(unroll):
                    cu = c * unroll + u
                    fwd.append(prepare(pl.multiple_of(cu * L, L), ff_ref, lb_f, code_f, tri_f, False))
                    bwd.append(prepare(pl.multiple_of((nc - 1 - cu) * L, L), fb_ref, lb_b, code_b, tri_b, True))
                out_f, out_b = [], []
                for (sl_f, prep_f), (sl_b, prep_b) in zip(fwd, bwd):
                    o_f, st_f = _hgrn_advance(st_f, prep_f)
                    o_b, st_b = _hgrn_advance(st_b, prep_b)
                    out_f.append((sl_f, o_f))
                    out_b.append((sl_b, o_b))
                if other_direction == "later":
                    for sl, o in out_f:
                        of_ref[sl, :] = o
                    for sl, o in out_b:
                        ob_ref[sl, :] = o
                elif other_direction == "now":
                    for u in range(unroll):
                        finish(out_f[u][0], out_f[u][1] + out_b[unroll - 1 - u][1])
                else:
                    for sl, o in out_f:
                        finish(sl, o + ob_ref[sl, :])
                    for sl, o in out_b:
                        finish(sl, of_ref[sl, :] + o)
                return st_f, st_b
            return body

        unroll = 8 if nc % 16 == 0 else 2
        assert nc % unroll == 0
        n_iter = nc // unroll
        half = n_iter // 2
        if has_state:
            states = (s0_ref[0, 0, 0, hd].T, s0_ref[0, 0, 1, hd].T)
        else:
            states = (jnp.zeros((HEAD_DIM, HEAD_DIM), F32), jnp.zeros((HEAD_DIM, HEAD_DIM), F32))
        states = lax.fori_loop(0, half, make_body("later"), states)
        if n_iter % 2:
            states = make_body("now")(half, states)
        st_f, st_b = lax.fori_loop(half + n_iter % 2, n_iter, make_body("earlier"), states)
        sout_ref[0, 0, 0, hd] = st_f.T
        sout_ref[0, 0, 1, hd] = st_b.T
        return carry

    lax.fori_loop(0, heads, one_head, 0)


def _hgrn(proj, lb, norm_g, state, *, n_batch, n_tok):
    heads = HG_HEADS
    while heads > 1 and 5 * 2 * n_tok * heads * HEAD_DIM * 4 > VMEM_LIMIT // 2:
        heads //= 2
    width = heads * HEAD_DIM
    groups = HG_HEADS // heads
    col = lambda k: (lambda b, h: (b, k * groups + h))
    tok_spec = lambda k: pl.BlockSpec((n_tok, width), col(k))
    st_spec = pl.BlockSpec((1, 1, 2, heads, HEAD_DIM, HEAD_DIM), lambda b, h: (b, 0, 0, h, 0, 0))
    has_state = state is not None
    return pl.pallas_call(
        functools.partial(_hgrn_kernel, n_tok=n_tok, has_state=has_state, heads=heads),
        out_shape=(jax.ShapeDtypeStruct((n_batch * n_tok, HG_WIDTH), BF16),
                   jax.ShapeDtypeStruct((n_batch, 1, 2, HG_HEADS, HEAD_DIM, HEAD_DIM), F32)),
        grid=(n_batch, groups),
        in_specs=[tok_spec(0), tok_spec(1), tok_spec(2), tok_spec(3), tok_spec(4),
                  pl.BlockSpec((2, width), lambda b, h: (0, h)),
                  pl.BlockSpec((1, width), lambda b, h: (0, h))] + ([st_spec] if has_state else []),
        out_specs=(pl.BlockSpec((n_tok, width), lambda b, h: (b, h)), st_spec),
        scratch_shapes=[pltpu.VMEM((n_tok, HEAD_DIM), F32), pltpu.VMEM((n_tok, HEAD_DIM), F32),
                        pltpu.VMEM((2, HG_CHUNK, HG_CHUNK), jnp.int32)],
        compiler_params=_cparams("arbitrary", "arbitrary"),
        name="hgrn2_scan",
    )(proj, proj, proj, proj, proj, lb, norm_g, *((state,) if has_state else ()))


def _stack_heads(x, kvh):
    return jnp.concatenate(
        [x[:, (kvh * ATT_GROUP + g) * HEAD_DIM:(kvh * ATT_GROUP + g + 1) * HEAD_DIM] for g in range(ATT_GROUP)],
        axis=0)


def _sink_column(sink_ref, kvh, rows):
    return jnp.concatenate(
        [jnp.full((rows, 1), sink_ref[kvh * ATT_GROUP + g], F32) for g in range(ATT_GROUP)],
        axis=0) * (1.0 / SOFTMAX_SCALE)


SOFTMAX_SCALE = HEAD_DIM ** -0.5
EXP2_SCALE = SOFTMAX_SCALE * 1.4426950408889634


def _ones_column(rows):
    lane = lax.broadcasted_iota(jnp.int32, (rows, HEAD_DIM), 1)
    return jnp.where(lane == 0, 1.0, 0.0).astype(BF16)


def _softmax_av(scores, values, sink_col):
    m = sink_col
    for s in scores:
        m = jnp.maximum(m, jnp.max(s, axis=-1, keepdims=True))
    acc = None
    for s, v in zip(scores, values):
        pv = _dot(jnp.exp2(((s - m) * EXP2_SCALE).astype(BF16)), v)
        acc = pv if acc is None else acc + pv
    denom = acc[:, HEAD_DIM:HEAD_DIM + 1] + jnp.exp2((sink_col - m) * EXP2_SCALE)
    return acc[:, :HEAD_DIM] / denom


def _ctx_attn_kernel(sink_ref, q_ref, k_ref, v_ref, o_ref):
    rows = q_ref.shape[0]
    q_all = q_ref[...]
    ones = _ones_column(k_ref.shape[0])
    for kvh in range(ATT_KV_HEADS):
        q = _stack_heads(q_all, kvh).astype(BF16)
        k = k_ref[:, kvh * HEAD_DIM:(kvh + 1) * HEAD_DIM].astype(BF16)
        v = jnp.concatenate([v_ref[:, kvh * HEAD_DIM:(kvh + 1) * HEAD_DIM].astype(BF16), ones], axis=1)
        o = _softmax_av([_dot_nt(q, k)], [v], _sink_column(sink_ref, kvh, rows))
        for g in range(ATT_GROUP):
            hd = kvh * ATT_GROUP + g
            o_ref[:, hd * HEAD_DIM:(hd + 1) * HEAD_DIM] = o[g * rows:(g + 1) * rows, :].astype(BF16)


def _context_attention(proj, sink, *, n_batch, n_tok):
    q_col = 5 * HG_WIDTH // ATT_WIDTH
    k_col = (5 * HG_WIDTH + ATT_WIDTH) // KV_WIDTH
    return pl.pallas_call(
        _ctx_attn_kernel,
        out_shape=jax.ShapeDtypeStruct((n_batch * n_tok, ATT_WIDTH), BF16),
        grid_spec=pltpu.PrefetchScalarGridSpec(
            num_scalar_prefetch=1,
            grid=(n_batch,),
            in_specs=[pl.BlockSpec((n_tok, ATT_WIDTH), lambda b, s: (b, q_col)),
                      pl.BlockSpec((n_tok, KV_WIDTH), lambda b, s: (b, k_col)),
                      pl.BlockSpec((n_tok, KV_WIDTH), lambda b, s: (b, k_col + 1))],
            out_specs=pl.BlockSpec((n_tok, ATT_WIDTH), lambda b, s: (b, 0))),
        compiler_params=_cparams("parallel"),
        name="context_attention",
    )(sink, proj, proj, proj)


def _rope(x, cos, sin_signed, even_group):
    partner = jnp.where(even_group, pltpu.roll(x, HEAD_DIM - ROPE_FREQS, 1), pltpu.roll(x, ROPE_FREQS, 1))
    return x * cos + partner * sin_signed


def _win_attn_kernel(sink_ref, q_ref, k_ref, v_ref, ck_ref, cv_ref, cos_ref, sin_ref, o_ref,
                     kpad_ref, vpad_ref, *, n_tok):
    blk = ATT_BLOCK
    nb = n_tok // blk
    i = pl.program_id(1)
    lane = lax.broadcasted_iota(jnp.int32, (blk, HEAD_DIM), 1)
    even_group = (lane & ROPE_FREQS) == 0

    @pl.when(i == 0)
    def _():
        kpad_ref[0:blk, :] = jnp.zeros((blk, KV_WIDTH), BF16)
        kpad_ref[blk + n_tok:2 * blk + n_tok, :] = jnp.zeros((blk, KV_WIDTH), BF16)
        vpad_ref[0:blk, :] = jnp.zeros((blk, 2 * KV_WIDTH), BF16)
        vpad_ref[blk + n_tok:2 * blk + n_tok, :] = jnp.zeros((blk, 2 * KV_WIDTH), BF16)
        ones = _ones_column(blk)

        def body(r, carry):
            src = pl.ds(pl.multiple_of(r * blk, blk), blk)
            dst = pl.ds(pl.multiple_of((r + 1) * blk, blk), blk)
            cos = cos_ref[src, :]
            sin = sin_ref[src, :]
            for kvh in range(ATT_KV_HEADS):
                cols = slice(kvh * HEAD_DIM, (kvh + 1) * HEAD_DIM)
                kpad_ref[dst, cols] = _rope(k_ref[src, cols], cos, sin, even_group).astype(BF16)
                vpad_ref[dst, 2 * kvh * HEAD_DIM:(2 * kvh + 1) * HEAD_DIM] = v_ref[src, cols].astype(BF16)
                vpad_ref[dst, (2 * kvh + 1) * HEAD_DIM:(2 * kvh + 2) * HEAD_DIM] = ones
            return carry

        lax.fori_loop(0, nb, body, 0)

    ctx_ones = _ones_column(ck_ref.shape[1])
    r = lax.broadcasted_iota(jnp.int32, (blk, 3 * blk), 0)
    j = lax.broadcasted_iota(jnp.int32, (blk, 3 * blk), 1)
    blocks = q_ref.shape[0] // blk

    def query_block(u, carry):
        qb = i * blocks + u
        rows = pl.ds(pl.multiple_of(qb * blk, blk), blk)
        cos = cos_ref[rows, :]
        sin = sin_ref[rows, :]
        band = pl.ds(pl.multiple_of(qb * blk, blk), 3 * blk)
        kpos = j + (qb - 1) * blk
        valid = (j >= r) & (j <= r + 2 * blk) & (kpos >= 0) & (kpos < n_tok)
        valid = jnp.concatenate([valid] * ATT_GROUP, axis=0)
        local = pl.ds(pl.multiple_of(u * blk, blk), blk)
        q_all = q_ref[local, :]
        for kvh in range(ATT_KV_HEADS):
            cols = slice(kvh * HEAD_DIM, (kvh + 1) * HEAD_DIM)
            q = jnp.concatenate(
                [_rope(q_all[:, (kvh * ATT_GROUP + g) * HEAD_DIM:(kvh * ATT_GROUP + g + 1) * HEAD_DIM],
                       cos, sin, even_group) for g in range(ATT_GROUP)], axis=0).astype(BF16)
            s_ctx = _dot_nt(q, ck_ref[0, :, cols].astype(BF16))
            s_loc = jnp.where(valid, _dot_nt(q, kpad_ref[band, cols]), -jnp.inf)
            v_ctx = jnp.concatenate([cv_ref[0, :, cols].astype(BF16), ctx_ones], axis=1)
            v_loc = vpad_ref[band, 2 * kvh * HEAD_DIM:(2 * kvh + 2) * HEAD_DIM]
            o = _softmax_av([s_ctx, s_loc], [v_ctx, v_loc], _sink_column(sink_ref, kvh, blk))
            for g in range(ATT_GROUP):
                hd = kvh * ATT_GROUP + g
                o_ref[local, hd * HEAD_DIM:(hd + 1) * HEAD_DIM] = o[g * blk:(g + 1) * blk, :].astype(BF16)
        return carry

    lax.fori_loop(0, blocks, query_block, 0)


def _window_attention(proj, cache_k, cache_v, sink, cos, sin_signed, *, n_batch, n_tok):
    tq = 4 * ATT_BLOCK
    steps = n_tok // tq
    n_ctx = cache_k.shape[1]
    q_col = 5 * HG_WIDTH // ATT_WIDTH
    k_col = (5 * HG_WIDTH + ATT_WIDTH) // KV_WIDTH
    table = pl.BlockSpec((n_tok, HEAD_DIM), lambda b, i, s: (0, 0))
    cache = pl.BlockSpec((1, n_ctx, KV_WIDTH), lambda b, i, s: (b, 0, 0))
    return pl.pallas_call(
        functools.partial(_win_attn_kernel, n_tok=n_tok),
        out_shape=jax.ShapeDtypeStruct((n_batch * n_tok, ATT_WIDTH), BF16),
        grid_spec=pltpu.PrefetchScalarGridSpec(
            num_scalar_prefetch=1,
            grid=(n_batch, steps),
            in_specs=[pl.BlockSpec((tq, ATT_WIDTH), lambda b, i, s: (b * steps + i, q_col)),
                      pl.BlockSpec((n_tok, KV_WIDTH), lambda b, i, s: (b, k_col)),
                      pl.BlockSpec((n_tok, KV_WIDTH), lambda b, i, s: (b, k_col + 1)),
                      cache, cache, table, table],
            out_specs=pl.BlockSpec((tq, ATT_WIDTH), lambda b, i, s: (b * steps + i, 0)),
            scratch_shapes=[pltpu.VMEM((n_tok + 2 * ATT_BLOCK, KV_WIDTH), BF16),
                            pltpu.VMEM((n_tok + 2 * ATT_BLOCK, 2 * KV_WIDTH), BF16)]),
        compiler_params=_cparams("parallel", "arbitrary"),
        name="window_attention",
    )(sink, proj, proj, proj, cache_k, cache_v, cos, sin_signed)


def _rope_tables(n_tok):
    rows = n_tok // GRID_W
    row = np.repeat(np.arange(rows), GRID_W).astype(np.float32)
    col = np.tile(np.arange(GRID_W), rows).astype(np.float32)
    inv = np.float32(ROPE_BASE) ** (-np.arange(ROPE_FREQS, dtype=np.float32) / np.float32(ROPE_FREQS))
    ar, ac = row[:, None] * inv, col[:, None] * inv
    cr, sr, cc, sc = np.cos(ar), np.sin(ar), np.cos(ac), np.sin(ac)
    return (jnp.asarray(np.concatenate([cr, cr, cc, cc], axis=1), F32),
            jnp.asarray(np.concatenate([-sr, sr, -sc, sc], axis=1), F32))


def _outproj_kernel(hg_ref, att_ref, w_ref, x_ref, mod_ref, g_ref, xo_ref, h_ref):
    chunk = 256
    for r in range(x_ref.shape[0] // chunk):
        sl = slice(r * chunk, (r + 1) * chunk)
        mix = _dot(hg_ref[sl, :], w_ref[0:HG_WIDTH, :]) + _dot(att_ref[sl, :], w_ref[HG_WIDTH:, :])
        x = x_ref[sl, :] + mod_ref[0, 2:3, :] * mix
        xo_ref[sl, :] = x
        h_ref[sl, :] = _norm_modulate(x, g_ref[...], mod_ref[0, 3:4, :], mod_ref[0, 4:5, :]).astype(BF16)


def _out_projection(o_hg, o_att, w_bf16, x, mod, gain, *, mod_base, rows_per_mod):
    m = x.shape[0]
    tm = 512
    row = lambda i: (i, 0)
    return pl.pallas_call(
        _outproj_kernel,
        out_shape=(jax.ShapeDtypeStruct((m, D_MODEL), F32), jax.ShapeDtypeStruct((m, D_MODEL), BF16)),
        grid=(m // tm,),
        in_specs=[pl.BlockSpec((tm, HG_WIDTH), row),
                  pl.BlockSpec((tm, ATT_WIDTH), row),
                  pl.BlockSpec((HG_WIDTH + ATT_WIDTH, D_MODEL), lambda i: (0, 0)),
                  pl.BlockSpec((tm, D_MODEL), row),
                  pl.BlockSpec((1, N_MOD, D_MODEL), lambda i: (mod_base + (i * tm) // rows_per_mod, 0, 0)),
                  pl.BlockSpec((1, D_MODEL), lambda i: (0, 0))],
        out_specs=(pl.BlockSpec((tm, D_MODEL), row), pl.BlockSpec((tm, D_MODEL), row)),
        compiler_params=_cparams("parallel"),
        name="out_projection",
    )(o_hg, o_att, w_bf16, x, mod, gain)


def _prefix_count(x):
    n = x.shape[1]
    i = lax.broadcasted_iota(jnp.int32, (SEL_BLOCK, SEL_BLOCK), 0)
    j = lax.broadcasted_iota(jnp.int32, (SEL_BLOCK, SEL_BLOCK), 1)
    upper = jnp.where(i < j, 1.0, 0.0).astype(BF16)
    off = jnp.zeros((x.shape[0], 1), F32)
    outs = []
    for blk in range(n // SEL_BLOCK):
        xb = x[:, blk * SEL_BLOCK:(blk + 1) * SEL_BLOCK]
        outs.append(_dot(xb.astype(BF16), upper) + off)
        off = off + jnp.sum(xb, axis=-1, keepdims=True)
    return outs[0] if len(outs) == 1 else jnp.concatenate(outs, axis=1)


def _route_select(h_ref, w_ref, cap):
    logits = _dot_nt(w_ref[...], h_ref[...])
    ex = jnp.exp(logits - jnp.max(logits, axis=0, keepdims=True))
    aff = ex / jnp.sum(ex, axis=0, keepdims=True)
    bits = pltpu.bitcast(aff, jnp.int32)

    thr = jnp.zeros((N_EXPERTS, 1), jnp.int32)
    shift = 31
    while shift > 0:
        width = min(4, shift)
        shift -= width
        digit = jnp.zeros((N_EXPERTS, 1), jnp.int32)
        for j in range(1, 2 ** width):
            cnt = jnp.sum(jnp.where(bits >= (thr | (j << shift)), 1.0, 0.0), axis=-1, keepdims=True)
            digit = digit + jnp.where(cnt >= cap, 1, 0)
        thr = thr | (digit << shift)
    above = jnp.where(bits > thr, 1.0, 0.0)
    tied = jnp.where(bits == thr, 1.0, 0.0)
    room = cap - jnp.sum(above, axis=-1, keepdims=True)
    sel = above + tied * jnp.where(_prefix_count(tied) < room, 1.0, 0.0)
    return jnp.where(sel > 0.0, _prefix_count(sel), -1.0), aff


def _one_hot_gather(slot_rows, aff_rows, h_ref, rows, cap):
    n_tok = h_ref.shape[0]
    c = (lax.broadcasted_iota(jnp.int32, (rows, n_tok), 0) & (cap - 1)).astype(F32)
    hit = c == slot_rows
    x = _dot(jnp.where(hit, 1.0, 0.0).astype(BF16), h_ref[...]).astype(BF16)
    return x, jnp.sum(jnp.where(hit, aff_rows, 0.0), axis=-1, keepdims=True)


def _route_request_kernel(h_ref, w_ref, x_ref, g_ref, slot_ref, *, cap):
    n_tok = h_ref.shape[0]
    slot, aff = _route_select(h_ref, w_ref, cap)
    slot_ref[0] = slot
    per_row = lambda a: jnp.concatenate(
        [jnp.broadcast_to(a[e:e + 1, :], (cap, n_tok)) for e in range(N_EXPERTS)], axis=0)
    x, g = _one_hot_gather(per_row(slot), per_row(aff), h_ref, N_EXPERTS * cap, cap)
    for e in range(N_EXPERTS):
        x_ref[e] = x[e * cap:(e + 1) * cap, :]
        g_ref[e] = g[e * cap:(e + 1) * cap, :]


def _route_only_kernel(h_ref, w_ref, slot_ref, aff_ref, bounds_ref, *, cap):
    n_tok = h_ref.shape[0]
    slot, aff = _route_select(h_ref, w_ref, cap)
    slot_ref[0] = slot
    aff_ref[0] = aff
    n = lax.broadcasted_iota(jnp.int32, (n_tok, LANES), 0)
    t = lax.broadcasted_iota(jnp.int32, (n_tok, LANES), 1)
    before = jnp.where(n < t * TOK_BLOCK, 1.0, 0.0).astype(BF16)
    chosen = jnp.where(slot >= 0.0, 1.0, 0.0).astype(BF16)
    bounds_ref[0] = _dot(chosen, before).astype(jnp.int32)


def _gather_window_kernel(bounds_ref, slot_ref, aff_ref, h_ref, xh_ref, gh_ref, x_ref, g_ref, xo_ref, go_ref, *,
                          n_batch, group):
    b = pl.program_id(0)
    eg = pl.program_id(1)
    n_tok = h_ref.shape[0]
    nt = n_tok // TOK_BLOCK
    cap = x_ref.shape[1]
    region = TOK_BLOCK // group

    @pl.when(b >= n_batch)
    def _():
        x_ref[...] = xh_ref[...]
        g_ref[...] = gh_ref[...]

    @pl.when(b < n_batch)
    def _():
        x_ref[...] = jnp.zeros(x_ref.shape, BF16)
        g_ref[...] = jnp.zeros(g_ref.shape, F32)
        local_i = lax.broadcasted_iota(jnp.int32, (region, TOK_BLOCK), 0).astype(F32)
        over_i = lax.broadcasted_iota(jnp.int32, (TOK_BLOCK, TOK_BLOCK), 0)

        def token_block(t, carry):
            toks = pl.ds(pl.multiple_of(t * TOK_BLOCK, TOK_BLOCK), TOK_BLOCK)
            hits, gates, firsts, n_overs, slots, affs = [], [], [], [], [], []
            for k in range(group):
                e = eg * group + k
                start, length = _window(bounds_ref, (b * N_EXPERTS + e) * (nt + 1), t)
                first = jnp.minimum(start, cap - region)
                s = slot_ref[0, pl.ds(e, 1), toks]
                a = aff_ref[0, pl.ds(e, 1), toks]
                hit = local_i == jnp.where(s >= 0.0, s - first.astype(F32), -1.0)
                hits.append(jnp.where(hit, 1.0, 0.0).astype(BF16))
                gates.append(jnp.sum(jnp.where(hit, a, 0.0), axis=-1, keepdims=True))
                firsts.append(first)
                n_overs.append(start + length - first - region)
                slots.append(s)
                affs.append(a)
            xc = _dot(jnp.concatenate(hits, axis=0), h_ref[toks, :]).astype(BF16)
            for k in range(group):
                dst = pl.ds(pl.multiple_of(firsts[k], PIECE), region)
                x_ref[k, dst, :] = x_ref[k, dst, :] + xc[k * region:(k + 1) * region, :]
                g_ref[k, dst, :] = g_ref[k, dst, :] + gates[k]
            for k in range(group):
                @pl.when(n_overs[k] > 0)
                def _(k=k):
                    base = firsts[k] + region
                    hit = (over_i + base).astype(F32) == slots[k]
                    xo_ref[...] = _dot(jnp.where(hit, 1.0, 0.0).astype(BF16), h_ref[toks, :]).astype(BF16)
                    go_ref[...] = jnp.sum(jnp.where(hit, affs[k], 0.0), axis=-1, keepdims=True)

                    def place(i, carry):
                        src = pl.ds(pl.multiple_of(i * PIECE, PIECE), PIECE)
                        dst = pl.ds(pl.multiple_of(base + i * PIECE, PIECE), PIECE)
                        x_ref[k, dst, :] = x_ref[k, dst, :] + xo_ref[src, :]
                        g_ref[k, dst, :] = g_ref[k, dst, :] + go_ref[src, :]
                        return carry

                    lax.fori_loop(0, n_overs[k] // PIECE, place, 0)
            return carry

        lax.fori_loop(0, nt, token_block, 0)


def _route_gather_requests(h, w_router_t, *, n_batch, n_tok):
    cap = CAPACITY_FACTOR * n_tok // N_EXPERTS
    out_block = lambda w: pl.BlockSpec((N_EXPERTS, cap, w), lambda b: (0, b, 0))
    return pl.pallas_call(
        functools.partial(_route_request_kernel, cap=cap),
        out_shape=(jax.ShapeDtypeStruct((N_EXPERTS, n_batch * cap, D_MODEL), BF16),
                   jax.ShapeDtypeStruct((N_EXPERTS, n_batch * cap, 1), F32),
                   jax.ShapeDtypeStruct((n_batch, N_EXPERTS, n_tok), F32)),
        grid=(n_batch,),
        in_specs=[pl.BlockSpec((n_tok, D_MODEL), lambda b: (b, 0)),
                  pl.BlockSpec((N_EXPERTS, D_MODEL), lambda b: (0, 0))],
        out_specs=(out_block(D_MODEL), out_block(1), pl.BlockSpec((1, N_EXPERTS, n_tok), lambda b: (b, 0, 0))),
        compiler_params=_cparams("parallel"),
        name="route_gather_requests",
    )(h, w_router_t)


def _route_only(h, w_router_t, *, n_batch, n_tok):
    cap = CAPACITY_FACTOR * n_tok // N_EXPERTS
    nt = n_tok // TOK_BLOCK
    expert_major = pl.BlockSpec((1, N_EXPERTS, n_tok), lambda b: (b, 0, 0))
    slot, aff, bounds = pl.pallas_call(
        functools.partial(_route_only_kernel, cap=cap),
        out_shape=(jax.ShapeDtypeStruct((n_batch, N_EXPERTS, n_tok), F32),
                   jax.ShapeDtypeStruct((n_batch, N_EXPERTS, n_tok), F32),
                   jax.ShapeDtypeStruct((n_batch, N_EXPERTS, LANES), jnp.int32)),
        grid=(n_batch,),
        in_specs=[pl.BlockSpec((n_tok, D_MODEL), lambda b: (b, 0)),
                  pl.BlockSpec((N_EXPERTS, D_MODEL), lambda b: (0, 0))],
        out_specs=(expert_major, expert_major, pl.BlockSpec((1, N_EXPERTS, LANES), lambda b: (b, 0, 0))),
        compiler_params=_cparams("parallel"),
        name="route_select",
    )(h, w_router_t)
    return slot, aff, bounds[:, :, :nt + 1].reshape(-1)


def _gather_windows(bounds, slot, aff, h, x_head, g_head, *, n_batch, n_tok):
    cap = CAPACITY_FACTOR * n_tok // N_EXPERTS
    group = TOK_BLOCK // SLOT_REGION
    head_rows = x_head.shape[1]
    assert head_rows % cap == 0 and N_EXPERTS % group == 0
    n_head = head_rows // cap
    rows_total = head_rows + n_batch * cap
    last = n_batch - 1
    request = lambda b: jnp.minimum(b, last)
    out_row = lambda b: jnp.where(b < n_batch, n_head + b, b - n_batch)
    head_row = lambda b: jnp.maximum(b - n_batch, 0)
    out_block = lambda w: pl.BlockSpec((group, cap, w), lambda b, g, s: (g, out_row(b), 0))
    head_block = lambda w: pl.BlockSpec((group, cap, w), lambda b, g, s: (g, head_row(b), 0))
    expert_major = pl.BlockSpec((1, N_EXPERTS, n_tok), lambda b, g, s: (request(b), 0, 0))
    return pl.pallas_call(
        functools.partial(_gather_window_kernel, n_batch=n_batch, group=group),
        out_shape=(jax.ShapeDtypeStruct((N_EXPERTS, rows_total, D_MODEL), BF16),
                   jax.ShapeDtypeStruct((N_EXPERTS, rows_total, 1), F32)),
        grid_spec=pltpu.PrefetchScalarGridSpec(
            num_scalar_prefetch=1,
            grid=(n_batch + n_head, N_EXPERTS // group),
            in_specs=[expert_major, expert_major,
                      pl.BlockSpec((n_tok, D_MODEL), lambda b, g, s: (request(b), 0)),
                      head_block(D_MODEL), head_block(1)],
            out_specs=(out_block(D_MODEL), out_block(1)),
            scratch_shapes=[pltpu.VMEM((TOK_BLOCK, D_MODEL), BF16), pltpu.VMEM((TOK_BLOCK, 1), F32)]),
        compiler_params=_cparams("arbitrary", "arbitrary"),
        name="gather_windows",
    )(bounds, slot, aff, h, x_head, g_head)


def _moe_kernel(x_ref, g_ref, wg_ref, wu_ref, wd_ref, y_ref, hid_ref, *, n_ff, rows):
    s = pl.program_id(2)
    tf = wg_ref.shape[2]

    @pl.when(s < n_ff)
    def _():
        wg = wg_ref[0].astype(BF16)
        wu = wu_ref[0].astype(BF16)
        cols = pl.ds(pl.multiple_of(s * tf, tf), tf)
        for r in range(x_ref.shape[1] // rows):
            sl = slice(r * rows, (r + 1) * rows)
            x = x_ref[0, sl, :]
            hid_ref[sl, cols] = (_silu(_dot(x, wg)) * _dot(x, wu)).astype(BF16)

    @pl.when(s >= n_ff)
    def _():
        y = _dot(hid_ref[...], wd_ref[0].astype(BF16))
        y_ref[0] = (y * g_ref[0]).astype(BF16)


def _experts(x, gate, w_gate, w_up, w_down):
    n_rows = x.shape[1]
    tr, tf, tn = n_rows // 2, 512, 256
    n_ff, n_out = EXPERT_FF // tf, D_MODEL // tn
    up_tile = lambda e, r, s: (e, 0, jnp.minimum(s, n_ff - 1))
    out_tile = lambda s: jnp.maximum(s - n_ff, 0)
    return pl.pallas_call(
        functools.partial(_moe_kernel, n_ff=n_ff, rows=256),
        out_shape=jax.ShapeDtypeStruct((N_EXPERTS, n_rows, D_MODEL), BF16),
        grid=(N_EXPERTS, n_rows // tr, n_ff + n_out),
        in_specs=[pl.BlockSpec((1, tr, D_MODEL), lambda e, r, s: (e, r, 0), pipeline_mode=pl.Buffered(1)),
                  pl.BlockSpec((1, tr, 1), lambda e, r, s: (e, r, 0)),
                  pl.BlockSpec((1, D_MODEL, tf), up_tile),
                  pl.BlockSpec((1, D_MODEL, tf), up_tile),
                  pl.BlockSpec((1, EXPERT_FF, tn), lambda e, r, s: (e, 0, out_tile(s)))],
        out_specs=pl.BlockSpec((1, tr, tn), lambda e, r, s: (e, r, out_tile(s))),
        scratch_shapes=[pltpu.VMEM((tr, EXPERT_FF), BF16)],
        compiler_params=_cparams("parallel", "parallel", "arbitrary"),
        name="expert_swiglu",
    )(x, gate, w_gate, w_up, w_down)


def _window(bounds_ref, base, t):
    p0 = bounds_ref[base + t]
    p1 = bounds_ref[base + t + 1]
    start = (p0 // PIECE) * PIECE
    return start, jnp.where(p1 > p0, ((p1 - start + PIECE - 1) // PIECE) * PIECE, 0)


def _combine_kernel(bounds_ref, y_ref, slot_ref, x_ref, mod_ref, g_ref, o_ref, ybuf_ref, obuf_ref, acc_ref, *,
                    nt, region):
    b = pl.program_id(0)
    t = pl.program_id(1)
    tt = x_ref.shape[0]
    cap = y_ref.shape[1]
    local_i = lax.broadcasted_iota(jnp.int32, (region, tt), 0).astype(F32)
    hits = []
    for e in range(N_EXPERTS):
        start, _ = _window(bounds_ref, (b * N_EXPERTS + e) * (nt + 1), t)
        first = jnp.minimum(start, cap - region)
        ybuf_ref[e * region:(e + 1) * region, :] = y_ref[e, pl.ds(pl.multiple_of(first, PIECE), region), :]
        s = slot_ref[0, e:e + 1, :]
        hit = local_i == jnp.where(s >= 0.0, s - first.astype(F32), -1.0)
        hits.append(jnp.where(hit, 1.0, 0.0).astype(BF16))
    acc_ref[...] = _dot_tn(jnp.concatenate(hits, axis=0), ybuf_ref[...])
    over_i = lax.broadcasted_iota(jnp.int32, (TOK_BLOCK, tt), 0)

    def overflow(e, carry):
        start, length = _window(bounds_ref, (b * N_EXPERTS + e) * (nt + 1), t)
        base = jnp.minimum(start, cap - region) + region
        n_over = start + length - base

        @pl.when(n_over > 0)
        def _():
            def copy(i, carry):
                obuf_ref[pl.ds(pl.multiple_of(i * PIECE, PIECE), PIECE), :] = (
                    y_ref[e, pl.ds(pl.multiple_of(base + i * PIECE, PIECE), PIECE), :])
                return carry

            def clear(i, carry):
                obuf_ref[pl.ds(pl.multiple_of(n_over + i * PIECE, PIECE), PIECE), :] = jnp.zeros((PIECE, D_MODEL), BF16)
                return carry

            lax.fori_loop(0, n_over // PIECE, copy, 0)
            lax.fori_loop(0, (TOK_BLOCK - n_over) // PIECE, clear, 0)
            hit = (over_i + base).astype(F32) == slot_ref[0, pl.ds(e, 1), :]
            acc_ref[...] += _dot_tn(jnp.where(hit, 1.0, 0.0).astype(BF16), obuf_ref[...])

        return carry

    lax.fori_loop(0, N_EXPERTS, overflow, 0)
    x = x_ref[...] + mod_ref[0, 5:6, :] * acc_ref[...]
    var = jnp.mean(x * x, axis=-1, keepdims=True)
    o_ref[...] = x * lax.rsqrt(var + NORM_EPS) * g_ref[...]


def _combine(bounds, y, slot, x_mid, mod, final_g, *, n_batch, n_tok, row_block_off, mod_base, mod_per_batch):
    cap = CAPACITY_FACTOR * n_tok // N_EXPERTS
    tt = TOK_BLOCK
    nt = n_tok // tt
    region = min(SLOT_REGION, cap)
    assert cap - region <= TOK_BLOCK
    return pl.pallas_call(
        functools.partial(_combine_kernel, nt=nt, region=region),
        out_shape=jax.ShapeDtypeStruct((n_batch * n_tok, D_MODEL), F32),
        grid_spec=pltpu.PrefetchScalarGridSpec(
            num_scalar_prefetch=1,
            grid=(n_batch, nt),
            in_specs=[pl.BlockSpec((N_EXPERTS, cap, D_MODEL), lambda b, t, s: (0, row_block_off + b, 0)),
                      pl.BlockSpec((1, N_EXPERTS, tt), lambda b, t, s: (b, 0, t)),
                      pl.BlockSpec((tt, D_MODEL), lambda b, t, s: (b * nt + t, 0)),
                      pl.BlockSpec((1, N_MOD, D_MODEL), lambda b, t, s: (mod_base + b * mod_per_batch, 0, 0)),
                      pl.BlockSpec((1, D_MODEL), lambda b, t, s: (0, 0))],
            out_specs=pl.BlockSpec((tt, D_MODEL), lambda b, t, s: (b * nt + t, 0)),
            scratch_shapes=[pltpu.VMEM((N_EXPERTS * region, D_MODEL), BF16),
                            pltpu.VMEM((TOK_BLOCK, D_MODEL), BF16), pltpu.VMEM((tt, D_MODEL), F32)]),
        compiler_params=_cparams("arbitrary", "arbitrary"),
        name="combine_final_norm",
    )(bounds, y, slot, x_mid, mod, final_g)


def kernel(x_prompt, x_sample, cache_k, cache_v, state_hgrn, c, c_ctx, w_ada, b_ada, norm1_g, w_in, hg_lb,
           hg_norm_g, attn_sink, w_out, norm2_g, w_router, w_gate, w_up, w_down, final_norm_g):
    n_p, t_p, _ = x_prompt.shape
    n_s, t_s, _ = x_sample.shape
    assert w_ada.shape[0] == 1 and 1 + n_s <= COND_ROWS
    layer = 0

    cond = jnp.zeros((COND_ROWS, D_MODEL), F32).at[0].set(c_ctx).at[1:1 + n_s].set(c)
    mod = _ada_modulation(cond, w_ada[layer], b_ada[layer]).reshape(COND_ROWS, N_MOD, D_MODEL)
    lb = jnp.cumsum(jax.nn.softmax(hg_lb.astype(F32), axis=0), axis=0)[layer]
    w_in_l = w_in[layer].astype(BF16)
    w_out_b = w_out[layer].astype(BF16)
    w_router_t = w_router[layer].T.astype(BF16)
    norm1 = norm1_g[layer].reshape(1, D_MODEL)
    norm2 = norm2_g[layer].reshape(1, D_MODEL)
    hg_gain = hg_norm_g[layer].reshape(1, HG_WIDTH)
    final_g = final_norm_g.reshape(1, D_MODEL)
    sink = attn_sink[layer]
    cos, sin_signed = _rope_tables(t_s)

    xp = x_prompt.reshape(n_p * t_p, D_MODEL)
    xs = x_sample.reshape(n_s * t_s, D_MODEL)
    groups = dict(p=dict(mod_base=0, rows_per_mod=n_p * t_p), s=dict(mod_base=1, rows_per_mod=t_s))

    proj_p = _in_projection(xp, mod, norm1, w_in_l, **groups["p"])
    proj_s = _in_projection(xs, mod, norm1, w_in_l, **groups["s"])

    ohg_p, new_state = _hgrn(proj_p, lb, hg_gain, None, n_batch=n_p, n_tok=t_p)
    ohg_s, _ = _hgrn(proj_s, lb, hg_gain, state_hgrn[:, layer:layer + 1].astype(F32), n_batch=n_s, n_tok=t_s)

    oatt_p = _context_attention(proj_p, sink, n_batch=n_p, n_tok=t_p)
    n_ctx = cache_k.shape[2]
    oatt_s = _window_attention(proj_s, cache_k[:, layer].reshape(n_s, n_ctx, KV_WIDTH),
                               cache_v[:, layer].reshape(n_s, n_ctx, KV_WIDTH), sink, cos, sin_signed,
                               n_batch=n_s, n_tok=t_s)

    xmid_p, h2_p = _out_projection(ohg_p, oatt_p, w_out_b, xp, mod, norm2, **groups["p"])
    xmid_s, h2_s = _out_projection(ohg_s, oatt_s, w_out_b, xs, mod, norm2, **groups["s"])

    cap_p = CAPACITY_FACTOR * t_p // N_EXPERTS
    cap_s = CAPACITY_FACTOR * t_s // N_EXPERTS
    off_s = n_p * cap_p // cap_s
    xg_p, gate_p, slot_p = _route_gather_requests(h2_p, w_router_t, n_batch=n_p, n_tok=t_p)
    slot_s, aff_s, bounds_s = _route_only(h2_s, w_router_t, n_batch=n_s, n_tok=t_s)
    xg, gate = _gather_windows(bounds_s, slot_s, aff_s, h2_s, xg_p, gate_p, n_batch=n_s, n_tok=t_s)
    bounds_p = jnp.tile(jnp.array([0, cap_p], jnp.int32), n_p * N_EXPERTS)

    y = _experts(xg, gate, w_gate[layer], w_up[layer], w_down[layer])

    y_prompt = _combine(bounds_p, y, slot_p, xmid_p, mod, final_g, n_batch=n_p, n_tok=t_p, row_block_off=0,
                        mod_base=0, mod_per_batch=0)
    y_sample = _combine(bounds_s, y, slot_s, xmid_s, mod, final_g, n_batch=n_s, n_tok=t_s, row_block_off=off_s,
                        mod_base=1, mod_per_batch=1)

    k_col = 5 * HG_WIDTH + ATT_WIDTH
    new_k = proj_p[:, k_col:k_col + KV_WIDTH].reshape(n_p, 1, t_p, ATT_KV_HEADS, HEAD_DIM)
    new_v = proj_p[:, k_col + KV_WIDTH:k_col + 2 * KV_WIDTH].reshape(n_p, 1, t_p, ATT_KV_HEADS, HEAD_DIM)
    return (y_prompt.reshape(n_p, t_p, D_MODEL), y_sample.reshape(n_s, t_s, D_MODEL), new_k, new_v, new_state)
```

```python
import functools

import jax
import jax.numpy as jnp
import numpy as np
from jax import lax
from jax.experimental import pallas as pl
from jax.experimental.pallas import tpu as pltpu

F32 = jnp.float32
BF16 = jnp.bfloat16

D_MODEL = 2048
HG_WIDTH = 1024
HG_HEADS = 8
HEAD_DIM = 128
ATT_HEADS = 8
ATT_KV_HEADS = 2
ATT_GROUP = ATT_HEADS // ATT_KV_HEADS
KV_WIDTH = ATT_KV_HEADS * HEAD_DIM
ATT_WIDTH = ATT_HEADS * HEAD_DIM
ATT_BLOCK = 128
GRID_W = 64
ROPE_BASE = 10000.0
ROPE_FREQS = HEAD_DIM // 4
N_EXPERTS = 16
CAPACITY_FACTOR = 2
EXPERT_FF = 5632
NORM_EPS = 1e-6
IN_WIDTH = 5 * HG_WIDTH + ATT_WIDTH + 2 * KV_WIDTH
N_MOD = 6
COND_ROWS = 16

HG_CHUNK = 128
HG_DIAG = 8
SEL_BLOCK = 256
TOK_BLOCK = 256
PIECE = 16
SLOT_REGION = 64
LANES = 128

V7X_VMEM_BYTES = 64 * 1024 * 1024
VMEM_LIMIT = V7X_VMEM_BYTES - 8 * 1024 * 1024


def _cparams(*sem):
    return pltpu.CompilerParams(dimension_semantics=sem, vmem_limit_bytes=VMEM_LIMIT)


def _sigmoid(x):
    return 1.0 / (1.0 + jnp.exp(-x))


def _silu(x):
    return x * _sigmoid(x)


def _dot(a, b):
    return jnp.dot(a, b, preferred_element_type=F32)


def _dot_nt(a, b):
    return lax.dot_general(a, b, (((1,), (1,)), ((), ())), preferred_element_type=F32)


def _dot_tn(a, b):
    return lax.dot_general(a, b, (((0,), (0,)), ((), ())), preferred_element_type=F32)


def _ada_kernel(c_ref, w_ref, b_ref, o_ref):
    s = _silu(c_ref[...]).astype(BF16)
    o_ref[...] = _dot(s, w_ref[...].astype(BF16)) + b_ref[...]


def _ada_modulation(cond, w_ada, b_ada):
    tn = 1024
    n = w_ada.shape[1]
    return pl.pallas_call(
        _ada_kernel,
        out_shape=jax.ShapeDtypeStruct((COND_ROWS, n), F32),
        grid=(n // tn,),
        in_specs=[pl.BlockSpec((COND_ROWS, D_MODEL), lambda j: (0, 0)),
                  pl.BlockSpec((D_MODEL, tn), lambda j: (0, j)),
                  pl.BlockSpec((1, tn), lambda j: (0, j))],
        out_specs=pl.BlockSpec((COND_ROWS, tn), lambda j: (0, j)),
        compiler_params=_cparams("arbitrary"),
        name="ada_modulation",
    )(cond, w_ada, b_ada.reshape(1, n))


def _norm_modulate(x, gain, shift, scale):
    var = jnp.mean(x * x, axis=-1, keepdims=True)
    return (x * lax.rsqrt(var + NORM_EPS) * gain) * (1.0 + scale) + shift


def _inproj_kernel(x_ref, mod_ref, g_ref, w_ref, o_ref, h_ref, *, rows):
    w = w_ref[...]
    chunks = [slice(r * rows, (r + 1) * rows) for r in range(x_ref.shape[0] // rows)]

    @pl.when(pl.program_id(1) == 0)
    def _():
        shift = mod_ref[0, 0:1, :]
        scale = mod_ref[0, 1:2, :]
        gain = g_ref[...]
        for sl in chunks:
            h = _norm_modulate(x_ref[sl, :], gain, shift, scale).astype(BF16)
            h_ref[sl, :] = h
            o_ref[sl, :] = _dot(h, w)

    @pl.when(pl.program_id(1) > 0)
    def _():
        for sl in chunks:
            o_ref[sl, :] = _dot(h_ref[sl, :], w)


def _in_projection(x, mod, gain, w, *, mod_base, rows_per_mod):
    m = x.shape[0]
    tm, tn = 1024, 512
    return pl.pallas_call(
        functools.partial(_inproj_kernel, rows=256),
        out_shape=jax.ShapeDtypeStruct((m, IN_WIDTH), F32),
        grid=(m // tm, IN_WIDTH // tn),
        in_specs=[pl.BlockSpec((tm, D_MODEL), lambda i, j: (i, 0)),
                  pl.BlockSpec((1, N_MOD, D_MODEL), lambda i, j: (mod_base + (i * tm) // rows_per_mod, 0, 0)),
                  pl.BlockSpec((1, D_MODEL), lambda i, j: (0, 0)),
                  pl.BlockSpec((D_MODEL, tn), lambda i, j: (0, j))],
        out_specs=pl.BlockSpec((tm, tn), lambda i, j: (i, j)),
        scratch_shapes=[pltpu.VMEM((tm, D_MODEL), BF16)],
        compiler_params=_cparams("parallel", "arbitrary"),
        name="in_projection",
    )(x, mod, gain, w)


def _hgrn_codes(reverse):
    L = HG_CHUNK
    t = lax.broadcasted_iota(jnp.int32, (L, L), 0)
    s = lax.broadcasted_iota(jnp.int32, (L, L), 1)
    code = jnp.where(t == s, 1, 0)
    h = L // 2
    while h >= 1:
        same = (t & ~(2 * h - 1)) == (s & ~(2 * h - 1))
        t_hi = (t & h) != 0
        s_hi = (s & h) != 0
        pair = (s_hi & ~t_hi) if reverse else (t_hi & ~s_hi)
        code = jnp.where(same & pair, h * 16, code)
        h //= 2
    return code


def _cumsum_rows(tri_bf16, g):
    g1 = g.astype(BF16)
    g2 = (g - g1.astype(F32)).astype(BF16)
    s = _dot(tri_bf16, jnp.concatenate([g1, g2], axis=1))
    return s[:, :HEAD_DIM] + s[:, HEAD_DIM:]


def _hgrn_intra(q, k, f, b, v_bf, code, reverse):
    L = HG_CHUNK
    G = HG_DIAG
    q_bf = q.astype(BF16)
    k_bf = k.astype(BF16)

    def level(h, ref, att):
        neg_abs = pltpu.bitcast(pltpu.bitcast(b - ref, jnp.int32) | jnp.int32(-2 ** 31), F32)
        e = jnp.exp2(neg_abs.astype(BF16))
        return jnp.where(code == h * 16, _dot_nt(q_bf * e, k_bf * e), att)

    att = jnp.where(code == 1, _dot_nt(q_bf, k_bf), 0.0)
    h = L // 2
    while h >= G:
        parts = []
        for p in range(L // (2 * h)):
            m = p * 2 * h + (h if reverse else h - 1)
            parts.append(jnp.broadcast_to(b[m:m + 1, :], (2 * h, HEAD_DIM)))
        att = level(h, parts[0] if len(parts) == 1 else jnp.concatenate(parts, axis=0), att)
        h //= 2

    b3 = b.reshape(L // G, G, HEAD_DIM)

    def group_row(r):
        return jnp.broadcast_to(b3[:, r:r + 1, :], (L // G, G, HEAD_DIM)).reshape(L, HEAD_DIM)

    row = lax.broadcasted_iota(jnp.int32, (L, HEAD_DIM), 0)
    att = level(4, group_row(4 if reverse else 3), att)
    lo, hi = (2, 6) if reverse else (1, 5)
    att = level(2, jnp.where((row & 4) == 0, group_row(lo), group_row(hi)), att)
    att = jnp.where(code == 16, _dot_nt(q_bf * f.astype(BF16), k_bf), att)
    return _dot(att.astype(BF16), v_bf)


def _hgrn_prepare(q, f, v, code, tri, reverse):
    L = HG_CHUNK
    k = 1.0 - f
    b = _cumsum_rows(tri, jnp.log2(f))
    b_tot = b[0:1, :] if reverse else b[L - 1:L, :]
    v_bf = v.astype(BF16)
    o_intra = _hgrn_intra(q, k, f, b, v_bf, code, reverse)
    q_in = (q * jnp.exp2(b)).astype(BF16)
    k_out = (k * jnp.exp2(b_tot - b)).astype(BF16)
    return o_intra, q_in, k_out, v_bf, jnp.exp2(b_tot)


def _hgrn_advance(st, prepared):
    o_intra, q_in, k_out, v_bf, decay = prepared
    return o_intra + _dot_nt(q_in, st.astype(BF16)), st * decay + _dot_tn(v_bf, k_out)


def _hgrn_kernel(*refs, n_tok, has_state, heads):
    if has_state:
        (q_ref, ff_ref, fb_ref, v_ref, gate_ref, lb_ref, ng_ref, s0_ref,
         o_ref, sout_ref, of_ref, ob_ref, code_ref) = refs
    else:
        (q_ref, ff_ref, fb_ref, v_ref, gate_ref, lb_ref, ng_ref,
         o_ref, sout_ref, of_ref, ob_ref, code_ref) = refs
    L = HG_CHUNK
    nc = n_tok // L

    @pl.when((pl.program_id(0) == 0) & (pl.program_id(1) == 0))
    def _():
        code_ref[0] = _hgrn_codes(False)
        code_ref[1] = _hgrn_codes(True)

    code_f = code_ref[0]
    code_b = code_ref[1]
    ti = lax.broadcasted_iota(jnp.int32, (L, L), 0)
    si = lax.broadcasted_iota(jnp.int32, (L, L), 1)
    tri_f = jnp.where(si <= ti, 1.0, 0.0).astype(BF16)
    tri_b = jnp.where(si >= ti, 1.0, 0.0).astype(BF16)

    def one_head(hd, carry):
        cols = pl.ds(pl.multiple_of(hd * HEAD_DIM, HEAD_DIM), HEAD_DIM)
        lb_f = lb_ref[0:1, cols]
        lb_b = lb_ref[1:2, cols]

        def prepare(row, f_ref, lb, code, tri, reverse):
            sl = pl.ds(row, L)
            q = _silu(q_ref[sl, cols])
            f = lb + (1.0 - lb) * _sigmoid(f_ref[sl, cols])
            return sl, _hgrn_prepare(q, f, v_ref[sl, cols], code, tri, reverse)

        gain = ng_ref[:, cols]

        def finish(sl, o):
            var = jnp.mean(o * o, axis=-1, keepdims=True)
            o = (o * lax.rsqrt(var + NORM_EPS) * gain) * _silu(gate_ref[sl, cols])
            o_ref[sl, cols] = o.astype(BF16)

        def make_body(other_direction):
            def body(c, states):
                st_f, st_b = states
                fwd, bwd = [], []
                for u in range(unroll):
                    cu = c * unroll + u
                    fwd.append(prepare(pl.multiple_of(cu * L, L), ff_ref, lb_f, code_f, tri_f, False))
                    bwd.append(prepare(pl.multiple_of((nc - 1 - cu) * L, L), fb_ref, lb_b, code_b, tri_b, True))
                out_f, out_b = [], []
                for (sl_f, prep_f), (sl_b, prep_b) in zip(fwd, bwd):
                    o_f, st_f = _hgrn_advance(st_f, prep_f)
                    o_b, st_b = _hgrn_advance(st_b, prep_b)
                    out_f.append((sl_f, o_f))
                    out_b.append((sl_b, o_b))
                if other_direction == "later":
                    for sl, o in out_f:
                        of_ref[sl, :] = o
                    for sl, o in out_b:
                        ob_ref[sl, :] = o
                elif other_direction == "now":
                    for u in range(unroll):
                        finish(out_f[u][0], out_f[u][1] + out_b[unroll - 1 - u][1])
                else:
                    for sl, o in out_f:
                        finish(sl, o + ob_ref[sl, :])
                    for sl, o in out_b:
                        finish(sl, of_ref[sl, :] + o)
                return st_f, st_b
            return body

        unroll = 8 if nc % 16 == 0 else 2
        assert nc % unroll == 0
        n_iter = nc // unroll
        half = n_iter // 2
        if has_state:
            states = (s0_ref[0, 0, 0, hd].T, s0_ref[0, 0, 1, hd].T)
        else:
            states = (jnp.zeros((HEAD_DIM, HEAD_DIM), F32), jnp.zeros((HEAD_DIM, HEAD_DIM), F32))
        states = lax.fori_loop(0, half, make_body("later"), states)
        if n_iter % 2:
            states = make_body("now")(half, states)
        st_f, st_b = lax.fori_loop(half + n_iter % 2, n_iter, make_body("earlier"), states)
        sout_ref[0, 0, 0, hd] = st_f.T
        sout_ref[0, 0, 1, hd] = st_b.T
        return carry

    lax.fori_loop(0, heads, one_head, 0)


def _hgrn(proj, lb, norm_g, state, *, n_batch, n_tok):
    heads = HG_HEADS
    while heads > 1 and 5 * 2 * n_tok * heads * HEAD_DIM * 4 > VMEM_LIMIT // 2:
        heads //= 2
    width = heads * HEAD_DIM
    groups = HG_HEADS // heads
    col = lambda k: (lambda b, h: (b, k * groups + h))
    tok_spec = lambda k: pl.BlockSpec((n_tok, width), col(k))
    st_spec = pl.BlockSpec((1, 1, 2, heads, HEAD_DIM, HEAD_DIM), lambda b, h: (b, 0, 0, h, 0, 0))
    has_state = state is not None
    return pl.pallas_call(
        functools.partial(_hgrn_kernel, n_tok=n_tok, has_state=has_state, heads=heads),
        out_shape=(jax.ShapeDtypeStruct((n_batch * n_tok, HG_WIDTH), BF16),
                   jax.ShapeDtypeStruct((n_batch, 1, 2, HG_HEADS, HEAD_DIM, HEAD_DIM), F32)),
        grid=(n_batch, groups),
        in_specs=[tok_spec(0), tok_spec(1), tok_spec(2), tok_spec(3), tok_spec(4),
                  pl.BlockSpec((2, width), lambda b, h: (0, h)),
                  pl.BlockSpec((1, width), lambda b, h: (0, h))] + ([st_spec] if has_state else []),
        out_specs=(pl.BlockSpec((n_tok, width), lambda b, h: (b, h)), st_spec),
        scratch_shapes=[pltpu.VMEM((n_tok, HEAD_DIM), F32), pltpu.VMEM((n_tok, HEAD_DIM), F32),
                        pltpu.VMEM((2, HG_CHUNK, HG_CHUNK), jnp.int32)],
        compiler_params=_cparams("arbitrary", "arbitrary"),
        name="hgrn2_scan",
    )(proj, proj, proj, proj, proj, lb, norm_g, *((state,) if has_state else ()))


def _stack_heads(x, kvh):
    return jnp.concatenate(
        [x[:, (kvh * ATT_GROUP + g) * HEAD_DIM:(kvh * ATT_GROUP + g + 1) * HEAD_DIM] for g in range(ATT_GROUP)],
        axis=0)


def _sink_column(sink_ref, kvh, rows):
    return jnp.concatenate(
        [jnp.full((rows, 1), sink_ref[kvh * ATT_GROUP + g], F32) for g in range(ATT_GROUP)],
        axis=0) * (1.0 / SOFTMAX_SCALE)


SOFTMAX_SCALE = HEAD_DIM ** -0.5
EXP2_SCALE = SOFTMAX_SCALE * 1.4426950408889634


def _ones_column(rows):
    lane = lax.broadcasted_iota(jnp.int32, (rows, HEAD_DIM), 1)
    return jnp.where(lane == 0, 1.0, 0.0).astype(BF16)


def _softmax_av(scores, values, sink_col):
    m = sink_col
    for s in scores:
        m = jnp.maximum(m, jnp.max(s, axis=-1, keepdims=True))
    acc = None
    for s, v in zip(scores, values):
        pv = _dot(jnp.exp2(((s - m) * EXP2_SCALE).astype(BF16)), v)
        acc = pv if acc is None else acc + pv
    denom = acc[:, HEAD_DIM:HEAD_DIM + 1] + jnp.exp2((sink_col - m) * EXP2_SCALE)
    return acc[:, :HEAD_DIM] / denom


def _ctx_attn_kernel(sink_ref, q_ref, k_ref, v_ref, o_ref):
    rows = q_ref.shape[0]
    q_all = q_ref[...]
    ones = _ones_column(k_ref.shape[0])
    for kvh in range(ATT_KV_HEADS):
        q = _stack_heads(q_all, kvh).astype(BF16)
        k = k_ref[:, kvh * HEAD_DIM:(kvh + 1) * HEAD_DIM].astype(BF16)
        v = jnp.concatenate([v_ref[:, kvh * HEAD_DIM:(kvh + 1) * HEAD_DIM].astype(BF16), ones], axis=1)
        o = _softmax_av([_dot_nt(q, k)], [v], _sink_column(sink_ref, kvh, rows))
        for g in range(ATT_GROUP):
            hd = kvh * ATT_GROUP + g
            o_ref[:, hd * HEAD_DIM:(hd + 1) * HEAD_DIM] = o[g * rows:(g + 1) * rows, :].astype(BF16)


def _context_attention(proj, sink, *, n_batch, n_tok):
    q_col = 5 * HG_WIDTH // ATT_WIDTH
    k_col = (5 * HG_WIDTH + ATT_WIDTH) // KV_WIDTH
    return pl.pallas_call(
        _ctx_attn_kernel,
        out_shape=jax.ShapeDtypeStruct((n_batch * n_tok, ATT_WIDTH), BF16),
        grid_spec=pltpu.PrefetchScalarGridSpec(
            num_scalar_prefetch=1,
            grid=(n_batch,),
            in_specs=[pl.BlockSpec((n_tok, ATT_WIDTH), lambda b, s: (b, q_col)),
                      pl.BlockSpec((n_tok, KV_WIDTH), lambda b, s: (b, k_col)),
                      pl.BlockSpec((n_tok, KV_WIDTH), lambda b, s: (b, k_col + 1))],
            out_specs=pl.BlockSpec((n_tok, ATT_WIDTH), lambda b, s: (b, 0))),
        compiler_params=_cparams("parallel"),
        name="context_attention",
    )(sink, proj, proj, proj)


def _rope(x, cos, sin_signed, even_group):
    partner = jnp.where(even_group, pltpu.roll(x, HEAD_DIM - ROPE_FREQS, 1), pltpu.roll(x, ROPE_FREQS, 1))
    return x * cos + partner * sin_signed


def _win_attn_kernel(sink_ref, q_ref, k_ref, v_ref, ck_ref, cv_ref, cos_ref, sin_ref, o_ref,
                     kpad_ref, vpad_ref, *, n_tok):
    blk = ATT_BLOCK
    nb = n_tok // blk
    i = pl.program_id(1)
    lane = lax.broadcasted_iota(jnp.int32, (blk, HEAD_DIM), 1)
    even_group = (lane & ROPE_FREQS) == 0

    @pl.when(i == 0)
    def _():
        kpad_ref[0:blk, :] = jnp.zeros((blk, KV_WIDTH), BF16)
        kpad_ref[blk + n_tok:2 * blk + n_tok, :] = jnp.zeros((blk, KV_WIDTH), BF16)
        vpad_ref[0:blk, :] = jnp.zeros((blk, 2 * KV_WIDTH), BF16)
        vpad_ref[blk + n_tok:2 * blk + n_tok, :] = jnp.zeros((blk, 2 * KV_WIDTH), BF16)
        ones = _ones_column(blk)

        def body(r, carry):
            src = pl.ds(pl.multiple_of(r * blk, blk), blk)
            dst = pl.ds(pl.multiple_of((r + 1) * blk, blk), blk)
            cos = cos_ref[src, :]
            sin = sin_ref[src, :]
            for kvh in range(ATT_KV_HEADS):
                cols = slice(kvh * HEAD_DIM, (kvh + 1) * HEAD_DIM)
                kpad_ref[dst, cols] = _rope(k_ref[src, cols], cos, sin, even_group).astype(BF16)
                vpad_ref[dst, 2 * kvh * HEAD_DIM:(2 * kvh + 1) * HEAD_DIM] = v_ref[src, cols].astype(BF16)
                vpad_ref[dst, (2 * kvh + 1) * HEAD_DIM:(2 * kvh + 2) * HEAD_DIM] = ones
            return carry

        lax.fori_loop(0, nb, body, 0)

    ctx_ones = _ones_column(ck_ref.shape[1])
    r = lax.broadcasted_iota(jnp.int32, (blk, 3 * blk), 0)
    j = lax.broadcasted_iota(jnp.int32, (blk, 3 * blk), 1)
    blocks = q_ref.shape[0] // blk

    def query_block(u, carry):
        qb = i * blocks + u
        rows = pl.ds(pl.multiple_of(qb * blk, blk), blk)
        cos = cos_ref[rows, :]
        sin = sin_ref[rows, :]
        band = pl.ds(pl.multiple_of(qb * blk, blk), 3 * blk)
        kpos = j + (qb - 1) * blk
        valid = (j >= r) & (j <= r + 2 * blk) & (kpos >= 0) & (kpos < n_tok)
        valid = jnp.concatenate([valid] * ATT_GROUP, axis=0)
        local = pl.ds(pl.multiple_of(u * blk, blk), blk)
        q_all = q_ref[local, :]
        for kvh in range(ATT_KV_HEADS):
            cols = slice(kvh * HEAD_DIM, (kvh + 1) * HEAD_DIM)
            q = jnp.concatenate(
                [_rope(q_all[:, (kvh * ATT_GROUP + g) * HEAD_DIM:(kvh * ATT_GROUP + g + 1) * HEAD_DIM],
                       cos, sin, even_group) for g in range(ATT_GROUP)], axis=0).astype(BF16)
            s_ctx = _dot_nt(q, ck_ref[0, :, cols].astype(BF16))
            s_loc = jnp.where(valid, _dot_nt(q, kpad_ref[band, cols]), -jnp.inf)
            v_ctx = jnp.concatenate([cv_ref[0, :, cols].astype(BF16), ctx_ones], axis=1)
            v_loc = vpad_ref[band, 2 * kvh * HEAD_DIM:(2 * kvh + 2) * HEAD_DIM]
            o = _softmax_av([s_ctx, s_loc], [v_ctx, v_loc], _sink_column(sink_ref, kvh, blk))
            for g in range(ATT_GROUP):
                hd = kvh * ATT_GROUP + g
                o_ref[local, hd * HEAD_DIM:(hd + 1) * HEAD_DIM] = o[g * blk:(g + 1) * blk, :].astype(BF16)
        return carry

    lax.fori_loop(0, blocks, query_block, 0)


def _window_attention(proj, cache_k, cache_v, sink, cos, sin_signed, *, n_batch, n_tok):
    tq = 4 * ATT_BLOCK
    steps = n_tok // tq
    n_ctx = cache_k.shape[1]
    q_col = 5 * HG_WIDTH // ATT_WIDTH
    k_col = (5 * HG_WIDTH + ATT_WIDTH) // KV_WIDTH
    table = pl.BlockSpec((n_tok, HEAD_DIM), lambda b, i, s: (0, 0))
    cache = pl.BlockSpec((1, n_ctx, KV_WIDTH), lambda b, i, s: (b, 0, 0))
    return pl.pallas_call(
        functools.partial(_win_attn_kernel, n_tok=n_tok),
        out_shape=jax.ShapeDtypeStruct((n_batch * n_tok, ATT_WIDTH), BF16),
        grid_spec=pltpu.PrefetchScalarGridSpec(
            num_scalar_prefetch=1,
            grid=(n_batch, steps),
            in_specs=[pl.BlockSpec((tq, ATT_WIDTH), lambda b, i, s: (b * steps + i, q_col)),
                      pl.BlockSpec((n_tok, KV_WIDTH), lambda b, i, s: (b, k_col)),
                      pl.BlockSpec((n_tok, KV_WIDTH), lambda b, i, s: (b, k_col + 1)),
                      cache, cache, table, table],
            out_specs=pl.BlockSpec((tq, ATT_WIDTH), lambda b, i, s: (b * steps + i, 0)),
            scratch_shapes=[pltpu.VMEM((n_tok + 2 * ATT_BLOCK, KV_WIDTH), BF16),
                            pltpu.VMEM((n_tok + 2 * ATT_BLOCK, 2 * KV_WIDTH), BF16)]),
        compiler_params=_cparams("parallel", "arbitrary"),
        name="window_attention",
    )(sink, proj, proj, proj, cache_k, cache_v, cos, sin_signed)


def _rope_tables(n_tok):
    rows = n_tok // GRID_W
    row = np.repeat(np.arange(rows), GRID_W).astype(np.float32)
    col = np.tile(np.arange(GRID_W), rows).astype(np.float32)
    inv = np.float32(ROPE_BASE) ** (-np.arange(ROPE_FREQS, dtype=np.float32) / np.float32(ROPE_FREQS))
    ar, ac = row[:, None] * inv, col[:, None] * inv
    cr, sr, cc, sc = np.cos(ar), np.sin(ar), np.cos(ac), np.sin(ac)
    return (jnp.asarray(np.concatenate([cr, cr, cc, cc], axis=1), F32),
            jnp.asarray(np.concatenate([-sr, sr, -sc, sc], axis=1), F32))


def _outproj_kernel(hg_ref, att_ref, w_ref, x_ref, mod_ref, g_ref, xo_ref, h_ref):
    chunk = 256
    for r in range(x_ref.shape[0] // chunk):
        sl = slice(r * chunk, (r + 1) * chunk)
        mix = _dot(hg_ref[sl, :], w_ref[0:HG_WIDTH, :]) + _dot(att_ref[sl, :], w_ref[HG_WIDTH:, :])
        x = x_ref[sl, :] + mod_ref[0, 2:3, :] * mix
        xo_ref[sl, :] = x
        h_ref[sl, :] = _norm_modulate(x, g_ref[...], mod_ref[0, 3:4, :], mod_ref[0, 4:5, :]).astype(BF16)


def _out_projection(o_hg, o_att, w_bf16, x, mod, gain, *, mod_base, rows_per_mod):
    m = x.shape[0]
    tm = 512
    row = lambda i: (i, 0)
    return pl.pallas_call(
        _outproj_kernel,
        out_shape=(jax.ShapeDtypeStruct((m, D_MODEL), F32), jax.ShapeDtypeStruct((m, D_MODEL), BF16)),
        grid=(m // tm,),
        in_specs=[pl.BlockSpec((tm, HG_WIDTH), row),
                  pl.BlockSpec((tm, ATT_WIDTH), row),
                  pl.BlockSpec((HG_WIDTH + ATT_WIDTH, D_MODEL), lambda i: (0, 0)),
                  pl.BlockSpec((tm, D_MODEL), row),
                  pl.BlockSpec((1, N_MOD, D_MODEL), lambda i: (mod_base + (i * tm) // rows_per_mod, 0, 0)),
                  pl.BlockSpec((1, D_MODEL), lambda i: (0, 0))],
        out_specs=(pl.BlockSpec((tm, D_MODEL), row), pl.BlockSpec((tm, D_MODEL), row)),
        compiler_params=_cparams("parallel"),
        name="out_projection",
    )(o_hg, o_att, w_bf16, x, mod, gain)


def _prefix_count(x):
    n = x.shape[1]
    i = lax.broadcasted_iota(jnp.int32, (SEL_BLOCK, SEL_BLOCK), 0)
    j = lax.broadcasted_iota(jnp.int32, (SEL_BLOCK, SEL_BLOCK), 1)
    upper = jnp.where(i < j, 1.0, 0.0).astype(BF16)
    off = jnp.zeros((x.shape[0], 1), F32)
    outs = []
    for blk in range(n // SEL_BLOCK):
        xb = x[:, blk * SEL_BLOCK:(blk + 1) * SEL_BLOCK]
        outs.append(_dot(xb.astype(BF16), upper) + off)
        off = off + jnp.sum(xb, axis=-1, keepdims=True)
    return outs[0] if len(outs) == 1 else jnp.concatenate(outs, axis=1)


def _route_select(h_ref, w_ref, cap):
    logits = _dot_nt(w_ref[...], h_ref[...])
    ex = jnp.exp(logits - jnp.max(logits, axis=0, keepdims=True))
    aff = ex / jnp.sum(ex, axis=0, keepdims=True)
    bits = pltpu.bitcast(aff, jnp.int32)

    thr = jnp.zeros((N_EXPERTS, 1), jnp.int32)
    shift = 31
    while shift > 0:
        width = min(4, shift)
        shift -= width
        digit = jnp.zeros((N_EXPERTS, 1), jnp.int32)
        for j in range(1, 2 ** width):
            cnt = jnp.sum(jnp.where(bits >= (thr | (j << shift)), 1.0, 0.0), axis=-1, keepdims=True)
            digit = digit + jnp.where(cnt >= cap, 1, 0)
        thr = thr | (digit << shift)
    above = jnp.where(bits > thr, 1.0, 0.0)
    tied = jnp.where(bits == thr, 1.0, 0.0)
    room = cap - jnp.sum(above, axis=-1, keepdims=True)
    sel = above + tied * jnp.where(_prefix_count(tied) < room, 1.0, 0.0)
    return jnp.where(sel > 0.0, _prefix_count(sel), -1.0), aff


def _one_hot_gather(slot_rows, aff_rows, h_ref, rows, cap):
    n_tok = h_ref.shape[0]
    c = (lax.broadcasted_iota(jnp.int32, (rows, n_tok), 0) & (cap - 1)).astype(F32)
    hit = c == slot_rows
    x = _dot(jnp.where(hit, 1.0, 0.0).astype(BF16), h_ref[...]).astype(BF16)
    return x, jnp.sum(jnp.where(hit, aff_rows, 0.0), axis=-1, keepdims=True)


def _route_request_kernel(h_ref, w_ref, x_ref, g_ref, slot_ref, *, cap):
    n_tok = h_ref.shape[0]
    slot, aff = _route_select(h_ref, w_ref, cap)
    slot_ref[0] = slot
    per_row = lambda a: jnp.concatenate(
        [jnp.broadcast_to(a[e:e + 1, :], (cap, n_tok)) for e in range(N_EXPERTS)], axis=0)
    x, g = _one_hot_gather(per_row(slot), per_row(aff), h_ref, N_EXPERTS * cap, cap)
    for e in range(N_EXPERTS):
        x_ref[e] = x[e * cap:(e + 1) * cap, :]
        g_ref[e] = g[e * cap:(e + 1) * cap, :]


def _route_only_kernel(h_ref, w_ref, slot_ref, aff_ref, bounds_ref, *, cap):
    n_tok = h_ref.shape[0]
    slot, aff = _route_select(h_ref, w_ref, cap)
    slot_ref[0] = slot
    aff_ref[0] = aff
    n = lax.broadcasted_iota(jnp.int32, (n_tok, LANES), 0)
    t = lax.broadcasted_iota(jnp.int32, (n_tok, LANES), 1)
    before = jnp.where(n < t * TOK_BLOCK, 1.0, 0.0).astype(BF16)
    chosen = jnp.where(slot >= 0.0, 1.0, 0.0).astype(BF16)
    bounds_ref[0] = _dot(chosen, before).astype(jnp.int32)


def _gather_window_kernel(bounds_ref, slot_ref, aff_ref, h_ref, xh_ref, gh_ref, x_ref, g_ref, xo_ref, go_ref, *,
                          n_batch, group):
    b = pl.program_id(0)
    eg = pl.program_id(1)
    n_tok = h_ref.shape[0]
    nt = n_tok // TOK_BLOCK
    cap = x_ref.shape[1]
    region = TOK_BLOCK // group

    @pl.when(b >= n_batch)
    def _():
        x_ref[...] = xh_ref[...]
        g_ref[...] = gh_ref[...]

    @pl.when(b < n_batch)
    def _():
        x_ref[...] = jnp.zeros(x_ref.shape, BF16)
        g_ref[...] = jnp.zeros(g_ref.shape, F32)
        local_i = lax.broadcasted_iota(jnp.int32, (region, TOK_BLOCK), 0).astype(F32)
        over_i = lax.broadcasted_iota(jnp.int32, (TOK_BLOCK, TOK_BLOCK), 0)

        def token_block(t, carry):
            toks = pl.ds(pl.multiple_of(t * TOK_BLOCK, TOK_BLOCK), TOK_BLOCK)
            hits, gates, firsts, n_overs, slots, affs = [], [], [], [], [], []
            for k in range(group):
                e = eg * group + k
                start, length = _window(bounds_ref, (b * N_EXPERTS + e) * (nt + 1), t)
                first = jnp.minimum(start, cap - region)
                s = slot_ref[0, pl.ds(e, 1), toks]
                a = aff_ref[0, pl.ds(e, 1), toks]
                hit = local_i == jnp.where(s >= 0.0, s - first.astype(F32), -1.0)
                hits.append(jnp.where(hit, 1.0, 0.0).astype(BF16))
                gates.append(jnp.sum(jnp.where(hit, a, 0.0), axis=-1, keepdims=True))
                firsts.append(first)
                n_overs.append(start + length - first - region)
                slots.append(s)
                affs.append(a)
            xc = _dot(jnp.concatenate(hits, axis=0), h_ref[toks, :]).astype(BF16)
            for k in range(group):
                dst = pl.ds(pl.multiple_of(firsts[k], PIECE), region)
                x_ref[k, dst, :] = x_ref[k, dst, :] + xc[k * region:(k + 1) * region, :]
                g_ref[k, dst, :] = g_ref[k, dst, :] + gates[k]
            for k in range(group):
                @pl.when(n_overs[k] > 0)
                def _(k=k):
                    base = firsts[k] + region
                    hit = (over_i + base).astype(F32) == slots[k]
                    xo_ref[...] = _dot(jnp.where(hit, 1.0, 0.0).astype(BF16), h_ref[toks, :]).astype(BF16)
                    go_ref[...] = jnp.sum(jnp.where(hit, affs[k], 0.0), axis=-1, keepdims=True)

                    def place(i, carry):
                        src = pl.ds(pl.multiple_of(i * PIECE, PIECE), PIECE)
                        dst = pl.ds(pl.multiple_of(base + i * PIECE, PIECE), PIECE)
                        x_ref[k, dst, :] = x_ref[k, dst, :] + xo_ref[src, :]
                        g_ref[k, dst, :] = g_ref[k, dst, :] + go_ref[src, :]
                        return carry

                    lax.fori_loop(0, n_overs[k] // PIECE, place, 0)
            return carry

        lax.fori_loop(0, nt, token_block, 0)


def _route_gather_requests(h, w_router_t, *, n_batch, n_tok):
    cap = CAPACITY_FACTOR * n_tok // N_EXPERTS
    out_block = lambda w: pl.BlockSpec((N_EXPERTS, cap, w), lambda b: (0, b, 0))
    return pl.pallas_call(
        functools.partial(_route_request_kernel, cap=cap),
        out_shape=(jax.ShapeDtypeStruct((N_EXPERTS, n_batch * cap, D_MODEL), BF16),
                   jax.ShapeDtypeStruct((N_EXPERTS, n_batch * cap, 1), F32),
                   jax.ShapeDtypeStruct((n_batch, N_EXPERTS, n_tok), F32)),
        grid=(n_batch,),
        in_specs=[pl.BlockSpec((n_tok, D_MODEL), lambda b: (b, 0)),
                  pl.BlockSpec((N_EXPERTS, D_MODEL), lambda b: (0, 0))],
        out_specs=(out_block(D_MODEL), out_block(1), pl.BlockSpec((1, N_EXPERTS, n_tok), lambda b: (b, 0, 0))),
        compiler_params=_cparams("parallel"),
        name="route_gather_requests",
    )(h, w_router_t)


def _route_only(h, w_router_t, *, n_batch, n_tok):
    cap = CAPACITY_FACTOR * n_tok // N_EXPERTS
    nt = n_tok // TOK_BLOCK
    expert_major = pl.BlockSpec((1, N_EXPERTS, n_tok), lambda b: (b, 0, 0))
    slot, aff, bounds = pl.pallas_call(
        functools.partial(_route_only_kernel, cap=cap),
        out_shape=(jax.ShapeDtypeStruct((n_batch, N_EXPERTS, n_tok), F32),
                   jax.ShapeDtypeStruct((n_batch, N_EXPERTS, n_tok), F32),
                   jax.ShapeDtypeStruct((n_batch, N_EXPERTS, LANES), jnp.int32)),
        grid=(n_batch,),
        in_specs=[pl.BlockSpec((n_tok, D_MODEL), lambda b: (b, 0)),
                  pl.BlockSpec((N_EXPERTS, D_MODEL), lambda b: (0, 0))],
        out_specs=(expert_major, expert_major, pl.BlockSpec((1, N_EXPERTS, LANES), lambda b: (b, 0, 0))),
        compiler_params=_cparams("parallel"),
        name="route_select",
    )(h, w_router_t)
    return slot, aff, bounds[:, :, :nt + 1].reshape(-1)


def _gather_windows(bounds, slot, aff, h, x_head, g_head, *, n_batch, n_tok):
    cap = CAPACITY_FACTOR * n_tok // N_EXPERTS
    group = TOK_BLOCK // SLOT_REGION
    head_rows = x_head.shape[1]
    assert head_rows % cap == 0 and N_EXPERTS % group == 0
    n_head = head_rows // cap
    rows_total = head_rows + n_batch * cap
    last = n_batch - 1
    request = lambda b: jnp.minimum(b, last)
    out_row = lambda b: jnp.where(b < n_batch, n_head + b, b - n_batch)
    head_row = lambda b: jnp.maximum(b - n_batch, 0)
    out_block = lambda w: pl.BlockSpec((group, cap, w), lambda b, g, s: (g, out_row(b), 0))
    head_block = lambda w: pl.BlockSpec((group, cap, w), lambda b, g, s: (g, head_row(b), 0))
    expert_major = pl.BlockSpec((1, N_EXPERTS, n_tok), lambda b, g, s: (request(b), 0, 0))
    return pl.pallas_call(
        functools.partial(_gather_window_kernel, n_batch=n_batch, group=group),
        out_shape=(jax.ShapeDtypeStruct((N_EXPERTS, rows_total, D_MODEL), BF16),
                   jax.ShapeDtypeStruct((N_EXPERTS, rows_total, 1), F32)),
        grid_spec=pltpu.PrefetchScalarGridSpec(
            num_scalar_prefetch=1,
            grid=(n_batch + n_head, N_EXPERTS // group),
            in_specs=[expert_major, expert_major,
                      pl.BlockSpec((n_tok, D_MODEL), lambda b, g, s: (request(b), 0)),
                      head_block(D_MODEL), head_block(1)],
            out_specs=(out_block(D_MODEL), out_block(1)),
            scratch_shapes=[pltpu.VMEM((TOK_BLOCK, D_MODEL), BF16), pltpu.VMEM((TOK_BLOCK, 1), F32)]),
        compiler_params=_cparams("arbitrary", "arbitrary"),
        name="gather_windows",
    )(bounds, slot, aff, h, x_head, g_head)


def _moe_kernel(x_ref, g_ref, wg_ref, wu_ref, wd_ref, y_ref, hid_ref, *, n_ff, rows):
    s = pl.program_id(2)
    tf = wg_ref.shape[2]

    @pl.when(s < n_ff)
    def _():
        wg = wg_ref[0].astype(BF16)
        wu = wu_ref[0].astype(BF16)
        cols = pl.ds(pl.multiple_of(s * tf, tf), tf)
        for r in range(x_ref.shape[1] // rows):
            sl = slice(r * rows, (r + 1) * rows)
            x = x_ref[0, sl, :]
            hid_ref[sl, cols] = (_silu(_dot(x, wg)) * _dot(x, wu)).astype(BF16)

    @pl.when(s >= n_ff)
    def _():
        y = _dot(hid_ref[...], wd_ref[0].astype(BF16))
        y_ref[0] = (y * g_ref[0]).astype(BF16)


def _experts(x, gate, w_gate, w_up, w_down):
    n_rows = x.shape[1]
    tr, tf, tn = n_rows // 2, 512, 256
    n_ff, n_out = EXPERT_FF // tf, D_MODEL // tn
    up_tile = lambda e, r, s: (e, 0, jnp.minimum(s, n_ff - 1))
    out_tile = lambda s: jnp.maximum(s - n_ff, 0)
    return pl.pallas_call(
        functools.partial(_moe_kernel, n_ff=n_ff, rows=256),
        out_shape=jax.ShapeDtypeStruct((N_EXPERTS, n_rows, D_MODEL), BF16),
        grid=(N_EXPERTS, n_rows // tr, n_ff + n_out),
        in_specs=[pl.BlockSpec((1, tr, D_MODEL), lambda e, r, s: (e, r, 0), pipeline_mode=pl.Buffered(1)),
                  pl.BlockSpec((1, tr, 1), lambda e, r, s: (e, r, 0)),
                  pl.BlockSpec((1, D_MODEL, tf), up_tile),
                  pl.BlockSpec((1, D_MODEL, tf), up_tile),
                  pl.BlockSpec((1, EXPERT_FF, tn), lambda e, r, s: (e, 0, out_tile(s)))],
        out_specs=pl.BlockSpec((1, tr, tn), lambda e, r, s: (e, r, out_tile(s))),
        scratch_shapes=[pltpu.VMEM((tr, EXPERT_FF), BF16)],
        compiler_params=_cparams("parallel", "parallel", "arbitrary"),
        name="expert_swiglu",
    )(x, gate, w_gate, w_up, w_down)


def _window(bounds_ref, base, t):
    p0 = bounds_ref[base + t]
    p1 = bounds_ref[base + t + 1]
    start = (p0 // PIECE) * PIECE
    return start, jnp.where(p1 > p0, ((p1 - start + PIECE - 1) // PIECE) * PIECE, 0)


def _combine_kernel(bounds_ref, y_ref, slot_ref, x_ref, mod_ref, g_ref, o_ref, ybuf_ref, obuf_ref, acc_ref, *,
                    nt, region):
    b = pl.program_id(0)
    t = pl.program_id(1)
    tt = x_ref.shape[0]
    cap = y_ref.shape[1]
    local_i = lax.broadcasted_iota(jnp.int32, (region, tt), 0).astype(F32)
    hits = []
    for e in range(N_EXPERTS):
        start, _ = _window(bounds_ref, (b * N_EXPERTS + e) * (nt + 1), t)
        first = jnp.minimum(start, cap - region)
        ybuf_ref[e * region:(e + 1) * region, :] = y_ref[e, pl.ds(pl.multiple_of(first, PIECE), region), :]
        s = slot_ref[0, e:e + 1, :]
        hit = local_i == jnp.where(s >= 0.0, s - first.astype(F32), -1.0)
        hits.append(jnp.where(hit, 1.0, 0.0).astype(BF16))
    acc_ref[...] = _dot_tn(jnp.concatenate(hits, axis=0), ybuf_ref[...])
    over_i = lax.broadcasted_iota(jnp.int32, (TOK_BLOCK, tt), 0)

    def overflow(e, carry):
        start, length = _window(bounds_ref, (b * N_EXPERTS + e) * (nt + 1), t)
        base = jnp.minimum(start, cap - region) + region
        n_over = start + length - base

        @pl.when(n_over > 0)
        def _():
            def copy(i, carry):
                obuf_ref[pl.ds(pl.multiple_of(i * PIECE, PIECE), PIECE), :] = (
                    y_ref[e, pl.ds(pl.multiple_of(base + i * PIECE, PIECE), PIECE), :])
                return carry

            def clear(i, carry):
                obuf_ref[pl.ds(pl.multiple_of(n_over + i * PIECE, PIECE), PIECE), :] = jnp.zeros((PIECE, D_MODEL), BF16)
                return carry

            lax.fori_loop(0, n_over // PIECE, copy, 0)
            lax.fori_loop(0, (TOK_BLOCK - n_over) // PIECE, clear, 0)
            hit = (over_i + base).astype(F32) == slot_ref[0, pl.ds(e, 1), :]
            acc_ref[...] += _dot_tn(jnp.where(hit, 1.0, 0.0).astype(BF16), obuf_ref[...])

        return carry

    lax.fori_loop(0, N_EXPERTS, overflow, 0)
    x = x_ref[...] + mod_ref[0, 5:6, :] * acc_ref[...]
    var = jnp.mean(x * x, axis=-1, keepdims=True)
    o_ref[...] = x * lax.rsqrt(var + NORM_EPS) * g_ref[...]


def _combine(bounds, y, slot, x_mid, mod, final_g, *, n_batch, n_tok, row_block_off, mod_base, mod_per_batch):
    cap = CAPACITY_FACTOR * n_tok // N_EXPERTS
    tt = TOK_BLOCK
    nt = n_tok // tt
    region = min(SLOT_REGION, cap)
    assert cap - region <= TOK_BLOCK
    return pl.pallas_call(
        functools.partial(_combine_kernel, nt=nt, region=region),
        out_shape=jax.ShapeDtypeStruct((n_batch * n_tok, D_MODEL), F32),
        grid_spec=pltpu.PrefetchScalarGridSpec(
            num_scalar_prefetch=1,
            grid=(n_batch, nt),
            in_specs=[pl.BlockSpec((N_EXPERTS, cap, D_MODEL), lambda b, t, s: (0, row_block_off + b, 0)),
                      pl.BlockSpec((1, N_EXPERTS, tt), lambda b, t, s: (b, 0, t)),
                      pl.BlockSpec((tt, D_MODEL), lambda b, t, s: (b * nt + t, 0)),
                      pl.BlockSpec((1, N_MOD, D_MODEL), lambda b, t, s: (mod_base + b * mod_per_batch, 0, 0)),
                      pl.BlockSpec((1, D_MODEL), lambda b, t, s: (0, 0))],
            out_specs=pl.BlockSpec((tt, D_MODEL), lambda b, t, s: (b * nt + t, 0)),
            scratch_shapes=[pltpu.VMEM((N_EXPERTS * region, D_MODEL), BF16),
                            pltpu.VMEM((TOK_BLOCK, D_MODEL), BF16), pltpu.VMEM((tt, D_MODEL), F32)]),
        compiler_params=_cparams("arbitrary", "arbitrary"),
        name="combine_final_norm",
    )(bounds, y, slot, x_mid, mod, final_g)


def kernel(x_prompt, x_sample, cache_k, cache_v, state_hgrn, c, c_ctx, w_ada, b_ada, norm1_g, w_in, hg_lb,
           hg_norm_g, attn_sink, w_out, norm2_g, w_router, w_gate, w_up, w_down, final_norm_g):
    n_p, t_p, _ = x_prompt.shape
    n_s, t_s, _ = x_sample.shape
    assert w_ada.shape[0] == 1 and 1 + n_s <= COND_ROWS
    layer = 0

    cond = jnp.zeros((COND_ROWS, D_MODEL), F32).at[0].set(c_ctx).at[1:1 + n_s].set(c)
    mod = _ada_modulation(cond, w_ada[layer], b_ada[layer]).reshape(COND_ROWS, N_MOD, D_MODEL)
    lb = jnp.cumsum(jax.nn.softmax(hg_lb.astype(F32), axis=0), axis=0)[layer]
    w_in_bf16 = w_in[layer].astype(BF16)
    w_out_bf16 = w_out[layer].astype(BF16)
    w_router_t = w_router[layer].T.astype(BF16)
    norm1 = norm1_g[layer].reshape(1, D_MODEL)
    norm2 = norm2_g[layer].reshape(1, D_MODEL)
    hg_gain = hg_norm_g[layer].reshape(1, HG_WIDTH)
    final_g = final_norm_g.reshape(1, D_MODEL)
    sink = attn_sink[layer]
    cos, sin_signed = _rope_tables(t_s)

    xp = x_prompt.reshape(n_p * t_p, D_MODEL)
    xs = x_sample.reshape(n_s * t_s, D_MODEL)
    groups = dict(p=dict(mod_base=0, rows_per_mod=n_p * t_p), s=dict(mod_base=1, rows_per_mod=t_s))

    proj_p = _in_projection(xp, mod, norm1, w_in_bf16, **groups["p"])
    proj_s = _in_projection(xs, mod, norm1, w_in_bf16, **groups["s"])

    ohg_p, new_state = _hgrn(proj_p, lb, hg_gain, None, n_batch=n_p, n_tok=t_p)
    ohg_s, _ = _hgrn(proj_s, lb, hg_gain, state_hgrn[:, layer:layer + 1].astype(F32), n_batch=n_s, n_tok=t_s)

    oatt_p = _context_attention(proj_p, sink, n_batch=n_p, n_tok=t_p)
    n_ctx = cache_k.shape[2]
    oatt_s = _window_attention(proj_s, cache_k[:, layer].reshape(n_s, n_ctx, KV_WIDTH),
                               cache_v[:, layer].reshape(n_s, n_ctx, KV_WIDTH), sink, cos, sin_signed,
                               n_batch=n_s, n_tok=t_s)

    xmid_p, h2_p = _out_projection(ohg_p, oatt_p, w_out_bf16, xp, mod, norm2, **groups["p"])
    xmid_s, h2_s = _out_projection(ohg_s, oatt_s, w_out_bf16, xs, mod, norm2, **groups["s"])

    cap_p = CAPACITY_FACTOR * t_p // N_EXPERTS
    cap_s = CAPACITY_FACTOR * t_s // N_EXPERTS
    off_s = n_p * cap_p // cap_s
    xg_p, gate_p, slot_p = _route_gather_requests(h2_p, w_router_t, n_batch=n_p, n_tok=t_p)
    slot_s, aff_s, bounds_s = _route_only(h2_s, w_router_t, n_batch=n_s, n_tok=t_s)
    xg, gate = _gather_windows(bounds_s, slot_s, aff_s, h2_s, xg_p, gate_p, n_batch=n_s, n_tok=t_s)
    assert t_p == TOK_BLOCK
    bounds_p = jnp.tile(jnp.array([0, cap_p], jnp.int32), n_p * N_EXPERTS)

    y = _experts(xg, gate, w_gate[layer], w_up[layer], w_down[layer])

    y_prompt = _combine(bounds_p, y, slot_p, xmid_p, mod, final_g, n_batch=n_p, n_tok=t_p, row_block_off=0,
                        mod_base=0, mod_per_batch=0)
    y_sample = _combine(bounds_s, y, slot_s, xmid_s, mod, final_g, n_batch=n_s, n_tok=t_s, row_block_off=off_s,
                        mod_base=1, mod_per_batch=1)

    k_col = 5 * HG_WIDTH + ATT_WIDTH
    new_k = proj_p[:, k_col:k_col + KV_WIDTH].reshape(n_p, 1, t_p, ATT_KV_HEADS, HEAD_DIM)
    new_v = proj_p[:, k_col + KV_WIDTH:k_col + 2 * KV_WIDTH].reshape(n_p, 1, t_p, ATT_KV_HEADS, HEAD_DIM)
    return (y_prompt.reshape(n_p, t_p, D_MODEL), y_sample.reshape(n_s, t_s, D_MODEL), new_k, new_v, new_state)
```

```python
import functools

import jax
import jax.numpy as jnp
import numpy as np
from jax import lax
from jax.experimental import pallas as pl
from jax.experimental.pallas import tpu as pltpu

F32 = jnp.float32
BF16 = jnp.bfloat16

D_MODEL = 2048
HG_WIDTH = 1024
HG_HEADS = 8
HEAD_DIM = 128
ATT_HEADS = 8
ATT_KV_HEADS = 2
ATT_GROUP = ATT_HEADS // ATT_KV_HEADS
KV_WIDTH = ATT_KV_HEADS * HEAD_DIM
ATT_WIDTH = ATT_HEADS * HEAD_DIM
ATT_BLOCK = 128
GRID_W = 64
ROPE_BASE = 10000.0
ROPE_FREQS = HEAD_DIM // 4
N_EXPERTS = 16
CAPACITY_FACTOR = 2
EXPERT_FF = 5632
NORM_EPS = 1e-6
IN_WIDTH = 5 * HG_WIDTH + ATT_WIDTH + 2 * KV_WIDTH
N_MOD = 6
COND_ROWS = 16

HG_CHUNK = 256
HG_DIAG = 8
SEL_BLOCK = 256
TOK_BLOCK = 256
PIECE = 16
SLOT_REGION = 64
LANES = 128

V7X_VMEM_BYTES = 64 * 1024 * 1024
VMEM_LIMIT = V7X_VMEM_BYTES - 8 * 1024 * 1024


def _cparams(*sem):
    return pltpu.CompilerParams(dimension_semantics=sem, vmem_limit_bytes=VMEM_LIMIT)


def _sigmoid(x):
    return 1.0 / (1.0 + jnp.exp(-x))


def _silu(x):
    return x * _sigmoid(x)


def _dot(a, b):
    return jnp.dot(a, b, preferred_element_type=F32)


def _dot_nt(a, b):
    return lax.dot_general(a, b, (((1,), (1,)), ((), ())), preferred_element_type=F32)


def _dot_tn(a, b):
    return lax.dot_general(a, b, (((0,), (0,)), ((), ())), preferred_element_type=F32)


def _ada_kernel(c_ref, w_ref, b_ref, o_ref):
    s = _silu(c_ref[...]).astype(BF16)
    o_ref[...] = _dot(s, w_ref[...].astype(BF16)) + b_ref[...]


def _ada_modulation(cond, w_ada, b_ada):
    tn = 1024
    n = w_ada.shape[1]
    return pl.pallas_call(
        _ada_kernel,
        out_shape=jax.ShapeDtypeStruct((COND_ROWS, n), F32),
        grid=(n // tn,),
        in_specs=[pl.BlockSpec((COND_ROWS, D_MODEL), lambda j: (0, 0)),
                  pl.BlockSpec((D_MODEL, tn), lambda j: (0, j)),
                  pl.BlockSpec((1, tn), lambda j: (0, j))],
        out_specs=pl.BlockSpec((COND_ROWS, tn), lambda j: (0, j)),
        compiler_params=_cparams("arbitrary"),
        name="ada_modulation",
    )(cond, w_ada, b_ada.reshape(1, n))


def _norm_modulate(x, gain, shift, scale):
    var = jnp.mean(x * x, axis=-1, keepdims=True)
    return (x * lax.rsqrt(var + NORM_EPS) * gain) * (1.0 + scale) + shift


def _inproj_kernel(x_ref, mod_ref, g_ref, w_ref, o_ref, h_ref, *, rows):
    w = w_ref[...]
    chunks = [slice(r * rows, (r + 1) * rows) for r in range(x_ref.shape[0] // rows)]

    @pl.when(pl.program_id(1) == 0)
    def _():
        shift = mod_ref[0, 0:1, :]
        scale = mod_ref[0, 1:2, :]
        gain = g_ref[...]
        for sl in chunks:
            h = _norm_modulate(x_ref[sl, :], gain, shift, scale).astype(BF16)
            h_ref[sl, :] = h
            o_ref[sl, :] = _dot(h, w)

    @pl.when(pl.program_id(1) > 0)
    def _():
        for sl in chunks:
            o_ref[sl, :] = _dot(h_ref[sl, :], w)


def _in_projection(x, mod, gain, w, *, mod_base, rows_per_mod):
    m = x.shape[0]
    tm, tn = 1024, 512
    return pl.pallas_call(
        functools.partial(_inproj_kernel, rows=256),
        out_shape=jax.ShapeDtypeStruct((m, IN_WIDTH), F32),
        grid=(m // tm, IN_WIDTH // tn),
        in_specs=[pl.BlockSpec((tm, D_MODEL), lambda i, j: (i, 0)),
                  pl.BlockSpec((1, N_MOD, D_MODEL), lambda i, j: (mod_base + (i * tm) // rows_per_mod, 0, 0)),
                  pl.BlockSpec((1, D_MODEL), lambda i, j: (0, 0)),
                  pl.BlockSpec((D_MODEL, tn), lambda i, j: (0, j))],
        out_specs=pl.BlockSpec((tm, tn), lambda i, j: (i, j)),
        scratch_shapes=[pltpu.VMEM((tm, D_MODEL), BF16)],
        compiler_params=_cparams("parallel", "arbitrary"),
        name="in_projection",
    )(x, mod, gain, w)


def _hgrn_codes(reverse):
    L = HG_CHUNK
    t = lax.broadcasted_iota(jnp.int32, (L, L), 0)
    s = lax.broadcasted_iota(jnp.int32, (L, L), 1)
    code = jnp.where(t == s, 1, 0)
    h = L // 2
    while h >= 1:
        same = (t & ~(2 * h - 1)) == (s & ~(2 * h - 1))
        t_hi = (t & h) != 0
        s_hi = (s & h) != 0
        pair = (s_hi & ~t_hi) if reverse else (t_hi & ~s_hi)
        code = jnp.where(same & pair, h * 16, code)
        h //= 2
    return code


def _cumsum_rows(tri_bf16, g):
    g1 = g.astype(BF16)
    g2 = (g - g1.astype(F32)).astype(BF16)
    s = _dot(tri_bf16, jnp.concatenate([g1, g2], axis=1))
    return s[:, :HEAD_DIM] + s[:, HEAD_DIM:]


def _hgrn_intra(q, k, f, b, v_bf, code, reverse):
    L = HG_CHUNK
    G = HG_DIAG
    q_bf = q.astype(BF16)
    k_bf = k.astype(BF16)

    def level(h, ref, att):
        neg_abs = pltpu.bitcast(pltpu.bitcast(b - ref, jnp.int32) | jnp.int32(-2 ** 31), F32)
        e = jnp.exp2(neg_abs.astype(BF16))
        return jnp.where(code == h * 16, _dot_nt(q_bf * e, k_bf * e), att)

    att = jnp.where(code == 1, _dot_nt(q_bf, k_bf), 0.0)
    h = L // 2
    while h >= G:
        parts = []
        for p in range(L // (2 * h)):
            m = p * 2 * h + (h if reverse else h - 1)
            parts.append(jnp.broadcast_to(b[m:m + 1, :], (2 * h, HEAD_DIM)))
        att = level(h, parts[0] if len(parts) == 1 else jnp.concatenate(parts, axis=0), att)
        h //= 2

    b3 = b.reshape(L // G, G, HEAD_DIM)

    def group_row(r):
        return jnp.broadcast_to(b3[:, r:r + 1, :], (L // G, G, HEAD_DIM)).reshape(L, HEAD_DIM)

    row = lax.broadcasted_iota(jnp.int32, (L, HEAD_DIM), 0)
    att = level(4, group_row(4 if reverse else 3), att)
    lo, hi = (2, 6) if reverse else (1, 5)
    att = level(2, jnp.where((row & 4) == 0, group_row(lo), group_row(hi)), att)
    att = jnp.where(code == 16, _dot_nt(q_bf * f.astype(BF16), k_bf), att)
    return _dot(att.astype(BF16), v_bf)


def _hgrn_prepare(q, f, v, code, tri, reverse):
    L = HG_CHUNK
    k = 1.0 - f
    b = _cumsum_rows(tri, jnp.log2(f))
    b_tot = b[0:1, :] if reverse else b[L - 1:L, :]
    v_bf = v.astype(BF16)
    o_intra = _hgrn_intra(q, k, f, b, v_bf, code, reverse)
    q_in = (q * jnp.exp2(b)).astype(BF16)
    k_out = (k * jnp.exp2(b_tot - b)).astype(BF16)
    return o_intra, q_in, k_out, v_bf, jnp.exp2(b_tot)


def _hgrn_advance(st, prepared):
    o_intra, q_in, k_out, v_bf, decay = prepared
    return o_intra + _dot_nt(q_in, st.astype(BF16)), st * decay + _dot_tn(v_bf, k_out)


def _hgrn_kernel(*refs, n_tok, has_state, heads):
    if has_state:
        (q_ref, ff_ref, fb_ref, v_ref, gate_ref, lb_ref, ng_ref, s0_ref,
         o_ref, sout_ref, of_ref, ob_ref, code_ref) = refs
    else:
        (q_ref, ff_ref, fb_ref, v_ref, gate_ref, lb_ref, ng_ref,
         o_ref, sout_ref, of_ref, ob_ref, code_ref) = refs
    L = HG_CHUNK
    nc = n_tok // L

    @pl.when((pl.program_id(0) == 0) & (pl.program_id(1) == 0))
    def _():
        code_ref[0] = _hgrn_codes(False)
        code_ref[1] = _hgrn_codes(True)

    code_f = code_ref[0]
    code_b = code_ref[1]
    ti = lax.broadcasted_iota(jnp.int32, (L, L), 0)
    si = lax.broadcasted_iota(jnp.int32, (L, L), 1)
    tri_f = jnp.where(si <= ti, 1.0, 0.0).astype(BF16)
    tri_b = jnp.where(si >= ti, 1.0, 0.0).astype(BF16)

    def one_head(hd, carry):
        cols = pl.ds(pl.multiple_of(hd * HEAD_DIM, HEAD_DIM), HEAD_DIM)
        lb_f = lb_ref[0:1, cols]
        lb_b = lb_ref[1:2, cols]

        def prepare(row, f_ref, lb, code, tri, reverse):
            sl = pl.ds(row, L)
            q = _silu(q_ref[sl, cols])
            f = lb + (1.0 - lb) * _sigmoid(f_ref[sl, cols])
            return sl, _hgrn_prepare(q, f, v_ref[sl, cols], code, tri, reverse)

        gain = ng_ref[:, cols]

        def finish(sl, o):
            var = jnp.mean(o * o, axis=-1, keepdims=True)
            o = (o * lax.rsqrt(var + NORM_EPS) * gain) * _silu(gate_ref[sl, cols])
            o_ref[sl, cols] = o.astype(BF16)

        def make_body(other_direction):
            def body(c, states):
                st_f, st_b = states
                fwd, bwd = [], []
                for u in range(unroll):
                    cu = c * unroll + u
                    fwd.append(prepare(pl.multiple_of(cu * L, L), ff_ref, lb_f, code_f, tri_f, False))
                    bwd.append(prepare(pl.multiple_of((nc - 1 - cu) * L, L), fb_ref, lb_b, code_b, tri_b, True))
                out_f, out_b = [], []
                for (sl_f, prep_f), (sl_b, prep_b) in zip(fwd, bwd):
                    o_f, st_f = _hgrn_advance(st_f, prep_f)
                    o_b, st_b = _hgrn_advance(st_b, prep_b)
                    out_f.append((sl_f, o_f))
                    out_b.append((sl_b, o_b))
                if other_direction == "later":
                    for sl, o in out_f:
                        of_ref[sl, :] = o
                    for sl, o in out_b:
                        ob_ref[sl, :] = o
                elif other_direction == "now":
                    for u in range(unroll):
                        finish(out_f[u][0], out_f[u][1] + out_b[unroll - 1 - u][1])
                else:
                    for sl, o in out_f:
                        finish(sl, o + ob_ref[sl, :])
                    for sl, o in out_b:
                        finish(sl, of_ref[sl, :] + o)
                return st_f, st_b
            return body

        unroll = min(nc, 8)
        assert nc % unroll == 0
        n_iter = nc // unroll
        half = n_iter // 2
        if has_state:
            states = (s0_ref[0, 0, 0, hd].T, s0_ref[0, 0, 1, hd].T)
        else:
            states = (jnp.zeros((HEAD_DIM, HEAD_DIM), F32), jnp.zeros((HEAD_DIM, HEAD_DIM), F32))
        states = lax.fori_loop(0, half, make_body("later"), states)
        if n_iter % 2:
            states = make_body("now")(half, states)
        st_f, st_b = lax.fori_loop(half + n_iter % 2, n_iter, make_body("earlier"), states)
        sout_ref[0, 0, 0, hd] = st_f.T
        sout_ref[0, 0, 1, hd] = st_b.T
        return carry

    lax.fori_loop(0, heads, one_head, 0)


def _hgrn(proj, lb, norm_g, state, *, n_batch, n_tok):
    heads = HG_HEADS
    while heads > 1 and 5 * 2 * n_tok * heads * HEAD_DIM * 4 > VMEM_LIMIT // 2:
        heads //= 2
    width = heads * HEAD_DIM
    groups = HG_HEADS // heads
    col = lambda k: (lambda b, h: (b, k * groups + h))
    tok_spec = lambda k: pl.BlockSpec((n_tok, width), col(k))
    st_spec = pl.BlockSpec((1, 1, 2, heads, HEAD_DIM, HEAD_DIM), lambda b, h: (b, 0, 0, h, 0, 0))
    has_state = state is not None
    return pl.pallas_call(
        functools.partial(_hgrn_kernel, n_tok=n_tok, has_state=has_state, heads=heads),
        out_shape=(jax.ShapeDtypeStruct((n_batch * n_tok, HG_WIDTH), BF16),
                   jax.ShapeDtypeStruct((n_batch, 1, 2, HG_HEADS, HEAD_DIM, HEAD_DIM), F32)),
        grid=(n_batch, groups),
        in_specs=[tok_spec(0), tok_spec(1), tok_spec(2), tok_spec(3), tok_spec(4),
                  pl.BlockSpec((2, width), lambda b, h: (0, h)),
                  pl.BlockSpec((1, width), lambda b, h: (0, h))] + ([st_spec] if has_state else []),
        out_specs=(pl.BlockSpec((n_tok, width), lambda b, h: (b, h)), st_spec),
        scratch_shapes=[pltpu.VMEM((n_tok, HEAD_DIM), F32), pltpu.VMEM((n_tok, HEAD_DIM), F32),
                        pltpu.VMEM((2, HG_CHUNK, HG_CHUNK), jnp.int32)],
        compiler_params=_cparams("arbitrary", "arbitrary"),
        name="hgrn2_scan",
    )(proj, proj, proj, proj, proj, lb, norm_g, *((state,) if has_state else ()))


def _stack_heads(x, kvh):
    return jnp.concatenate(
        [x[:, (kvh * ATT_GROUP + g) * HEAD_DIM:(kvh * ATT_GROUP + g + 1) * HEAD_DIM] for g in range(ATT_GROUP)],
        axis=0)


def _sink_column(sink_ref, kvh, rows):
    return jnp.concatenate(
        [jnp.full((rows, 1), sink_ref[kvh * ATT_GROUP + g], F32) for g in range(ATT_GROUP)],
        axis=0) * (1.0 / SOFTMAX_SCALE)


SOFTMAX_SCALE = HEAD_DIM ** -0.5
EXP2_SCALE = SOFTMAX_SCALE * 1.4426950408889634


def _ones_column(rows):
    lane = lax.broadcasted_iota(jnp.int32, (rows, HEAD_DIM), 1)
    return jnp.where(lane == 0, 1.0, 0.0).astype(BF16)


def _softmax_av(scores, values, sink_col):
    m = sink_col
    for s in scores:
        m = jnp.maximum(m, jnp.max(s, axis=-1, keepdims=True))
    acc = None
    for s, v in zip(scores, values):
        pv = _dot(jnp.exp2(((s - m) * EXP2_SCALE).astype(BF16)), v)
        acc = pv if acc is None else acc + pv
    denom = acc[:, HEAD_DIM:HEAD_DIM + 1] + jnp.exp2((sink_col - m) * EXP2_SCALE)
    return acc[:, :HEAD_DIM] / denom


def _ctx_attn_kernel(sink_ref, q_ref, k_ref, v_ref, o_ref):
    rows = q_ref.shape[0]
    q_all = q_ref[...]
    ones = _ones_column(k_ref.shape[0])
    for kvh in range(ATT_KV_HEADS):
        q = _stack_heads(q_all, kvh).astype(BF16)
        k = k_ref[:, kvh * HEAD_DIM:(kvh + 1) * HEAD_DIM].astype(BF16)
        v = jnp.concatenate([v_ref[:, kvh * HEAD_DIM:(kvh + 1) * HEAD_DIM].astype(BF16), ones], axis=1)
        o = _softmax_av([_dot_nt(q, k)], [v], _sink_column(sink_ref, kvh, rows))
        for g in range(ATT_GROUP):
            hd = kvh * ATT_GROUP + g
            o_ref[:, hd * HEAD_DIM:(hd + 1) * HEAD_DIM] = o[g * rows:(g + 1) * rows, :].astype(BF16)


def _context_attention(proj, sink, *, n_batch, n_tok):
    q_col = 5 * HG_WIDTH // ATT_WIDTH
    k_col = (5 * HG_WIDTH + ATT_WIDTH) // KV_WIDTH
    return pl.pallas_call(
        _ctx_attn_kernel,
        out_shape=jax.ShapeDtypeStruct((n_batch * n_tok, ATT_WIDTH), BF16),
        grid_spec=pltpu.PrefetchScalarGridSpec(
            num_scalar_prefetch=1,
            grid=(n_batch,),
            in_specs=[pl.BlockSpec((n_tok, ATT_WIDTH), lambda b, s: (b, q_col)),
                      pl.BlockSpec((n_tok, KV_WIDTH), lambda b, s: (b, k_col)),
                      pl.BlockSpec((n_tok, KV_WIDTH), lambda b, s: (b, k_col + 1))],
            out_specs=pl.BlockSpec((n_tok, ATT_WIDTH), lambda b, s: (b, 0))),
        compiler_params=_cparams("parallel"),
        name="context_attention",
    )(sink, proj, proj, proj)


def _rope(x, cos, sin_signed, even_group):
    partner = jnp.where(even_group, pltpu.roll(x, HEAD_DIM - ROPE_FREQS, 1), pltpu.roll(x, ROPE_FREQS, 1))
    return x * cos + partner * sin_signed


def _win_attn_kernel(sink_ref, q_ref, k_ref, v_ref, ck_ref, cv_ref, cos_ref, sin_ref, o_ref,
                     kpad_ref, vpad_ref, *, n_tok):
    blk = ATT_BLOCK
    nb = n_tok // blk
    i = pl.program_id(1)
    lane = lax.broadcasted_iota(jnp.int32, (blk, HEAD_DIM), 1)
    even_group = (lane & ROPE_FREQS) == 0

    @pl.when(i == 0)
    def _():
        kpad_ref[0:blk, :] = jnp.zeros((blk, KV_WIDTH), BF16)
        kpad_ref[blk + n_tok:2 * blk + n_tok, :] = jnp.zeros((blk, KV_WIDTH), BF16)
        vpad_ref[0:blk, :] = jnp.zeros((blk, 2 * KV_WIDTH), BF16)
        vpad_ref[blk + n_tok:2 * blk + n_tok, :] = jnp.zeros((blk, 2 * KV_WIDTH), BF16)
        ones = _ones_column(blk)

        def body(r, carry):
            src = pl.ds(pl.multiple_of(r * blk, blk), blk)
            dst = pl.ds(pl.multiple_of((r + 1) * blk, blk), blk)
            cos = cos_ref[src, :]
            sin = sin_ref[src, :]
            for kvh in range(ATT_KV_HEADS):
                cols = slice(kvh * HEAD_DIM, (kvh + 1) * HEAD_DIM)
                kpad_ref[dst, cols] = _rope(k_ref[src, cols], cos, sin, even_group).astype(BF16)
                vpad_ref[dst, 2 * kvh * HEAD_DIM:(2 * kvh + 1) * HEAD_DIM] = v_ref[src, cols].astype(BF16)
                vpad_ref[dst, (2 * kvh + 1) * HEAD_DIM:(2 * kvh + 2) * HEAD_DIM] = ones
            return carry

        lax.fori_loop(0, nb, body, 0)

    ctx_ones = _ones_column(ck_ref.shape[1])
    r = lax.broadcasted_iota(jnp.int32, (blk, 3 * blk), 0)
    j = lax.broadcasted_iota(jnp.int32, (blk, 3 * blk), 1)
    blocks = q_ref.shape[0] // blk

    def query_block(u, carry):
        qb = i * blocks + u
        rows = pl.ds(pl.multiple_of(qb * blk, blk), blk)
        cos = cos_ref[rows, :]
        sin = sin_ref[rows, :]
        band = pl.ds(pl.multiple_of(qb * blk, blk), 3 * blk)
        kpos = j + (qb - 1) * blk
        valid = (j >= r) & (j <= r + 2 * blk) & (kpos >= 0) & (kpos < n_tok)
        valid = jnp.concatenate([valid] * ATT_GROUP, axis=0)
        local = pl.ds(pl.multiple_of(u * blk, blk), blk)
        q_all = q_ref[local, :]
        for kvh in range(ATT_KV_HEADS):
            cols = slice(kvh * HEAD_DIM, (kvh + 1) * HEAD_DIM)
            q = jnp.concatenate(
                [_rope(q_all[:, (kvh * ATT_GROUP + g) * HEAD_DIM:(kvh * ATT_GROUP + g + 1) * HEAD_DIM],
                       cos, sin, even_group) for g in range(ATT_GROUP)], axis=0).astype(BF16)
            s_ctx = _dot_nt(q, ck_ref[0, :, cols].astype(BF16))
            s_loc = jnp.where(valid, _dot_nt(q, kpad_ref[band, cols]), -jnp.inf)
            v_ctx = jnp.concatenate([cv_ref[0, :, cols].astype(BF16), ctx_ones], axis=1)
            v_loc = vpad_ref[band, 2 * kvh * HEAD_DIM:(2 * kvh + 2) * HEAD_DIM]
            o = _softmax_av([s_ctx, s_loc], [v_ctx, v_loc], _sink_column(sink_ref, kvh, blk))
            for g in range(ATT_GROUP):
                hd = kvh * ATT_GROUP + g
                o_ref[local, hd * HEAD_DIM:(hd + 1) * HEAD_DIM] = o[g * blk:(g + 1) * blk, :].astype(BF16)
        return carry

    lax.fori_loop(0, blocks, query_block, 0)


def _window_attention(proj, cache_k, cache_v, sink, cos, sin_signed, *, n_batch, n_tok):
    tq = 4 * ATT_BLOCK
    steps = n_tok // tq
    n_ctx = cache_k.shape[1]
    q_col = 5 * HG_WIDTH // ATT_WIDTH
    k_col = (5 * HG_WIDTH + ATT_WIDTH) // KV_WIDTH
    table = pl.BlockSpec((n_tok, HEAD_DIM), lambda b, i, s: (0, 0))
    cache = pl.BlockSpec((1, n_ctx, KV_WIDTH), lambda b, i, s: (b, 0, 0))
    return pl.pallas_call(
        functools.partial(_win_attn_kernel, n_tok=n_tok),
        out_shape=jax.ShapeDtypeStruct((n_batch * n_tok, ATT_WIDTH), BF16),
        grid_spec=pltpu.PrefetchScalarGridSpec(
            num_scalar_prefetch=1,
            grid=(n_batch, steps),
            in_specs=[pl.BlockSpec((tq, ATT_WIDTH), lambda b, i, s: (b * steps + i, q_col)),
                      pl.BlockSpec((n_tok, KV_WIDTH), lambda b, i, s: (b, k_col)),
                      pl.BlockSpec((n_tok, KV_WIDTH), lambda b, i, s: (b, k_col + 1)),
                      cache, cache, table, table],
            out_specs=pl.BlockSpec((tq, ATT_WIDTH), lambda b, i, s: (b * steps + i, 0)),
            scratch_shapes=[pltpu.VMEM((n_tok + 2 * ATT_BLOCK, KV_WIDTH), BF16),
                            pltpu.VMEM((n_tok + 2 * ATT_BLOCK, 2 * KV_WIDTH), BF16)]),
        compiler_params=_cparams("parallel", "arbitrary"),
        name="window_attention",
    )(sink, proj, proj, proj, cache_k, cache_v, cos, sin_signed)


def _rope_tables(n_tok):
    rows = n_tok // GRID_W
    row = np.repeat(np.arange(rows), GRID_W).astype(np.float32)
    col = np.tile(np.arange(GRID_W), rows).astype(np.float32)
    inv = np.float32(ROPE_BASE) ** (-np.arange(ROPE_FREQS, dtype=np.float32) / np.float32(ROPE_FREQS))
    ar, ac = row[:, None] * inv, col[:, None] * inv
    cr, sr, cc, sc = np.cos(ar), np.sin(ar), np.cos(ac), np.sin(ac)
    return (jnp.asarray(np.concatenate([cr, cr, cc, cc], axis=1), F32),
            jnp.asarray(np.concatenate([-sr, sr, -sc, sc], axis=1), F32))


def _outproj_kernel(hg_ref, att_ref, w_ref, x_ref, mod_ref, g_ref, xo_ref, h_ref):
    chunk = 256
    for r in range(x_ref.shape[0] // chunk):
        sl = slice(r * chunk, (r + 1) * chunk)
        mix = _dot(hg_ref[sl, :], w_ref[0:HG_WIDTH, :]) + _dot(att_ref[sl, :], w_ref[HG_WIDTH:, :])
        x = x_ref[sl, :] + mod_ref[0, 2:3, :] * mix
        xo_ref[sl, :] = x
        h_ref[sl, :] = _norm_modulate(x, g_ref[...], mod_ref[0, 3:4, :], mod_ref[0, 4:5, :]).astype(BF16)


def _out_projection(o_hg, o_att, w_bf16, x, mod, gain, *, mod_base, rows_per_mod):
    m = x.shape[0]
    tm = 512
    row = lambda i: (i, 0)
    return pl.pallas_call(
        _outproj_kernel,
        out_shape=(jax.ShapeDtypeStruct((m, D_MODEL), F32), jax.ShapeDtypeStruct((m, D_MODEL), BF16)),
        grid=(m // tm,),
        in_specs=[pl.BlockSpec((tm, HG_WIDTH), row),
                  pl.BlockSpec((tm, ATT_WIDTH), row),
                  pl.BlockSpec((HG_WIDTH + ATT_WIDTH, D_MODEL), lambda i: (0, 0)),
                  pl.BlockSpec((tm, D_MODEL), row),
                  pl.BlockSpec((1, N_MOD, D_MODEL), lambda i: (mod_base + (i * tm) // rows_per_mod, 0, 0)),
                  pl.BlockSpec((1, D_MODEL), lambda i: (0, 0))],
        out_specs=(pl.BlockSpec((tm, D_MODEL), row), pl.BlockSpec((tm, D_MODEL), row)),
        compiler_params=_cparams("parallel"),
        name="out_projection",
    )(o_hg, o_att, w_bf16, x, mod, gain)


def _prefix_count(x):
    n = x.shape[1]
    i = lax.broadcasted_iota(jnp.int32, (SEL_BLOCK, SEL_BLOCK), 0)
    j = lax.broadcasted_iota(jnp.int32, (SEL_BLOCK, SEL_BLOCK), 1)
    upper = jnp.where(i < j, 1.0, 0.0).astype(BF16)
    off = jnp.zeros((x.shape[0], 1), F32)
    outs = []
    for blk in range(n // SEL_BLOCK):
        xb = x[:, blk * SEL_BLOCK:(blk + 1) * SEL_BLOCK]
        outs.append(_dot(xb.astype(BF16), upper) + off)
        off = off + jnp.sum(xb, axis=-1, keepdims=True)
    return outs[0] if len(outs) == 1 else jnp.concatenate(outs, axis=1)


def _route_select(h_ref, w_ref, cap):
    logits = _dot_nt(w_ref[...], h_ref[...])
    ex = jnp.exp(logits - jnp.max(logits, axis=0, keepdims=True))
    aff = ex / jnp.sum(ex, axis=0, keepdims=True)
    bits = pltpu.bitcast(aff, jnp.int32)

    thr = jnp.zeros((N_EXPERTS, 1), jnp.int32)
    shift = 31
    while shift > 0:
        width = min(4, shift)
        shift -= width
        digit = jnp.zeros((N_EXPERTS, 1), jnp.int32)
        for j in range(1, 2 ** width):
            cnt = jnp.sum(jnp.where(bits >= (thr | (j << shift)), 1.0, 0.0), axis=-1, keepdims=True)
            digit = digit + jnp.where(cnt >= cap, 1, 0)
        thr = thr | (digit << shift)
    above = jnp.where(bits > thr, 1.0, 0.0)
    tied = jnp.where(bits == thr, 1.0, 0.0)
    room = cap - jnp.sum(above, axis=-1, keepdims=True)
    sel = above + tied * jnp.where(_prefix_count(tied) < room, 1.0, 0.0)
    return jnp.where(sel > 0.0, _prefix_count(sel), -1.0), aff


def _one_hot_gather(slot_rows, aff_rows, h_ref, rows, cap):
    n_tok = h_ref.shape[0]
    c = (lax.broadcasted_iota(jnp.int32, (rows, n_tok), 0) & (cap - 1)).astype(F32)
    hit = c == slot_rows
    x = _dot(jnp.where(hit, 1.0, 0.0).astype(BF16), h_ref[...]).astype(BF16)
    return x, jnp.sum(jnp.where(hit, aff_rows, 0.0), axis=-1, keepdims=True)


def _route_request_kernel(h_ref, w_ref, x_ref, g_ref, slot_ref, *, cap):
    n_tok = h_ref.shape[0]
    slot, aff = _route_select(h_ref, w_ref, cap)
    slot_ref[0] = slot
    per_row = lambda a: jnp.concatenate(
        [jnp.broadcast_to(a[e:e + 1, :], (cap, n_tok)) for e in range(N_EXPERTS)], axis=0)
    x, g = _one_hot_gather(per_row(slot), per_row(aff), h_ref, N_EXPERTS * cap, cap)
    for e in range(N_EXPERTS):
        x_ref[e] = x[e * cap:(e + 1) * cap, :]
        g_ref[e] = g[e * cap:(e + 1) * cap, :]


def _route_only_kernel(h_ref, w_ref, slot_ref, aff_ref, bounds_ref, *, cap):
    n_tok = h_ref.shape[0]
    slot, aff = _route_select(h_ref, w_ref, cap)
    slot_ref[0] = slot
    aff_ref[0] = aff
    n = lax.broadcasted_iota(jnp.int32, (n_tok, LANES), 0)
    t = lax.broadcasted_iota(jnp.int32, (n_tok, LANES), 1)
    before = jnp.where(n < t * TOK_BLOCK, 1.0, 0.0).astype(BF16)
    chosen = jnp.where(slot >= 0.0, 1.0, 0.0).astype(BF16)
    bounds_ref[0] = _dot(chosen, before).astype(jnp.int32)


def _gather_window_kernel(bounds_ref, slot_ref, aff_ref, h_ref, xh_ref, gh_ref, x_ref, g_ref, xo_ref, go_ref, *,
                          n_batch, group):
    b = pl.program_id(0)
    eg = pl.program_id(1)
    n_tok = h_ref.shape[0]
    nt = n_tok // TOK_BLOCK
    cap = x_ref.shape[1]
    region = TOK_BLOCK // group

    @pl.when(b >= n_batch)
    def _():
        x_ref[...] = xh_ref[...]
        g_ref[...] = gh_ref[...]

    @pl.when(b < n_batch)
    def _():
        x_ref[...] = jnp.zeros(x_ref.shape, BF16)
        g_ref[...] = jnp.zeros(g_ref.shape, F32)
        local_i = lax.broadcasted_iota(jnp.int32, (region, TOK_BLOCK), 0).astype(F32)
        over_i = lax.broadcasted_iota(jnp.int32, (TOK_BLOCK, TOK_BLOCK), 0)

        def token_block(t, carry):
            toks = pl.ds(pl.multiple_of(t * TOK_BLOCK, TOK_BLOCK), TOK_BLOCK)
            hits, gates, firsts, n_overs, slots, affs = [], [], [], [], [], []
            for k in range(group):
                e = eg * group + k
                start, length = _window(bounds_ref, (b * N_EXPERTS + e) * (nt + 1), t)
                first = jnp.minimum(start, cap - region)
                s = slot_ref[0, pl.ds(e, 1), toks]
                a = aff_ref[0, pl.ds(e, 1), toks]
                hit = local_i == jnp.where(s >= 0.0, s - first.astype(F32), -1.0)
                hits.append(jnp.where(hit, 1.0, 0.0).astype(BF16))
                gates.append(jnp.sum(jnp.where(hit, a, 0.0), axis=-1, keepdims=True))
                firsts.append(first)
                n_overs.append(start + length - first - region)
                slots.append(s)
                affs.append(a)
            xc = _dot(jnp.concatenate(hits, axis=0), h_ref[toks, :]).astype(BF16)
            for k in range(group):
                dst = pl.ds(pl.multiple_of(firsts[k], PIECE), region)
                x_ref[k, dst, :] = x_ref[k, dst, :] + xc[k * region:(k + 1) * region, :]
                g_ref[k, dst, :] = g_ref[k, dst, :] + gates[k]
            for k in range(group):
                @pl.when(n_overs[k] > 0)
                def _(k=k):
                    base = firsts[k] + region
                    hit = (over_i + base).astype(F32) == slots[k]
                    xo_ref[...] = _dot(jnp.where(hit, 1.0, 0.0).astype(BF16), h_ref[toks, :]).astype(BF16)
                    go_ref[...] = jnp.sum(jnp.where(hit, affs[k], 0.0), axis=-1, keepdims=True)

                    def place(i, carry):
                        src = pl.ds(pl.multiple_of(i * PIECE, PIECE), PIECE)
                        dst = pl.ds(pl.multiple_of(base + i * PIECE, PIECE), PIECE)
                        x_ref[k, dst, :] = x_ref[k, dst, :] + xo_ref[src, :]
                        g_ref[k, dst, :] = g_ref[k, dst, :] + go_ref[src, :]
                        return carry

                    lax.fori_loop(0, n_overs[k] // PIECE, place, 0)
            return carry

        lax.fori_loop(0, nt, token_block, 0)


def _route_gather_requests(h, w_router_t, *, n_batch, n_tok):
    cap = CAPACITY_FACTOR * n_tok // N_EXPERTS
    out_block = lambda w: pl.BlockSpec((N_EXPERTS, cap, w), lambda b: (0, b, 0))
    return pl.pallas_call(
        functools.partial(_route_request_kernel, cap=cap),
        out_shape=(jax.ShapeDtypeStruct((N_EXPERTS, n_batch * cap, D_MODEL), BF16),
                   jax.ShapeDtypeStruct((N_EXPERTS, n_batch * cap, 1), F32),
                   jax.ShapeDtypeStruct((n_batch, N_EXPERTS, n_tok), F32)),
        grid=(n_batch,),
        in_specs=[pl.BlockSpec((n_tok, D_MODEL), lambda b: (b, 0)),
                  pl.BlockSpec((N_EXPERTS, D_MODEL), lambda b: (0, 0))],
        out_specs=(out_block(D_MODEL), out_block(1), pl.BlockSpec((1, N_EXPERTS, n_tok), lambda b: (b, 0, 0))),
        compiler_params=_cparams("parallel"),
        name="route_gather_requests",
    )(h, w_router_t)


def _route_only(h, w_router_t, *, n_batch, n_tok):
    cap = CAPACITY_FACTOR * n_tok // N_EXPERTS
    nt = n_tok // TOK_BLOCK
    expert_major = pl.BlockSpec((1, N_EXPERTS, n_tok), lambda b: (b, 0, 0))
    slot, aff, bounds = pl.pallas_call(
        functools.partial(_route_only_kernel, cap=cap),
        out_shape=(jax.ShapeDtypeStruct((n_batch, N_EXPERTS, n_tok), F32),
                   jax.ShapeDtypeStruct((n_batch, N_EXPERTS, n_tok), F32),
                   jax.ShapeDtypeStruct((n_batch, N_EXPERTS, LANES), jnp.int32)),
        grid=(n_batch,),
        in_specs=[pl.BlockSpec((n_tok, D_MODEL), lambda b: (b, 0)),
                  pl.BlockSpec((N_EXPERTS, D_MODEL), lambda b: (0, 0))],
        out_specs=(expert_major, expert_major, pl.BlockSpec((1, N_EXPERTS, LANES), lambda b: (b, 0, 0))),
        compiler_params=_cparams("parallel"),
        name="route_select",
    )(h, w_router_t)
    return slot, aff, bounds[:, :, :nt + 1].reshape(-1)


def _gather_windows(bounds, slot, aff, h, x_head, g_head, *, n_batch, n_tok):
    cap = CAPACITY_FACTOR * n_tok // N_EXPERTS
    group = TOK_BLOCK // SLOT_REGION
    head_rows = x_head.shape[1]
    assert head_rows % cap == 0 and N_EXPERTS % group == 0
    n_head = head_rows // cap
    rows_total = head_rows + n_batch * cap
    last = n_batch - 1
    request = lambda b: jnp.minimum(b, last)
    out_row = lambda b: jnp.where(b < n_batch, n_head + b, b - n_batch)
    head_row = lambda b: jnp.maximum(b - n_batch, 0)
    out_block = lambda w: pl.BlockSpec((group, cap, w), lambda b, g, s: (g, out_row(b), 0))
    head_block = lambda w: pl.BlockSpec((group, cap, w), lambda b, g, s: (g, head_row(b), 0))
    expert_major = pl.BlockSpec((1, N_EXPERTS, n_tok), lambda b, g, s: (request(b), 0, 0))
    return pl.pallas_call(
        functools.partial(_gather_window_kernel, n_batch=n_batch, group=group),
        out_shape=(jax.ShapeDtypeStruct((N_EXPERTS, rows_total, D_MODEL), BF16),
                   jax.ShapeDtypeStruct((N_EXPERTS, rows_total, 1), F32)),
        grid_spec=pltpu.PrefetchScalarGridSpec(
            num_scalar_prefetch=1,
            grid=(n_batch + n_head, N_EXPERTS // group),
            in_specs=[expert_major, expert_major,
                      pl.BlockSpec((n_tok, D_MODEL), lambda b, g, s: (request(b), 0)),
                      head_block(D_MODEL), head_block(1)],
            out_specs=(out_block(D_MODEL), out_block(1)),
            scratch_shapes=[pltpu.VMEM((TOK_BLOCK, D_MODEL), BF16), pltpu.VMEM((TOK_BLOCK, 1), F32)]),
        compiler_params=_cparams("arbitrary", "arbitrary"),
        name="gather_windows",
    )(bounds, slot, aff, h, x_head, g_head)


def _moe_kernel(x_ref, g_ref, wg_ref, wu_ref, wd_ref, y_ref, hid_ref, *, n_ff, rows):
    s = pl.program_id(2)
    tf = wg_ref.shape[2]

    @pl.when(s < n_ff)
    def _():
        wg = wg_ref[0].astype(BF16)
        wu = wu_ref[0].astype(BF16)
        cols = pl.ds(pl.multiple_of(s * tf, tf), tf)
        for r in range(x_ref.shape[1] // rows):
            sl = slice(r * rows, (r + 1) * rows)
            x = x_ref[0, sl, :]
            hid_ref[sl, cols] = (_silu(_dot(x, wg)) * _dot(x, wu)).astype(BF16)

    @pl.when(s >= n_ff)
    def _():
        y = _dot(hid_ref[...], wd_ref[0].astype(BF16))
        y_ref[0] = (y * g_ref[0]).astype(BF16)


def _experts(x, gate, w_gate, w_up, w_down):
    n_rows = x.shape[1]
    tr, tf, tn = n_rows // 2, 512, 256
    n_ff, n_out = EXPERT_FF // tf, D_MODEL // tn
    up_tile = lambda e, r, s: (e, 0, jnp.minimum(s, n_ff - 1))
    out_tile = lambda s: jnp.maximum(s - n_ff, 0)
    return pl.pallas_call(
        functools.partial(_moe_kernel, n_ff=n_ff, rows=256),
        out_shape=jax.ShapeDtypeStruct((N_EXPERTS, n_rows, D_MODEL), BF16),
        grid=(N_EXPERTS, n_rows // tr, n_ff + n_out),
        in_specs=[pl.BlockSpec((1, tr, D_MODEL), lambda e, r, s: (e, r, 0), pipeline_mode=pl.Buffered(1)),
                  pl.BlockSpec((1, tr, 1), lambda e, r, s: (e, r, 0)),
                  pl.BlockSpec((1, D_MODEL, tf), up_tile),
                  pl.BlockSpec((1, D_MODEL, tf), up_tile),
                  pl.BlockSpec((1, EXPERT_FF, tn), lambda e, r, s: (e, 0, out_tile(s)))],
        out_specs=pl.BlockSpec((1, tr, tn), lambda e, r, s: (e, r, out_tile(s))),
        scratch_shapes=[pltpu.VMEM((tr, EXPERT_FF), BF16)],
        compiler_params=_cparams("parallel", "parallel", "arbitrary"),
        name="expert_swiglu",
    )(x, gate, w_gate, w_up, w_down)


def _window(bounds_ref, base, t):
    p0 = bounds_ref[base + t]
    p1 = bounds_ref[base + t + 1]
    start = (p0 // PIECE) * PIECE
    return start, jnp.where(p1 > p0, ((p1 - start + PIECE - 1) // PIECE) * PIECE, 0)


def _combine_kernel(bounds_ref, y_ref, slot_ref, x_ref, mod_ref, g_ref, o_ref, ybuf_ref, obuf_ref, acc_ref, *,
                    nt, region):
    b = pl.program_id(0)
    t = pl.program_id(1)
    tt = x_ref.shape[0]
    cap = y_ref.shape[1]
    local_i = lax.broadcasted_iota(jnp.int32, (region, tt), 0).astype(F32)
    hits = []
    for e in range(N_EXPERTS):
        start, _ = _window(bounds_ref, (b * N_EXPERTS + e) * (nt + 1), t)
        first = jnp.minimum(start, cap - region)
        ybuf_ref[e * region:(e + 1) * region, :] = y_ref[e, pl.ds(pl.multiple_of(first, PIECE), region), :]
        s = slot_ref[0, e:e + 1, :]
        hit = local_i == jnp.where(s >= 0.0, s - first.astype(F32), -1.0)
        hits.append(jnp.where(hit, 1.0, 0.0).astype(BF16))
    acc_ref[...] = _dot_tn(jnp.concatenate(hits, axis=0), ybuf_ref[...])
    over_i = lax.broadcasted_iota(jnp.int32, (TOK_BLOCK, tt), 0)

    def overflow(e, carry):
        start, length = _window(bounds_ref, (b * N_EXPERTS + e) * (nt + 1), t)
        base = jnp.minimum(start, cap - region) + region
        n_over = start + length - base

        @pl.when(n_over > 0)
        def _():
            def copy(i, carry):
                obuf_ref[pl.ds(pl.multiple_of(i * PIECE, PIECE), PIECE), :] = (
                    y_ref[e, pl.ds(pl.multiple_of(base + i * PIECE, PIECE), PIECE), :])
                return carry

            def clear(i, carry):
                obuf_ref[pl.ds(pl.multiple_of(n_over + i * PIECE, PIECE), PIECE), :] = jnp.zeros((PIECE, D_MODEL), BF16)
                return carry

            lax.fori_loop(0, n_over // PIECE, copy, 0)
            lax.fori_loop(0, (TOK_BLOCK - n_over) // PIECE, clear, 0)
            hit = (over_i + base).astype(F32) == slot_ref[0, pl.ds(e, 1), :]
            acc_ref[...] += _dot_tn(jnp.where(hit, 1.0, 0.0).astype(BF16), obuf_ref[...])

        return carry

    lax.fori_loop(0, N_EXPERTS, overflow, 0)
    x = x_ref[...] + mod_ref[0, 5:6, :] * acc_ref[...]
    var = jnp.mean(x * x, axis=-1, keepdims=True)
    o_ref[...] = x * lax.rsqrt(var + NORM_EPS) * g_ref[...]


def _combine(bounds, y, slot, x_mid, mod, final_g, *, n_batch, n_tok, row_block_off, mod_base, mod_per_batch):
    cap = CAPACITY_FACTOR * n_tok // N_EXPERTS
    tt = TOK_BLOCK
    nt = n_tok // tt
    region = min(SLOT_REGION, cap)
    assert cap - region <= TOK_BLOCK
    return pl.pallas_call(
        functools.partial(_combine_kernel, nt=nt, region=region),
        out_shape=jax.ShapeDtypeStruct((n_batch * n_tok, D_MODEL), F32),
        grid_spec=pltpu.PrefetchScalarGridSpec(
            num_scalar_prefetch=1,
            grid=(n_batch, nt),
            in_specs=[pl.BlockSpec((N_EXPERTS, cap, D_MODEL), lambda b, t, s: (0, row_block_off + b, 0)),
                      pl.BlockSpec((1, N_EXPERTS, tt), lambda b, t, s: (b, 0, t)),
                      pl.BlockSpec((tt, D_MODEL), lambda b, t, s: (b * nt + t, 0)),
                      pl.BlockSpec((1, N_MOD, D_MODEL), lambda b, t, s: (mod_base + b * mod_per_batch, 0, 0)),
                      pl.BlockSpec((1, D_MODEL), lambda b, t, s: (0, 0))],
            out_specs=pl.BlockSpec((tt, D_MODEL), lambda b, t, s: (b * nt + t, 0)),
            scratch_shapes=[pltpu.VMEM((N_EXPERTS * region, D_MODEL), BF16),
                            pltpu.VMEM((TOK_BLOCK, D_MODEL), BF16), pltpu.VMEM((tt, D_MODEL), F32)]),
        compiler_params=_cparams("arbitrary", "arbitrary"),
        name="combine_final_norm",
    )(bounds, y, slot, x_mid, mod, final_g)


def kernel(x_prompt, x_sample, cache_k, cache_v, state_hgrn, c, c_ctx, w_ada, b_ada, norm1_g, w_in, hg_lb,
           hg_norm_g, attn_sink, w_out, norm2_g, w_router, w_gate, w_up, w_down, final_norm_g):
    n_p, t_p, _ = x_prompt.shape
    n_s, t_s, _ = x_sample.shape
    assert w_ada.shape[0] == 1 and 1 + n_s <= COND_ROWS
    layer = 0

    cond = jnp.zeros((COND_ROWS, D_MODEL), F32).at[0].set(c_ctx).at[1:1 + n_s].set(c)
    mod = _ada_modulation(cond, w_ada[layer], b_ada[layer]).reshape(COND_ROWS, N_MOD, D_MODEL)
    lb = jnp.cumsum(jax.nn.softmax(hg_lb.astype(F32), axis=0), axis=0)[layer]
    w_in_bf16 = w_in[layer].astype(BF16)
    w_out_bf16 = w_out[layer].astype(BF16)
    w_router_t = w_router[layer].T.astype(BF16)
    norm1 = norm1_g[layer].reshape(1, D_MODEL)
    norm2 = norm2_g[layer].reshape(1, D_MODEL)
    hg_gain = hg_norm_g[layer].reshape(1, HG_WIDTH)
    final_g = final_norm_g.reshape(1, D_MODEL)
    sink = attn_sink[layer]
    cos, sin_signed = _rope_tables(t_s)

    xp = x_prompt.reshape(n_p * t_p, D_MODEL)
    xs = x_sample.reshape(n_s * t_s, D_MODEL)
    groups = dict(p=dict(mod_base=0, rows_per_mod=n_p * t_p), s=dict(mod_base=1, rows_per_mod=t_s))

    proj_p = _in_projection(xp, mod, norm1, w_in_bf16, **groups["p"])
    proj_s = _in_projection(xs, mod, norm1, w_in_bf16, **groups["s"])

    ohg_p, new_state = _hgrn(proj_p, lb, hg_gain, None, n_batch=n_p, n_tok=t_p)
    ohg_s, _ = _hgrn(proj_s, lb, hg_gain, state_hgrn[:, layer:layer + 1].astype(F32), n_batch=n_s, n_tok=t_s)

    oatt_p = _context_attention(proj_p, sink, n_batch=n_p, n_tok=t_p)
    n_ctx = cache_k.shape[2]
    oatt_s = _window_attention(proj_s, cache_k[:, layer].reshape(n_s, n_ctx, KV_WIDTH),
                               cache_v[:, layer].reshape(n_s, n_ctx, KV_WIDTH), sink, cos, sin_signed,
                               n_batch=n_s, n_tok=t_s)

    xmid_p, h2_p = _out_projection(ohg_p, oatt_p, w_out_bf16, xp, mod, norm2, **groups["p"])
    xmid_s, h2_s = _out_projection(ohg_s, oatt_s, w_out_bf16, xs, mod, norm2, **groups["s"])

    cap_p = CAPACITY_FACTOR * t_p // N_EXPERTS
    cap_s = CAPACITY_FACTOR * t_s // N_EXPERTS
    off_s = n_p * cap_p // cap_s
    xg_p, gate_p, slot_p = _route_gather_requests(h2_p, w_router_t, n_batch=n_p, n_tok=t_p)
    slot_s, aff_s, bounds_s = _route_only(h2_s, w_router_t, n_batch=n_s, n_tok=t_s)
    xg, gate = _gather_windows(bounds_s, slot_s, aff_s, h2_s, xg_p, gate_p, n_batch=n_s, n_tok=t_s)
    assert t_p == TOK_BLOCK
    bounds_p = jnp.tile(jnp.array([0, cap_p], jnp.int32), n_p * N_EXPERTS)

    y = _experts(xg, gate, w_gate[layer], w_up[layer], w_down[layer])

    y_prompt = _combine(bounds_p, y, slot_p, xmid_p, mod, final_g, n_batch=n_p, n_tok=t_p, row_block_off=0,
                        mod_base=0, mod_per_batch=0)
    y_sample = _combine(bounds_s, y, slot_s, xmid_s, mod, final_g, n_batch=n_s, n_tok=t_s, row_block_off=off_s,
                        mod_base=1, mod_per_batch=1)

    k_col = 5 * HG_WIDTH + ATT_WIDTH
    new_k = proj_p[:, k_col:k_col + KV_WIDTH].reshape(n_p, 1, t_p, ATT_KV_HEADS, HEAD_DIM)
    new_v = proj_p[:, k_col + KV_WIDTH:k_col + 2 * KV_WIDTH].reshape(n_p, 1, t_p, ATT_KV_HEADS, HEAD_DIM)
    return (y_prompt.reshape(n_p, t_p, D_MODEL), y_sample.reshape(n_s, t_s, D_MODEL), new_k, new_v, new_state)
```

```python
import functools

import jax
import jax.numpy as jnp
import numpy as np
from jax import lax
from jax.experimental import pallas as pl
from jax.experimental.pallas import tpu as pltpu

F32 = jnp.float32
BF16 = jnp.bfloat16

D_MODEL = 2048
HG_WIDTH = 1024
HG_HEADS = 8
HEAD_DIM = 128
ATT_HEADS = 8
ATT_KV_HEADS = 2
ATT_GROUP = ATT_HEADS // ATT_KV_HEADS
KV_WIDTH = ATT_KV_HEADS * HEAD_DIM
ATT_WIDTH = ATT_HEADS * HEAD_DIM
ATT_BLOCK = 128
GRID_W = 64
ROPE_BASE = 10000.0
ROPE_FREQS = HEAD_DIM // 4
N_EXPERTS = 16
CAPACITY_FACTOR = 2
EXPERT_FF = 5632
NORM_EPS = 1e-6
IN_WIDTH = 5 * HG_WIDTH + ATT_WIDTH + 2 * KV_WIDTH
N_MOD = 6
COND_ROWS = 16

HG_CHUNK = 128
HG_DIAG = 8
SEL_BLOCK = 256
TOK_BLOCK = 256
PIECE = 16
SLOT_REGION = 64
LANES = 128
W_RING = 3

V7X_VMEM_BYTES = 64 * 1024 * 1024
VMEM_LIMIT = V7X_VMEM_BYTES - 8 * 1024 * 1024


def _cparams(*sem):
    return pltpu.CompilerParams(dimension_semantics=sem, vmem_limit_bytes=VMEM_LIMIT)


def _sigmoid(x):
    return 1.0 / (1.0 + jnp.exp(-x))


def _silu(x):
    return x * _sigmoid(x)


def _dot(a, b):
    return jnp.dot(a, b, preferred_element_type=F32)


def _dot_nt(a, b):
    return lax.dot_general(a, b, (((1,), (1,)), ((), ())), preferred_element_type=F32)


def _dot_tn(a, b):
    return lax.dot_general(a, b, (((0,), (0,)), ((), ())), preferred_element_type=F32)


def _ada_kernel(c_ref, w_ref, b_ref, o_ref):
    s = _silu(c_ref[...]).astype(BF16)
    o_ref[...] = _dot(s, w_ref[...].astype(BF16)) + b_ref[...]


def _ada_modulation(cond, w_ada, b_ada):
    tn = 1024
    n = w_ada.shape[1]
    return pl.pallas_call(
        _ada_kernel,
        out_shape=jax.ShapeDtypeStruct((COND_ROWS, n), F32),
        grid=(n // tn,),
        in_specs=[pl.BlockSpec((COND_ROWS, D_MODEL), lambda j: (0, 0)),
                  pl.BlockSpec((D_MODEL, tn), lambda j: (0, j)),
                  pl.BlockSpec((1, tn), lambda j: (0, j))],
        out_specs=pl.BlockSpec((COND_ROWS, tn), lambda j: (0, j)),
        compiler_params=_cparams("arbitrary"),
        name="ada_modulation",
    )(cond, w_ada, b_ada.reshape(1, n))


def _norm_modulate(x, gain, shift, scale):
    var = jnp.mean(x * x, axis=-1, keepdims=True)
    return (x * lax.rsqrt(var + NORM_EPS) * gain) * (1.0 + scale) + shift


def _inproj_kernel(x_ref, mod_ref, g_ref, w_hbm, o_ref, h_ref, wbuf_ref, sem_ref, *, rows, n_col, n_steps):
    tn = o_ref.shape[1]
    s = pl.program_id(0) * n_col + pl.program_id(1)

    def tile_copy(step):
        col = pl.multiple_of((step % n_col) * tn, tn)
        slot = step % W_RING
        return pltpu.make_async_copy(w_hbm.at[:, pl.ds(col, tn)], wbuf_ref.at[slot], sem_ref.at[slot])

    @pl.when(s == 0)
    def _():
        for first in range(min(W_RING - 1, n_steps)):
            tile_copy(first).start()

    @pl.when(s + W_RING - 1 < n_steps)
    def _():
        tile_copy(s + W_RING - 1).start()

    tile_copy(s).wait()
    w = wbuf_ref[s % W_RING]
    chunks = [slice(r * rows, (r + 1) * rows) for r in range(x_ref.shape[0] // rows)]

    @pl.when(pl.program_id(1) == 0)
    def _():
        shift = mod_ref[0, 0:1, :]
        scale = mod_ref[0, 1:2, :]
        gain = g_ref[...]
        for sl in chunks:
            h = _norm_modulate(x_ref[sl, :], gain, shift, scale).astype(BF16)
            h_ref[sl, :] = h
            o_ref[sl, :] = _dot(h, w)

    @pl.when(pl.program_id(1) > 0)
    def _():
        for sl in chunks:
            o_ref[sl, :] = _dot(h_ref[sl, :], w)


def _in_projection(x, mod, gain, w, *, mod_base, rows_per_mod):
    m = x.shape[0]
    tm, tn = 1024, 512
    n_col = IN_WIDTH // tn
    return pl.pallas_call(
        functools.partial(_inproj_kernel, rows=256, n_col=n_col, n_steps=(m // tm) * n_col),
        out_shape=jax.ShapeDtypeStruct((m, IN_WIDTH), F32),
        grid=(m // tm, n_col),
        in_specs=[pl.BlockSpec((tm, D_MODEL), lambda i, j: (i, 0)),
                  pl.BlockSpec((1, N_MOD, D_MODEL), lambda i, j: (mod_base + (i * tm) // rows_per_mod, 0, 0)),
                  pl.BlockSpec((1, D_MODEL), lambda i, j: (0, 0)),
                  pl.BlockSpec(memory_space=pl.ANY)],
        out_specs=pl.BlockSpec((tm, tn), lambda i, j: (i, j)),
        scratch_shapes=[pltpu.VMEM((tm, D_MODEL), BF16), pltpu.VMEM((W_RING, D_MODEL, tn), BF16),
                        pltpu.SemaphoreType.DMA((W_RING,))],
        compiler_params=_cparams("arbitrary", "arbitrary"),
        name="in_projection",
    )(x, mod, gain, w)


def _hgrn_codes(reverse):
    L = HG_CHUNK
    t = lax.broadcasted_iota(jnp.int32, (L, L), 0)
    s = lax.broadcasted_iota(jnp.int32, (L, L), 1)
    code = jnp.where(t == s, 1, 0)
    h = L // 2
    while h >= 1:
        same = (t & ~(2 * h - 1)) == (s & ~(2 * h - 1))
        t_hi = (t & h) != 0
        s_hi = (s & h) != 0
        pair = (s_hi & ~t_hi) if reverse else (t_hi & ~s_hi)
        code = jnp.where(same & pair, h * 16, code)
        h //= 2
    return code


def _cumsum_rows(tri_bf16, g):
    g1 = g.astype(BF16)
    g2 = (g - g1.astype(F32)).astype(BF16)
    s = _dot(tri_bf16, jnp.concatenate([g1, g2], axis=1))
    return s[:, :HEAD_DIM] + s[:, HEAD_DIM:]


def _hgrn_intra(q, k, f, b, v_bf, code, reverse):
    L = HG_CHUNK
    G = HG_DIAG
    q_bf = q.astype(BF16)
    k_bf = k.astype(BF16)

    def level(h, ref, att):
        neg_abs = pltpu.bitcast(pltpu.bitcast(b - ref, jnp.int32) | jnp.int32(-2 ** 31), F32)
        e = jnp.exp2(neg_abs.astype(BF16))
        return jnp.where(code == h * 16, _dot_nt(q_bf * e, k_bf * e), att)

    att = jnp.where(code == 1, _dot_nt(q_bf, k_bf), 0.0)
    h = L // 2
    while h >= G:
        parts = []
        for p in range(L // (2 * h)):
            m = p * 2 * h + (h if reverse else h - 1)
            parts.append(jnp.broadcast_to(b[m:m + 1, :], (2 * h, HEAD_DIM)))
        att = level(h, parts[0] if len(parts) == 1 else jnp.concatenate(parts, axis=0), att)
        h //= 2

    b3 = b.reshape(L // G, G, HEAD_DIM)

    def group_row(r):
        return jnp.broadcast_to(b3[:, r:r + 1, :], (L // G, G, HEAD_DIM)).reshape(L, HEAD_DIM)

    row = lax.broadcasted_iota(jnp.int32, (L, HEAD_DIM), 0)
    att = level(4, group_row(4 if reverse else 3), att)
    lo, hi = (2, 6) if reverse else (1, 5)
    att = level(2, jnp.where((row & 4) == 0, group_row(lo), group_row(hi)), att)
    att = jnp.where(code == 16, _dot_nt(q_bf * f.astype(BF16), k_bf), att)
    return _dot(att.astype(BF16), v_bf)


def _hgrn_prepare(q, f, v, code, tri, reverse):
    L = HG_CHUNK
    k = 1.0 - f
    b = _cumsum_rows(tri, jnp.log2(f))
    b_tot = b[0:1, :] if reverse else b[L - 1:L, :]
    v_bf = v.astype(BF16)
    o_intra = _hgrn_intra(q, k, f, b, v_bf, code, reverse)
    q_in = (q * jnp.exp2(b)).astype(BF16)
    k_out = (k * jnp.exp2(b_tot - b)).astype(BF16)
    return o_intra, q_in, k_out, v_bf, jnp.exp2(b_tot)


def _hgrn_advance(st, prepared):
    o_intra, q_in, k_out, v_bf, decay = prepared
    return o_intra + _dot_nt(q_in, st.astype(BF16)), st * decay + _dot_tn(v_bf, k_out)


def _hgrn_kernel(*refs, n_tok, has_state, heads):
    if has_state:
        (q_ref, ff_ref, fb_ref, v_ref, gate_ref, lb_ref, ng_ref, s0_ref,
         o_ref, sout_ref, of_ref, ob_ref, code_ref) = refs
    else:
        (q_ref, ff_ref, fb_ref, v_ref, gate_ref, lb_ref, ng_ref,
         o_ref, sout_ref, of_ref, ob_ref, code_ref) = refs
    L = HG_CHUNK
    nc = n_tok // L

    @pl.when((pl.program_id(0) == 0) & (pl.program_id(1) == 0))
    def _():
        code_ref[0] = _hgrn_codes(False)
        code_ref[1] = _hgrn_codes(True)

    code_f = code_ref[0]
    code_b = code_ref[1]
    ti = lax.broadcasted_iota(jnp.int32, (L, L), 0)
    si = lax.broadcasted_iota(jnp.int32, (L, L), 1)
    tri_f = jnp.where(si <= ti, 1.0, 0.0).astype(BF16)
    tri_b = jnp.where(si >= ti, 1.0, 0.0).astype(BF16)

    def one_head(hd, carry):
        cols = pl.ds(pl.multiple_of(hd * HEAD_DIM, HEAD_DIM), HEAD_DIM)
        lb_f = lb_ref[0:1, cols]
        lb_b = lb_ref[1:2, cols]

        def prepare(row, f_ref, lb, code, tri, reverse):
            sl = pl.ds(row, L)
            q = _silu(q_ref[sl, cols])
            f = lb + (1.0 - lb) * _sigmoid(f_ref[sl, cols])
            return sl, _hgrn_prepare(q, f, v_ref[sl, cols], code, tri, reverse)

        gain = ng_ref[:, cols]

        def finish(sl, o):
            var = jnp.mean(o * o, axis=-1, keepdims=True)
            o = (o * lax.rsqrt(var + NORM_EPS) * gain) * _silu(gate_ref[sl, cols])
            o_ref[sl, cols] = o.astype(BF16)

        def make_body(other_direction):
            def body(c, states):
                st_f, st_b = states
                fwd, bwd = [], []
                for u in range(unroll):
                    cu = c * unroll + u
                    fwd.append(prepare(pl.multiple_of(cu * L, L), ff_ref, lb_f, code_f, tri_f, False))
                    bwd.append(prepare(pl.multiple_of((nc - 1 - cu) * L, L), fb_ref, lb_b, code_b, tri_b, True))
                out_f, out_b = [], []
                for (sl_f, prep_f), (sl_b, prep_b) in zip(fwd, bwd):
                    o_f, st_f = _hgrn_advance(st_f, prep_f)
                    o_b, st_b = _hgrn_advance(st_b, prep_b)
                    out_f.append((sl_f, o_f))
                    out_b.append((sl_b, o_b))
                if other_direction == "later":
                    for sl, o in out_f:
                        of_ref[sl, :] = o
                    for sl, o in out_b:
                        ob_ref[sl, :] = o
                elif other_direction == "now":
                    for u in range(unroll):
                        finish(out_f[u][0], out_f[u][1] + out_b[unroll - 1 - u][1])
                else:
                    for sl, o in out_f:
                        finish(sl, o + ob_ref[sl, :])
                    for sl, o in out_b:
                        finish(sl, of_ref[sl, :] + o)
                return st_f, st_b
            return body

        unroll = 8 if nc % 16 == 0 else 2
        assert nc % unroll == 0
        n_iter = nc // unroll
        half = n_iter // 2
        if has_state:
            states = (s0_ref[0, 0, 0, hd].T, s0_ref[0, 0, 1, hd].T)
        else:
            states = (jnp.zeros((HEAD_DIM, HEAD_DIM), F32), jnp.zeros((HEAD_DIM, HEAD_DIM), F32))
        states = lax.fori_loop(0, half, make_body("later"), states)
        if n_iter % 2:
            states = make_body("now")(half, states)
        st_f, st_b = lax.fori_loop(half + n_iter % 2, n_iter, make_body("earlier"), states)
        sout_ref[0, 0, 0, hd] = st_f.T
        sout_ref[0, 0, 1, hd] = st_b.T
        return carry

    lax.fori_loop(0, heads, one_head, 0)


def _hgrn(proj, lb, norm_g, state, *, n_batch, n_tok):
    heads = HG_HEADS
    while heads > 1 and 5 * 2 * n_tok * heads * HEAD_DIM * 4 > VMEM_LIMIT // 2:
        heads //= 2
    width = heads * HEAD_DIM
    groups = HG_HEADS // heads
    col = lambda k: (lambda b, h: (b, k * groups + h))
    tok_spec = lambda k: pl.BlockSpec((n_tok, width), col(k))
    st_spec = pl.BlockSpec((1, 1, 2, heads, HEAD_DIM, HEAD_DIM), lambda b, h: (b, 0, 0, h, 0, 0))
    has_state = state is not None
    return pl.pallas_call(
        functools.partial(_hgrn_kernel, n_tok=n_tok, has_state=has_state, heads=heads),
        out_shape=(jax.ShapeDtypeStruct((n_batch * n_tok, HG_WIDTH), BF16),
                   jax.ShapeDtypeStruct((n_batch, 1, 2, HG_HEADS, HEAD_DIM, HEAD_DIM), F32)),
        grid=(n_batch, groups),
        in_specs=[tok_spec(0), tok_spec(1), tok_spec(2), tok_spec(3), tok_spec(4),
                  pl.BlockSpec((2, width), lambda b, h: (0, h)),
                  pl.BlockSpec((1, width), lambda b, h: (0, h))] + ([st_spec] if has_state else []),
        out_specs=(pl.BlockSpec((n_tok, width), lambda b, h: (b, h)), st_spec),
        scratch_shapes=[pltpu.VMEM((n_tok, HEAD_DIM), F32), pltpu.VMEM((n_tok, HEAD_DIM), F32),
                        pltpu.VMEM((2, HG_CHUNK, HG_CHUNK), jnp.int32)],
        compiler_params=_cparams("arbitrary", "arbitrary"),
        name="hgrn2_scan",
    )(proj, proj, proj, proj, proj, lb, norm_g, *((state,) if has_state else ()))


def _stack_heads(x, kvh):
    return jnp.concatenate(
        [x[:, (kvh * ATT_GROUP + g) * HEAD_DIM:(kvh * ATT_GROUP + g + 1) * HEAD_DIM] for g in range(ATT_GROUP)],
        axis=0)


def _sink_column(sink_ref, kvh, rows):
    return jnp.concatenate(
        [jnp.full((rows, 1), sink_ref[kvh * ATT_GROUP + g], F32) for g in range(ATT_GROUP)],
        axis=0) * (1.0 / SOFTMAX_SCALE)


SOFTMAX_SCALE = HEAD_DIM ** -0.5
EXP2_SCALE = SOFTMAX_SCALE * 1.4426950408889634


def _ones_column(rows):
    lane = lax.broadcasted_iota(jnp.int32, (rows, HEAD_DIM), 1)
    return jnp.where(lane == 0, 1.0, 0.0).astype(BF16)


def _softmax_av(scores, values, sink_col):
    m = sink_col
    for s in scores:
        m = jnp.maximum(m, jnp.max(s, axis=-1, keepdims=True))
    acc = None
    for s, v in zip(scores, values):
        pv = _dot(jnp.exp2(((s - m) * EXP2_SCALE).astype(BF16)), v)
        acc = pv if acc is None else acc + pv
    denom = acc[:, HEAD_DIM:HEAD_DIM + 1] + jnp.exp2((sink_col - m) * EXP2_SCALE)
    return acc[:, :HEAD_DIM] / denom


def _ctx_attn_kernel(sink_ref, q_ref, k_ref, v_ref, o_ref):
    rows = q_ref.shape[0]
    q_all = q_ref[...]
    ones = _ones_column(k_ref.shape[0])
    for kvh in range(ATT_KV_HEADS):
        q = _stack_heads(q_all, kvh).astype(BF16)
        k = k_ref[:, kvh * HEAD_DIM:(kvh + 1) * HEAD_DIM].astype(BF16)
        v = jnp.concatenate([v_ref[:, kvh * HEAD_DIM:(kvh + 1) * HEAD_DIM].astype(BF16), ones], axis=1)
        o = _softmax_av([_dot_nt(q, k)], [v], _sink_column(sink_ref, kvh, rows))
        for g in range(ATT_GROUP):
            hd = kvh * ATT_GROUP + g
            o_ref[:, hd * HEAD_DIM:(hd + 1) * HEAD_DIM] = o[g * rows:(g + 1) * rows, :].astype(BF16)


def _context_attention(proj, sink, *, n_batch, n_tok):
    q_col = 5 * HG_WIDTH // ATT_WIDTH
    k_col = (5 * HG_WIDTH + ATT_WIDTH) // KV_WIDTH
    return pl.pallas_call(
        _ctx_attn_kernel,
        out_shape=jax.ShapeDtypeStruct((n_batch * n_tok, ATT_WIDTH), BF16),
        grid_spec=pltpu.PrefetchScalarGridSpec(
            num_scalar_prefetch=1,
            grid=(n_batch,),
            in_specs=[pl.BlockSpec((n_tok, ATT_WIDTH), lambda b, s: (b, q_col)),
                      pl.BlockSpec((n_tok, KV_WIDTH), lambda b, s: (b, k_col)),
                      pl.BlockSpec((n_tok, KV_WIDTH), lambda b, s: (b, k_col + 1))],
            out_specs=pl.BlockSpec((n_tok, ATT_WIDTH), lambda b, s: (b, 0))),
        compiler_params=_cparams("parallel"),
        name="context_attention",
    )(sink, proj, proj, proj)


def _rope(x, cos, sin_signed, even_group):
    partner = jnp.where(even_group, pltpu.roll(x, HEAD_DIM - ROPE_FREQS, 1), pltpu.roll(x, ROPE_FREQS, 1))
    return x * cos + partner * sin_signed


def _win_attn_kernel(sink_ref, q_ref, k_ref, v_ref, ck_ref, cv_ref, cos_ref, sin_ref, o_ref,
                     kpad_ref, vpad_ref, *, n_tok):
    blk = ATT_BLOCK
    nb = n_tok // blk
    i = pl.program_id(1)
    lane = lax.broadcasted_iota(jnp.int32, (blk, HEAD_DIM), 1)
    even_group = (lane & ROPE_FREQS) == 0

    @pl.when(i == 0)
    def _():
        kpad_ref[0:blk, :] = jnp.zeros((blk, KV_WIDTH), BF16)
        kpad_ref[blk + n_tok:2 * blk + n_tok, :] = jnp.zeros((blk, KV_WIDTH), BF16)
        vpad_ref[0:blk, :] = jnp.zeros((blk, 2 * KV_WIDTH), BF16)
        vpad_ref[blk + n_tok:2 * blk + n_tok, :] = jnp.zeros((blk, 2 * KV_WIDTH), BF16)
        ones = _ones_column(blk)

        def body(r, carry):
            src = pl.ds(pl.multiple_of(r * blk, blk), blk)
            dst = pl.ds(pl.multiple_of((r + 1) * blk, blk), blk)
            cos = cos_ref[src, :]
            sin = sin_ref[src, :]
            for kvh in range(ATT_KV_HEADS):
                cols = slice(kvh * HEAD_DIM, (kvh + 1) * HEAD_DIM)
                kpad_ref[dst, cols] = _rope(k_ref[src, cols], cos, sin, even_group).astype(BF16)
                vpad_ref[dst, 2 * kvh * HEAD_DIM:(2 * kvh + 1) * HEAD_DIM] = v_ref[src, cols].astype(BF16)
                vpad_ref[dst, (2 * kvh + 1) * HEAD_DIM:(2 * kvh + 2) * HEAD_DIM] = ones
            return carry

        lax.fori_loop(0, nb, body, 0)

    ctx_ones = _ones_column(ck_ref.shape[1])
    r = lax.broadcasted_iota(jnp.int32, (blk, 3 * blk), 0)
    j = lax.broadcasted_iota(jnp.int32, (blk, 3 * blk), 1)
    blocks = q_ref.shape[0] // blk

    def query_block(u, carry):
        qb = i * blocks + u
        rows = pl.ds(pl.multiple_of(qb * blk, blk), blk)
        cos = cos_ref[rows, :]
        sin = sin_ref[rows, :]
        band = pl.ds(pl.multiple_of(qb * blk, blk), 3 * blk)
        kpos = j + (qb - 1) * blk
        valid = (j >= r) & (j <= r + 2 * blk) & (kpos >= 0) & (kpos < n_tok)
        valid = jnp.concatenate([valid] * ATT_GROUP, axis=0)
        local = pl.ds(pl.multiple_of(u * blk, blk), blk)
        q_all = q_ref[local, :]
        for kvh in range(ATT_KV_HEADS):
            cols = slice(kvh * HEAD_DIM, (kvh + 1) * HEAD_DIM)
            q = jnp.concatenate(
                [_rope(q_all[:, (kvh * ATT_GROUP + g) * HEAD_DIM:(kvh * ATT_GROUP + g + 1) * HEAD_DIM],
                       cos, sin, even_group) for g in range(ATT_GROUP)], axis=0).astype(BF16)
            s_ctx = _dot_nt(q, ck_ref[0, :, cols].astype(BF16))
            s_loc = jnp.where(valid, _dot_nt(q, kpad_ref[band, cols]), -jnp.inf)
            v_ctx = jnp.concatenate([cv_ref[0, :, cols].astype(BF16), ctx_ones], axis=1)
            v_loc = vpad_ref[band, 2 * kvh * HEAD_DIM:(2 * kvh + 2) * HEAD_DIM]
            o = _softmax_av([s_ctx, s_loc], [v_ctx, v_loc], _sink_column(sink_ref, kvh, blk))
            for g in range(ATT_GROUP):
                hd = kvh * ATT_GROUP + g
                o_ref[local, hd * HEAD_DIM:(hd + 1) * HEAD_DIM] = o[g * blk:(g + 1) * blk, :].astype(BF16)
        return carry

    lax.fori_loop(0, blocks, query_block, 0)


def _window_attention(proj, cache_k, cache_v, sink, cos, sin_signed, *, n_batch, n_tok):
    tq = 4 * ATT_BLOCK
    steps = n_tok // tq
    n_ctx = cache_k.shape[1]
    q_col = 5 * HG_WIDTH // ATT_WIDTH
    k_col = (5 * HG_WIDTH + ATT_WIDTH) // KV_WIDTH
    table = pl.BlockSpec((n_tok, HEAD_DIM), lambda b, i, s: (0, 0))
    cache = pl.BlockSpec((1, n_ctx, KV_WIDTH), lambda b, i, s: (b, 0, 0))
    return pl.pallas_call(
        functools.partial(_win_attn_kernel, n_tok=n_tok),
        out_shape=jax.ShapeDtypeStruct((n_batch * n_tok, ATT_WIDTH), BF16),
        grid_spec=pltpu.PrefetchScalarGridSpec(
            num_scalar_prefetch=1,
            grid=(n_batch, steps),
            in_specs=[pl.BlockSpec((tq, ATT_WIDTH), lambda b, i, s: (b * steps + i, q_col)),
                      pl.BlockSpec((n_tok, KV_WIDTH), lambda b, i, s: (b, k_col)),
                      pl.BlockSpec((n_tok, KV_WIDTH), lambda b, i, s: (b, k_col + 1)),
                      cache, cache, table, table],
            out_specs=pl.BlockSpec((tq, ATT_WIDTH), lambda b, i, s: (b * steps + i, 0)),
            scratch_shapes=[pltpu.VMEM((n_tok + 2 * ATT_BLOCK, KV_WIDTH), BF16),
                            pltpu.VMEM((n_tok + 2 * ATT_BLOCK, 2 * KV_WIDTH), BF16)]),
        compiler_params=_cparams("parallel", "arbitrary"),
        name="window_attention",
    )(sink, proj, proj, proj, cache_k, cache_v, cos, sin_signed)


def _rope_tables(n_tok):
    rows = n_tok // GRID_W
    row = np.repeat(np.arange(rows), GRID_W).astype(np.float32)
    col = np.tile(np.arange(GRID_W), rows).astype(np.float32)
    inv = np.float32(ROPE_BASE) ** (-np.arange(ROPE_FREQS, dtype=np.float32) / np.float32(ROPE_FREQS))
    ar, ac = row[:, None] * inv, col[:, None] * inv
    cr, sr, cc, sc = np.cos(ar), np.sin(ar), np.cos(ac), np.sin(ac)
    return (jnp.asarray(np.concatenate([cr, cr, cc, cc], axis=1), F32),
            jnp.asarray(np.concatenate([-sr, sr, -sc, sc], axis=1), F32))


def _outproj_kernel(hg_ref, att_ref, w_ref, x_ref, mod_ref, g_ref, xo_ref, h_ref):
    chunk = 256
    for r in range(x_ref.shape[0] // chunk):
        sl = slice(r * chunk, (r + 1) * chunk)
        mix = _dot(hg_ref[sl, :], w_ref[0:HG_WIDTH, :]) + _dot(att_ref[sl, :], w_ref[HG_WIDTH:, :])
        x = x_ref[sl, :] + mod_ref[0, 2:3, :] * mix
        xo_ref[sl, :] = x
        h_ref[sl, :] = _norm_modulate(x, g_ref[...], mod_ref[0, 3:4, :], mod_ref[0, 4:5, :]).astype(BF16)


def _out_projection(o_hg, o_att, w_bf16, x, mod, gain, *, mod_base, rows_per_mod):
    m = x.shape[0]
    tm = 512
    row = lambda i: (i, 0)
    return pl.pallas_call(
        _outproj_kernel,
        out_shape=(jax.ShapeDtypeStruct((m, D_MODEL), F32), jax.ShapeDtypeStruct((m, D_MODEL), BF16)),
        grid=(m // tm,),
        in_specs=[pl.BlockSpec((tm, HG_WIDTH), row),
                  pl.BlockSpec((tm, ATT_WIDTH), row),
                  pl.BlockSpec((HG_WIDTH + ATT_WIDTH, D_MODEL), lambda i: (0, 0)),
                  pl.BlockSpec((tm, D_MODEL), row),
                  pl.BlockSpec((1, N_MOD, D_MODEL), lambda i: (mod_base + (i * tm) // rows_per_mod, 0, 0)),
                  pl.BlockSpec((1, D_MODEL), lambda i: (0, 0))],
        out_specs=(pl.BlockSpec((tm, D_MODEL), row), pl.BlockSpec((tm, D_MODEL), row)),
        compiler_params=_cparams("parallel"),
        name="out_projection",
    )(o_hg, o_att, w_bf16, x, mod, gain)


def _prefix_count(x):
    n = x.shape[1]
    i = lax.broadcasted_iota(jnp.int32, (SEL_BLOCK, SEL_BLOCK), 0)
    j = lax.broadcasted_iota(jnp.int32, (SEL_BLOCK, SEL_BLOCK), 1)
    upper = jnp.where(i < j, 1.0, 0.0).astype(BF16)
    off = jnp.zeros((x.shape[0], 1), F32)
    outs = []
    for blk in range(n // SEL_BLOCK):
        xb = x[:, blk * SEL_BLOCK:(blk + 1) * SEL_BLOCK]
        outs.append(_dot(xb.astype(BF16), upper) + off)
        off = off + jnp.sum(xb, axis=-1, keepdims=True)
    return outs[0] if len(outs) == 1 else jnp.concatenate(outs, axis=1)


def _route_select(h_ref, w_ref, cap):
    logits = _dot_nt(w_ref[...], h_ref[...])
    ex = jnp.exp(logits - jnp.max(logits, axis=0, keepdims=True))
    aff = ex / jnp.sum(ex, axis=0, keepdims=True)
    bits = pltpu.bitcast(aff, jnp.int32)

    thr = jnp.zeros((N_EXPERTS, 1), jnp.int32)
    shift = 31
    while shift > 0:
        width = min(4, shift)
        shift -= width
        digit = jnp.zeros((N_EXPERTS, 1), jnp.int32)
        for j in range(1, 2 ** width):
            cnt = jnp.sum(jnp.where(bits >= (thr | (j << shift)), 1.0, 0.0), axis=-1, keepdims=True)
            digit = digit + jnp.where(cnt >= cap, 1, 0)
        thr = thr | (digit << shift)
    above = jnp.where(bits > thr, 1.0, 0.0)
    tied = jnp.where(bits == thr, 1.0, 0.0)
    room = cap - jnp.sum(above, axis=-1, keepdims=True)
    sel = above + tied * jnp.where(_prefix_count(tied) < room, 1.0, 0.0)
    return jnp.where(sel > 0.0, _prefix_count(sel), -1.0), aff


def _one_hot_gather(slot_rows, aff_rows, h_ref, rows, cap):
    n_tok = h_ref.shape[0]
    c = (lax.broadcasted_iota(jnp.int32, (rows, n_tok), 0) & (cap - 1)).astype(F32)
    hit = c == slot_rows
    x = _dot(jnp.where(hit, 1.0, 0.0).astype(BF16), h_ref[...]).astype(BF16)
    return x, jnp.sum(jnp.where(hit, aff_rows, 0.0), axis=-1, keepdims=True)


def _route_request_kernel(h_ref, w_ref, x_ref, g_ref, slot_ref, *, cap):
    n_tok = h_ref.shape[0]
    slot, aff = _route_select(h_ref, w_ref, cap)
    slot_ref[0] = slot
    per_row = lambda a: jnp.concatenate(
        [jnp.broadcast_to(a[e:e + 1, :], (cap, n_tok)) for e in range(N_EXPERTS)], axis=0)
    x, g = _one_hot_gather(per_row(slot), per_row(aff), h_ref, N_EXPERTS * cap, cap)
    for e in range(N_EXPERTS):
        x_ref[e] = x[e * cap:(e + 1) * cap, :]
        g_ref[e] = g[e * cap:(e + 1) * cap, :]


def _route_only_kernel(h_ref, w_ref, slot_ref, aff_ref, bounds_ref, *, cap):
    n_tok = h_ref.shape[0]
    slot, aff = _route_select(h_ref, w_ref, cap)
    slot_ref[0] = slot
    aff_ref[0] = aff
    n = lax.broadcasted_iota(jnp.int32, (n_tok, LANES), 0)
    t = lax.broadcasted_iota(jnp.int32, (n_tok, LANES), 1)
    before = jnp.where(n < t * TOK_BLOCK, 1.0, 0.0).astype(BF16)
    chosen = jnp.where(slot >= 0.0, 1.0, 0.0).astype(BF16)
    bounds_ref[0] = _dot(chosen, before).astype(jnp.int32)


def _gather_window_kernel(bounds_ref, slot_ref, aff_ref, h_ref, xh_ref, gh_ref, x_ref, g_ref, xo_ref, go_ref, *,
                          n_batch, group):
    b = pl.program_id(0)
    eg = pl.program_id(1)
    n_tok = h_ref.shape[0]
    nt = n_tok // TOK_BLOCK
    cap = x_ref.shape[1]
    region = TOK_BLOCK // group

    @pl.when(b >= n_batch)
    def _():
        x_ref[...] = xh_ref[...]
        g_ref[...] = gh_ref[...]

    @pl.when(b < n_batch)
    def _():
        x_ref[...] = jnp.zeros(x_ref.shape, BF16)
        g_ref[...] = jnp.zeros(g_ref.shape, F32)
        local_i = lax.broadcasted_iota(jnp.int32, (region, TOK_BLOCK), 0).astype(F32)
        over_i = lax.broadcasted_iota(jnp.int32, (TOK_BLOCK, TOK_BLOCK), 0)

        def token_block(t, carry):
            toks = pl.ds(pl.multiple_of(t * TOK_BLOCK, TOK_BLOCK), TOK_BLOCK)
            hits, gates, firsts, n_overs, slots, affs = [], [], [], [], [], []
            for k in range(group):
                e = eg * group + k
                start, length = _window(bounds_ref, (b * N_EXPERTS + e) * (nt + 1), t)
                first = jnp.minimum(start, cap - region)
                s = slot_ref[0, pl.ds(e, 1), toks]
                a = aff_ref[0, pl.ds(e, 1), toks]
                hit = local_i == jnp.where(s >= 0.0, s - first.astype(F32), -1.0)
                hits.append(jnp.where(hit, 1.0, 0.0).astype(BF16))
                gates.append(jnp.sum(jnp.where(hit, a, 0.0), axis=-1, keepdims=True))
                firsts.append(first)
                n_overs.append(start + length - first - region)
                slots.append(s)
                affs.append(a)
            xc = _dot(jnp.concatenate(hits, axis=0), h_ref[toks, :]).astype(BF16)
            for k in range(group):
                dst = pl.ds(pl.multiple_of(firsts[k], PIECE), region)
                x_ref[k, dst, :] = x_ref[k, dst, :] + xc[k * region:(k + 1) * region, :]
                g_ref[k, dst, :] = g_ref[k, dst, :] + gates[k]
            for k in range(group):
                @pl.when(n_overs[k] > 0)
                def _(k=k):
                    base = firsts[k] + region
                    hit = (over_i + base).astype(F32) == slots[k]
                    xo_ref[...] = _dot(jnp.where(hit, 1.0, 0.0).astype(BF16), h_ref[toks, :]).astype(BF16)
                    go_ref[...] = jnp.sum(jnp.where(hit, affs[k], 0.0), axis=-1, keepdims=True)

                    def place(i, carry):
                        src = pl.ds(pl.multiple_of(i * PIECE, PIECE), PIECE)
                        dst = pl.ds(pl.multiple_of(base + i * PIECE, PIECE), PIECE)
                        x_ref[k, dst, :] = x_ref[k, dst, :] + xo_ref[src, :]
                        g_ref[k, dst, :] = g_ref[k, dst, :] + go_ref[src, :]
                        return carry

                    lax.fori_loop(0, n_overs[k] // PIECE, place, 0)
            return carry

        lax.fori_loop(0, nt, token_block, 0)


def _route_gather_requests(h, w_router_t, *, n_batch, n_tok):
    cap = CAPACITY_FACTOR * n_tok // N_EXPERTS
    out_block = lambda w: pl.BlockSpec((N_EXPERTS, cap, w), lambda b: (0, b, 0))
    return pl.pallas_call(
        functools.partial(_route_request_kernel, cap=cap),
        out_shape=(jax.ShapeDtypeStruct((N_EXPERTS, n_batch * cap, D_MODEL), BF16),
                   jax.ShapeDtypeStruct((N_EXPERTS, n_batch * cap, 1), F32),
                   jax.ShapeDtypeStruct((n_batch, N_EXPERTS, n_tok), F32)),
        grid=(n_batch,),
        in_specs=[pl.BlockSpec((n_tok, D_MODEL), lambda b: (b, 0)),
                  pl.BlockSpec((N_EXPERTS, D_MODEL), lambda b: (0, 0))],
        out_specs=(out_block(D_MODEL), out_block(1), pl.BlockSpec((1, N_EXPERTS, n_tok), lambda b: (b, 0, 0))),
        compiler_params=_cparams("parallel"),
        name="route_gather_requests",
    )(h, w_router_t)


def _route_only(h, w_router_t, *, n_batch, n_tok):
    cap = CAPACITY_FACTOR * n_tok // N_EXPERTS
    nt = n_tok // TOK_BLOCK
    expert_major = pl.BlockSpec((1, N_EXPERTS, n_tok), lambda b: (b, 0, 0))
    slot, aff, bounds = pl.pallas_call(
        functools.partial(_route_only_kernel, cap=cap),
        out_shape=(jax.ShapeDtypeStruct((n_batch, N_EXPERTS, n_tok), F32),
                   jax.ShapeDtypeStruct((n_batch, N_EXPERTS, n_tok), F32),
                   jax.ShapeDtypeStruct((n_batch, N_EXPERTS, LANES), jnp.int32)),
        grid=(n_batch,),
        in_specs=[pl.BlockSpec((n_tok, D_MODEL), lambda b: (b, 0)),
                  pl.BlockSpec((N_EXPERTS, D_MODEL), lambda b: (0, 0))],
        out_specs=(expert_major, expert_major, pl.BlockSpec((1, N_EXPERTS, LANES), lambda b: (b, 0, 0))),
        compiler_params=_cparams("parallel"),
        name="route_select",
    )(h, w_router_t)
    return slot, aff, bounds[:, :, :nt + 1].reshape(-1)


def _gather_windows(bounds, slot, aff, h, x_head, g_head, *, n_batch, n_tok):
    cap = CAPACITY_FACTOR * n_tok // N_EXPERTS
    group = TOK_BLOCK // SLOT_REGION
    head_rows = x_head.shape[1]
    assert head_rows % cap == 0 and N_EXPERTS % group == 0
    n_head = head_rows // cap
    rows_total = head_rows + n_batch * cap
    last = n_batch - 1
    request = lambda b: jnp.minimum(b, last)
    out_row = lambda b: jnp.where(b < n_batch, n_head + b, b - n_batch)
    head_row = lambda b: jnp.maximum(b - n_batch, 0)
    out_block = lambda w: pl.BlockSpec((group, cap, w), lambda b, g, s: (g, out_row(b), 0))
    head_block = lambda w: pl.BlockSpec((group, cap, w), lambda b, g, s: (g, head_row(b), 0))
    expert_major = pl.BlockSpec((1, N_EXPERTS, n_tok), lambda b, g, s: (request(b), 0, 0))
    return pl.pallas_call(
        functools.partial(_gather_window_kernel, n_batch=n_batch, group=group),
        out_shape=(jax.ShapeDtypeStruct((N_EXPERTS, rows_total, D_MODEL), BF16),
                   jax.ShapeDtypeStruct((N_EXPERTS, rows_total, 1), F32)),
        grid_spec=pltpu.PrefetchScalarGridSpec(
            num_scalar_prefetch=1,
            grid=(n_batch + n_head, N_EXPERTS // group),
            in_specs=[expert_major, expert_major,
                      pl.BlockSpec((n_tok, D_MODEL), lambda b, g, s: (request(b), 0)),
                      head_block(D_MODEL), head_block(1)],
            out_specs=(out_block(D_MODEL), out_block(1)),
            scratch_shapes=[pltpu.VMEM((TOK_BLOCK, D_MODEL), BF16), pltpu.VMEM((TOK_BLOCK, 1), F32)]),
        compiler_params=_cparams("arbitrary", "arbitrary"),
        name="gather_windows",
    )(bounds, slot, aff, h, x_head, g_head)


def _moe_kernel(x_ref, g_ref, wg_ref, wu_ref, wd_ref, y_ref, hid_ref, *, n_ff, rows):
    s = pl.program_id(2)
    tf = wg_ref.shape[2]

    @pl.when(s < n_ff)
    def _():
        wg = wg_ref[0].astype(BF16)
        wu = wu_ref[0].astype(BF16)
        cols = pl.ds(pl.multiple_of(s * tf, tf), tf)
        for r in range(x_ref.shape[1] // rows):
            sl = slice(r * rows, (r + 1) * rows)
            x = x_ref[0, sl, :]
            hid_ref[sl, cols] = (_silu(_dot(x, wg)) * _dot(x, wu)).astype(BF16)

    @pl.when(s >= n_ff)
    def _():
        y = _dot(hid_ref[...], wd_ref[0].astype(BF16))
        y_ref[0] = (y * g_ref[0]).astype(BF16)


def _experts(x, gate, w_gate, w_up, w_down):
    n_rows = x.shape[1]
    tr, tf, tn = n_rows // 2, 512, 256
    n_ff, n_out = EXPERT_FF // tf, D_MODEL // tn
    up_tile = lambda e, r, s: (e, 0, jnp.minimum(s, n_ff - 1))
    out_tile = lambda s: jnp.maximum(s - n_ff, 0)
    return pl.pallas_call(
        functools.partial(_moe_kernel, n_ff=n_ff, rows=256),
        out_shape=jax.ShapeDtypeStruct((N_EXPERTS, n_rows, D_MODEL), BF16),
        grid=(N_EXPERTS, n_rows // tr, n_ff + n_out),
        in_specs=[pl.BlockSpec((1, tr, D_MODEL), lambda e, r, s: (e, r, 0), pipeline_mode=pl.Buffered(1)),
                  pl.BlockSpec((1, tr, 1), lambda e, r, s: (e, r, 0)),
                  pl.BlockSpec((1, D_MODEL, tf), up_tile),
                  pl.BlockSpec((1, D_MODEL, tf), up_tile),
                  pl.BlockSpec((1, EXPERT_FF, tn), lambda e, r, s: (e, 0, out_tile(s)))],
        out_specs=pl.BlockSpec((1, tr, tn), lambda e, r, s: (e, r, out_tile(s))),
        scratch_shapes=[pltpu.VMEM((tr, EXPERT_FF), BF16)],
        compiler_params=_cparams("parallel", "parallel", "arbitrary"),
        name="expert_swiglu",
    )(x, gate, w_gate, w_up, w_down)


def _window(bounds_ref, base, t):
    p0 = bounds_ref[base + t]
    p1 = bounds_ref[base + t + 1]
    start = (p0 // PIECE) * PIECE
    return start, jnp.where(p1 > p0, ((p1 - start + PIECE - 1) // PIECE) * PIECE, 0)


def _combine_kernel(bounds_ref, y_ref, slot_ref, x_ref, mod_ref, g_ref, o_ref, ybuf_ref, obuf_ref, acc_ref, *,
                    nt, region):
    b = pl.program_id(0)
    t = pl.program_id(1)
    tt = x_ref.shape[0]
    cap = y_ref.shape[1]
    local_i = lax.broadcasted_iota(jnp.int32, (region, tt), 0).astype(F32)
    hits = []
    for e in range(N_EXPERTS):
        start, _ = _window(bounds_ref, (b * N_EXPERTS + e) * (nt + 1), t)
        first = jnp.minimum(start, cap - region)
        ybuf_ref[e * region:(e + 1) * region, :] = y_ref[e, pl.ds(pl.multiple_of(first, PIECE), region), :]
        s = slot_ref[0, e:e + 1, :]
        hit = local_i == jnp.where(s >= 0.0, s - first.astype(F32), -1.0)
        hits.append(jnp.where(hit, 1.0, 0.0).astype(BF16))
    acc_ref[...] = _dot_tn(jnp.concatenate(hits, axis=0), ybuf_ref[...])
    over_i = lax.broadcasted_iota(jnp.int32, (TOK_BLOCK, tt), 0)

    def overflow(e, carry):
        start, length = _window(bounds_ref, (b * N_EXPERTS + e) * (nt + 1), t)
        base = jnp.minimum(start, cap - region) + region
        n_over = start + length - base

        @pl.when(n_over > 0)
        def _():
            def copy(i, carry):
                obuf_ref[pl.ds(pl.multiple_of(i * PIECE, PIECE), PIECE), :] = (
                    y_ref[e, pl.ds(pl.multiple_of(base + i * PIECE, PIECE), PIECE), :])
                return carry

            def clear(i, carry):
                obuf_ref[pl.ds(pl.multiple_of(n_over + i * PIECE, PIECE), PIECE), :] = jnp.zeros((PIECE, D_MODEL), BF16)
                return carry

            lax.fori_loop(0, n_over // PIECE, copy, 0)
            lax.fori_loop(0, (TOK_BLOCK - n_over) // PIECE, clear, 0)
            hit = (over_i + base).astype(F32) == slot_ref[0, pl.ds(e, 1), :]
            acc_ref[...] += _dot_tn(jnp.where(hit, 1.0, 0.0).astype(BF16), obuf_ref[...])

        return carry

    lax.fori_loop(0, N_EXPERTS, overflow, 0)
    x = x_ref[...] + mod_ref[0, 5:6, :] * acc_ref[...]
    var = jnp.mean(x * x, axis=-1, keepdims=True)
    o_ref[...] = x * lax.rsqrt(var + NORM_EPS) * g_ref[...]


def _combine(bounds, y, slot, x_mid, mod, final_g, *, n_batch, n_tok, row_block_off, mod_base, mod_per_batch):
    cap = CAPACITY_FACTOR * n_tok // N_EXPERTS
    tt = TOK_BLOCK
    nt = n_tok // tt
    region = min(SLOT_REGION, cap)
    assert cap - region <= TOK_BLOCK
    return pl.pallas_call(
        functools.partial(_combine_kernel, nt=nt, region=region),
        out_shape=jax.ShapeDtypeStruct((n_batch * n_tok, D_MODEL), F32),
        grid_spec=pltpu.PrefetchScalarGridSpec(
            num_scalar_prefetch=1,
            grid=(n_batch, nt),
            in_specs=[pl.BlockSpec((N_EXPERTS, cap, D_MODEL), lambda b, t, s: (0, row_block_off + b, 0)),
                      pl.BlockSpec((1, N_EXPERTS, tt), lambda b, t, s: (b, 0, t)),
                      pl.BlockSpec((tt, D_MODEL), lambda b, t, s: (b * nt + t, 0)),
                      pl.BlockSpec((1, N_MOD, D_MODEL), lambda b, t, s: (mod_base + b * mod_per_batch, 0, 0)),
                      pl.BlockSpec((1, D_MODEL), lambda b, t, s: (0, 0))],
            out_specs=pl.BlockSpec((tt, D_MODEL), lambda b, t, s: (b * nt + t, 0)),
            scratch_shapes=[pltpu.VMEM((N_EXPERTS * region, D_MODEL), BF16),
                            pltpu.VMEM((TOK_BLOCK, D_MODEL), BF16), pltpu.VMEM((tt, D_MODEL), F32)]),
        compiler_params=_cparams("arbitrary", "arbitrary"),
        name="combine_final_norm",
    )(bounds, y, slot, x_mid, mod, final_g)


def kernel(x_prompt, x_sample, cache_k, cache_v, state_hgrn, c, c_ctx, w_ada, b_ada, norm1_g, w_in, hg_lb,
           hg_norm_g, attn_sink, w_out, norm2_g, w_router, w_gate, w_up, w_down, final_norm_g):
    n_p, t_p, _ = x_prompt.shape
    n_s, t_s, _ = x_sample.shape
    assert w_ada.shape[0] == 1 and 1 + n_s <= COND_ROWS
    layer = 0

    cond = jnp.zeros((COND_ROWS, D_MODEL), F32).at[0].set(c_ctx).at[1:1 + n_s].set(c)
    mod = _ada_modulation(cond, w_ada[layer], b_ada[layer]).reshape(COND_ROWS, N_MOD, D_MODEL)
    lb = jnp.cumsum(jax.nn.softmax(hg_lb.astype(F32), axis=0), axis=0)[layer]
    w_in_bf16 = w_in[layer].astype(BF16)
    w_out_bf16 = w_out[layer].astype(BF16)
    w_router_t = w_router[layer].T.astype(BF16)
    norm1 = norm1_g[layer].reshape(1, D_MODEL)
    norm2 = norm2_g[layer].reshape(1, D_MODEL)
    hg_gain = hg_norm_g[layer].reshape(1, HG_WIDTH)
    final_g = final_norm_g.reshape(1, D_MODEL)
    sink = attn_sink[layer]
    cos, sin_signed = _rope_tables(t_s)

    xp = x_prompt.reshape(n_p * t_p, D_MODEL)
    xs = x_sample.reshape(n_s * t_s, D_MODEL)
    groups = dict(p=dict(mod_base=0, rows_per_mod=n_p * t_p), s=dict(mod_base=1, rows_per_mod=t_s))

    proj_p = _in_projection(xp, mod, norm1, w_in_bf16, **groups["p"])
    proj_s = _in_projection(xs, mod, norm1, w_in_bf16, **groups["s"])

    ohg_p, new_state = _hgrn(proj_p, lb, hg_gain, None, n_batch=n_p, n_tok=t_p)
    ohg_s, _ = _hgrn(proj_s, lb, hg_gain, state_hgrn[:, layer:layer + 1].astype(F32), n_batch=n_s, n_tok=t_s)

    oatt_p = _context_attention(proj_p, sink, n_batch=n_p, n_tok=t_p)
    n_ctx = cache_k.shape[2]
    oatt_s = _window_attention(proj_s, cache_k[:, layer].reshape(n_s, n_ctx, KV_WIDTH),
                               cache_v[:, layer].reshape(n_s, n_ctx, KV_WIDTH), sink, cos, sin_signed,
                               n_batch=n_s, n_tok=t_s)

    xmid_p, h2_p = _out_projection(ohg_p, oatt_p, w_out_bf16, xp, mod, norm2, **groups["p"])
    xmid_s, h2_s = _out_projection(ohg_s, oatt_s, w_out_bf16, xs, mod, norm2, **groups["s"])

    cap_p = CAPACITY_FACTOR * t_p // N_EXPERTS
    cap_s = CAPACITY_FACTOR * t_s // N_EXPERTS
    off_s = n_p * cap_p // cap_s
    xg_p, gate_p, slot_p = _route_gather_requests(h2_p, w_router_t, n_batch=n_p, n_tok=t_p)
    slot_s, aff_s, bounds_s = _route_only(h2_s, w_router_t, n_batch=n_s, n_tok=t_s)
    xg, gate = _gather_windows(bounds_s, slot_s, aff_s, h2_s, xg_p, gate_p, n_batch=n_s, n_tok=t_s)
    assert t_p == TOK_BLOCK
    bounds_p = jnp.tile(jnp.array([0, cap_p], jnp.int32), n_p * N_EXPERTS)

    y = _experts(xg, gate, w_gate[layer], w_up[layer], w_down[layer])

    y_prompt = _combine(bounds_p, y, slot_p, xmid_p, mod, final_g, n_batch=n_p, n_tok=t_p, row_block_off=0,
                        mod_base=0, mod_per_batch=0)
    y_sample = _combine(bounds_s, y, slot_s, xmid_s, mod, final_g, n_batch=n_s, n_tok=t_s, row_block_off=off_s,
                        mod_base=1, mod_per_batch=1)

    k_col = 5 * HG_WIDTH + ATT_WIDTH
    new_k = proj_p[:, k_col:k_col + KV_WIDTH].reshape(n_p, 1, t_p, ATT_KV_HEADS, HEAD_DIM)
    new_v = proj_p[:, k_col + KV_WIDTH:k_col + 2 * KV_WIDTH].reshape(n_p, 1, t_p, ATT_KV_HEADS, HEAD_DIM)
    return (y_prompt.reshape(n_p, t_p, D_MODEL), y_sample.reshape(n_s, t_s, D_MODEL), new_k, new_v, new_state)
```
